```python
import math
import jax, jax.numpy as jnp
from jax import lax
import numpy as np

D_MODEL = 1024
BATCH = 8
SEQ = 4096
DEPTH = 4

CHUNK = 64
N_MIXERS = 3
EPS = 1e-6

POOL_WINDOWS = (2, 4, 8, 16)
N_POOL_GROUPS = len(POOL_WINDOWS)
POOL_GROUP = D_MODEL // N_POOL_GROUPS

SB_HEAD_DIM = 64
SB_HEADS = D_MODEL // SB_HEAD_DIM
Q_BLOCK = 128

S5_GROUP = 16
S5_GROUPS = D_MODEL // S5_GROUP
S5_STATE = 64
S5_DT_MIN = 1e-3
S5_DT_MAX = 1e-1

MEM_LEN = 256
XA_HEADS = 4
XA_HEAD_DIM = D_MODEL // XA_HEADS

D_FF = ((8 * D_MODEL // 3 + 127) // 128) * 128
CONV_WIDTH = 3

N_POOL_LAYERS = (DEPTH + 2) // 3
N_SB_LAYERS = (DEPTH + 1) // 3
N_S5_LAYERS = DEPTH // 3

kernel_name = "hybrid_pool_stickbreak_s5_streaming_trunk"


def rms_norm(x, g):
    xf = x.astype(jnp.float32)
    y = xf * lax.rsqrt(jnp.mean(xf * xf, axis=-1, keepdims=True) + EPS)
    return (y * g.astype(jnp.float32)).astype(x.dtype)


def causal_shift(x, k):
    return jnp.pad(x, ((0, 0), (k, 0), (0, 0)))[:, : x.shape[1]]


def pool_mixer(h, w, scale):
    B, S, D = h.shape
    hf = h.astype(jnp.float32)
    cs = jnp.cumsum(hf, axis=1)
    counts = jnp.arange(1, S + 1, dtype=jnp.float32)
    groups = []
    for gi, win in enumerate(POOL_WINDOWS):
        sl = slice(gi * POOL_GROUP, (gi + 1) * POOL_GROUP)
        c = cs[..., sl]
        win_sum = c - causal_shift(c, win)
        cnt = jnp.minimum(counts, float(win))[None, :, None]
        groups.append(win_sum / cnt - hf[..., sl])
    p = jnp.stack(groups, axis=2).astype(h.dtype)
    y = jnp.einsum('bsgc,gcd->bsgd', p, w).reshape(B, S, D)
    return y * scale


def stick_breaking_attention(h, w_qkv, w_o):
    B, S, D = h.shape
    qkv = (h @ w_qkv).reshape(B, S, 3, SB_HEADS, SB_HEAD_DIM)
    q, k, v = qkv[:, :, 0], qkv[:, :, 1], qkv[:, :, 2]
    scale = SB_HEAD_DIM ** -0.5
    outs = []
    for i in range(S // Q_BLOCK):
        q0 = i * Q_BLOCK
        kend = q0 + Q_BLOCK
        z = jnp.einsum('bqhd,bkhd->bhqk', q[:, q0:kend], k[:, :kend]).astype(jnp.float32) * scale
        qpos = q0 + jnp.arange(Q_BLOCK)
        kpos = jnp.arange(kend)
        mask = kpos[None, :] < qpos[:, None]
        log_one_minus = jnp.where(mask, jax.nn.log_sigmoid(-z), 0.0)
        between = lax.cumsum(log_one_minus, axis=3, reverse=True) - log_one_minus
        a = jnp.where(mask, jnp.exp(jax.nn.log_sigmoid(z) + between), 0.0)
        outs.append(jnp.einsum('bhqk,bkhd->bqhd', a.astype(v.dtype), v[:, :kend]))
    o = jnp.concatenate(outs, axis=1).reshape(B, S, D)
    return o @ w_o


def _lin_rec_combine(left, right):
    a1, b1 = left
    a2, b2 = right
    return a1 * a2, a2 * b1 + b2


def s5_mixer(h, a_re, a_im, log_dt, b_re, b_im, c_re, c_im, d, w_glu):
    B, S, D = h.shape
    f32 = jnp.float32
    u = h.astype(f32)
    lam = lax.complex(a_re.astype(f32), a_im.astype(f32))
    dt = jnp.exp(log_dt.astype(f32))[:, None]
    dt_lam = lam * dt
    a_bar = jnp.exp(dt_lam)
    b_bar = ((a_bar - 1.0) / lam)[..., None] * lax.complex(b_re.astype(f32), b_im.astype(f32))
    c_mat = lax.complex(c_re.astype(f32), c_im.astype(f32))
    steps = jnp.arange(1, CHUNK + 1, dtype=f32)
    a_pow = jnp.exp(steps[:, None, None] * dt_lam[None])
    n_chunks = S // CHUNK
    u_chunks = u.reshape(B, n_chunks, CHUNK, S5_GROUPS, S5_GROUP).transpose(1, 0, 2, 3, 4)

    def chunk_step(state, u_blk):
        bu = jnp.einsum('bcgi,gpi->bcgp', u_blk.astype(jnp.complex64), b_bar)
        a_seq = jnp.broadcast_to(a_bar, bu.shape)
        _, xs = lax.associative_scan(_lin_rec_combine, (a_seq, bu), axis=1)
        xs = xs + a_pow[None] * state[:, None]
        y = jnp.real(jnp.einsum('bcgp,gip->bcgi', xs, c_mat))
        return xs[:, -1], y

    state0 = jnp.zeros((B, S5_GROUPS, S5_STATE), jnp.complex64)
    _, ys = lax.scan(chunk_step, state0, u_chunks)
    y = ys.transpose(1, 0, 2, 3, 4).reshape(B, S, D) + d.astype(f32) * u
    y = jax.nn.gelu(y).astype(h.dtype)
    val, gate = jnp.split(y @ w_glu, 2, axis=-1)
    return val * jax.nn.sigmoid(gate)


def memory_cross_attention(h, mem_n, wq, wkv, wo):
    B, S, D = h.shape
    M = mem_n.shape[1]
    q = (h @ wq).reshape(B, S, XA_HEADS, XA_HEAD_DIM)
    kv = (mem_n @ wkv).reshape(B, M, 2, XA_HEADS, XA_HEAD_DIM)
    k, v = kv[:, :, 0], kv[:, :, 1]
    s = jnp.einsum('bshd,bmhd->bhsm', q, k).astype(jnp.float32) * (XA_HEAD_DIM ** -0.5)
    p = jax.nn.softmax(s, axis=-1).astype(v.dtype)
    o = jnp.einsum('bhsm,bmhd->bshd', p, v).reshape(B, S, D)
    return o @ wo


def conv_glu_ffn(h, w_up, conv_w, conv_b, w_down):
    u = h @ w_up
    u = sum(conv_w[CONV_WIDTH - 1 - k] * causal_shift(u, k) for k in range(CONV_WIDTH)) + conv_b
    val, gate = jnp.split(u, 2, axis=-1)
    return (jax.nn.silu(gate) * val) @ w_down


def _fwd_setup_inputs(seed: int = 0) -> dict:
    key = jax.random.key(seed)
    keys = iter(jax.random.split(key, 40))
    f32 = jnp.float32

    def nrm(shape, scale):
        return jax.random.normal(next(keys), shape, f32) * scale

    def gain(shape):
        return 1.0 + nrm(shape, 0.02)

    D, F = D_MODEL, D_FF
    G, P, Cg = S5_GROUPS, S5_STATE, S5_GROUP
    a_im_init = jnp.pi * jnp.arange(P, dtype=f32)
    return {
        "x": nrm((BATCH, SEQ, D), 1.0),
        "mem": nrm((BATCH, MEM_LEN, D), 1.0),
        "mix_norm_g": gain((DEPTH, D)),
        "pool_w": nrm((N_POOL_LAYERS, N_POOL_GROUPS, POOL_GROUP, POOL_GROUP), POOL_GROUP ** -0.5),
        "pool_scale": gain((N_POOL_LAYERS, D)),
        "sb_w_qkv": nrm((N_SB_LAYERS, D, 3 * D), D ** -0.5),
        "sb_w_o": nrm((N_SB_LAYERS, D, D), D ** -0.5),
        "s5_a_re": -0.5 + nrm((N_S5_LAYERS, G, P), 0.01),
        "s5_a_im": a_im_init + nrm((N_S5_LAYERS, G, P), 0.01),
        "s5_log_dt": jax.random.uniform(next(keys), (N_S5_LAYERS, G), f32,
                                        minval=math.log(S5_DT_MIN), maxval=math.log(S5_DT_MAX)),
        "s5_b_re": nrm((N_S5_LAYERS, G, P, Cg), (2 * Cg) ** -0.5),
        "s5_b_im": nrm((N_S5_LAYERS, G, P, Cg), (2 * Cg) ** -0.5),
        "s5_c_re": nrm((N_S5_LAYERS, G, Cg, P), (2 * P) ** -0.5),
        "s5_c_im": nrm((N_S5_LAYERS, G, Cg, P), (2 * P) ** -0.5),
        "s5_d": nrm((N_S5_LAYERS, D), 1.0),
        "s5_w_glu": nrm((N_S5_LAYERS, D, 2 * D), D ** -0.5),
        "xa_norm_g": gain((DEPTH, D)),
        "mem_norm_g": gain((DEPTH, D)),
        "xa_wq": nrm((DEPTH, D, D), D ** -0.5),
        "xa_wkv": nrm((DEPTH, D, 2 * D), D ** -0.5),
        "xa_wo": nrm((DEPTH, D, D), D ** -0.5),
        "ffn_norm_g": gain((DEPTH, D)),
        "ffn_w_up": nrm((DEPTH, D, 2 * F), D ** -0.5),
        "ffn_conv_w": nrm((DEPTH, CONV_WIDTH, 2 * F), CONV_WIDTH ** -0.5),
        "ffn_conv_b": nrm((DEPTH, 2 * F), 0.01),
        "ffn_w_down": nrm((DEPTH, F, D), F ** -0.5),
        "final_norm_g": gain((D,)),
    }


def _fwd_reference(x, mem, mix_norm_g, pool_w, pool_scale, sb_w_qkv, sb_w_o,
              s5_a_re, s5_a_im, s5_log_dt, s5_b_re, s5_b_im, s5_c_re, s5_c_im, s5_d, s5_w_glu,
              xa_norm_g, mem_norm_g, xa_wq, xa_wkv, xa_wo,
              ffn_norm_g, ffn_w_up, ffn_conv_w, ffn_conv_b, ffn_w_down, final_norm_g):
    h = x
    for i in range(DEPTH):
        kind = i % N_MIXERS
        j = i // N_MIXERS
        hn = rms_norm(h, mix_norm_g[i])
        if kind == 0:
            t = pool_mixer(hn, pool_w[j], pool_scale[j])
        elif kind == 1:
            t = stick_breaking_attention(hn, sb_w_qkv[j], sb_w_o[j])
        else:
            t = s5_mixer(hn, s5_a_re[j], s5_a_im[j], s5_log_dt[j], s5_b_re[j], s5_b_im[j],
                         s5_c_re[j], s5_c_im[j], s5_d[j], s5_w_glu[j])
        h = h + t.astype(h.dtype)
        m = memory_cross_attention(rms_norm(h, xa_norm_g[i]), rms_norm(mem, mem_norm_g[i]),
                                   xa_wq[i], xa_wkv[i], xa_wo[i])
        h = h + m.astype(h.dtype)
        f = conv_glu_ffn(rms_norm(h, ffn_norm_g[i]), ffn_w_up[i], ffn_conv_w[i], ffn_conv_b[i], ffn_w_down[i])
        h = h + f.astype(h.dtype)
    return rms_norm(h, final_norm_g)


import jax as _jax
import jax.numpy as _jnp

TWIN_FORMAT = 'train_step'
FWD_PARAMS = ['x', 'mem', 'mix_norm_g', 'pool_w', 'pool_scale', 'sb_w_qkv', 'sb_w_o', 's5_a_re', 's5_a_im', 's5_log_dt', 's5_b_re', 's5_b_im', 's5_c_re', 's5_c_im', 's5_d', 's5_w_glu', 'xa_norm_g', 'mem_norm_g', 'xa_wq', 'xa_wkv', 'xa_wo', 'ffn_norm_g', 'ffn_w_up', 'ffn_conv_w', 'ffn_conv_b', 'ffn_w_down', 'final_norm_g']
TWIN_WEIGHTS = ['mix_norm_g', 'pool_w', 'pool_scale', 'sb_w_qkv', 'sb_w_o', 's5_a_re', 's5_a_im', 's5_log_dt', 's5_b_re', 's5_b_im', 's5_c_re', 's5_c_im', 's5_d', 's5_w_glu', 'xa_norm_g', 'mem_norm_g', 'xa_wq', 'xa_wkv', 'xa_wo', 'ffn_norm_g', 'ffn_w_up', 'ffn_conv_w', 'ffn_conv_b', 'ffn_w_down', 'final_norm_g']
TWIN_DIFF_INPUT = 'x'
TWIN_INPUTS = ['x', 'mem', 'mix_norm_g', 'pool_w', 'pool_scale', 'sb_w_qkv', 'sb_w_o', 's5_a_re', 's5_a_im', 's5_log_dt', 's5_b_re', 's5_b_im', 's5_c_re', 's5_c_im', 's5_d', 's5_w_glu', 'xa_norm_g', 'mem_norm_g', 'xa_wq', 'xa_wkv', 'xa_wo', 'ffn_norm_g', 'ffn_w_up', 'ffn_conv_w', 'ffn_conv_b', 'ffn_w_down', 'final_norm_g', 'loss_target', 'm_mix_norm_g', 'm_pool_w', 'm_pool_scale', 'm_sb_w_qkv', 'm_sb_w_o', 'm_s5_a_re', 'm_s5_a_im', 'm_s5_log_dt', 'm_s5_b_re', 'm_s5_b_im', 'm_s5_c_re', 'm_s5_c_im', 'm_s5_d', 'm_s5_w_glu', 'm_xa_norm_g', 'm_mem_norm_g', 'm_xa_wq', 'm_xa_wkv', 'm_xa_wo', 'm_ffn_norm_g', 'm_ffn_w_up', 'm_ffn_conv_w', 'm_ffn_conv_b', 'm_ffn_w_down', 'm_final_norm_g', 'v_mix_norm_g', 'v_pool_w', 'v_pool_scale', 'v_sb_w_qkv', 'v_sb_w_o', 'v_s5_a_re', 'v_s5_a_im', 'v_s5_log_dt', 'v_s5_b_re', 'v_s5_b_im', 'v_s5_c_re', 'v_s5_c_im', 'v_s5_d', 'v_s5_w_glu', 'v_xa_norm_g', 'v_mem_norm_g', 'v_xa_wq', 'v_xa_wkv', 'v_xa_wo', 'v_ffn_norm_g', 'v_ffn_w_up', 'v_ffn_conv_w', 'v_ffn_conv_b', 'v_ffn_w_down', 'v_final_norm_g']
TWIN_OUTPUTS = ['loss', 'grad_x', 'grad_mix_norm_g', 'grad_pool_w', 'grad_pool_scale', 'grad_sb_w_qkv', 'grad_sb_w_o', 'grad_s5_a_re', 'grad_s5_a_im', 'grad_s5_log_dt', 'grad_s5_b_re', 'grad_s5_b_im', 'grad_s5_c_re', 'grad_s5_c_im', 'grad_s5_d', 'grad_s5_w_glu', 'grad_xa_norm_g', 'grad_mem_norm_g', 'grad_xa_wq', 'grad_xa_wkv', 'grad_xa_wo', 'grad_ffn_norm_g', 'grad_ffn_w_up', 'grad_ffn_conv_w', 'grad_ffn_conv_b', 'grad_ffn_w_down', 'grad_final_norm_g', 'delta_mix_norm_g', 'delta_pool_w', 'delta_pool_scale', 'delta_sb_w_qkv', 'delta_sb_w_o', 'delta_s5_a_re', 'delta_s5_a_im', 'delta_s5_log_dt', 'delta_s5_b_re', 'delta_s5_b_im', 'delta_s5_c_re', 'delta_s5_c_im', 'delta_s5_d', 'delta_s5_w_glu', 'delta_xa_norm_g', 'delta_mem_norm_g', 'delta_xa_wq', 'delta_xa_wkv', 'delta_xa_wo', 'delta_ffn_norm_g', 'delta_ffn_w_up', 'delta_ffn_conv_w', 'delta_ffn_conv_b', 'delta_ffn_w_down', 'delta_final_norm_g', 'new_m_mix_norm_g', 'new_m_pool_w', 'new_m_pool_scale', 'new_m_sb_w_qkv', 'new_m_sb_w_o', 'new_m_s5_a_re', 'new_m_s5_a_im', 'new_m_s5_log_dt', 'new_m_s5_b_re', 'new_m_s5_b_im', 'new_m_s5_c_re', 'new_m_s5_c_im', 'new_m_s5_d', 'new_m_s5_w_glu', 'new_m_xa_norm_g', 'new_m_mem_norm_g', 'new_m_xa_wq', 'new_m_xa_wkv', 'new_m_xa_wo', 'new_m_ffn_norm_g', 'new_m_ffn_w_up', 'new_m_ffn_conv_w', 'new_m_ffn_conv_b', 'new_m_ffn_w_down', 'new_m_final_norm_g', 'new_v_mix_norm_g', 'new_v_pool_w', 'new_v_pool_scale', 'new_v_sb_w_qkv', 'new_v_sb_w_o', 'new_v_s5_a_re', 'new_v_s5_a_im', 'new_v_s5_log_dt', 'new_v_s5_b_re', 'new_v_s5_b_im', 'new_v_s5_c_re', 'new_v_s5_c_im', 'new_v_s5_d', 'new_v_s5_w_glu', 'new_v_xa_norm_g', 'new_v_mem_norm_g', 'new_v_xa_wq', 'new_v_xa_wkv', 'new_v_xa_wo', 'new_v_ffn_norm_g', 'new_v_ffn_w_up', 'new_v_ffn_conv_w', 'new_v_ffn_conv_b', 'new_v_ffn_w_down', 'new_v_final_norm_g']
TWIN_LEAF_KINDS = {'loss': 'loss', 'grad_x': 'grad_x', 'grad_mix_norm_g': 'grad_w', 'grad_pool_w': 'grad_w', 'grad_pool_scale': 'grad_w', 'grad_sb_w_qkv': 'grad_w', 'grad_sb_w_o': 'grad_w', 'grad_s5_a_re': 'grad_w', 'grad_s5_a_im': 'grad_w', 'grad_s5_log_dt': 'grad_w', 'grad_s5_b_re': 'grad_w', 'grad_s5_b_im': 'grad_w', 'grad_s5_c_re': 'grad_w', 'grad_s5_c_im': 'grad_w', 'grad_s5_d': 'grad_w', 'grad_s5_w_glu': 'grad_w', 'grad_xa_norm_g': 'grad_w', 'grad_mem_norm_g': 'grad_w', 'grad_xa_wq': 'grad_w', 'grad_xa_wkv': 'grad_w', 'grad_xa_wo': 'grad_w', 'grad_ffn_norm_g': 'grad_w', 'grad_ffn_w_up': 'grad_w', 'grad_ffn_conv_w': 'grad_w', 'grad_ffn_conv_b': 'grad_w', 'grad_ffn_w_down': 'grad_w', 'grad_final_norm_g': 'grad_w', 'delta_mix_norm_g': 'delta_w', 'delta_pool_w': 'delta_w', 'delta_pool_scale': 'delta_w', 'delta_sb_w_qkv': 'delta_w', 'delta_sb_w_o': 'delta_w', 'delta_s5_a_re': 'delta_w', 'delta_s5_a_im': 'delta_w', 'delta_s5_log_dt': 'delta_w', 'delta_s5_b_re': 'delta_w', 'delta_s5_b_im': 'delta_w', 'delta_s5_c_re': 'delta_w', 'delta_s5_c_im': 'delta_w', 'delta_s5_d': 'delta_w', 'delta_s5_w_glu': 'delta_w', 'delta_xa_norm_g': 'delta_w', 'delta_mem_norm_g': 'delta_w', 'delta_xa_wq': 'delta_w', 'delta_xa_wkv': 'delta_w', 'delta_xa_wo': 'delta_w', 'delta_ffn_norm_g': 'delta_w', 'delta_ffn_w_up': 'delta_w', 'delta_ffn_conv_w': 'delta_w', 'delta_ffn_conv_b': 'delta_w', 'delta_ffn_w_down': 'delta_w', 'delta_final_norm_g': 'delta_w', 'new_m_mix_norm_g': 'new_m', 'new_m_pool_w': 'new_m', 'new_m_pool_scale': 'new_m', 'new_m_sb_w_qkv': 'new_m', 'new_m_sb_w_o': 'new_m', 'new_m_s5_a_re': 'new_m', 'new_m_s5_a_im': 'new_m', 'new_m_s5_log_dt': 'new_m', 'new_m_s5_b_re': 'new_m', 'new_m_s5_b_im': 'new_m', 'new_m_s5_c_re': 'new_m', 'new_m_s5_c_im': 'new_m', 'new_m_s5_d': 'new_m', 'new_m_s5_w_glu': 'new_m', 'new_m_xa_norm_g': 'new_m', 'new_m_mem_norm_g': 'new_m', 'new_m_xa_wq': 'new_m', 'new_m_xa_wkv': 'new_m', 'new_m_xa_wo': 'new_m', 'new_m_ffn_norm_g': 'new_m', 'new_m_ffn_w_up': 'new_m', 'new_m_ffn_conv_w': 'new_m', 'new_m_ffn_conv_b': 'new_m', 'new_m_ffn_w_down': 'new_m', 'new_m_final_norm_g': 'new_m', 'new_v_mix_norm_g': 'new_v', 'new_v_pool_w': 'new_v', 'new_v_pool_scale': 'new_v', 'new_v_sb_w_qkv': 'new_v', 'new_v_sb_w_o': 'new_v', 'new_v_s5_a_re': 'new_v', 'new_v_s5_a_im': 'new_v', 'new_v_s5_log_dt': 'new_v', 'new_v_s5_b_re': 'new_v', 'new_v_s5_b_im': 'new_v', 'new_v_s5_c_re': 'new_v', 'new_v_s5_c_im': 'new_v', 'new_v_s5_d': 'new_v', 'new_v_s5_w_glu': 'new_v', 'new_v_xa_norm_g': 'new_v', 'new_v_mem_norm_g': 'new_v', 'new_v_xa_wq': 'new_v', 'new_v_xa_wkv': 'new_v', 'new_v_xa_wo': 'new_v', 'new_v_ffn_norm_g': 'new_v', 'new_v_ffn_w_up': 'new_v', 'new_v_ffn_conv_w': 'new_v', 'new_v_ffn_conv_b': 'new_v', 'new_v_ffn_w_down': 'new_v', 'new_v_final_norm_g': 'new_v'}


def _forward(args):
    return _fwd_reference(*[args[k] for k in FWD_PARAMS])


def _output_shape():
    def fwd():
        inp = _fwd_setup_inputs(0)
        return _fwd_reference(*[inp[k] for k in FWD_PARAMS])
    out = _jax.eval_shape(fwd)
    return out.shape, out.dtype

N_MICROBATCH = 1
ADAM_LR = 0.001
ADAM_B1 = 0.9
ADAM_B2 = 0.999
ADAM_EPS = 1e-08
ADAM_WD = 0.01
ADAM_STEP = 10
PER_EXAMPLE_BATCH_AXIS = {'x': 0, 'mem': 0, 'loss_target': 0}
SHARED_INPUTS = []
_WEIGHT_DTYPES = {'mix_norm_g': _jnp.float32, 'pool_w': _jnp.float32, 'pool_scale': _jnp.float32, 'sb_w_qkv': _jnp.float32, 'sb_w_o': _jnp.float32, 's5_a_re': _jnp.float32, 's5_a_im': _jnp.float32, 's5_log_dt': _jnp.float32, 's5_b_re': _jnp.float32, 's5_b_im': _jnp.float32, 's5_c_re': _jnp.float32, 's5_c_im': _jnp.float32, 's5_d': _jnp.float32, 's5_w_glu': _jnp.float32, 'xa_norm_g': _jnp.float32, 'mem_norm_g': _jnp.float32, 'xa_wq': _jnp.float32, 'xa_wkv': _jnp.float32, 'xa_wo': _jnp.float32, 'ffn_norm_g': _jnp.float32, 'ffn_w_up': _jnp.float32, 'ffn_conv_w': _jnp.float32, 'ffn_conv_b': _jnp.float32, 'ffn_w_down': _jnp.float32, 'final_norm_g': _jnp.float32}
MOMENT_SCALE = {'mix_norm_g': 1.041783e-01, 'pool_w': 1.249712e-01, 'pool_scale': 2.107837e-01, 'sb_w_qkv': 5.734856e-02, 'sb_w_o': 8.215616e-02, 's5_a_re': 2.074931e-03, 's5_a_im': 2.025999e-03, 's5_log_dt': 1.252016e+00, 's5_b_re': 1.357880e-03, 's5_b_im': 1.348365e-03, 's5_c_re': 2.613151e-03, 's5_c_im': 2.614845e-03, 's5_d': 4.136638e-02, 's5_w_glu': 2.896047e-02, 'xa_norm_g': 1.367651e-02, 'mem_norm_g': 2.108472e-02, 'xa_wq': 1.368818e-02, 'xa_wkv': 1.379375e-02, 'xa_wo': 1.392681e-02, 'ffn_norm_g': 1.019417e-01, 'ffn_w_up': 4.273928e-02, 'ffn_conv_w': 4.270635e-02, 'ffn_conv_b': 4.143747e-02, 'ffn_w_down': 6.984250e-02, 'final_norm_g': 3.211818e+01}


def _to_microbatches(a, axis):
    t = _jnp.moveaxis(a, axis, 0)
    t = t.reshape((N_MICROBATCH, t.shape[0] // N_MICROBATCH) + t.shape[1:])
    return _jnp.moveaxis(t, 1, axis + 1)


def setup_inputs(seed: int = 0) -> dict:
    inp = _fwd_setup_inputs(seed)
    key = _jax.random.fold_in(_jax.random.key(seed), 7919)
    shape, _ = _output_shape()
    out = dict(inp)
    out["loss_target"] = _jax.random.normal(_jax.random.fold_in(key, 0), shape, _jnp.float32)
    for i, name in enumerate(TWIN_WEIGHTS):
        w = inp[name].astype(_jnp.float32)
        if MOMENT_SCALE is None:
            s = _jnp.sqrt(_jnp.mean(_jnp.square(w)) + 1e-30)
        else:
            s = MOMENT_SCALE[name]
        km, kv = _jax.random.split(_jax.random.fold_in(key, i + 1))
        out[name] = w
        out["m_" + name] = s * _jax.random.normal(km, w.shape, _jnp.float32)
        out["v_" + name] = (s * s) * _jax.random.uniform(kv, w.shape, _jnp.float32, 0.5, 1.5)
    if N_MICROBATCH > 1:
        for name, axis in PER_EXAMPLE_BATCH_AXIS.items():
            out[name] = _to_microbatches(out[name], axis)
    return {'x': out['x'], 'mem': out['mem'], 'mix_norm_g': out['mix_norm_g'], 'pool_w': out['pool_w'], 'pool_scale': out['pool_scale'], 'sb_w_qkv': out['sb_w_qkv'], 'sb_w_o': out['sb_w_o'], 's5_a_re': out['s5_a_re'], 's5_a_im': out['s5_a_im'], 's5_log_dt': out['s5_log_dt'], 's5_b_re': out['s5_b_re'], 's5_b_im': out['s5_b_im'], 's5_c_re': out['s5_c_re'], 's5_c_im': out['s5_c_im'], 's5_d': out['s5_d'], 's5_w_glu': out['s5_w_glu'], 'xa_norm_g': out['xa_norm_g'], 'mem_norm_g': out['mem_norm_g'], 'xa_wq': out['xa_wq'], 'xa_wkv': out['xa_wkv'], 'xa_wo': out['xa_wo'], 'ffn_norm_g': out['ffn_norm_g'], 'ffn_w_up': out['ffn_w_up'], 'ffn_conv_w': out['ffn_conv_w'], 'ffn_conv_b': out['ffn_conv_b'], 'ffn_w_down': out['ffn_w_down'], 'final_norm_g': out['final_norm_g'], 'loss_target': out['loss_target'], 'm_mix_norm_g': out['m_mix_norm_g'], 'm_pool_w': out['m_pool_w'], 'm_pool_scale': out['m_pool_scale'], 'm_sb_w_qkv': out['m_sb_w_qkv'], 'm_sb_w_o': out['m_sb_w_o'], 'm_s5_a_re': out['m_s5_a_re'], 'm_s5_a_im': out['m_s5_a_im'], 'm_s5_log_dt': out['m_s5_log_dt'], 'm_s5_b_re': out['m_s5_b_re'], 'm_s5_b_im': out['m_s5_b_im'], 'm_s5_c_re': out['m_s5_c_re'], 'm_s5_c_im': out['m_s5_c_im'], 'm_s5_d': out['m_s5_d'], 'm_s5_w_glu': out['m_s5_w_glu'], 'm_xa_norm_g': out['m_xa_norm_g'], 'm_mem_norm_g': out['m_mem_norm_g'], 'm_xa_wq': out['m_xa_wq'], 'm_xa_wkv': out['m_xa_wkv'], 'm_xa_wo': out['m_xa_wo'], 'm_ffn_norm_g': out['m_ffn_norm_g'], 'm_ffn_w_up': out['m_ffn_w_up'], 'm_ffn_conv_w': out['m_ffn_conv_w'], 'm_ffn_conv_b': out['m_ffn_conv_b'], 'm_ffn_w_down': out['m_ffn_w_down'], 'm_final_norm_g': out['m_final_norm_g'], 'v_mix_norm_g': out['v_mix_norm_g'], 'v_pool_w': out['v_pool_w'], 'v_pool_scale': out['v_pool_scale'], 'v_sb_w_qkv': out['v_sb_w_qkv'], 'v_sb_w_o': out['v_sb_w_o'], 'v_s5_a_re': out['v_s5_a_re'], 'v_s5_a_im': out['v_s5_a_im'], 'v_s5_log_dt': out['v_s5_log_dt'], 'v_s5_b_re': out['v_s5_b_re'], 'v_s5_b_im': out['v_s5_b_im'], 'v_s5_c_re': out['v_s5_c_re'], 'v_s5_c_im': out['v_s5_c_im'], 'v_s5_d': out['v_s5_d'], 'v_s5_w_glu': out['v_s5_w_glu'], 'v_xa_norm_g': out['v_xa_norm_g'], 'v_mem_norm_g': out['v_mem_norm_g'], 'v_xa_wq': out['v_xa_wq'], 'v_xa_wkv': out['v_xa_wkv'], 'v_xa_wo': out['v_xa_wo'], 'v_ffn_norm_g': out['v_ffn_norm_g'], 'v_ffn_w_up': out['v_ffn_w_up'], 'v_ffn_conv_w': out['v_ffn_conv_w'], 'v_ffn_conv_b': out['v_ffn_conv_b'], 'v_ffn_w_down': out['v_ffn_w_down'], 'v_final_norm_g': out['v_final_norm_g']}


def _loss(weights, diff, rest, loss_target):
    with _jax.named_scope("forward"):
        args = {**rest, TWIN_DIFF_INPUT: diff, **{k: w.astype(_WEIGHT_DTYPES[k]) for k, w in weights.items()}}
        y = _forward(args)
    with _jax.named_scope("loss_head"):
        err = _jnp.square(y.astype(_jnp.float32) - loss_target)
        return 0.5 * _jnp.sum(_jnp.mean(err, axis=-1)) if err.ndim else 0.5 * err


def _adamw(w, g, m, v):
    m = ADAM_B1 * m + (1.0 - ADAM_B1) * g
    v = ADAM_B2 * v + (1.0 - ADAM_B2) * _jnp.square(g)
    m_hat = m / (1.0 - ADAM_B1 ** ADAM_STEP)
    v_hat = v / (1.0 - ADAM_B2 ** ADAM_STEP)
    delta = -ADAM_LR * (m_hat / (_jnp.sqrt(v_hat) + ADAM_EPS) + ADAM_WD * w)
    return delta, m, v


def reference(x, mem, mix_norm_g, pool_w, pool_scale, sb_w_qkv, sb_w_o, s5_a_re, s5_a_im, s5_log_dt, s5_b_re, s5_b_im, s5_c_re, s5_c_im, s5_d, s5_w_glu, xa_norm_g, mem_norm_g, xa_wq, xa_wkv, xa_wo, ffn_norm_g, ffn_w_up, ffn_conv_w, ffn_conv_b, ffn_w_down, final_norm_g, loss_target, m_mix_norm_g, m_pool_w, m_pool_scale, m_sb_w_qkv, m_sb_w_o, m_s5_a_re, m_s5_a_im, m_s5_log_dt, m_s5_b_re, m_s5_b_im, m_s5_c_re, m_s5_c_im, m_s5_d, m_s5_w_glu, m_xa_norm_g, m_mem_norm_g, m_xa_wq, m_xa_wkv, m_xa_wo, m_ffn_norm_g, m_ffn_w_up, m_ffn_conv_w, m_ffn_conv_b, m_ffn_w_down, m_final_norm_g, v_mix_norm_g, v_pool_w, v_pool_scale, v_sb_w_qkv, v_sb_w_o, v_s5_a_re, v_s5_a_im, v_s5_log_dt, v_s5_b_re, v_s5_b_im, v_s5_c_re, v_s5_c_im, v_s5_d, v_s5_w_glu, v_xa_norm_g, v_mem_norm_g, v_xa_wq, v_xa_wkv, v_xa_wo, v_ffn_norm_g, v_ffn_w_up, v_ffn_conv_w, v_ffn_conv_b, v_ffn_w_down, v_final_norm_g):
    given = dict(x=x, mem=mem, mix_norm_g=mix_norm_g, pool_w=pool_w, pool_scale=pool_scale, sb_w_qkv=sb_w_qkv, sb_w_o=sb_w_o, s5_a_re=s5_a_re, s5_a_im=s5_a_im, s5_log_dt=s5_log_dt, s5_b_re=s5_b_re, s5_b_im=s5_b_im, s5_c_re=s5_c_re, s5_c_im=s5_c_im, s5_d=s5_d, s5_w_glu=s5_w_glu, xa_norm_g=xa_norm_g, mem_norm_g=mem_norm_g, xa_wq=xa_wq, xa_wkv=xa_wkv, xa_wo=xa_wo, ffn_norm_g=ffn_norm_g, ffn_w_up=ffn_w_up, ffn_conv_w=ffn_conv_w, ffn_conv_b=ffn_conv_b, ffn_w_down=ffn_w_down, final_norm_g=final_norm_g, loss_target=loss_target, m_mix_norm_g=m_mix_norm_g, m_pool_w=m_pool_w, m_pool_scale=m_pool_scale, m_sb_w_qkv=m_sb_w_qkv, m_sb_w_o=m_sb_w_o, m_s5_a_re=m_s5_a_re, m_s5_a_im=m_s5_a_im, m_s5_log_dt=m_s5_log_dt, m_s5_b_re=m_s5_b_re, m_s5_b_im=m_s5_b_im, m_s5_c_re=m_s5_c_re, m_s5_c_im=m_s5_c_im, m_s5_d=m_s5_d, m_s5_w_glu=m_s5_w_glu, m_xa_norm_g=m_xa_norm_g, m_mem_norm_g=m_mem_norm_g, m_xa_wq=m_xa_wq, m_xa_wkv=m_xa_wkv, m_xa_wo=m_xa_wo, m_ffn_norm_g=m_ffn_norm_g, m_ffn_w_up=m_ffn_w_up, m_ffn_conv_w=m_ffn_conv_w, m_ffn_conv_b=m_ffn_conv_b, m_ffn_w_down=m_ffn_w_down, m_final_norm_g=m_final_norm_g, v_mix_norm_g=v_mix_norm_g, v_pool_w=v_pool_w, v_pool_scale=v_pool_scale, v_sb_w_qkv=v_sb_w_qkv, v_sb_w_o=v_sb_w_o, v_s5_a_re=v_s5_a_re, v_s5_a_im=v_s5_a_im, v_s5_log_dt=v_s5_log_dt, v_s5_b_re=v_s5_b_re, v_s5_b_im=v_s5_b_im, v_s5_c_re=v_s5_c_re, v_s5_c_im=v_s5_c_im, v_s5_d=v_s5_d, v_s5_w_glu=v_s5_w_glu, v_xa_norm_g=v_xa_norm_g, v_mem_norm_g=v_mem_norm_g, v_xa_wq=v_xa_wq, v_xa_wkv=v_xa_wkv, v_xa_wo=v_xa_wo, v_ffn_norm_g=v_ffn_norm_g, v_ffn_w_up=v_ffn_w_up, v_ffn_conv_w=v_ffn_conv_w, v_ffn_conv_b=v_ffn_conv_b, v_ffn_w_down=v_ffn_w_down, v_final_norm_g=v_final_norm_g)
    weights = {n: given[n] for n in TWIN_WEIGHTS}
    shared = {n: given[n] for n in SHARED_INPUTS}
    per_example = {n: given[n] for n in ['x', 'mem']}
    grad_fn = _jax.value_and_grad(_loss, argnums=(0, 1))

    def one_microbatch(ex, loss_target):
        ex = dict(ex)
        diff = ex.pop(TWIN_DIFF_INPUT)
        return grad_fn(weights, diff, {**shared, **ex}, loss_target)

    if N_MICROBATCH == 1:
        loss, (grad_w, grad_x) = one_microbatch(per_example, given["loss_target"])
    else:
        def body(carry, xs):
            loss_sum, grad_sum = carry
            l_k, (gw_k, gx_k) = one_microbatch(xs[0], xs[1])
            with _jax.named_scope("update"):
                return (loss_sum + l_k, _jax.tree.map(_jnp.add, grad_sum, gw_k)), gx_k

        init = (_jnp.zeros((), _jnp.float32), _jax.tree.map(_jnp.zeros_like, weights))
        (loss, grad_w), grad_x = _jax.lax.scan(body, init, (per_example, given["loss_target"]))
    with _jax.named_scope("update"):
        delta_w, new_m, new_v = {}, {}, {}
        for n in TWIN_WEIGHTS:
            delta_w[n], new_m[n], new_v[n] = _adamw(weights[n], grad_w[n], given["m_" + n], given["v_" + n])
    return (loss, grad_x, *[grad_w[n] for n in TWIN_WEIGHTS], *[delta_w[n] for n in TWIN_WEIGHTS],
            *[new_m[n] for n in TWIN_WEIGHTS], *[new_v[n] for n in TWIN_WEIGHTS])
```

```python
import functools
import math

import jax
import jax.numpy as jnp
from jax import lax
from jax.experimental import pallas as pl
from jax.experimental.pallas import tpu as pltpu

f32 = jnp.float32
_MXU = jnp.bfloat16
_VMEM_LIMIT = 48 * 1024 * 1024
_LANES = 128
_PACK_COLS = 1024

NDEV = 8
EPS = 1e-6
POOL_WINDOWS = (2, 4, 8, 16)
SB_HEAD_DIM = 64
XA_HEADS = 4
S5_GROUP = 16
S5_BLOCK_GROUPS = _LANES // S5_GROUP
ADAM_LR, ADAM_B1, ADAM_B2, ADAM_EPS, ADAM_WD, ADAM_STEP = 0.001, 0.9, 0.999, 1e-08, 0.01, 10

WEIGHTS = ['mix_norm_g', 'pool_w', 'pool_scale', 'sb_w_qkv', 'sb_w_o', 's5_a_re', 's5_a_im', 's5_log_dt',
           's5_b_re', 's5_b_im', 's5_c_re', 's5_c_im', 's5_d', 's5_w_glu', 'xa_norm_g', 'mem_norm_g', 'xa_wq',
           'xa_wkv', 'xa_wo', 'ffn_norm_g', 'ffn_w_up', 'ffn_conv_w', 'ffn_conv_b', 'ffn_w_down', 'final_norm_g']
SHARD_AXIS = {'pool_w': 2, 'pool_scale': 1, 'sb_w_qkv': 2, 'sb_w_o': 1, 's5_d': 1, 's5_w_glu': 2, 'xa_wq': 1,
              'xa_wkv': 2, 'xa_wo': 1, 'ffn_w_up': 2, 'ffn_conv_w': 2, 'ffn_w_down': 1}
MXU_WEIGHTS = ['pool_w', 'sb_w_qkv', 'sb_w_o', 's5_w_glu', 'xa_wq', 'xa_wkv', 'xa_wo', 'ffn_w_up', 'ffn_w_down']
VEC_WEIGHTS = ['pool_scale', 's5_d', 'ffn_conv_w']
REPLICATED = [n for n in WEIGHTS if n not in SHARD_AXIS]

_NN = (((1,), (0,)), ((), ()))
_NT = (((1,), (1,)), ((), ()))
_TN = (((0,), (0,)), ((), ()))
MESH_ID = pl.DeviceIdType.MESH
ANY = pl.BlockSpec(memory_space=pl.ANY)


def _call(body, **kw):
    return pl.pallas_call(body, **kw)


def _cp(*sem):
    return pltpu.CompilerParams(dimension_semantics=sem, vmem_limit_bytes=_VMEM_LIMIT)


def _tile(n, target, mult=_LANES):
    if n <= target:
        return n
    t = (target // mult) * mult
    while t >= mult:
        if n % t == 0:
            return t
        t -= mult
    return n


def _dot(a, b, dims=_NN):
    return lax.dot_general(a, b, dims, preferred_element_type=f32)


def _split(a):
    hi = a.astype(_MXU)
    lo = (a - hi.astype(f32)).astype(_MXU)
    return hi, lo


def _dot_hilo(a, u, dims=_NN):
    hi, lo = _split(a)
    return _dot(hi, u, dims) + _dot(lo, u, dims)


def _dot3(a, b, dims=_NN):
    ah, al = _split(a)
    bh, bl = _split(b)
    return _dot(ah, bh, dims) + _dot(al, bh, dims) + _dot(ah, bl, dims)


def _sigmoid(x):
    return 1.0 / (1.0 + jnp.exp(-x))


_GELU_C = math.sqrt(2.0 / math.pi)


def _gelu(x):
    return x * (0.5 * (1.0 + jnp.tanh(_GELU_C * (x + 0.044715 * (x * x * x)))))


def _gelu_grad(x):
    t = jnp.tanh(_GELU_C * (x + 0.044715 * (x * x * x)))
    return 0.5 * (1.0 + t) + x * 0.5 * (1.0 - t * t) * _GELU_C * (1.0 + 3.0 * 0.044715 * x * x)


def _shift_down(x, k, rows):
    return jnp.where(rows >= k, pltpu.roll(x, k, 0), 0.0)


def _shift_up(x, k, rows, n):
    return jnp.where(rows < n - k, pltpu.roll(x, n - k, 0), 0.0)


def _mm(a, b, *, ta=False, tb=False, out_dtype=f32):
    M, K = (a.shape[1], a.shape[0]) if ta else a.shape
    N = b.shape[0] if tb else b.shape[1]
    tm, tn, tk = _tile(M, 512), _tile(N, 1536), _tile(K, 1408)
    nk = K // tk
    a_spec = pl.BlockSpec((tk, tm), lambda i, j, k: (k, i)) if ta else pl.BlockSpec((tm, tk), lambda i, j, k: (i, k))
    b_spec = pl.BlockSpec((tn, tk), lambda i, j, k: (j, k)) if tb else pl.BlockSpec((tk, tn), lambda i, j, k: (k, j))
    dims = (((0 if ta else 1,), (1 if tb else 0,)), ((), ()))

    def body(a_ref, b_ref, o_ref, acc_ref):
        k = pl.program_id(2)

        @pl.when(k == 0)
        def _():
            acc_ref[...] = jnp.zeros_like(acc_ref)

        acc_ref[...] += _dot(a_ref[...].astype(_MXU), b_ref[...].astype(_MXU), dims)

        @pl.when(k == nk - 1)
        def _():
            o_ref[...] = acc_ref[...].astype(out_dtype)

    return _call(
        body, name=f"mm_{'t' if ta else 'n'}{'t' if tb else 'n'}_{M}x{K}x{N}",
        grid=(M // tm, N // tn, nk), in_specs=[a_spec, b_spec],
        out_specs=pl.BlockSpec((tm, tn), lambda i, j, k: (i, j)),
        out_shape=jax.ShapeDtypeStruct((M, N), out_dtype),
        scratch_shapes=[pltpu.VMEM((tm, tn), f32)],
        compiler_params=_cp("parallel", "parallel", "arbitrary"))(a, b)


def _rms_fwd(x, g, delta=None, *, out_dtype=f32):
    S, D = x.shape
    ts = _tile(S, 512, 8)
    row = pl.BlockSpec((ts, D), lambda i: (i, 0))
    vec = pl.BlockSpec((1, D), lambda i: (0, 0))
    g2 = g.reshape(1, D)

    def norm(xv, g_ref):
        r = lax.rsqrt(jnp.mean(xv * xv, axis=-1, keepdims=True) + EPS)
        return ((xv * r) * g_ref[...]).astype(out_dtype)

    if delta is None:
        def body(x_ref, g_ref, y_ref):
            y_ref[...] = norm(x_ref[...], g_ref)

        return _call(body, name=f"rms_fwd_{S}", grid=(S // ts,), in_specs=[row, vec], out_specs=row,
                     out_shape=jax.ShapeDtypeStruct((S, D), out_dtype), compiler_params=_cp("parallel"))(x, g2)

    def body(x_ref, d_ref, g_ref, s_ref, y_ref):
        xv = x_ref[...] + d_ref[...]
        s_ref[...] = xv
        y_ref[...] = norm(xv, g_ref)

    return _call(body, name=f"add_rms_fwd_{S}", grid=(S // ts,), in_specs=[row, row, vec], out_specs=[row, row],
                 out_shape=[jax.ShapeDtypeStruct((S, D), f32), jax.ShapeDtypeStruct((S, D), out_dtype)],
                 compiler_params=_cp("parallel"))(x, delta, g2)


def _rms_bwd(x, g, dy, dres=None):
    S, D = x.shape
    ts = _tile(S, 512, 8)
    row = pl.BlockSpec((ts, D), lambda i: (i, 0))
    vec = pl.BlockSpec((1, D), lambda i: (0, 0))
    has_res = dres is not None

    def body(*refs):
        if has_res:
            x_ref, g_ref, dy_ref, dr_ref, dx_ref, dg_ref = refs
        else:
            x_ref, g_ref, dy_ref, dx_ref, dg_ref = refs
        xv = x_ref[...]
        r = lax.rsqrt(jnp.mean(xv * xv, axis=-1, keepdims=True) + EPS)
        xh = xv * r
        dyv = dy_ref[...].astype(f32)

        @pl.when(pl.program_id(0) == 0)
        def _():
            dg_ref[...] = jnp.zeros_like(dg_ref)

        dg_ref[...] += jnp.sum(dyv * xh, axis=0, keepdims=True)
        dxh = dyv * g_ref[...]
        dx = r * (dxh - xh * jnp.mean(dxh * xh, axis=-1, keepdims=True))
        if has_res:
            dx = dx + dr_ref[...]
        dx_ref[...] = dx

    ins = [x, g.reshape(1, D), dy] + ([dres] if has_res else [])
    dx, dg = _call(body, name=f"rms_bwd_{S}_{int(has_res)}", grid=(S // ts,),
                   in_specs=[row, vec, row] + ([row] if has_res else []), out_specs=[row, vec],
                   out_shape=[jax.ShapeDtypeStruct((S, D), f32), jax.ShapeDtypeStruct((1, D), f32)],
                   compiler_params=_cp("arbitrary"))(*ins)
    return dx, dg.reshape(D)


def _pool_windows_sum(x, win, rows):
    s, k = x, 1
    while k < win:
        s = s + _shift_down(s, k, rows)
        k *= 2
    return s


def _pool_windows_sum_up(x, win, rows, n):
    s, k = x, 1
    while k < win:
        s = s + _shift_up(s, k, rows, n)
        k *= 2
    return s


def _pool_fwd(hn):
    S, D = hn.shape
    cg = D // len(POOL_WINDOWS)
    tc = _tile(cg, 128)
    nb = cg // tc
    blk = pl.BlockSpec((S, tc), lambda gi, j: (0, gi * nb + j))

    def body(x_ref, p_ref):
        gi = pl.program_id(0)
        rows = lax.broadcasted_iota(jnp.int32, (S, 1), 0)
        cnt = (rows + 1).astype(f32)
        for k, win in enumerate(POOL_WINDOWS):
            @pl.when(gi == k)
            def _(win=win):
                x = x_ref[...]
                s = _pool_windows_sum(x, win, rows)
                p_ref[...] = (s / jnp.minimum(cnt, float(win)) - x).astype(p_ref.dtype)

    return _call(body, name=f"pool_fwd_{S}", grid=(len(POOL_WINDOWS), nb), in_specs=[blk], out_specs=blk,
                 out_shape=jax.ShapeDtypeStruct((S, D), _MXU), compiler_params=_cp("parallel", "parallel"))(hn)


def _pool_bwd(dp):
    S, D = dp.shape
    cg = D // len(POOL_WINDOWS)
    tc = _tile(cg, 128)
    nb = cg // tc
    blk = pl.BlockSpec((S, tc), lambda gi, j: (0, gi * nb + j))

    def body(dp_ref, dx_ref):
        gi = pl.program_id(0)
        rows = lax.broadcasted_iota(jnp.int32, (S, 1), 0)
        cnt = (rows + 1).astype(f32)
        for k, win in enumerate(POOL_WINDOWS):
            @pl.when(gi == k)
            def _(win=win):
                d = dp_ref[...]
                e = d / jnp.minimum(cnt, float(win))
                dx_ref[...] = _pool_windows_sum_up(e, win, rows, S) - d

    return _call(body, name=f"pool_bwd_{S}", grid=(len(POOL_WINDOWS), nb), in_specs=[blk], out_specs=blk,
                 out_shape=jax.ShapeDtypeStruct((S, D), f32), compiler_params=_cp("parallel", "parallel"))(dp)


def _pool_mix_fwd(p, w, scale):
    S, D = p.shape
    G, cg, _ = w.shape
    ts = _tile(S, 1024, 8)

    def body(p_ref, w_ref, s_ref, y_ref):
        y_ref[...] = _dot(p_ref[...], w_ref[0]) * s_ref[...]

    return _call(body, name=f"pool_mix_fwd_{S}", grid=(S // ts, G),
                 in_specs=[pl.BlockSpec((ts, cg), lambda i, g: (i, g)), pl.BlockSpec((1, cg, cg), lambda i, g: (g, 0, 0)),
                           pl.BlockSpec((1, cg), lambda i, g: (0, g))],
                 out_specs=pl.BlockSpec((ts, cg), lambda i, g: (i, g)),
                 out_shape=jax.ShapeDtypeStruct((S, D), f32), compiler_params=_cp("parallel", "parallel"))(
                     p, w, scale.reshape(1, D))


def _pool_mix_bwd(p, w, scale, dy):
    S, D = p.shape
    G, cg, _ = w.shape
    ts = _tile(S, 1024, 8)

    def body(p_ref, w_ref, s_ref, dy_ref, dp_ref, dw_ref, ds_ref):
        @pl.when(pl.program_id(1) == 0)
        def _():
            dw_ref[...] = jnp.zeros_like(dw_ref)
            ds_ref[...] = jnp.zeros_like(ds_ref)

        pv, wv, dyv = p_ref[...], w_ref[0], dy_ref[...]
        ypre = _dot(pv, wv)
        ds_ref[...] += jnp.sum(dyv * ypre, axis=0, keepdims=True)
        dyp = (dyv * s_ref[...]).astype(_MXU)
        dp_ref[...] = _dot(dyp, wv, _NT)
        dw_ref[0] += _dot(pv, dyp, _TN)

    dp, dw, ds = _call(
        body, name=f"pool_mix_bwd_{S}", grid=(G, S // ts),
        in_specs=[pl.BlockSpec((ts, cg), lambda g, i: (i, g)), pl.BlockSpec((1, cg, cg), lambda g, i: (g, 0, 0)),
                  pl.BlockSpec((1, cg), lambda g, i: (0, g)), pl.BlockSpec((ts, cg), lambda g, i: (i, g))],
        out_specs=[pl.BlockSpec((ts, cg), lambda g, i: (i, g)), pl.BlockSpec((1, cg, cg), lambda g, i: (g, 0, 0)),
                   pl.BlockSpec((1, cg), lambda g, i: (0, g))],
        out_shape=[jax.ShapeDtypeStruct((S, D), f32), jax.ShapeDtypeStruct((G, cg, cg), f32),
                   jax.ShapeDtypeStruct((1, D), f32)],
        compiler_params=_cp("parallel", "arbitrary"))(p, w, scale.reshape(1, D), dy)
    return dp, dw, ds.reshape(D)


def _sb_tile(S):
    return min(256, max(128, S // 4))


def _sb_scores(qm, ks, kb, T, kk, qpos, R, U, scale):
    z = _dot(qm, ks, _NT) * scale
    mask = (kb * T + kk) < qpos
    soft = jnp.log(1.0 + jnp.exp(-jnp.abs(z)))
    lm = jnp.where(mask, jnp.minimum(-z, 0.0) - soft, 0.0)
    lb = jnp.minimum(z, 0.0) - soft
    c = _dot_hilo(lm, U)
    a = jnp.where(mask, jnp.exp(lb + c + R), 0.0)
    return mask, lm, lb, a


def _sb_consts(T):
    lane = lax.broadcasted_iota(jnp.int32, (1, _LANES), 1)
    kk = lax.broadcasted_iota(jnp.int32, (1, T), 1)
    above = lax.broadcasted_iota(jnp.int32, (T, T), 0) > lax.broadcasted_iota(jnp.int32, (T, T), 1)
    U = jnp.where(above, 1.0, 0.0).astype(_MXU)
    return lane, kk, U


def _sb_fwd(qkv):
    S, D3 = qkv.shape
    D = D3 // 3
    HP = D // _LANES
    T = _sb_tile(S)
    scale = SB_HEAD_DIM ** -0.5

    def body(q_ref, k_ref, v_ref, o_ref):
        i = pl.program_id(1)
        q = q_ref[...]
        lane, kk, U = _sb_consts(T)
        qpos = i * T + lax.broadcasted_iota(jnp.int32, (T, 1), 0)
        out = jnp.zeros((T, _LANES), f32)
        for h in range(_LANES // SB_HEAD_DIM):
            hm = (lane >= SB_HEAD_DIM * h) & (lane < SB_HEAD_DIM * (h + 1))
            qm = jnp.where(hm, q, jnp.zeros_like(q))

            def step(j, carry, qm=qm):
                R, acc = carry
                kb = i - j
                off = pl.multiple_of(kb * T, T)
                ks, vs = k_ref[pl.ds(off, T), :], v_ref[pl.ds(off, T), :]
                _, lm, _, a = _sb_scores(qm, ks, kb, T, kk, qpos, R, U, scale)
                acc = acc + _dot(a.astype(_MXU), vs)
                return R + jnp.sum(lm, axis=1, keepdims=True), acc

            _, acc = lax.fori_loop(0, i + 1, step, (jnp.zeros((T, 1), f32), jnp.zeros((T, _LANES), f32)))
            out = out + jnp.where(hm, acc, 0.0)
        o_ref[...] = out

    return _call(
        body, name=f"sb_fwd_{S}", grid=(HP, S // T),
        in_specs=[pl.BlockSpec((T, _LANES), lambda hp, i: (i, hp)), pl.BlockSpec((S, _LANES), lambda hp, i: (0, HP + hp)),
                  pl.BlockSpec((S, _LANES), lambda hp, i: (0, 2 * HP + hp))],
        out_specs=pl.BlockSpec((T, _LANES), lambda hp, i: (i, hp)),
        out_shape=jax.ShapeDtypeStruct((S, D), f32), compiler_params=_cp("parallel", "parallel"))(qkv, qkv, qkv)


def _sb_bwd(qkv, o, do):
    S, D3 = qkv.shape
    D = D3 // 3
    HP = D // _LANES
    T = _sb_tile(S)
    nq = S // T
    scale = SB_HEAD_DIM ** -0.5

    def body(q_ref, k_ref, v_ref, o_ref, do_ref, dq_ref, dk_ref, dv_ref, dk_acc, dv_acc):
        i = pl.program_id(1)

        @pl.when(i == 0)
        def _():
            dk_acc[...] = jnp.zeros_like(dk_acc)
            dv_acc[...] = jnp.zeros_like(dv_acc)

        q = q_ref[...]
        dob = do_ref[...].astype(_MXU)
        prod = dob.astype(f32) * o_ref[...]
        lane, kk, U = _sb_consts(T)
        qpos = i * T + lax.broadcasted_iota(jnp.int32, (T, 1), 0)
        dq_out = jnp.zeros((T, _LANES), f32)
        for h in range(_LANES // SB_HEAD_DIM):
            hm = (lane >= SB_HEAD_DIM * h) & (lane < SB_HEAD_DIM * (h + 1))
            qm = jnp.where(hm, q, jnp.zeros_like(q))
            dom = jnp.where(hm, dob, jnp.zeros_like(dob))
            total = jnp.sum(jnp.where(hm, prod, 0.0), axis=1, keepdims=True)

            def step(j, carry, qm=qm, dom=dom, total=total):
                R, Gs, dq = carry
                kb = i - j
                off = pl.multiple_of(kb * T, T)
                ks, vs = k_ref[pl.ds(off, T), :], v_ref[pl.ds(off, T), :]
                mask, lm, lb, a = _sb_scores(qm, ks, kb, T, kk, qpos, R, U, scale)
                ab = a.astype(_MXU)
                g = ab.astype(f32) * _dot(dom, vs, _NT)
                before = total - (g + _dot_hilo(g, U) + Gs)
                beta = jnp.exp(lb)
                dz = jnp.where(mask, g * (1.0 - beta) - before * beta, 0.0) * scale
                dzb = dz.astype(_MXU)
                dq = dq + _dot(dzb, ks)
                dk_acc[pl.ds(off, T), :] += _dot(dzb, qm, _TN)
                dv_acc[pl.ds(off, T), :] += _dot(ab, dom, _TN)
                return R + jnp.sum(lm, axis=1, keepdims=True), Gs + jnp.sum(g, axis=1, keepdims=True), dq

            zero1 = jnp.zeros((T, 1), f32)
            _, _, dq = lax.fori_loop(0, i + 1, step, (zero1, zero1, jnp.zeros((T, _LANES), f32)))
            dq_out = dq_out + jnp.where(hm, dq, 0.0)
        dq_ref[...] = dq_out.astype(dq_ref.dtype)

        @pl.when(i == nq - 1)
        def _():
            dk_ref[...] = dk_acc[...].astype(dk_ref.dtype)
            dv_ref[...] = dv_acc[...].astype(dv_ref.dtype)

    qb = pl.BlockSpec((T, _LANES), lambda hp, i: (i, hp))
    col = pl.BlockSpec((S, _LANES), lambda hp, i: (0, hp))
    out = jax.ShapeDtypeStruct((S, D), _MXU)
    return _call(
        body, name=f"sb_bwd_{S}", grid=(HP, nq),
        in_specs=[qb, pl.BlockSpec((S, _LANES), lambda hp, i: (0, HP + hp)),
                  pl.BlockSpec((S, _LANES), lambda hp, i: (0, 2 * HP + hp)), qb, qb],
        out_specs=[qb, col, col], out_shape=[out, out, out],
        scratch_shapes=[pltpu.VMEM((S, _LANES), f32), pltpu.VMEM((S, _LANES), f32)],
        compiler_params=_cp("parallel", "arbitrary"))(qkv, qkv, qkv, o, do)


def _cmul(ar, ai, br, bi):
    return ar * br - ai * bi, ar * bi + ai * br


def _s5_coef(ar, ai, ldt):
    dt = jnp.exp(ldt)
    e = jnp.exp(ar * dt)
    abr, abi = e * jnp.cos(ai * dt), e * jnp.sin(ai * dt)
    inv = 1.0 / (ar * ar + ai * ai)
    cr, ci = _cmul(abr - 1.0, abi, ar * inv, -ai * inv)
    return dt, abr, abi, inv, cr, ci


def _s5_prep_fwd(ar, ai, ldt, br, bi):
    N = ar.shape[1]
    cg = br.shape[0]

    def body(ar_ref, ai_ref, ldt_ref, br_ref, bi_ref, abr_ref, abi_ref, bbr_ref, bbi_ref):
        _, abr, abi, _, cr, ci = _s5_coef(ar_ref[...], ai_ref[...], ldt_ref[...])
        abr_ref[...], abi_ref[...] = abr, abi
        bbr_ref[...], bbi_ref[...] = _cmul(cr, ci, br_ref[...], bi_ref[...])

    v, m = jax.ShapeDtypeStruct((1, N), f32), jax.ShapeDtypeStruct((cg, N), f32)
    return _call(body, name="s5_prep_fwd", out_shape=[v, v, m, m])(ar, ai, ldt, br, bi)


def _s5_prep_bwd(ar, ai, ldt, br, bi, dabr, dabi, dbbr, dbbi):
    N = ar.shape[1]
    cg = br.shape[0]

    def body(ar_ref, ai_ref, ldt_ref, br_ref, bi_ref, dabr_ref, dabi_ref, dbbr_ref, dbbi_ref,
             dar_ref, dai_ref, dldt_ref, dbr_ref, dbi_ref):
        a_r, a_i = ar_ref[...], ai_ref[...]
        dt, abr, abi, inv, cr, ci = _s5_coef(a_r, a_i, ldt_ref[...])
        b_r, b_i, gr, gi = br_ref[...], bi_ref[...], dbbr_ref[...], dbbi_ref[...]
        dbr_ref[...], dbi_ref[...] = _cmul(cr, -ci, gr, gi)
        dcr = jnp.sum(gr * b_r + gi * b_i, axis=0, keepdims=True)
        dci = jnp.sum(gi * b_r - gr * b_i, axis=0, keepdims=True)
        ilr, ili = a_r * inv, -a_i * inv
        dwr, dwi = _cmul(ilr, -ili, dcr, dci)
        qr, qi = _cmul(cr, ci, ilr, ili)
        dl1r, dl1i = _cmul(-qr, qi, dcr, dci)
        tr, ti = dabr_ref[...] + dwr, dabi_ref[...] + dwi
        ddlr, ddli = _cmul(abr, -abi, tr, ti)
        dar_ref[...] = dl1r + ddlr * dt
        dai_ref[...] = dl1i + ddli * dt
        dldt_ref[...] = (a_r * ddlr + a_i * ddli) * dt

    v, m = jax.ShapeDtypeStruct((1, N), f32), jax.ShapeDtypeStruct((cg, N), f32)
    return _call(body, name="s5_prep_bwd", out_shape=[v, v, v, m, m])(ar, ai, ldt, br, bi, dabr, dabi, dbbr, dbbi)


def _s5_in_fwd(u, Br, Bi):
    S, D = u.shape
    nb, _, nw = Br.shape
    ts = _tile(S, 512, 8)

    def body(u_ref, br_ref, bi_ref, or_ref, oi_ref):
        uv = u_ref[...]
        or_ref[...] = _dot3(uv, br_ref[0])
        oi_ref[...] = _dot3(uv, bi_ref[0])

    ub = pl.BlockSpec((ts, _LANES), lambda i, k: (i, k))
    wb = pl.BlockSpec((1, _LANES, nw), lambda i, k: (k, 0, 0))
    ob = pl.BlockSpec((ts, nw), lambda i, k: (i, k))
    o = jax.ShapeDtypeStruct((S, nb * nw), f32)
    return _call(body, name=f"s5_in_fwd_{S}", grid=(S // ts, nb), in_specs=[ub, wb, wb], out_specs=[ob, ob],
                 out_shape=[o, o], compiler_params=_cp("parallel", "parallel"))(u, Br, Bi)


def _s5_scan_fwd(bur, bui, abr, abi):
    S, N = bur.shape
    tt, tn = _tile(S, 256, 8), _tile(N, 1024)

    def body(br_ref, bi_ref, ar_ref, ai_ref, xr_ref, xi_ref, sr, si):
        @pl.when(pl.program_id(1) == 0)
        def _():
            sr[...] = jnp.zeros_like(sr)
            si[...] = jnp.zeros_like(si)

        a_r, a_i = ar_ref[...], ai_ref[...]

        def step(t, carry):
            xr, xi = carry
            nr = a_r * xr - a_i * xi + br_ref[pl.ds(t, 1), :]
            ni = a_r * xi + a_i * xr + bi_ref[pl.ds(t, 1), :]
            xr_ref[pl.ds(t, 1), :] = nr
            xi_ref[pl.ds(t, 1), :] = ni
            return nr, ni

        xr, xi = lax.fori_loop(0, tt, step, (sr[...], si[...]))
        sr[...], si[...] = xr, xi

    blk = pl.BlockSpec((tt, tn), lambda n, i: (i, n))
    vec = pl.BlockSpec((1, tn), lambda n, i: (0, n))
    o = jax.ShapeDtypeStruct((S, N), f32)
    return _call(body, name=f"s5_scan_fwd_{S}", grid=(N // tn, S // tt), in_specs=[blk, blk, vec, vec],
                 out_specs=[blk, blk], out_shape=[o, o],
                 scratch_shapes=[pltpu.VMEM((1, tn), f32), pltpu.VMEM((1, tn), f32)],
                 compiler_params=_cp("parallel", "arbitrary"))(bur, bui, abr, abi)


def _s5_scan_bwd(dr, di, xr, xi, abr, abi):
    S, N = dr.shape
    tt, tn = _tile(S, 256, 8), _tile(N, 1024)
    nt = S // tt

    def body(dr_ref, di_ref, xr_ref, xi_ref, ar_ref, ai_ref, gr_ref, gi_ref, dar_ref, dai_ref, sr, si):
        @pl.when(pl.program_id(1) == 0)
        def _():
            sr[...] = jnp.zeros_like(sr)
            si[...] = jnp.zeros_like(si)
            dar_ref[...] = jnp.zeros_like(dar_ref)
            dai_ref[...] = jnp.zeros_like(dai_ref)

        a_r, a_i = ar_ref[...], ai_ref[...]

        def step(j, carry):
            gr, gi, accr, acci = carry
            t = tt - 1 - j
            xr_t, xi_t = xr_ref[pl.ds(t, 1), :], xi_ref[pl.ds(t, 1), :]
            accr = accr + gr * xr_t + gi * xi_t
            acci = acci + gi * xr_t - gr * xi_t
            nr = dr_ref[pl.ds(t, 1), :] + a_r * gr + a_i * gi
            ni = di_ref[pl.ds(t, 1), :] + a_r * gi - a_i * gr
            gr_ref[pl.ds(t, 1), :] = nr
            gi_ref[pl.ds(t, 1), :] = ni
            return nr, ni, accr, acci

        gr, gi, accr, acci = lax.fori_loop(0, tt, step, (sr[...], si[...], dar_ref[...], dai_ref[...]))
        sr[...], si[...] = gr, gi
        dar_ref[...], dai_ref[...] = accr, acci

    blk = pl.BlockSpec((tt, tn), lambda n, i: (nt - 1 - i, n))
    vec = pl.BlockSpec((1, tn), lambda n, i: (0, n))
    o, v = jax.ShapeDtypeStruct((S, N), f32), jax.ShapeDtypeStruct((1, N), f32)
    return _call(body, name=f"s5_scan_bwd_{S}", grid=(N // tn, nt), in_specs=[blk, blk, blk, blk, vec, vec],
                 out_specs=[blk, blk, vec, vec], out_shape=[o, o, v, v],
                 scratch_shapes=[pltpu.VMEM((1, tn), f32), pltpu.VMEM((1, tn), f32)],
                 compiler_params=_cp("parallel", "arbitrary"))(dr, di, xr, xi, abr, abi)


def _s5_out_fwd(xr, xi, Cr, Ci, u, d):
    S, N = xr.shape
    nb, nw, _ = Cr.shape
    D = u.shape[1]
    ts = _tile(S, 512, 8)

    def body(xr_ref, xi_ref, cr_ref, ci_ref, u_ref, d_ref, y_ref, z_ref):
        y = _dot3(xr_ref[...], cr_ref[0]) - _dot3(xi_ref[...], ci_ref[0]) + d_ref[...] * u_ref[...]
        y_ref[...] = y
        z_ref[...] = _gelu(y).astype(z_ref.dtype)

    xb = pl.BlockSpec((ts, nw), lambda i, k: (i, k))
    cb = pl.BlockSpec((1, nw, _LANES), lambda i, k: (k, 0, 0))
    ub = pl.BlockSpec((ts, _LANES), lambda i, k: (i, k))
    db = pl.BlockSpec((1, _LANES), lambda i, k: (0, k))
    return _call(body, name=f"s5_out_fwd_{S}", grid=(S // ts, nb), in_specs=[xb, xb, cb, cb, ub, db],
                 out_specs=[ub, ub], out_shape=[jax.ShapeDtypeStruct((S, D), f32), jax.ShapeDtypeStruct((S, D), _MXU)],
                 compiler_params=_cp("parallel", "parallel"))(xr, xi, Cr, Ci, u, d.reshape(1, D))


def _s5_out_bwd(dz, y, u, d, Cr, Ci):
    S, D = y.shape
    nb, nw, _ = Cr.shape
    ts = _tile(S, 512, 8)

    def body(dz_ref, y_ref, u_ref, d_ref, cr_ref, ci_ref, dy_ref, gr_ref, gi_ref, du_ref, dd_ref):
        @pl.when(pl.program_id(1) == 0)
        def _():
            dd_ref[...] = jnp.zeros_like(dd_ref)

        dy = dz_ref[...] * _gelu_grad(y_ref[...])
        dy_ref[...] = dy
        gr_ref[...] = _dot3(dy, cr_ref[0], _NT)
        gi_ref[...] = -_dot3(dy, ci_ref[0], _NT)
        du_ref[...] = dy * d_ref[...]
        dd_ref[...] += jnp.sum(dy * u_ref[...], axis=0, keepdims=True)

    xb = pl.BlockSpec((ts, nw), lambda k, i: (i, k))
    cb = pl.BlockSpec((1, nw, _LANES), lambda k, i: (k, 0, 0))
    ub = pl.BlockSpec((ts, _LANES), lambda k, i: (i, k))
    db = pl.BlockSpec((1, _LANES), lambda k, i: (0, k))
    a, s = jax.ShapeDtypeStruct((S, D), f32), jax.ShapeDtypeStruct((S, nb * nw), f32)
    dy, gr, gi, du, dd = _call(
        body, name=f"s5_out_bwd_{S}", grid=(nb, S // ts), in_specs=[ub, ub, ub, db, cb, cb],
        out_specs=[ub, xb, xb, ub, db], out_shape=[a, s, s, a, jax.ShapeDtypeStruct((1, D), f32)],
        compiler_params=_cp("parallel", "arbitrary"))(dz, y, u, d.reshape(1, D), Cr, Ci)
    return dy, gr, gi, du, dd.reshape(D)


def _s5_in_bwd(gr, gi, Br, Bi, du0):
    S, N = gr.shape
    nb, _, nw = Br.shape
    ts = _tile(S, 512, 8)

    def body(gr_ref, gi_ref, br_ref, bi_ref, d0_ref, du_ref):
        du_ref[...] = d0_ref[...] + _dot3(gr_ref[...], br_ref[0], _NT) + _dot3(gi_ref[...], bi_ref[0], _NT)

    xb = pl.BlockSpec((ts, nw), lambda i, k: (i, k))
    wb = pl.BlockSpec((1, _LANES, nw), lambda i, k: (k, 0, 0))
    ub = pl.BlockSpec((ts, _LANES), lambda i, k: (i, k))
    return _call(body, name=f"s5_in_bwd_{S}", grid=(S // ts, nb), in_specs=[xb, xb, wb, wb, ub], out_specs=ub,
                 out_shape=jax.ShapeDtypeStruct((S, nb * _LANES), f32), compiler_params=_cp("parallel", "parallel"))(
                     gr, gi, Br, Bi, du0)


def _s5_wgrad(u, gr, gi, xr, xi, dy):
    S, D = u.shape
    nb = D // _LANES
    nw = gr.shape[1] // nb
    ts = _tile(S, 512, 8)

    def body(u_ref, gr_ref, gi_ref, xr_ref, xi_ref, dy_ref, dbr_ref, dbi_ref, dcr_ref, dci_ref):
        @pl.when(pl.program_id(1) == 0)
        def _():
            for r in (dbr_ref, dbi_ref, dcr_ref, dci_ref):
                r[...] = jnp.zeros_like(r)

        uv, dyv = u_ref[...], dy_ref[...]
        dbr_ref[0] += _dot3(uv, gr_ref[...], _TN)
        dbi_ref[0] += _dot3(uv, gi_ref[...], _TN)
        dcr_ref[0] += _dot3(xr_ref[...], dyv, _TN)
        dci_ref[0] -= _dot3(xi_ref[...], dyv, _TN)

    xb = pl.BlockSpec((ts, nw), lambda k, i: (i, k))
    ub = pl.BlockSpec((ts, _LANES), lambda k, i: (i, k))
    wb = pl.BlockSpec((1, _LANES, nw), lambda k, i: (k, 0, 0))
    cb = pl.BlockSpec((1, nw, _LANES), lambda k, i: (k, 0, 0))
    w, c = jax.ShapeDtypeStruct((nb, _LANES, nw), f32), jax.ShapeDtypeStruct((nb, nw, _LANES), f32)
    return _call(body, name=f"s5_wgrad_{S}", grid=(nb, S // ts), in_specs=[ub, xb, xb, xb, xb, ub],
                 out_specs=[wb, wb, cb, cb], out_shape=[w, w, c, c],
                 compiler_params=_cp("parallel", "arbitrary"))(u, gr, gi, xr, xi, dy)


def _glu_fwd(vg):
    S, D2 = vg.shape
    D = D2 // 2
    ts, tc = _tile(S, 1024, 8), _tile(D, 512)
    nc = D // tc

    def body(v_ref, g_ref, o_ref):
        o_ref[...] = v_ref[...] * _sigmoid(g_ref[...])

    return _call(body, name=f"glu_fwd_{S}", grid=(S // ts, nc),
                 in_specs=[pl.BlockSpec((ts, tc), lambda i, j: (i, j)), pl.BlockSpec((ts, tc), lambda i, j: (i, nc + j))],
                 out_specs=pl.BlockSpec((ts, tc), lambda i, j: (i, j)), out_shape=jax.ShapeDtypeStruct((S, D), f32),
                 compiler_params=_cp("parallel", "parallel"))(vg, vg)


def _glu_bwd(vg, dout):
    S, D2 = vg.shape
    D = D2 // 2
    ts, tc = _tile(S, 1024, 8), _tile(D, 512)
    nc = D // tc

    def body(v_ref, g_ref, do_ref, dv_ref, dg_ref):
        sg = _sigmoid(g_ref[...])
        do = do_ref[...]
        dv_ref[...] = (do * sg).astype(dv_ref.dtype)
        dg_ref[...] = (do * v_ref[...] * sg * (1.0 - sg)).astype(dg_ref.dtype)

    blk = pl.BlockSpec((ts, tc), lambda i, j: (i, j))
    o = jax.ShapeDtypeStruct((S, D), _MXU)
    return _call(body, name=f"glu_bwd_{S}", grid=(S // ts, nc),
                 in_specs=[blk, pl.BlockSpec((ts, tc), lambda i, j: (i, nc + j)), blk], out_specs=[blk, blk],
                 out_shape=[o, o], compiler_params=_cp("parallel", "parallel"))(vg, vg, dout)


def _blockdiag_in(b2, nb, P):
    cg = b2.shape[0]
    gb = S5_BLOCK_GROUPS
    t = b2.reshape(cg, nb, gb, P).transpose(1, 0, 2, 3)
    eye = jnp.eye(gb, dtype=b2.dtype)
    return (eye[None, :, None, :, None] * t[:, None]).reshape(nb, gb * cg, gb * P)


def _blockdiag_in_extract(db, cg, P):
    nb = db.shape[0]
    gb = S5_BLOCK_GROUPS
    eye = jnp.eye(gb, dtype=db.dtype)
    t = (db.reshape(nb, gb, cg, gb, P) * eye[None, :, None, :, None]).sum(3)
    return t.transpose(2, 0, 1, 3).reshape(cg, nb * gb * P)


def _blockdiag_out(c, nb):
    G, cg, P = c.shape
    gb = S5_BLOCK_GROUPS
    t = c.reshape(nb, gb, cg, P).transpose(0, 3, 1, 2)
    eye = jnp.eye(gb, dtype=c.dtype)
    return (eye[None, :, None, :, None] * t[:, None]).reshape(nb, gb * P, gb * cg)


def _blockdiag_out_extract(dc, cg, P):
    nb = dc.shape[0]
    gb = S5_BLOCK_GROUPS
    eye = jnp.eye(gb, dtype=dc.dtype)
    t = (dc.reshape(nb, gb, P, gb, cg) * eye[None, :, None, :, None]).sum(1)
    return t.transpose(0, 2, 3, 1).reshape(nb * gb, cg, P)


def _xa_fwd(q, kv):
    S, D = q.shape
    M = kv.shape[0]
    dh = D // XA_HEADS
    ts = _tile(S, 512, 8)
    scale = dh ** -0.5

    def body(q_ref, k_ref, v_ref, o_ref):
        s = _dot(q_ref[...], k_ref[...], _NT) * scale
        e = jnp.exp(s - jnp.max(s, axis=-1, keepdims=True))
        p = e / jnp.sum(e, axis=-1, keepdims=True)
        o_ref[...] = _dot(p.astype(_MXU), v_ref[...]).astype(o_ref.dtype)

    return _call(body, name=f"xa_fwd_{S}", grid=(S // ts, XA_HEADS),
                 in_specs=[pl.BlockSpec((ts, dh), lambda i, h: (i, h)), pl.BlockSpec((M, dh), lambda i, h: (0, h)),
                           pl.BlockSpec((M, dh), lambda i, h: (0, XA_HEADS + h))],
                 out_specs=pl.BlockSpec((ts, dh), lambda i, h: (i, h)), out_shape=jax.ShapeDtypeStruct((S, D), _MXU),
                 compiler_params=_cp("parallel", "parallel"))(q, kv, kv)


def _xa_bwd(q, kv, do):
    S, D = q.shape
    M = kv.shape[0]
    dh = D // XA_HEADS
    ts = _tile(S, 512, 8)
    scale = dh ** -0.5

    def body(q_ref, k_ref, v_ref, do_ref, dq_ref, dk_ref, dv_ref):
        @pl.when(pl.program_id(1) == 0)
        def _():
            dk_ref[...] = jnp.zeros_like(dk_ref)
            dv_ref[...] = jnp.zeros_like(dv_ref)

        qv, kv_, vv, dov = q_ref[...], k_ref[...], v_ref[...], do_ref[...]
        s = _dot(qv, kv_, _NT) * scale
        e = jnp.exp(s - jnp.max(s, axis=-1, keepdims=True))
        p = e / jnp.sum(e, axis=-1, keepdims=True)
        dp = _dot(dov, vv, _NT)
        dv_ref[...] += _dot(p.astype(_MXU), dov, _TN)
        ds = (p * (dp - jnp.sum(dp * p, axis=-1, keepdims=True)) * scale).astype(_MXU)
        dq_ref[...] = _dot(ds, kv_).astype(dq_ref.dtype)
        dk_ref[...] += _dot(ds, qv, _TN)

    qb = pl.BlockSpec((ts, dh), lambda h, i: (i, h))
    mb = pl.BlockSpec((M, dh), lambda h, i: (0, h))
    m = jax.ShapeDtypeStruct((M, D), f32)
    return _call(body, name=f"xa_bwd_{S}", grid=(XA_HEADS, S // ts),
                 in_specs=[qb, mb, pl.BlockSpec((M, dh), lambda h, i: (0, XA_HEADS + h)), qb],
                 out_specs=[qb, mb, mb], out_shape=[jax.ShapeDtypeStruct((S, D), _MXU), m, m],
                 compiler_params=_cp("parallel", "arbitrary"))(q, kv, kv, do)


def _conv(u, cw_ref, cb_ref, rows):
    return cw_ref[2:3, :] * u + cw_ref[1:2, :] * _shift_down(u, 1, rows) + cw_ref[0:1, :] * _shift_down(u, 2, rows) + cb_ref[...]


def _convglu_fwd(u, cw, cb):
    S, F2 = u.shape
    F = F2 // 2
    tc = _tile(F, 128)
    nc = F // tc

    def body(uv_ref, ug_ref, cwv_ref, cwg_ref, cbv_ref, cbg_ref, o_ref):
        rows = lax.broadcasted_iota(jnp.int32, (S, 1), 0)
        val = _conv(uv_ref[...], cwv_ref, cbv_ref, rows)
        gate = _conv(ug_ref[...], cwg_ref, cbg_ref, rows)
        o_ref[...] = (gate * _sigmoid(gate) * val).astype(o_ref.dtype)

    def col(r, off):
        return pl.BlockSpec((r, tc), lambda j: (0, off + j))

    return _call(body, name=f"convglu_fwd_{S}", grid=(nc,),
                 in_specs=[col(S, 0), col(S, nc), col(3, 0), col(3, nc), col(1, 0), col(1, nc)], out_specs=col(S, 0),
                 out_shape=jax.ShapeDtypeStruct((S, F), _MXU), compiler_params=_cp("parallel"))(
                     u, u, cw, cw, cb.reshape(1, F2), cb.reshape(1, F2))


def _convglu_bwd(u, cw, cb, dact):
    S, F2 = u.shape
    F = F2 // 2
    tc = _tile(F, 128)
    nc = F // tc

    def body(uv_ref, ug_ref, cwv_ref, cwg_ref, cbv_ref, cbg_ref, da_ref, duv_ref, dug_ref, dcwv_ref, dcwg_ref, dcbv_ref, dcbg_ref):
        rows = lax.broadcasted_iota(jnp.int32, (S, 1), 0)
        uv, ug = uv_ref[...], ug_ref[...]
        val = _conv(uv, cwv_ref, cbv_ref, rows)
        gate = _conv(ug, cwg_ref, cbg_ref, rows)
        sg = _sigmoid(gate)
        da = da_ref[...]
        dval = da * (gate * sg)
        dgate = da * val * (sg * (1.0 + gate * (1.0 - sg)))
        for uu, d, cw_ref, du_ref, dcw_ref, dcb_ref in ((uv, dval, cwv_ref, duv_ref, dcwv_ref, dcbv_ref),
                                                        (ug, dgate, cwg_ref, dug_ref, dcwg_ref, dcbg_ref)):
            dcb_ref[...] = jnp.sum(d, axis=0, keepdims=True)
            dcw_ref[2:3, :] = jnp.sum(d * uu, axis=0, keepdims=True)
            dcw_ref[1:2, :] = jnp.sum(d * _shift_down(uu, 1, rows), axis=0, keepdims=True)
            dcw_ref[0:1, :] = jnp.sum(d * _shift_down(uu, 2, rows), axis=0, keepdims=True)
            du = cw_ref[2:3, :] * d + cw_ref[1:2, :] * _shift_up(d, 1, rows, S) + cw_ref[0:1, :] * _shift_up(d, 2, rows, S)
            du_ref[...] = du.astype(du_ref.dtype)

    def col(r, off):
        return pl.BlockSpec((r, tc), lambda j: (0, off + j))

    o = jax.ShapeDtypeStruct((S, F), _MXU)
    w, b = jax.ShapeDtypeStruct((3, F), f32), jax.ShapeDtypeStruct((1, F), f32)
    duv, dug, dcwv, dcwg, dcbv, dcbg = _call(
        body, name=f"convglu_bwd_{S}", grid=(nc,),
        in_specs=[col(S, 0), col(S, nc), col(3, 0), col(3, nc), col(1, 0), col(1, nc), col(S, 0)],
        out_specs=[col(S, 0), col(S, 0), col(3, 0), col(3, 0), col(1, 0), col(1, 0)],
        out_shape=[o, o, w, w, b, b], compiler_params=_cp("parallel"))(
            u, u, cw, cw, cb.reshape(1, F2), cb.reshape(1, F2), dact)
    return (jnp.concatenate([duv, dug], axis=1), jnp.concatenate([dcwv, dcwg], axis=1),
            jnp.concatenate([dcbv, dcbg], axis=1).reshape(F2))


def _final_loss(h, delta, g, target):
    S, D = h.shape
    ts = _tile(S, 512, 8)
    row = pl.BlockSpec((ts, D), lambda i: (i, 0))
    vec = pl.BlockSpec((1, D), lambda i: (0, 0))
    one = pl.BlockSpec((1, _LANES), lambda i: (0, 0))

    def body(h_ref, d_ref, g_ref, t_ref, l_ref, dh_ref, dg_ref):
        @pl.when(pl.program_id(0) == 0)
        def _():
            l_ref[...] = jnp.zeros_like(l_ref)
            dg_ref[...] = jnp.zeros_like(dg_ref)

        xv = h_ref[...] + d_ref[...]
        gv = g_ref[...]
        r = lax.rsqrt(jnp.mean(xv * xv, axis=-1, keepdims=True) + EPS)
        xh = xv * r
        err = xh * gv - t_ref[...]
        l_ref[...] += 0.5 * jnp.sum(jnp.mean(err * err, axis=-1, keepdims=True), axis=0, keepdims=True)
        dy = err * (1.0 / D)
        dg_ref[...] += jnp.sum(dy * xh, axis=0, keepdims=True)
        dxh = dy * gv
        dh_ref[...] = r * (dxh - xh * jnp.mean(dxh * xh, axis=-1, keepdims=True))

    loss, dh, dg = _call(body, name=f"final_loss_{S}", grid=(S // ts,), in_specs=[row, row, vec, row],
                         out_specs=[one, row, vec],
                         out_shape=[jax.ShapeDtypeStruct((1, _LANES), f32), jax.ShapeDtypeStruct((S, D), f32),
                                    jax.ShapeDtypeStruct((1, D), f32)],
                         compiler_params=_cp("arbitrary"))(h, delta, g.reshape(1, D), target)
    return loss[0, 0], dh, dg.reshape(D)


def _adamw(w, g, m, v):
    R, C = w.shape
    tr = _tile(R, max(8, (1 << 19) // C // 8 * 8), 8)
    blk = pl.BlockSpec((tr, C), lambda i: (i, 0))

    def body(w_ref, g_ref, m_ref, v_ref, d_ref, nm_ref, nv_ref):
        gv = g_ref[...]
        m_new = ADAM_B1 * m_ref[...] + (1.0 - ADAM_B1) * gv
        v_new = ADAM_B2 * v_ref[...] + (1.0 - ADAM_B2) * (gv * gv)
        m_hat = m_new / (1.0 - ADAM_B1 ** ADAM_STEP)
        v_hat = v_new / (1.0 - ADAM_B2 ** ADAM_STEP)
        d_ref[...] = -ADAM_LR * (m_hat / (jnp.sqrt(v_hat) + ADAM_EPS) + ADAM_WD * w_ref[...])
        nm_ref[...] = m_new
        nv_ref[...] = v_new

    o = jax.ShapeDtypeStruct((R, C), f32)
    return _call(body, name=f"adamw_{R}x{C}", grid=(R // tr,), in_specs=[blk] * 4, out_specs=[blk] * 3,
                 out_shape=[o, o, o], compiler_params=_cp("parallel"))(w, g, m, v)


def _sum_slabs(xs):
    K, R, C = xs.shape
    tr = _tile(R, 256, 8)

    def body(x_ref, o_ref):
        s = x_ref[0]
        for k in range(1, K):
            s = s + x_ref[k]
        o_ref[...] = s

    return _call(body, name=f"sum_slabs_{K}x{R}", grid=(R // tr,),
                 in_specs=[pl.BlockSpec((K, tr, C), lambda i: (0, i, 0))], out_specs=pl.BlockSpec((tr, C), lambda i: (i, 0)),
                 out_shape=jax.ShapeDtypeStruct((R, C), f32), compiler_params=_cp("parallel"))(xs)


def _add_slabs(a, b, lead):
    _, K, R, C = a.shape
    tr = _tile(R, 256, 8)

    def body(c_ref, a_ref, b_ref, o_ref):
        o_ref[...] = a_ref[0] + b_ref[...]

    return _call(
        body, name=f"add_slabs_{K}x{R}",
        grid_spec=pltpu.PrefetchScalarGridSpec(
            num_scalar_prefetch=1, grid=(K, R // tr),
            in_specs=[pl.BlockSpec((1, 1, tr, C), lambda k, i, c: (c[0], k, i, 0)),
                      pl.BlockSpec((1, tr, C), lambda k, i, c: (k, i, 0))],
            out_specs=pl.BlockSpec((1, tr, C), lambda k, i, c: (k, i, 0))),
        out_shape=jax.ShapeDtypeStruct((K, R, C), f32), compiler_params=_cp("parallel", "parallel"))(lead, a, b)


def _my_pos():
    return lax.axis_index("x"), lax.axis_index("y"), lax.axis_index("c")


def _all_gather(shard):
    R, C = shard.shape

    def body(x_ref, out_ref, send_sems, recv_sems, local_sem):
        x, y, c = _my_pos()
        me, sibling = (x, y, c), (x, y, 1 - c)
        chips = [(1 - x, y), (x, 1 - y), (1 - x, 1 - y)]

        def slab(px, py, pc):
            return out_ref.at[4 * px + 2 * py + pc]

        def copy(k, block, to, src=None):
            return pltpu.make_async_remote_copy(
                src_ref=slab(*block) if src is None else src, dst_ref=slab(*block),
                send_sem=send_sems.at[k], recv_sem=recv_sems.at[k], device_id=to, device_id_type=MESH_ID)

        mine = pltpu.make_async_copy(x_ref, slab(*me), local_sem)
        mine.start()
        first = [copy(0, me, sibling, src=x_ref)]
        first += [copy(1 + j, me, (*chip, c), src=x_ref) for j, chip in enumerate(chips)]
        for cp in first:
            cp.start()
        passed = [copy(4 + j, (*chip, c), sibling) for j, chip in enumerate(chips)]
        for j, chip in enumerate(chips):
            copy(1 + j, (*chip, c), me).wait_recv()
            passed[j].start()
        copy(0, sibling, me).wait_recv()
        for j, chip in enumerate(chips):
            copy(4 + j, (*chip, 1 - c), me).wait_recv()
        for cp in first + passed:
            cp.wait_send()
        mine.wait()

    return _call(body, name=f"all_gather_{R}x{C}_{jnp.dtype(shard.dtype).name}", in_specs=[ANY], out_specs=ANY,
                 out_shape=jax.ShapeDtypeStruct((NDEV, R, C), shard.dtype),
                 scratch_shapes=[pltpu.SemaphoreType.DMA((7,)), pltpu.SemaphoreType.DMA((7,)), pltpu.SemaphoreType.DMA])(shard)


def _exchange_cores(g):
    _, K, R, C = g.shape

    def body(g_ref, out_ref, send_sem, recv_sem):
        x, y, c = _my_pos()
        cp = pltpu.make_async_remote_copy(src_ref=g_ref.at[1 - c], dst_ref=out_ref, send_sem=send_sem, recv_sem=recv_sem,
                                          device_id=(x, y, 1 - c), device_id_type=MESH_ID)
        cp.start()
        cp.wait()

    return _call(body, name=f"exchange_cores_{R}", in_specs=[ANY], out_specs=ANY,
                 out_shape=jax.ShapeDtypeStruct((K, R, C), g.dtype),
                 scratch_shapes=[pltpu.SemaphoreType.DMA, pltpu.SemaphoreType.DMA])(g)


def _exchange_chips(p):
    K, R, C = p.shape

    def body(p_ref, out_ref, send_sems, recv_sems):
        x, y, c = _my_pos()
        chips = [(x, 1 - y), (1 - x, y), (1 - x, 1 - y)]
        cps = [pltpu.make_async_remote_copy(src_ref=p_ref.at[2 * px + py], dst_ref=out_ref.at[r],
                                            send_sem=send_sems.at[r], recv_sem=recv_sems.at[r],
                                            device_id=(px, py, c), device_id_type=MESH_ID)
               for r, (px, py) in enumerate(chips)]
        for cp in cps:
            cp.start()
        for cp in cps:
            cp.wait()

    return _call(body, name=f"exchange_chips_{R}", in_specs=[ANY], out_specs=ANY,
                 out_shape=jax.ShapeDtypeStruct((3, R, C), p.dtype),
                 scratch_shapes=[pltpu.SemaphoreType.DMA((3,)), pltpu.SemaphoreType.DMA((3,))])(p)


def _pack(arrs, dtype, rows_mult):
    flat = jnp.concatenate([a.astype(dtype).reshape(-1) for a in arrs])
    q = rows_mult * _PACK_COLS
    tot = -(-flat.shape[0] // q) * q
    return jnp.pad(flat, (0, tot - flat.shape[0])).reshape(tot // _PACK_COLS, _PACK_COLS)


def _unpack(flat, shapes):
    out, off = [], 0
    for s in shapes:
        n = math.prod(s)
        out.append(flat[..., off:off + n].reshape(flat.shape[:-1] + tuple(s)))
        off += n
    return out


def _gather_weights(shards, names, dtype, rows_mult):
    g = _all_gather(_pack([shards[n] for n in names], dtype, rows_mult))
    pieces = _unpack(g.reshape(NDEV, -1), [shards[n].shape for n in names])
    full = {}
    for n, pc in zip(names, pieces):
        ax = SHARD_AXIS[n]
        t = jnp.moveaxis(pc, 0, ax)
        full[n] = t.reshape(t.shape[:ax] + (t.shape[ax] * t.shape[ax + 1],) + t.shape[ax + 2:])
    return full


def _reduce_scatter_grads(grads, names, shard_shapes):
    parts = []
    for n in names:
        ax, g = SHARD_AXIS[n], grads[n]
        t = g.reshape(g.shape[:ax] + (NDEV, g.shape[ax] // NDEV) + g.shape[ax + 1:])
        parts.append(jnp.moveaxis(t, ax, 0).reshape(NDEV, -1))
    flat = jnp.concatenate(parts, axis=1)
    q = 8 * _PACK_COLS
    tot = -(-flat.shape[1] // q) * q
    rows = tot // _PACK_COLS
    flat = jnp.pad(flat, ((0, 0), (0, tot - flat.shape[1]))).reshape(4, 2, rows, _PACK_COLS)
    by_core = jnp.transpose(flat, (1, 0, 2, 3))
    c = lax.axis_index("c")
    from_sibling = _exchange_cores(by_core)
    chip_sum = _add_slabs(by_core, from_sibling, jnp.reshape(c, (1,)).astype(jnp.int32))
    recv = _exchange_chips(chip_sum)
    q_me = 2 * lax.axis_index("x") + lax.axis_index("y")
    own = lax.dynamic_index_in_dim(chip_sum, q_me, 0, keepdims=True)
    total = _sum_slabs(jnp.concatenate([own, recv], axis=0))
    return dict(zip(names, _unpack(total.reshape(-1), [shard_shapes[n] for n in names])))


def _all_reduce_small(grads, names):
    shapes = [grads[n].shape for n in names]
    packed = _pack([grads[n] for n in names], f32, 8)
    total = _sum_slabs(_all_gather(packed))
    return dict(zip(names, _unpack(total.reshape(-1), shapes)))


def _as2d(a):
    if a.ndim == 1:
        return a.reshape(1, -1)
    return a.reshape(-1, a.shape[-1])


def kernel(x, mem, mix_norm_g, pool_w, pool_scale, sb_w_qkv, sb_w_o, s5_a_re, s5_a_im, s5_log_dt, s5_b_re, s5_b_im, s5_c_re, s5_c_im, s5_d, s5_w_glu, xa_norm_g, mem_norm_g, xa_wq, xa_wkv, xa_wo, ffn_norm_g, ffn_w_up, ffn_conv_w, ffn_conv_b, ffn_w_down, final_norm_g, loss_target, m_mix_norm_g, m_pool_w, m_pool_scale, m_sb_w_qkv, m_sb_w_o, m_s5_a_re, m_s5_a_im, m_s5_log_dt, m_s5_b_re, m_s5_b_im, m_s5_c_re, m_s5_c_im, m_s5_d, m_s5_w_glu, m_xa_norm_g, m_mem_norm_g, m_xa_wq, m_xa_wkv, m_xa_wo, m_ffn_norm_g, m_ffn_w_up, m_ffn_conv_w, m_ffn_conv_b, m_ffn_w_down, m_final_norm_g, v_mix_norm_g, v_pool_w, v_pool_scale, v_sb_w_qkv, v_sb_w_o, v_s5_a_re, v_s5_a_im, v_s5_log_dt, v_s5_b_re, v_s5_b_im, v_s5_c_re, v_s5_c_im, v_s5_d, v_s5_w_glu, v_xa_norm_g, v_mem_norm_g, v_xa_wq, v_xa_wkv, v_xa_wo, v_ffn_norm_g, v_ffn_w_up, v_ffn_conv_w, v_ffn_conv_b, v_ffn_w_down, v_final_norm_g):
    args = locals()
    w = {n: args[n] for n in WEIGHTS}
    mom = {n: args["m_" + n] for n in WEIGHTS}
    var = {n: args["v_" + n] for n in WEIGHTS}
    h0, memv, target = x[0], mem[0], loss_target[0]
    S, D = h0.shape
    depth = mix_norm_g.shape[0]
    n_mix = 3

    full = dict(w)
    full.update(_gather_weights(w, MXU_WEIGHTS, _MXU, 16))
    full.update(_gather_weights(w, VEC_WEIGHTS, f32, 8))

    grads = {}

    def acc(name, j, val):
        grads.setdefault(name, {})[j] = val

    s5 = []
    for j in range(s5_a_re.shape[0]):
        G, P = s5_a_re.shape[1:]
        cg = s5_b_re.shape[3]
        N = G * P
        nb = D // _LANES
        ar, ai = s5_a_re[j].reshape(1, N), s5_a_im[j].reshape(1, N)
        ldt = jnp.repeat(s5_log_dt[j], P).reshape(1, N)
        br = s5_b_re[j].transpose(2, 0, 1).reshape(cg, N)
        bi = s5_b_im[j].transpose(2, 0, 1).reshape(cg, N)
        abr, abi, bbr, bbi = _s5_prep_fwd(ar, ai, ldt, br, bi)
        s5.append(dict(ar=ar, ai=ai, ldt=ldt, br=br, bi=bi, abr=abr, abi=abi, G=G, P=P, cg=cg, N=N, nb=nb,
                       Br=_blockdiag_in(bbr, nb, P), Bi=_blockdiag_in(bbi, nb, P),
                       Cr=_blockdiag_out(s5_c_re[j], nb), Ci=_blockdiag_out(s5_c_im[j], nb)))

    saved = []
    h, delta = h0, None
    for i in range(depth):
        kind, j = i % n_mix, i // n_mix
        sv = dict(kind=kind, j=j)
        if delta is None:
            hn = _rms_fwd(h, full['mix_norm_g'][i])
        else:
            h, hn = _rms_fwd(h, full['mix_norm_g'][i], delta)
        sv['h'] = h
        if kind == 0:
            p = _pool_fwd(hn)
            t = _pool_mix_fwd(p, full['pool_w'][j], full['pool_scale'][j])
            sv.update(p=p)
        elif kind == 1:
            qkv = _mm(hn, full['sb_w_qkv'][j], out_dtype=_MXU)
            o = _sb_fwd(qkv)
            t = _mm(o, full['sb_w_o'][j])
            sv.update(hn=hn, qkv=qkv, o=o)
        else:
            pr = s5[j]
            bur, bui = _s5_in_fwd(hn, pr['Br'], pr['Bi'])
            xr, xi = _s5_scan_fwd(bur, bui, pr['abr'], pr['abi'])
            y, z = _s5_out_fwd(xr, xi, pr['Cr'], pr['Ci'], hn, full['s5_d'][j])
            vg = _mm(z, full['s5_w_glu'][j])
            t = _glu_fwd(vg)
            sv.update(hn=hn, xr=xr, xi=xi, y=y, z=z, vg=vg)
        h1, a = _rms_fwd(h, full['xa_norm_g'][i], t)
        memn = _rms_fwd(memv, full['mem_norm_g'][i])
        q = _mm(a, full['xa_wq'][i], out_dtype=_MXU)
        kv = _mm(memn, full['xa_wkv'][i], out_dtype=_MXU)
        o2 = _xa_fwd(q, kv)
        mo = _mm(o2, full['xa_wo'][i])
        h2, b = _rms_fwd(h1, full['ffn_norm_g'][i], mo)
        u = _mm(b, full['ffn_w_up'][i])
        act = _convglu_fwd(u, full['ffn_conv_w'][i], full['ffn_conv_b'][i])
        delta = _mm(act, full['ffn_w_down'][i])
        sv.update(h1=h1, a=a, memn=memn, q=q, kv=kv, o2=o2, h2=h2, b=b, u=u, act=act)
        saved.append(sv)
        h = h2

    loss_local, dh, dg_final = _final_loss(h, delta, full['final_norm_g'], target)
    loss = lax.psum(loss_local, ("x", "y", "c"))
    grads['final_norm_g'] = dg_final

    for i in reversed(range(depth)):
        sv = saved[i]
        kind, j = sv['kind'], sv['j']
        dact = _mm(dh, full['ffn_w_down'][i], tb=True)
        acc('ffn_w_down', i, _mm(sv['act'], dh, ta=True))
        du, dcw, dcb = _convglu_bwd(sv['u'], full['ffn_conv_w'][i], full['ffn_conv_b'][i], dact)
        acc('ffn_conv_w', i, dcw)
        acc('ffn_conv_b', i, dcb)
        db = _mm(du, full['ffn_w_up'][i], tb=True)
        acc('ffn_w_up', i, _mm(sv['b'], du, ta=True))
        dh2, dg = _rms_bwd(sv['h2'], full['ffn_norm_g'][i], db, dh)
        acc('ffn_norm_g', i, dg)
        do2 = _mm(dh2, full['xa_wo'][i], tb=True, out_dtype=_MXU)
        acc('xa_wo', i, _mm(sv['o2'], dh2, ta=True))
        dq, dk, dv = _xa_bwd(sv['q'], sv['kv'], do2)
        dkv = jnp.concatenate([dk, dv], axis=1)
        da = _mm(dq, full['xa_wq'][i], tb=True)
        acc('xa_wq', i, _mm(sv['a'], dq, ta=True))
        dmemn = _mm(dkv, full['xa_wkv'][i], tb=True)
        acc('xa_wkv', i, _mm(sv['memn'], dkv, ta=True))
        _, dg = _rms_bwd(memv, full['mem_norm_g'][i], dmemn)
        acc('mem_norm_g', i, dg)
        dh1, dg = _rms_bwd(sv['h1'], full['xa_norm_g'][i], da, dh2)
        acc('xa_norm_g', i, dg)
        if kind == 0:
            dp, dw, ds = _pool_mix_bwd(sv['p'], full['pool_w'][j], full['pool_scale'][j], dh1)
            acc('pool_w', j, dw)
            acc('pool_scale', j, ds)
            dhn = _pool_bwd(dp)
        elif kind == 1:
            do = _mm(dh1, full['sb_w_o'][j], tb=True)
            acc('sb_w_o', j, _mm(sv['o'], dh1, ta=True))
            dqs, dks, dvs = _sb_bwd(sv['qkv'], sv['o'], do)
            dqkv = jnp.concatenate([dqs, dks, dvs], axis=1)
            dhn = _mm(dqkv, full['sb_w_qkv'][j], tb=True)
            acc('sb_w_qkv', j, _mm(sv['hn'], dqkv, ta=True))
        else:
            pr = s5[j]
            dval, dgate = _glu_bwd(sv['vg'], dh1)
            dvg = jnp.concatenate([dval, dgate], axis=1)
            dz = _mm(dvg, full['s5_w_glu'][j], tb=True)
            acc('s5_w_glu', j, _mm(sv['z'], dvg, ta=True))
            dy, gdr, gdi, du0, dd = _s5_out_bwd(dz, sv['y'], sv['hn'], full['s5_d'][j], pr['Cr'], pr['Ci'])
            acc('s5_d', j, dd)
            gr, gi, dabr, dabi = _s5_scan_bwd(gdr, gdi, sv['xr'], sv['xi'], pr['abr'], pr['abi'])
            dhn = _s5_in_bwd(gr, gi, pr['Br'], pr['Bi'], du0)
            dBr, dBi, dCr, dCi = _s5_wgrad(sv['hn'], gr, gi, sv['xr'], sv['xi'], dy)
            cg, P, G = pr['cg'], pr['P'], pr['G']
            acc('s5_c_re', j, _blockdiag_out_extract(dCr, cg, P))
            acc('s5_c_im', j, _blockdiag_out_extract(dCi, cg, P))
            dar, dai, dldt, dbr, dbi = _s5_prep_bwd(pr['ar'], pr['ai'], pr['ldt'], pr['br'], pr['bi'], dabr, dabi,
                                                    _blockdiag_in_extract(dBr, cg, P), _blockdiag_in_extract(dBi, cg, P))
            acc('s5_a_re', j, dar.reshape(G, P))
            acc('s5_a_im', j, dai.reshape(G, P))
            acc('s5_log_dt', j, dldt.reshape(G, P).sum(axis=1))
            acc('s5_b_re', j, dbr.reshape(cg, G, P).transpose(1, 2, 0))
            acc('s5_b_im', j, dbi.reshape(cg, G, P).transpose(1, 2, 0))
        dh, dg = _rms_bwd(sv['h'], full['mix_norm_g'][i], dhn, dh1)
        acc('mix_norm_g', i, dg)
    grad_x = dh[None]

    gfull = {}
    for n in WEIGHTS:
        gfull[n] = grads[n] if n == 'final_norm_g' else jnp.stack([grads[n][k] for k in range(len(grads[n]))])

    gw = {}
    sharded = [n for n in WEIGHTS if n in SHARD_AXIS]
    gw.update(_reduce_scatter_grads(gfull, sharded, {n: w[n].shape for n in sharded}))
    gw.update(_all_reduce_small(gfull, REPLICATED))

    deltas, new_m, new_v = {}, {}, {}
    for n in WEIGHTS:
        shp = w[n].shape
        d_, m_, v_ = _adamw(_as2d(w[n]), _as2d(gw[n]), _as2d(mom[n]), _as2d(var[n]))
        deltas[n], new_m[n], new_v[n] = d_.reshape(shp), m_.reshape(shp), v_.reshape(shp)

    return (loss, grad_x, *[gw[n] for n in WEIGHTS], *[deltas[n] for n in WEIGHTS],
            *[new_m[n] for n in WEIGHTS], *[new_v[n] for n in WEIGHTS])
```

```python
import functools
import math

import jax
import jax.numpy as jnp
from jax import lax
from jax.experimental import pallas as pl
from jax.experimental.pallas import tpu as pltpu

f32 = jnp.float32
_MXU = jnp.bfloat16
_VMEM_LIMIT = 48 * 1024 * 1024
_LANES = 128
_PACK_COLS = 1024

NDEV = 8
EPS = 1e-6
POOL_WINDOWS = (2, 4, 8, 16)
SB_HEAD_DIM = 64
XA_HEADS = 4
S5_GROUP = 16
S5_BLOCK_GROUPS = _LANES // S5_GROUP
ADAM_LR, ADAM_B1, ADAM_B2, ADAM_EPS, ADAM_WD, ADAM_STEP = 0.001, 0.9, 0.999, 1e-08, 0.01, 10

WEIGHTS = ['mix_norm_g', 'pool_w', 'pool_scale', 'sb_w_qkv', 'sb_w_o', 's5_a_re', 's5_a_im', 's5_log_dt',
           's5_b_re', 's5_b_im', 's5_c_re', 's5_c_im', 's5_d', 's5_w_glu', 'xa_norm_g', 'mem_norm_g', 'xa_wq',
           'xa_wkv', 'xa_wo', 'ffn_norm_g', 'ffn_w_up', 'ffn_conv_w', 'ffn_conv_b', 'ffn_w_down', 'final_norm_g']
SHARD_AXIS = {'pool_w': 2, 'pool_scale': 1, 'sb_w_qkv': 2, 'sb_w_o': 1, 's5_d': 1, 's5_w_glu': 2, 'xa_wq': 1,
              'xa_wkv': 2, 'xa_wo': 1, 'ffn_w_up': 2, 'ffn_conv_w': 2, 'ffn_w_down': 1}
MXU_WEIGHTS = ['pool_w', 'sb_w_qkv', 'sb_w_o', 's5_w_glu', 'xa_wq', 'xa_wkv', 'xa_wo', 'ffn_w_up', 'ffn_w_down']
VEC_WEIGHTS = ['pool_scale', 's5_d', 'ffn_conv_w']
REPLICATED = [n for n in WEIGHTS if n not in SHARD_AXIS]

_NN = (((1,), (0,)), ((), ()))
_NT = (((1,), (1,)), ((), ()))
_TN = (((0,), (0,)), ((), ()))
MESH_ID = pl.DeviceIdType.MESH
ANY = pl.BlockSpec(memory_space=pl.ANY)


def _call(body, **kw):
    return pl.pallas_call(body, **kw)


def _cp(*sem):
    return pltpu.CompilerParams(dimension_semantics=sem, vmem_limit_bytes=_VMEM_LIMIT)


def _tile(n, target, mult=_LANES):
    if n <= target:
        return n
    t = (target // mult) * mult
    while t >= mult:
        if n % t == 0:
            return t
        t -= mult
    return n


def _dot(a, b, dims=_NN):
    return lax.dot_general(a, b, dims, preferred_element_type=f32)


def _split(a):
    hi = a.astype(_MXU)
    lo = (a - hi.astype(f32)).astype(_MXU)
    return hi, lo


def _dot_hilo(a, u, dims=_NN):
    hi, lo = _split(a)
    return _dot(hi, u, dims) + _dot(lo, u, dims)


def _dot3(a, b, dims=_NN):
    ah, al = _split(a)
    bh, bl = _split(b)
    return _dot(ah, bh, dims) + _dot(al, bh, dims) + _dot(ah, bl, dims)


def _sigmoid(x):
    return 1.0 / (1.0 + jnp.exp(-x))


_GELU_C = math.sqrt(2.0 / math.pi)


def _gelu(x):
    return x * (0.5 * (1.0 + jnp.tanh(_GELU_C * (x + 0.044715 * (x * x * x)))))


def _gelu_grad(x):
    t = jnp.tanh(_GELU_C * (x + 0.044715 * (x * x * x)))
    return 0.5 * (1.0 + t) + x * 0.5 * (1.0 - t * t) * _GELU_C * (1.0 + 3.0 * 0.044715 * x * x)


def _shift_down(x, k, rows):
    return jnp.where(rows >= k, pltpu.roll(x, k, 0), 0.0)


def _shift_up(x, k, rows, n):
    return jnp.where(rows < n - k, pltpu.roll(x, n - k, 0), 0.0)


def _mm(a, b, *, ta=False, tb=False, out_dtype=f32):
    M, K = (a.shape[1], a.shape[0]) if ta else a.shape
    N = b.shape[0] if tb else b.shape[1]
    tm, tn, tk = _tile(M, 512), _tile(N, 1536), _tile(K, 1408)
    nk = K // tk
    a_spec = pl.BlockSpec((tk, tm), lambda i, j, k: (k, i)) if ta else pl.BlockSpec((tm, tk), lambda i, j, k: (i, k))
    b_spec = pl.BlockSpec((tn, tk), lambda i, j, k: (j, k)) if tb else pl.BlockSpec((tk, tn), lambda i, j, k: (k, j))
    dims = (((0 if ta else 1,), (1 if tb else 0,)), ((), ()))

    def body(a_ref, b_ref, o_ref, acc_ref):
        k = pl.program_id(2)

        @pl.when(k == 0)
        def _():
            acc_ref[...] = jnp.zeros_like(acc_ref)

        acc_ref[...] += _dot(a_ref[...].astype(_MXU), b_ref[...].astype(_MXU), dims)

        @pl.when(k == nk - 1)
        def _():
            o_ref[...] = acc_ref[...].astype(out_dtype)

    return _call(
        body, name=f"mm_{'t' if ta else 'n'}{'t' if tb else 'n'}_{M}x{K}x{N}",
        grid=(M // tm, N // tn, nk), in_specs=[a_spec, b_spec],
        out_specs=pl.BlockSpec((tm, tn), lambda i, j, k: (i, j)),
        out_shape=jax.ShapeDtypeStruct((M, N), out_dtype),
        scratch_shapes=[pltpu.VMEM((tm, tn), f32)],
        compiler_params=_cp("parallel", "parallel", "arbitrary"))(a, b)


def _rms_fwd(x, g, delta=None, *, out_dtype=f32):
    S, D = x.shape
    ts = _tile(S, 512, 8)
    row = pl.BlockSpec((ts, D), lambda i: (i, 0))
    vec = pl.BlockSpec((1, D), lambda i: (0, 0))
    g2 = g.reshape(1, D)

    def norm(xv, g_ref):
        r = lax.rsqrt(jnp.mean(xv * xv, axis=-1, keepdims=True) + EPS)
        return ((xv * r) * g_ref[...]).astype(out_dtype)

    if delta is None:
        def body(x_ref, g_ref, y_ref):
            y_ref[...] = norm(x_ref[...], g_ref)

        return _call(body, name=f"rms_fwd_{S}", grid=(S // ts,), in_specs=[row, vec], out_specs=row,
                     out_shape=jax.ShapeDtypeStruct((S, D), out_dtype), compiler_params=_cp("parallel"))(x, g2)

    def body(x_ref, d_ref, g_ref, s_ref, y_ref):
        xv = x_ref[...] + d_ref[...]
        s_ref[...] = xv
        y_ref[...] = norm(xv, g_ref)

    return _call(body, name=f"add_rms_fwd_{S}", grid=(S // ts,), in_specs=[row, row, vec], out_specs=[row, row],
                 out_shape=[jax.ShapeDtypeStruct((S, D), f32), jax.ShapeDtypeStruct((S, D), out_dtype)],
                 compiler_params=_cp("parallel"))(x, delta, g2)


def _rms_bwd(x, g, dy, dres=None):
    S, D = x.shape
    ts = _tile(S, 512, 8)
    row = pl.BlockSpec((ts, D), lambda i: (i, 0))
    vec = pl.BlockSpec((1, D), lambda i: (0, 0))
    has_res = dres is not None

    def body(*refs):
        if has_res:
            x_ref, g_ref, dy_ref, dr_ref, dx_ref, dg_ref = refs
        else:
            x_ref, g_ref, dy_ref, dx_ref, dg_ref = refs
        xv = x_ref[...]
        r = lax.rsqrt(jnp.mean(xv * xv, axis=-1, keepdims=True) + EPS)
        xh = xv * r
        dyv = dy_ref[...].astype(f32)

        @pl.when(pl.program_id(0) == 0)
        def _():
            dg_ref[...] = jnp.zeros_like(dg_ref)

        dg_ref[...] += jnp.sum(dyv * xh, axis=0, keepdims=True)
        dxh = dyv * g_ref[...]
        dx = r * (dxh - xh * jnp.mean(dxh * xh, axis=-1, keepdims=True))
        if has_res:
            dx = dx + dr_ref[...]
        dx_ref[...] = dx

    ins = [x, g.reshape(1, D), dy] + ([dres] if has_res else [])
    dx, dg = _call(body, name=f"rms_bwd_{S}_{int(has_res)}", grid=(S // ts,),
                   in_specs=[row, vec, row] + ([row] if has_res else []), out_specs=[row, vec],
                   out_shape=[jax.ShapeDtypeStruct((S, D), f32), jax.ShapeDtypeStruct((1, D), f32)],
                   compiler_params=_cp("arbitrary"))(*ins)
    return dx, dg.reshape(D)


def _pool_windows_sum(x, win, rows):
    s, k = x, 1
    while k < win:
        s = s + _shift_down(s, k, rows)
        k *= 2
    return s


def _pool_windows_sum_up(x, win, rows, n):
    s, k = x, 1
    while k < win:
        s = s + _shift_up(s, k, rows, n)
        k *= 2
    return s


def _pool_fwd(hn):
    S, D = hn.shape
    cg = D // len(POOL_WINDOWS)
    tc = _tile(cg, 128)
    nb = cg // tc
    blk = pl.BlockSpec((S, tc), lambda gi, j: (0, gi * nb + j))

    def body(x_ref, p_ref):
        gi = pl.program_id(0)
        rows = lax.broadcasted_iota(jnp.int32, (S, 1), 0)
        cnt = (rows + 1).astype(f32)
        for k, win in enumerate(POOL_WINDOWS):
            @pl.when(gi == k)
            def _(win=win):
                x = x_ref[...]
                s = _pool_windows_sum(x, win, rows)
                p_ref[...] = (s / jnp.minimum(cnt, float(win)) - x).astype(p_ref.dtype)

    return _call(body, name=f"pool_fwd_{S}", grid=(len(POOL_WINDOWS), nb), in_specs=[blk], out_specs=blk,
                 out_shape=jax.ShapeDtypeStruct((S, D), _MXU), compiler_params=_cp("parallel", "parallel"))(hn)


def _pool_bwd(dp):
    S, D = dp.shape
    cg = D // len(POOL_WINDOWS)
    tc = _tile(cg, 128)
    nb = cg // tc
    blk = pl.BlockSpec((S, tc), lambda gi, j: (0, gi * nb + j))

    def body(dp_ref, dx_ref):
        gi = pl.program_id(0)
        rows = lax.broadcasted_iota(jnp.int32, (S, 1), 0)
        cnt = (rows + 1).astype(f32)
        for k, win in enumerate(POOL_WINDOWS):
            @pl.when(gi == k)
            def _(win=win):
                d = dp_ref[...]
                e = d / jnp.minimum(cnt, float(win))
                dx_ref[...] = _pool_windows_sum_up(e, win, rows, S) - d

    return _call(body, name=f"pool_bwd_{S}", grid=(len(POOL_WINDOWS), nb), in_specs=[blk], out_specs=blk,
                 out_shape=jax.ShapeDtypeStruct((S, D), f32), compiler_params=_cp("parallel", "parallel"))(dp)


def _pool_mix_fwd(p, w, scale):
    S, D = p.shape
    G, cg, _ = w.shape
    ts = _tile(S, 1024, 8)

    def body(p_ref, w_ref, s_ref, y_ref):
        y_ref[...] = _dot(p_ref[...], w_ref[0]) * s_ref[...]

    return _call(body, name=f"pool_mix_fwd_{S}", grid=(S // ts, G),
                 in_specs=[pl.BlockSpec((ts, cg), lambda i, g: (i, g)), pl.BlockSpec((1, cg, cg), lambda i, g: (g, 0, 0)),
                           pl.BlockSpec((1, cg), lambda i, g: (0, g))],
                 out_specs=pl.BlockSpec((ts, cg), lambda i, g: (i, g)),
                 out_shape=jax.ShapeDtypeStruct((S, D), f32), compiler_params=_cp("parallel", "parallel"))(
                     p, w, scale.reshape(1, D))


def _pool_mix_bwd(p, w, scale, dy):
    S, D = p.shape
    G, cg, _ = w.shape
    ts = _tile(S, 1024, 8)

    def body(p_ref, w_ref, s_ref, dy_ref, dp_ref, dw_ref, ds_ref):
        @pl.when(pl.program_id(1) == 0)
        def _():
            dw_ref[...] = jnp.zeros_like(dw_ref)
            ds_ref[...] = jnp.zeros_like(ds_ref)

        pv, wv, dyv = p_ref[...], w_ref[0], dy_ref[...]
        ypre = _dot(pv, wv)
        ds_ref[...] += jnp.sum(dyv * ypre, axis=0, keepdims=True)
        dyp = (dyv * s_ref[...]).astype(_MXU)
        dp_ref[...] = _dot(dyp, wv, _NT)
        dw_ref[0] += _dot(pv, dyp, _TN)

    dp, dw, ds = _call(
        body, name=f"pool_mix_bwd_{S}", grid=(G, S // ts),
        in_specs=[pl.BlockSpec((ts, cg), lambda g, i: (i, g)), pl.BlockSpec((1, cg, cg), lambda g, i: (g, 0, 0)),
                  pl.BlockSpec((1, cg), lambda g, i: (0, g)), pl.BlockSpec((ts, cg), lambda g, i: (i, g))],
        out_specs=[pl.BlockSpec((ts, cg), lambda g, i: (i, g)), pl.BlockSpec((1, cg, cg), lambda g, i: (g, 0, 0)),
                   pl.BlockSpec((1, cg), lambda g, i: (0, g))],
        out_shape=[jax.ShapeDtypeStruct((S, D), f32), jax.ShapeDtypeStruct((G, cg, cg), f32),
                   jax.ShapeDtypeStruct((1, D), f32)],
        compiler_params=_cp("parallel", "arbitrary"))(p, w, scale.reshape(1, D), dy)
    return dp, dw, ds.reshape(D)


def _sb_tile(S):
    return min(256, max(128, S // 4))


def _sb_scores(qm, ks, tri, R, U, scale):
    z = _dot(qm, ks, _NT) * scale
    soft = jnp.log(1.0 + jnp.exp(-jnp.abs(z)))
    lm = jnp.minimum(-z, 0.0) - soft
    lb = jnp.minimum(z, 0.0) - soft
    if tri is not None:
        lm = jnp.where(tri, lm, 0.0)
    c = _dot_hilo(lm, U)
    a = jnp.exp(lb + c + R)
    if tri is not None:
        a = jnp.where(tri, a, 0.0)
    return lm, lb, a


def _sb_consts(T):
    lane = lax.broadcasted_iota(jnp.int32, (1, _LANES), 1)
    row, col = lax.broadcasted_iota(jnp.int32, (T, T), 0), lax.broadcasted_iota(jnp.int32, (T, T), 1)
    U = jnp.where(row > col, 1.0, 0.0).astype(_MXU)
    heads = [(lane >= SB_HEAD_DIM * h) & (lane < SB_HEAD_DIM * (h + 1)) for h in range(_LANES // SB_HEAD_DIM)]
    return heads, col < row, U


def _sb_fwd(qkv):
    S, D3 = qkv.shape
    D = D3 // 3
    HP = D // _LANES
    T = _sb_tile(S)
    scale = SB_HEAD_DIM ** -0.5

    def body(q_ref, k_ref, v_ref, o_ref):
        i = pl.program_id(1)
        q = q_ref[...]
        heads, tri, U = _sb_consts(T)
        qms = [jnp.where(hm, q, jnp.zeros_like(q)) for hm in heads]

        def tile(kb, carry, mask):
            off = pl.multiple_of(kb * T, T)
            ks, vs = k_ref[pl.ds(off, T), :], v_ref[pl.ds(off, T), :]
            new = []
            for h, qm in enumerate(qms):
                R, acc = carry[2 * h], carry[2 * h + 1]
                lm, _, a = _sb_scores(qm, ks, mask, R, U, scale)
                new += [R + jnp.sum(lm, axis=1, keepdims=True), acc + _dot(a.astype(_MXU), vs)]
            return tuple(new)

        zero = (jnp.zeros((T, 1), f32), jnp.zeros((T, _LANES), f32)) * len(heads)
        carry = tile(i, zero, tri)
        carry = lax.fori_loop(1, i + 1, lambda j, c: tile(i - j, c, None), carry)
        out = jnp.zeros((T, _LANES), f32)
        for h, hm in enumerate(heads):
            out = out + jnp.where(hm, carry[2 * h + 1], 0.0)
        o_ref[...] = out

    return _call(
        body, name=f"sb_fwd_{S}", grid=(HP, S // T),
        in_specs=[pl.BlockSpec((T, _LANES), lambda hp, i: (i, hp)), pl.BlockSpec((S, _LANES), lambda hp, i: (0, HP + hp)),
                  pl.BlockSpec((S, _LANES), lambda hp, i: (0, 2 * HP + hp))],
        out_specs=pl.BlockSpec((T, _LANES), lambda hp, i: (i, hp)),
        out_shape=jax.ShapeDtypeStruct((S, D), f32), compiler_params=_cp("parallel", "parallel"))(qkv, qkv, qkv)


def _sb_bwd(qkv, o, do):
    S, D3 = qkv.shape
    D = D3 // 3
    HP = D // _LANES
    T = _sb_tile(S)
    nq = S // T
    scale = SB_HEAD_DIM ** -0.5

    def body(q_ref, k_ref, v_ref, o_ref, do_ref, dq_ref, dk_ref, dv_ref, dk_acc, dv_acc):
        i = pl.program_id(1)

        @pl.when(i == 0)
        def _():
            dk_acc[...] = jnp.zeros_like(dk_acc)
            dv_acc[...] = jnp.zeros_like(dv_acc)

        q = q_ref[...]
        dob = do_ref[...].astype(_MXU)
        prod = dob.astype(f32) * o_ref[...]
        heads, tri, U = _sb_consts(T)
        qms = [jnp.where(hm, q, jnp.zeros_like(q)) for hm in heads]
        doms = [jnp.where(hm, dob, jnp.zeros_like(dob)) for hm in heads]
        totals = [jnp.sum(jnp.where(hm, prod, 0.0), axis=1, keepdims=True) for hm in heads]

        def tile(kb, carry, mask):
            off = pl.multiple_of(kb * T, T)
            ks, vs = k_ref[pl.ds(off, T), :], v_ref[pl.ds(off, T), :]
            new = []
            dk_t = jnp.zeros((T, _LANES), f32)
            dv_t = jnp.zeros((T, _LANES), f32)
            for h, (qm, dom, total) in enumerate(zip(qms, doms, totals)):
                R, Gs, dq = carry[3 * h:3 * h + 3]
                lm, lb, a = _sb_scores(qm, ks, mask, R, U, scale)
                ab = a.astype(_MXU)
                g = ab.astype(f32) * _dot(dom, vs, _NT)
                before = total - (g + _dot_hilo(g, U) + Gs)
                beta = jnp.exp(lb)
                dz = (g * (1.0 - beta) - before * beta) * scale
                if mask is not None:
                    dz = jnp.where(mask, dz, 0.0)
                dzb = dz.astype(_MXU)
                dk_t = dk_t + _dot(dzb, qm, _TN)
                dv_t = dv_t + _dot(ab, dom, _TN)
                new += [R + jnp.sum(lm, axis=1, keepdims=True), Gs + jnp.sum(g, axis=1, keepdims=True), dq + _dot(dzb, ks)]
            dk_acc[pl.ds(off, T), :] += dk_t
            dv_acc[pl.ds(off, T), :] += dv_t
            return tuple(new)

        zero1 = jnp.zeros((T, 1), f32)
        carry = tile(i, (zero1, zero1, jnp.zeros((T, _LANES), f32)) * len(heads), tri)
        carry = lax.fori_loop(1, i + 1, lambda j, c: tile(i - j, c, None), carry)
        dq_out = jnp.zeros((T, _LANES), f32)
        for h, hm in enumerate(heads):
            dq_out = dq_out + jnp.where(hm, carry[3 * h + 2], 0.0)
        dq_ref[...] = dq_out.astype(dq_ref.dtype)

        @pl.when(i == nq - 1)
        def _():
            dk_ref[...] = dk_acc[...].astype(dk_ref.dtype)
            dv_ref[...] = dv_acc[...].astype(dv_ref.dtype)

    qb = pl.BlockSpec((T, _LANES), lambda hp, i: (i, hp))
    col = pl.BlockSpec((S, _LANES), lambda hp, i: (0, hp))
    out = jax.ShapeDtypeStruct((S, D), _MXU)
    return _call(
        body, name=f"sb_bwd_{S}", grid=(HP, nq),
        in_specs=[qb, pl.BlockSpec((S, _LANES), lambda hp, i: (0, HP + hp)),
                  pl.BlockSpec((S, _LANES), lambda hp, i: (0, 2 * HP + hp)), qb, qb],
        out_specs=[qb, col, col], out_shape=[out, out, out],
        scratch_shapes=[pltpu.VMEM((S, _LANES), f32), pltpu.VMEM((S, _LANES), f32)],
        compiler_params=_cp("parallel", "arbitrary"))(qkv, qkv, qkv, o, do)


def _cmul(ar, ai, br, bi):
    return ar * br - ai * bi, ar * bi + ai * br


def _s5_coef(ar, ai, ldt):
    dt = jnp.exp(ldt)
    e = jnp.exp(ar * dt)
    abr, abi = e * jnp.cos(ai * dt), e * jnp.sin(ai * dt)
    inv = 1.0 / (ar * ar + ai * ai)
    cr, ci = _cmul(abr - 1.0, abi, ar * inv, -ai * inv)
    return dt, abr, abi, inv, cr, ci


def _s5_prep_fwd(ar, ai, ldt, br, bi):
    N = ar.shape[1]
    cg = br.shape[0]

    def body(ar_ref, ai_ref, ldt_ref, br_ref, bi_ref, abr_ref, abi_ref, bbr_ref, bbi_ref):
        _, abr, abi, _, cr, ci = _s5_coef(ar_ref[...], ai_ref[...], ldt_ref[...])
        abr_ref[...], abi_ref[...] = abr, abi
        bbr_ref[...], bbi_ref[...] = _cmul(cr, ci, br_ref[...], bi_ref[...])

    v, m = jax.ShapeDtypeStruct((1, N), f32), jax.ShapeDtypeStruct((cg, N), f32)
    return _call(body, name="s5_prep_fwd", out_shape=[v, v, m, m])(ar, ai, ldt, br, bi)


def _s5_prep_bwd(ar, ai, ldt, br, bi, dabr, dabi, dbbr, dbbi):
    N = ar.shape[1]
    cg = br.shape[0]

    def body(ar_ref, ai_ref, ldt_ref, br_ref, bi_ref, dabr_ref, dabi_ref, dbbr_ref, dbbi_ref,
             dar_ref, dai_ref, dldt_ref, dbr_ref, dbi_ref):
        a_r, a_i = ar_ref[...], ai_ref[...]
        dt, abr, abi, inv, cr, ci = _s5_coef(a_r, a_i, ldt_ref[...])
        b_r, b_i, gr, gi = br_ref[...], bi_ref[...], dbbr_ref[...], dbbi_ref[...]
        dbr_ref[...], dbi_ref[...] = _cmul(cr, -ci, gr, gi)
        dcr = jnp.sum(gr * b_r + gi * b_i, axis=0, keepdims=True)
        dci = jnp.sum(gi * b_r - gr * b_i, axis=0, keepdims=True)
        ilr, ili = a_r * inv, -a_i * inv
        dwr, dwi = _cmul(ilr, -ili, dcr, dci)
        qr, qi = _cmul(cr, ci, ilr, ili)
        dl1r, dl1i = _cmul(-qr, qi, dcr, dci)
        tr, ti = dabr_ref[...] + dwr, dabi_ref[...] + dwi
        ddlr, ddli = _cmul(abr, -abi, tr, ti)
        dar_ref[...] = dl1r + ddlr * dt
        dai_ref[...] = dl1i + ddli * dt
        dldt_ref[...] = (a_r * ddlr + a_i * ddli) * dt

    v, m = jax.ShapeDtypeStruct((1, N), f32), jax.ShapeDtypeStruct((cg, N), f32)
    return _call(body, name="s5_prep_bwd", out_shape=[v, v, v, m, m])(ar, ai, ldt, br, bi, dabr, dabi, dbbr, dbbi)


def _s5_in_fwd(u, Br, Bi):
    S, D = u.shape
    nb, _, nw = Br.shape
    ts = _tile(S, 512, 8)

    def body(u_ref, br_ref, bi_ref, or_ref, oi_ref):
        uv = u_ref[...]
        or_ref[...] = _dot3(uv, br_ref[0])
        oi_ref[...] = _dot3(uv, bi_ref[0])

    ub = pl.BlockSpec((ts, _LANES), lambda i, k: (i, k))
    wb = pl.BlockSpec((1, _LANES, nw), lambda i, k: (k, 0, 0))
    ob = pl.BlockSpec((ts, nw), lambda i, k: (i, k))
    o = jax.ShapeDtypeStruct((S, nb * nw), f32)
    return _call(body, name=f"s5_in_fwd_{S}", grid=(S // ts, nb), in_specs=[ub, wb, wb], out_specs=[ob, ob],
                 out_shape=[o, o], compiler_params=_cp("parallel", "parallel"))(u, Br, Bi)


def _s5_scan_fwd(bur, bui, abr, abi):
    S, N = bur.shape
    tt, tn = _tile(S, 256, 8), _tile(N, 1024)

    def body(br_ref, bi_ref, ar_ref, ai_ref, xr_ref, xi_ref, sr, si):
        @pl.when(pl.program_id(1) == 0)
        def _():
            sr[...] = jnp.zeros_like(sr)
            si[...] = jnp.zeros_like(si)

        a_r, a_i = ar_ref[...], ai_ref[...]

        def step(t, carry):
            xr, xi = carry
            nr = a_r * xr - a_i * xi + br_ref[pl.ds(t, 1), :]
            ni = a_r * xi + a_i * xr + bi_ref[pl.ds(t, 1), :]
            xr_ref[pl.ds(t, 1), :] = nr
            xi_ref[pl.ds(t, 1), :] = ni
            return nr, ni

        xr, xi = lax.fori_loop(0, tt, step, (sr[...], si[...]))
        sr[...], si[...] = xr, xi

    blk = pl.BlockSpec((tt, tn), lambda n, i: (i, n))
    vec = pl.BlockSpec((1, tn), lambda n, i: (0, n))
    o = jax.ShapeDtypeStruct((S, N), f32)
    return _call(body, name=f"s5_scan_fwd_{S}", grid=(N // tn, S // tt), in_specs=[blk, blk, vec, vec],
                 out_specs=[blk, blk], out_shape=[o, o],
                 scratch_shapes=[pltpu.VMEM((1, tn), f32), pltpu.VMEM((1, tn), f32)],
                 compiler_params=_cp("parallel", "arbitrary"))(bur, bui, abr, abi)


def _s5_scan_bwd(dr, di, xr, xi, abr, abi):
    S, N = dr.shape
    tt, tn = _tile(S, 256, 8), _tile(N, 1024)
    nt = S // tt

    def body(dr_ref, di_ref, xr_ref, xi_ref, ar_ref, ai_ref, gr_ref, gi_ref, dar_ref, dai_ref, sr, si):
        @pl.when(pl.program_id(1) == 0)
        def _():
            sr[...] = jnp.zeros_like(sr)
            si[...] = jnp.zeros_like(si)
            dar_ref[...] = jnp.zeros_like(dar_ref)
            dai_ref[...] = jnp.zeros_like(dai_ref)

        a_r, a_i = ar_ref[...], ai_ref[...]

        def step(j, carry):
            gr, gi, accr, acci = carry
            t = tt - 1 - j
            xr_t, xi_t = xr_ref[pl.ds(t, 1), :], xi_ref[pl.ds(t, 1), :]
            accr = accr + gr * xr_t + gi * xi_t
            acci = acci + gi * xr_t - gr * xi_t
            nr = dr_ref[pl.ds(t, 1), :] + a_r * gr + a_i * gi
            ni = di_ref[pl.ds(t, 1), :] + a_r * gi - a_i * gr
            gr_ref[pl.ds(t, 1), :] = nr
            gi_ref[pl.ds(t, 1), :] = ni
            return nr, ni, accr, acci

        gr, gi, accr, acci = lax.fori_loop(0, tt, step, (sr[...], si[...], dar_ref[...], dai_ref[...]))
        sr[...], si[...] = gr, gi
        dar_ref[...], dai_ref[...] = accr, acci

    blk = pl.BlockSpec((tt, tn), lambda n, i: (nt - 1 - i, n))
    vec = pl.BlockSpec((1, tn), lambda n, i: (0, n))
    o, v = jax.ShapeDtypeStruct((S, N), f32), jax.ShapeDtypeStruct((1, N), f32)
    return _call(body, name=f"s5_scan_bwd_{S}", grid=(N // tn, nt), in_specs=[blk, blk, blk, blk, vec, vec],
                 out_specs=[blk, blk, vec, vec], out_shape=[o, o, v, v],
                 scratch_shapes=[pltpu.VMEM((1, tn), f32), pltpu.VMEM((1, tn), f32)],
                 compiler_params=_cp("parallel", "arbitrary"))(dr, di, xr, xi, abr, abi)


def _s5_out_fwd(xr, xi, Cr, Ci, u, d):
    S, N = xr.shape
    nb, nw, _ = Cr.shape
    D = u.shape[1]
    ts = _tile(S, 512, 8)

    def body(xr_ref, xi_ref, cr_ref, ci_ref, u_ref, d_ref, y_ref, z_ref):
        y = _dot3(xr_ref[...], cr_ref[0]) - _dot3(xi_ref[...], ci_ref[0]) + d_ref[...] * u_ref[...]
        y_ref[...] = y
        z_ref[...] = _gelu(y).astype(z_ref.dtype)

    xb = pl.BlockSpec((ts, nw), lambda i, k: (i, k))
    cb = pl.BlockSpec((1, nw, _LANES), lambda i, k: (k, 0, 0))
    ub = pl.BlockSpec((ts, _LANES), lambda i, k: (i, k))
    db = pl.BlockSpec((1, _LANES), lambda i, k: (0, k))
    return _call(body, name=f"s5_out_fwd_{S}", grid=(S // ts, nb), in_specs=[xb, xb, cb, cb, ub, db],
                 out_specs=[ub, ub], out_shape=[jax.ShapeDtypeStruct((S, D), f32), jax.ShapeDtypeStruct((S, D), _MXU)],
                 compiler_params=_cp("parallel", "parallel"))(xr, xi, Cr, Ci, u, d.reshape(1, D))


def _s5_out_bwd(dz, y, u, d, Cr, Ci):
    S, D = y.shape
    nb, nw, _ = Cr.shape
    ts = _tile(S, 512, 8)

    def body(dz_ref, y_ref, u_ref, d_ref, cr_ref, ci_ref, dy_ref, gr_ref, gi_ref, du_ref, dd_ref):
        @pl.when(pl.program_id(1) == 0)
        def _():
            dd_ref[...] = jnp.zeros_like(dd_ref)

        dy = dz_ref[...] * _gelu_grad(y_ref[...])
        dy_ref[...] = dy
        gr_ref[...] = _dot3(dy, cr_ref[0], _NT)
        gi_ref[...] = -_dot3(dy, ci_ref[0], _NT)
        du_ref[...] = dy * d_ref[...]
        dd_ref[...] += jnp.sum(dy * u_ref[...], axis=0, keepdims=True)

    xb = pl.BlockSpec((ts, nw), lambda k, i: (i, k))
    cb = pl.BlockSpec((1, nw, _LANES), lambda k, i: (k, 0, 0))
    ub = pl.BlockSpec((ts, _LANES), lambda k, i: (i, k))
    db = pl.BlockSpec((1, _LANES), lambda k, i: (0, k))
    a, s = jax.ShapeDtypeStruct((S, D), f32), jax.ShapeDtypeStruct((S, nb * nw), f32)
    dy, gr, gi, du, dd = _call(
        body, name=f"s5_out_bwd_{S}", grid=(nb, S // ts), in_specs=[ub, ub, ub, db, cb, cb],
        out_specs=[ub, xb, xb, ub, db], out_shape=[a, s, s, a, jax.ShapeDtypeStruct((1, D), f32)],
        compiler_params=_cp("parallel", "arbitrary"))(dz, y, u, d.reshape(1, D), Cr, Ci)
    return dy, gr, gi, du, dd.reshape(D)


def _s5_in_bwd(gr, gi, Br, Bi, du0):
    S, N = gr.shape
    nb, _, nw = Br.shape
    ts = _tile(S, 512, 8)

    def body(gr_ref, gi_ref, br_ref, bi_ref, d0_ref, du_ref):
        du_ref[...] = d0_ref[...] + _dot3(gr_ref[...], br_ref[0], _NT) + _dot3(gi_ref[...], bi_ref[0], _NT)

    xb = pl.BlockSpec((ts, nw), lambda i, k: (i, k))
    wb = pl.BlockSpec((1, _LANES, nw), lambda i, k: (k, 0, 0))
    ub = pl.BlockSpec((ts, _LANES), lambda i, k: (i, k))
    return _call(body, name=f"s5_in_bwd_{S}", grid=(S // ts, nb), in_specs=[xb, xb, wb, wb, ub], out_specs=ub,
                 out_shape=jax.ShapeDtypeStruct((S, nb * _LANES), f32), compiler_params=_cp("parallel", "parallel"))(
                     gr, gi, Br, Bi, du0)


def _s5_wgrad(u, gr, gi, xr, xi, dy):
    S, D = u.shape
    nb = D // _LANES
    nw = gr.shape[1] // nb
    ts = _tile(S, 512, 8)

    def body(u_ref, gr_ref, gi_ref, xr_ref, xi_ref, dy_ref, dbr_ref, dbi_ref, dcr_ref, dci_ref):
        @pl.when(pl.program_id(1) == 0)
        def _():
            for r in (dbr_ref, dbi_ref, dcr_ref, dci_ref):
                r[...] = jnp.zeros_like(r)

        uv, dyv = u_ref[...], dy_ref[...]
        dbr_ref[0] += _dot3(uv, gr_ref[...], _TN)
        dbi_ref[0] += _dot3(uv, gi_ref[...], _TN)
        dcr_ref[0] += _dot3(xr_ref[...], dyv, _TN)
        dci_ref[0] -= _dot3(xi_ref[...], dyv, _TN)

    xb = pl.BlockSpec((ts, nw), lambda k, i: (i, k))
    ub = pl.BlockSpec((ts, _LANES), lambda k, i: (i, k))
    wb = pl.BlockSpec((1, _LANES, nw), lambda k, i: (k, 0, 0))
    cb = pl.BlockSpec((1, nw, _LANES), lambda k, i: (k, 0, 0))
    w, c = jax.ShapeDtypeStruct((nb, _LANES, nw), f32), jax.ShapeDtypeStruct((nb, nw, _LANES), f32)
    return _call(body, name=f"s5_wgrad_{S}", grid=(nb, S // ts), in_specs=[ub, xb, xb, xb, xb, ub],
                 out_specs=[wb, wb, cb, cb], out_shape=[w, w, c, c],
                 compiler_params=_cp("parallel", "arbitrary"))(u, gr, gi, xr, xi, dy)


def _glu_fwd(vg):
    S, D2 = vg.shape
    D = D2 // 2
    ts, tc = _tile(S, 1024, 8), _tile(D, 512)
    nc = D // tc

    def body(v_ref, g_ref, o_ref):
        o_ref[...] = v_ref[...] * _sigmoid(g_ref[...])

    return _call(body, name=f"glu_fwd_{S}", grid=(S // ts, nc),
                 in_specs=[pl.BlockSpec((ts, tc), lambda i, j: (i, j)), pl.BlockSpec((ts, tc), lambda i, j: (i, nc + j))],
                 out_specs=pl.BlockSpec((ts, tc), lambda i, j: (i, j)), out_shape=jax.ShapeDtypeStruct((S, D), f32),
                 compiler_params=_cp("parallel", "parallel"))(vg, vg)


def _glu_bwd(vg, dout):
    S, D2 = vg.shape
    D = D2 // 2
    ts, tc = _tile(S, 1024, 8), _tile(D, 512)
    nc = D // tc

    def body(v_ref, g_ref, do_ref, dv_ref, dg_ref):
        sg = _sigmoid(g_ref[...])
        do = do_ref[...]
        dv_ref[...] = (do * sg).astype(dv_ref.dtype)
        dg_ref[...] = (do * v_ref[...] * sg * (1.0 - sg)).astype(dg_ref.dtype)

    blk = pl.BlockSpec((ts, tc), lambda i, j: (i, j))
    o = jax.ShapeDtypeStruct((S, D), _MXU)
    return _call(body, name=f"glu_bwd_{S}", grid=(S // ts, nc),
                 in_specs=[blk, pl.BlockSpec((ts, tc), lambda i, j: (i, nc + j)), blk], out_specs=[blk, blk],
                 out_shape=[o, o], compiler_params=_cp("parallel", "parallel"))(vg, vg, dout)


def _blockdiag_in(b2, nb, P):
    cg = b2.shape[0]
    gb = S5_BLOCK_GROUPS
    t = b2.reshape(cg, nb, gb, P).transpose(1, 0, 2, 3)
    eye = jnp.eye(gb, dtype=b2.dtype)
    return (eye[None, :, None, :, None] * t[:, None]).reshape(nb, gb * cg, gb * P)


def _blockdiag_in_extract(db, cg, P):
    nb = db.shape[0]
    gb = S5_BLOCK_GROUPS
    eye = jnp.eye(gb, dtype=db.dtype)
    t = (db.reshape(nb, gb, cg, gb, P) * eye[None, :, None, :, None]).sum(3)
    return t.transpose(2, 0, 1, 3).reshape(cg, nb * gb * P)


def _blockdiag_out(c, nb):
    G, cg, P = c.shape
    gb = S5_BLOCK_GROUPS
    t = c.reshape(nb, gb, cg, P).transpose(0, 3, 1, 2)
    eye = jnp.eye(gb, dtype=c.dtype)
    return (eye[None, :, None, :, None] * t[:, None]).reshape(nb, gb * P, gb * cg)


def _blockdiag_out_extract(dc, cg, P):
    nb = dc.shape[0]
    gb = S5_BLOCK_GROUPS
    eye = jnp.eye(gb, dtype=dc.dtype)
    t = (dc.reshape(nb, gb, P, gb, cg) * eye[None, :, None, :, None]).sum(1)
    return t.transpose(0, 2, 3, 1).reshape(nb * gb, cg, P)


def _xa_fwd(q, kv):
    S, D = q.shape
    M = kv.shape[0]
    dh = D // XA_HEADS
    ts = _tile(S, 512, 8)
    scale = dh ** -0.5

    def body(q_ref, k_ref, v_ref, o_ref):
        s = _dot(q_ref[...], k_ref[...], _NT) * scale
        e = jnp.exp(s - jnp.max(s, axis=-1, keepdims=True))
        p = e / jnp.sum(e, axis=-1, keepdims=True)
        o_ref[...] = _dot(p.astype(_MXU), v_ref[...]).astype(o_ref.dtype)

    return _call(body, name=f"xa_fwd_{S}", grid=(S // ts, XA_HEADS),
                 in_specs=[pl.BlockSpec((ts, dh), lambda i, h: (i, h)), pl.BlockSpec((M, dh), lambda i, h: (0, h)),
                           pl.BlockSpec((M, dh), lambda i, h: (0, XA_HEADS + h))],
                 out_specs=pl.BlockSpec((ts, dh), lambda i, h: (i, h)), out_shape=jax.ShapeDtypeStruct((S, D), _MXU),
                 compiler_params=_cp("parallel", "parallel"))(q, kv, kv)


def _xa_bwd(q, kv, do):
    S, D = q.shape
    M = kv.shape[0]
    dh = D // XA_HEADS
    ts = _tile(S, 512, 8)
    scale = dh ** -0.5

    def body(q_ref, k_ref, v_ref, do_ref, dq_ref, dk_ref, dv_ref):
        @pl.when(pl.program_id(1) == 0)
        def _():
            dk_ref[...] = jnp.zeros_like(dk_ref)
            dv_ref[...] = jnp.zeros_like(dv_ref)

        qv, kv_, vv, dov = q_ref[...], k_ref[...], v_ref[...], do_ref[...]
        s = _dot(qv, kv_, _NT) * scale
        e = jnp.exp(s - jnp.max(s, axis=-1, keepdims=True))
        p = e / jnp.sum(e, axis=-1, keepdims=True)
        dp = _dot(dov, vv, _NT)
        dv_ref[...] += _dot(p.astype(_MXU), dov, _TN)
        ds = (p * (dp - jnp.sum(dp * p, axis=-1, keepdims=True)) * scale).astype(_MXU)
        dq_ref[...] = _dot(ds, kv_).astype(dq_ref.dtype)
        dk_ref[...] += _dot(ds, qv, _TN)

    qb = pl.BlockSpec((ts, dh), lambda h, i: (i, h))
    mb = pl.BlockSpec((M, dh), lambda h, i: (0, h))
    m = jax.ShapeDtypeStruct((M, D), f32)
    return _call(body, name=f"xa_bwd_{S}", grid=(XA_HEADS, S // ts),
                 in_specs=[qb, mb, pl.BlockSpec((M, dh), lambda h, i: (0, XA_HEADS + h)), qb],
                 out_specs=[qb, mb, mb], out_shape=[jax.ShapeDtypeStruct((S, D), _MXU), m, m],
                 compiler_params=_cp("parallel", "arbitrary"))(q, kv, kv, do)


def _conv(u, cw_ref, cb_ref, rows):
    return cw_ref[2:3, :] * u + cw_ref[1:2, :] * _shift_down(u, 1, rows) + cw_ref[0:1, :] * _shift_down(u, 2, rows) + cb_ref[...]


def _convglu_fwd(u, cw, cb):
    S, F2 = u.shape
    F = F2 // 2
    tc = _tile(F, 128)
    nc = F // tc

    def body(uv_ref, ug_ref, cwv_ref, cwg_ref, cbv_ref, cbg_ref, o_ref):
        rows = lax.broadcasted_iota(jnp.int32, (S, 1), 0)
        val = _conv(uv_ref[...], cwv_ref, cbv_ref, rows)
        gate = _conv(ug_ref[...], cwg_ref, cbg_ref, rows)
        o_ref[...] = (gate * _sigmoid(gate) * val).astype(o_ref.dtype)

    def col(r, off):
        return pl.BlockSpec((r, tc), lambda j: (0, off + j))

    return _call(body, name=f"convglu_fwd_{S}", grid=(nc,),
                 in_specs=[col(S, 0), col(S, nc), col(3, 0), col(3, nc), col(1, 0), col(1, nc)], out_specs=col(S, 0),
                 out_shape=jax.ShapeDtypeStruct((S, F), _MXU), compiler_params=_cp("parallel"))(
                     u, u, cw, cw, cb.reshape(1, F2), cb.reshape(1, F2))


def _convglu_bwd(u, cw, cb, dact):
    S, F2 = u.shape
    F = F2 // 2
    tc = _tile(F, 128)
    nc = F // tc

    def body(uv_ref, ug_ref, cwv_ref, cwg_ref, cbv_ref, cbg_ref, da_ref, duv_ref, dug_ref, dcwv_ref, dcwg_ref, dcbv_ref, dcbg_ref):
        rows = lax.broadcasted_iota(jnp.int32, (S, 1), 0)
        uv, ug = uv_ref[...], ug_ref[...]
        val = _conv(uv, cwv_ref, cbv_ref, rows)
        gate = _conv(ug, cwg_ref, cbg_ref, rows)
        sg = _sigmoid(gate)
        da = da_ref[...]
        dval = da * (gate * sg)
        dgate = da * val * (sg * (1.0 + gate * (1.0 - sg)))
        for uu, d, cw_ref, du_ref, dcw_ref, dcb_ref in ((uv, dval, cwv_ref, duv_ref, dcwv_ref, dcbv_ref),
                                                        (ug, dgate, cwg_ref, dug_ref, dcwg_ref, dcbg_ref)):
            dcb_ref[...] = jnp.sum(d, axis=0, keepdims=True)
            dcw_ref[2:3, :] = jnp.sum(d * uu, axis=0, keepdims=True)
            dcw_ref[1:2, :] = jnp.sum(d * _shift_down(uu, 1, rows), axis=0, keepdims=True)
            dcw_ref[0:1, :] = jnp.sum(d * _shift_down(uu, 2, rows), axis=0, keepdims=True)
            du = cw_ref[2:3, :] * d + cw_ref[1:2, :] * _shift_up(d, 1, rows, S) + cw_ref[0:1, :] * _shift_up(d, 2, rows, S)
            du_ref[...] = du.astype(du_ref.dtype)

    def col(r, off):
        return pl.BlockSpec((r, tc), lambda j: (0, off + j))

    o = jax.ShapeDtypeStruct((S, F), _MXU)
    w, b = jax.ShapeDtypeStruct((3, F), f32), jax.ShapeDtypeStruct((1, F), f32)
    duv, dug, dcwv, dcwg, dcbv, dcbg = _call(
        body, name=f"convglu_bwd_{S}", grid=(nc,),
        in_specs=[col(S, 0), col(S, nc), col(3, 0), col(3, nc), col(1, 0), col(1, nc), col(S, 0)],
        out_specs=[col(S, 0), col(S, 0), col(3, 0), col(3, 0), col(1, 0), col(1, 0)],
        out_shape=[o, o, w, w, b, b], compiler_params=_cp("parallel"))(
            u, u, cw, cw, cb.reshape(1, F2), cb.reshape(1, F2), dact)
    return (jnp.concatenate([duv, dug], axis=1), jnp.concatenate([dcwv, dcwg], axis=1),
            jnp.concatenate([dcbv, dcbg], axis=1).reshape(F2))


def _final_loss(h, delta, g, target):
    S, D = h.shape
    ts = _tile(S, 512, 8)
    row = pl.BlockSpec((ts, D), lambda i: (i, 0))
    vec = pl.BlockSpec((1, D), lambda i: (0, 0))
    one = pl.BlockSpec((1, _LANES), lambda i: (0, 0))

    def body(h_ref, d_ref, g_ref, t_ref, l_ref, dh_ref, dg_ref):
        @pl.when(pl.program_id(0) == 0)
        def _():
            l_ref[...] = jnp.zeros_like(l_ref)
            dg_ref[...] = jnp.zeros_like(dg_ref)

        xv = h_ref[...] + d_ref[...]
        gv = g_ref[...]
        r = lax.rsqrt(jnp.mean(xv * xv, axis=-1, keepdims=True) + EPS)
        xh = xv * r
        err = xh * gv - t_ref[...]
        l_ref[...] += 0.5 * jnp.sum(jnp.mean(err * err, axis=-1, keepdims=True), axis=0, keepdims=True)
        dy = err * (1.0 / D)
        dg_ref[...] += jnp.sum(dy * xh, axis=0, keepdims=True)
        dxh = dy * gv
        dh_ref[...] = r * (dxh - xh * jnp.mean(dxh * xh, axis=-1, keepdims=True))

    loss, dh, dg = _call(body, name=f"final_loss_{S}", grid=(S // ts,), in_specs=[row, row, vec, row],
                         out_specs=[one, row, vec],
                         out_shape=[jax.ShapeDtypeStruct((1, _LANES), f32), jax.ShapeDtypeStruct((S, D), f32),
                                    jax.ShapeDtypeStruct((1, D), f32)],
                         compiler_params=_cp("arbitrary"))(h, delta, g.reshape(1, D), target)
    return loss[0, 0], dh, dg.reshape(D)


def _adamw(w, g, m, v):
    R, C = w.shape
    tr = _tile(R, max(8, (1 << 19) // C // 8 * 8), 8)
    blk = pl.BlockSpec((tr, C), lambda i: (i, 0))

    def body(w_ref, g_ref, m_ref, v_ref, d_ref, nm_ref, nv_ref):
        gv = g_ref[...]
        m_new = ADAM_B1 * m_ref[...] + (1.0 - ADAM_B1) * gv
        v_new = ADAM_B2 * v_ref[...] + (1.0 - ADAM_B2) * (gv * gv)
        m_hat = m_new / (1.0 - ADAM_B1 ** ADAM_STEP)
        v_hat = v_new / (1.0 - ADAM_B2 ** ADAM_STEP)
        d_ref[...] = -ADAM_LR * (m_hat / (jnp.sqrt(v_hat) + ADAM_EPS) + ADAM_WD * w_ref[...])
        nm_ref[...] = m_new
        nv_ref[...] = v_new

    o = jax.ShapeDtypeStruct((R, C), f32)
    return _call(body, name=f"adamw_{R}x{C}", grid=(R // tr,), in_specs=[blk] * 4, out_specs=[blk] * 3,
                 out_shape=[o, o, o], compiler_params=_cp("parallel"))(w, g, m, v)


def _sum_slabs(xs):
    K, R, C = xs.shape
    tr = _tile(R, 256, 8)

    def body(x_ref, o_ref):
        s = x_ref[0]
        for k in range(1, K):
            s = s + x_ref[k]
        o_ref[...] = s

    return _call(body, name=f"sum_slabs_{K}x{R}", grid=(R // tr,),
                 in_specs=[pl.BlockSpec((K, tr, C), lambda i: (0, i, 0))], out_specs=pl.BlockSpec((tr, C), lambda i: (i, 0)),
                 out_shape=jax.ShapeDtypeStruct((R, C), f32), compiler_params=_cp("parallel"))(xs)


def _slab_rows(R, C):
    return _tile(R, max(8, (1 << 19) // C // 8 * 8), 8)


def _add_partial(g, recv, core):
    _, R, C = g.shape
    tr = _slab_rows(R, C)

    def body(c_ref, a_ref, b_ref, o_ref):
        o_ref[...] = a_ref[...] + b_ref[...]

    blk = pl.BlockSpec((1, tr, C), lambda k, i, c: (k, i, 0))
    return _call(
        body, name=f"add_partial_{R}x{C}",
        grid_spec=pltpu.PrefetchScalarGridSpec(
            num_scalar_prefetch=1, grid=(4, R // tr),
            in_specs=[pl.BlockSpec((1, tr, C), lambda k, i, c: (2 * k + c[0], i, 0)), blk], out_specs=blk),
        out_shape=jax.ShapeDtypeStruct((4, R, C), f32), compiler_params=_cp("parallel", "parallel"))(core, g, recv)


def _sum_final(p, recv, chip):
    _, R, C = p.shape
    tr = _slab_rows(R, C)

    def body(q_ref, p_ref, r_ref, o_ref):
        o_ref[...] = ((p_ref[0] + r_ref[0]) + r_ref[1]) + r_ref[2]

    return _call(
        body, name=f"sum_final_{R}x{C}",
        grid_spec=pltpu.PrefetchScalarGridSpec(
            num_scalar_prefetch=1, grid=(R // tr,),
            in_specs=[pl.BlockSpec((1, tr, C), lambda i, q: (q[0], i, 0)), pl.BlockSpec((3, tr, C), lambda i, q: (0, i, 0))],
            out_specs=pl.BlockSpec((tr, C), lambda i, q: (i, 0))),
        out_shape=jax.ShapeDtypeStruct((R, C), f32), compiler_params=_cp("parallel"))(chip, p, recv)


def _my_pos():
    return lax.axis_index("x"), lax.axis_index("y"), lax.axis_index("c")


def _all_gather(shards):
    n = len(shards)

    def body(*refs):
        x_refs, out_refs = refs[:n], refs[n:2 * n]
        send_sems, recv_sems, local_sems = refs[2 * n:]
        x, y, c = _my_pos()
        me, sibling = (x, y, c), (x, y, 1 - c)
        chips = [(1 - x, y), (x, 1 - y), (1 - x, 1 - y)]

        def slab(t, px, py, pc):
            return out_refs[t].at[4 * px + 2 * py + pc]

        def copy(k, t, block, to, from_input=False):
            return pltpu.make_async_remote_copy(
                src_ref=x_refs[t] if from_input else slab(t, *block), dst_ref=slab(t, *block),
                send_sem=send_sems.at[k, t], recv_sem=recv_sems.at[k, t], device_id=to, device_id_type=MESH_ID)

        mine = [pltpu.make_async_copy(x_refs[t], slab(t, *me), local_sems.at[t]) for t in range(n)]
        first = [copy(0, t, me, sibling, True) for t in range(n)]
        first += [copy(1 + j, t, me, (*chip, c), True) for j, chip in enumerate(chips) for t in range(n)]
        for cp in mine + first:
            cp.start()
        passed = []
        for j, chip in enumerate(chips):
            for t in range(n):
                copy(1 + j, t, (*chip, c), me).wait_recv()
                passed.append(copy(4 + j, t, (*chip, c), sibling))
                passed[-1].start()
        for t in range(n):
            copy(0, t, sibling, me).wait_recv()
        for j, chip in enumerate(chips):
            for t in range(n):
                copy(4 + j, t, (*chip, 1 - c), me).wait_recv()
        for cp in first + passed:
            cp.wait_send()
        for cp in mine:
            cp.wait()

    tag = "_".join(f"{s.shape[0]}x{s.shape[1]}" for s in shards)
    return _call(body, name=f"all_gather_{tag}", in_specs=[ANY] * n, out_specs=[ANY] * n,
                 out_shape=[jax.ShapeDtypeStruct((NDEV,) + s.shape, s.dtype) for s in shards],
                 scratch_shapes=[pltpu.SemaphoreType.DMA((7, n)), pltpu.SemaphoreType.DMA((7, n)),
                                 pltpu.SemaphoreType.DMA((n,))])(*shards)


def _exchange_cores(gs):
    n = len(gs)

    def body(*refs):
        g_refs, out_refs = refs[:n], refs[n:2 * n]
        send_sems, recv_sems = refs[2 * n:]
        x, y, c = _my_pos()
        cps = [pltpu.make_async_remote_copy(src_ref=g_refs[t].at[2 * q + (1 - c)], dst_ref=out_refs[t].at[q],
                                            send_sem=send_sems.at[q, t], recv_sem=recv_sems.at[q, t],
                                            device_id=(x, y, 1 - c), device_id_type=MESH_ID)
               for t in range(n) for q in range(4)]
        for cp in cps:
            cp.start()
        for cp in cps:
            cp.wait()

    tag = "_".join(f"{g.shape[1]}x{g.shape[2]}" for g in gs)
    return _call(body, name=f"exchange_cores_{tag}", in_specs=[ANY] * n, out_specs=[ANY] * n,
                 out_shape=[jax.ShapeDtypeStruct((4,) + g.shape[1:], g.dtype) for g in gs],
                 scratch_shapes=[pltpu.SemaphoreType.DMA((4, n)), pltpu.SemaphoreType.DMA((4, n))])(*gs)


def _exchange_chips(ps):
    n = len(ps)

    def body(*refs):
        p_refs, out_refs = refs[:n], refs[n:2 * n]
        send_sems, recv_sems = refs[2 * n:]
        x, y, c = _my_pos()
        chips = [(x, 1 - y), (1 - x, y), (1 - x, 1 - y)]
        cps = [pltpu.make_async_remote_copy(src_ref=p_refs[t].at[2 * px + py], dst_ref=out_refs[t].at[r],
                                            send_sem=send_sems.at[r, t], recv_sem=recv_sems.at[r, t],
                                            device_id=(px, py, c), device_id_type=MESH_ID)
               for t in range(n) for r, (px, py) in enumerate(chips)]
        for cp in cps:
            cp.start()
        for cp in cps:
            cp.wait()

    tag = "_".join(f"{p.shape[1]}x{p.shape[2]}" for p in ps)
    return _call(body, name=f"exchange_chips_{tag}", in_specs=[ANY] * n, out_specs=[ANY] * n,
                 out_shape=[jax.ShapeDtypeStruct((3,) + p.shape[1:], p.dtype) for p in ps],
                 scratch_shapes=[pltpu.SemaphoreType.DMA((3, n)), pltpu.SemaphoreType.DMA((3, n))])(*ps)


def _pack(arrs, dtype, rows_mult):
    flat = jnp.concatenate([a.astype(dtype).reshape(-1) for a in arrs])
    q = rows_mult * _PACK_COLS
    tot = -(-flat.shape[0] // q) * q
    return jnp.pad(flat, (0, tot - flat.shape[0])).reshape(tot // _PACK_COLS, _PACK_COLS)


def _unpack(flat, shapes):
    out, off = [], 0
    for s in shapes:
        n = math.prod(s)
        out.append(flat[..., off:off + n].reshape(flat.shape[:-1] + tuple(s)))
        off += n
    return out


def _width_groups(names, shapes):
    groups = {}
    for n in names:
        groups.setdefault(shapes[n][-1], []).append(n)
    return list(groups.values())


def _to_full(piece, ax):
    t = jnp.moveaxis(piece, 0, ax)
    return t.reshape(t.shape[:ax] + (t.shape[ax] * t.shape[ax + 1],) + t.shape[ax + 2:])


def _to_shards(g, ax):
    t = g.reshape(g.shape[:ax] + (NDEV, g.shape[ax] // NDEV) + g.shape[ax + 1:])
    return jnp.moveaxis(t, ax, 0)


def _split_rows(buf, shapes):
    out, off = [], 0
    for s in shapes:
        r = math.prod(s[:-1])
        out.append(buf[..., off:off + r, :].reshape(buf.shape[:-2] + tuple(s)))
        off += r
    return out


def _gather_weights(shards):
    shapes = {n: shards[n].shape for n in SHARD_AXIS}
    groups = _width_groups(MXU_WEIGHTS, shapes)
    bufs = [jnp.concatenate([shards[n].astype(_MXU).reshape(-1, shapes[n][-1]) for n in grp], axis=0) for grp in groups]
    outs = _all_gather(bufs + [_pack([shards[n] for n in VEC_WEIGHTS], f32, 8)])
    full = {}
    for grp, g in zip(groups, outs):
        for n, piece in zip(grp, _split_rows(g, [shapes[n] for n in grp])):
            full[n] = _to_full(piece, SHARD_AXIS[n])
    for n, piece in zip(VEC_WEIGHTS, _unpack(outs[-1].reshape(NDEV, -1), [shapes[n] for n in VEC_WEIGHTS])):
        full[n] = _to_full(piece, SHARD_AXIS[n])
    return full


def _reduce_scatter_grads(grads, shapes):
    groups = _width_groups(MXU_WEIGHTS, shapes)
    bufs = [jnp.concatenate([_to_shards(grads[n], SHARD_AXIS[n]).reshape(NDEV, -1, shapes[n][-1]) for n in grp], axis=1)
            for grp in groups]
    small = jnp.concatenate([_to_shards(grads[n], SHARD_AXIS[n]).reshape(NDEV, -1) for n in VEC_WEIGHTS], axis=1)
    q = 8 * _PACK_COLS
    tot = -(-small.shape[1] // q) * q
    bufs.append(jnp.pad(small, ((0, 0), (0, tot - small.shape[1]))).reshape(NDEV, tot // _PACK_COLS, _PACK_COLS))
    core = jnp.reshape(lax.axis_index("c"), (1,)).astype(jnp.int32)
    chip = jnp.reshape(2 * lax.axis_index("x") + lax.axis_index("y"), (1,)).astype(jnp.int32)
    from_sibling = _exchange_cores(bufs)
    chip_sums = [_add_partial(b, r, core) for b, r in zip(bufs, from_sibling)]
    from_chips = _exchange_chips(chip_sums)
    totals = [_sum_final(p, r, chip) for p, r in zip(chip_sums, from_chips)]
    out = {}
    for grp, t in zip(groups, totals):
        out.update(zip(grp, _split_rows(t, [shapes[n] for n in grp])))
    out.update(zip(VEC_WEIGHTS, _unpack(totals[-1].reshape(-1), [shapes[n] for n in VEC_WEIGHTS])))
    return out


def _all_reduce_small(grads, names):
    shapes = [grads[n].shape for n in names]
    packed = _pack([grads[n] for n in names], f32, 8)
    total = _sum_slabs(_all_gather([packed])[0])
    return dict(zip(names, _unpack(total.reshape(-1), shapes)))


def _as2d(a):
    if a.ndim == 1:
        return a.reshape(1, -1)
    return a.reshape(-1, a.shape[-1])


def kernel(x, mem, mix_norm_g, pool_w, pool_scale, sb_w_qkv, sb_w_o, s5_a_re, s5_a_im, s5_log_dt, s5_b_re, s5_b_im, s5_c_re, s5_c_im, s5_d, s5_w_glu, xa_norm_g, mem_norm_g, xa_wq, xa_wkv, xa_wo, ffn_norm_g, ffn_w_up, ffn_conv_w, ffn_conv_b, ffn_w_down, final_norm_g, loss_target, m_mix_norm_g, m_pool_w, m_pool_scale, m_sb_w_qkv, m_sb_w_o, m_s5_a_re, m_s5_a_im, m_s5_log_dt, m_s5_b_re, m_s5_b_im, m_s5_c_re, m_s5_c_im, m_s5_d, m_s5_w_glu, m_xa_norm_g, m_mem_norm_g, m_xa_wq, m_xa_wkv, m_xa_wo, m_ffn_norm_g, m_ffn_w_up, m_ffn_conv_w, m_ffn_conv_b, m_ffn_w_down, m_final_norm_g, v_mix_norm_g, v_pool_w, v_pool_scale, v_sb_w_qkv, v_sb_w_o, v_s5_a_re, v_s5_a_im, v_s5_log_dt, v_s5_b_re, v_s5_b_im, v_s5_c_re, v_s5_c_im, v_s5_d, v_s5_w_glu, v_xa_norm_g, v_mem_norm_g, v_xa_wq, v_xa_wkv, v_xa_wo, v_ffn_norm_g, v_ffn_w_up, v_ffn_conv_w, v_ffn_conv_b, v_ffn_w_down, v_final_norm_g):
    args = locals()
    w = {n: args[n] for n in WEIGHTS}
    mom = {n: args["m_" + n] for n in WEIGHTS}
    var = {n: args["v_" + n] for n in WEIGHTS}
    h0, memv, target = x[0], mem[0], loss_target[0]
    S, D = h0.shape
    depth = mix_norm_g.shape[0]
    n_mix = 3

    full = dict(w)
    full.update(_gather_weights(w))

    grads = {}

    def acc(name, j, val):
        grads.setdefault(name, {})[j] = val

    s5 = []
    for j in range(s5_a_re.shape[0]):
        G, P = s5_a_re.shape[1:]
        cg = s5_b_re.shape[3]
        N = G * P
        nb = D // _LANES
        ar, ai = s5_a_re[j].reshape(1, N), s5_a_im[j].reshape(1, N)
        ldt = jnp.repeat(s5_log_dt[j], P).reshape(1, N)
        br = s5_b_re[j].transpose(2, 0, 1).reshape(cg, N)
        bi = s5_b_im[j].transpose(2, 0, 1).reshape(cg, N)
        abr, abi, bbr, bbi = _s5_prep_fwd(ar, ai, ldt, br, bi)
        s5.append(dict(ar=ar, ai=ai, ldt=ldt, br=br, bi=bi, abr=abr, abi=abi, G=G, P=P, cg=cg, N=N, nb=nb,
                       Br=_blockdiag_in(bbr, nb, P), Bi=_blockdiag_in(bbi, nb, P),
                       Cr=_blockdiag_out(s5_c_re[j], nb), Ci=_blockdiag_out(s5_c_im[j], nb)))

    saved = []
    h, delta = h0, None
    for i in range(depth):
        kind, j = i % n_mix, i // n_mix
        sv = dict(kind=kind, j=j)
        if delta is None:
            hn = _rms_fwd(h, full['mix_norm_g'][i])
        else:
            h, hn = _rms_fwd(h, full['mix_norm_g'][i], delta)
        sv['h'] = h
        if kind == 0:
            p = _pool_fwd(hn)
            t = _pool_mix_fwd(p, full['pool_w'][j], full['pool_scale'][j])
            sv.update(p=p)
        elif kind == 1:
            qkv = _mm(hn, full['sb_w_qkv'][j], out_dtype=_MXU)
            o = _sb_fwd(qkv)
            t = _mm(o, full['sb_w_o'][j])
            sv.update(hn=hn, qkv=qkv, o=o)
        else:
            pr = s5[j]
            bur, bui = _s5_in_fwd(hn, pr['Br'], pr['Bi'])
            xr, xi = _s5_scan_fwd(bur, bui, pr['abr'], pr['abi'])
            y, z = _s5_out_fwd(xr, xi, pr['Cr'], pr['Ci'], hn, full['s5_d'][j])
            vg = _mm(z, full['s5_w_glu'][j])
            t = _glu_fwd(vg)
            sv.update(hn=hn, xr=xr, xi=xi, y=y, z=z, vg=vg)
        h1, a = _rms_fwd(h, full['xa_norm_g'][i], t)
        memn = _rms_fwd(memv, full['mem_norm_g'][i])
        q = _mm(a, full['xa_wq'][i], out_dtype=_MXU)
        kv = _mm(memn, full['xa_wkv'][i], out_dtype=_MXU)
        o2 = _xa_fwd(q, kv)
        mo = _mm(o2, full['xa_wo'][i])
        h2, b = _rms_fwd(h1, full['ffn_norm_g'][i], mo)
        u = _mm(b, full['ffn_w_up'][i])
        act = _convglu_fwd(u, full['ffn_conv_w'][i], full['ffn_conv_b'][i])
        delta = _mm(act, full['ffn_w_down'][i])
        sv.update(h1=h1, a=a, memn=memn, q=q, kv=kv, o2=o2, h2=h2, b=b, u=u, act=act)
        saved.append(sv)
        h = h2

    loss_local, dh, dg_final = _final_loss(h, delta, full['final_norm_g'], target)
    loss = lax.psum(loss_local, ("x", "y", "c"))
    grads['final_norm_g'] = dg_final

    for i in reversed(range(depth)):
        sv = saved[i]
        kind, j = sv['kind'], sv['j']
        dact = _mm(dh, full['ffn_w_down'][i], tb=True)
        acc('ffn_w_down', i, _mm(sv['act'], dh, ta=True))
        du, dcw, dcb = _convglu_bwd(sv['u'], full['ffn_conv_w'][i], full['ffn_conv_b'][i], dact)
        acc('ffn_conv_w', i, dcw)
        acc('ffn_conv_b', i, dcb)
        db = _mm(du, full['ffn_w_up'][i], tb=True)
        acc('ffn_w_up', i, _mm(sv['b'], du, ta=True))
        dh2, dg = _rms_bwd(sv['h2'], full['ffn_norm_g'][i], db, dh)
        acc('ffn_norm_g', i, dg)
        do2 = _mm(dh2, full['xa_wo'][i], tb=True, out_dtype=_MXU)
        acc('xa_wo', i, _mm(sv['o2'], dh2, ta=True))
        dq, dk, dv = _xa_bwd(sv['q'], sv['kv'], do2)
        dkv = jnp.concatenate([dk, dv], axis=1)
        da = _mm(dq, full['xa_wq'][i], tb=True)
        acc('xa_wq', i, _mm(sv['a'], dq, ta=True))
        dmemn = _mm(dkv, full['xa_wkv'][i], tb=True)
        acc('xa_wkv', i, _mm(sv['memn'], dkv, ta=True))
        _, dg = _rms_bwd(memv, full['mem_norm_g'][i], dmemn)
        acc('mem_norm_g', i, dg)
        dh1, dg = _rms_bwd(sv['h1'], full['xa_norm_g'][i], da, dh2)
        acc('xa_norm_g', i, dg)
        if kind == 0:
            dp, dw, ds = _pool_mix_bwd(sv['p'], full['pool_w'][j], full['pool_scale'][j], dh1)
            acc('pool_w', j, dw)
            acc('pool_scale', j, ds)
            dhn = _pool_bwd(dp)
        elif kind == 1:
            do = _mm(dh1, full['sb_w_o'][j], tb=True)
            acc('sb_w_o', j, _mm(sv['o'], dh1, ta=True))
            dqs, dks, dvs = _sb_bwd(sv['qkv'], sv['o'], do)
            dqkv = jnp.concatenate([dqs, dks, dvs], axis=1)
            dhn = _mm(dqkv, full['sb_w_qkv'][j], tb=True)
            acc('sb_w_qkv', j, _mm(sv['hn'], dqkv, ta=True))
        else:
            pr = s5[j]
            dval, dgate = _glu_bwd(sv['vg'], dh1)
            dvg = jnp.concatenate([dval, dgate], axis=1)
            dz = _mm(dvg, full['s5_w_glu'][j], tb=True)
            acc('s5_w_glu', j, _mm(sv['z'], dvg, ta=True))
            dy, gdr, gdi, du0, dd = _s5_out_bwd(dz, sv['y'], sv['hn'], full['s5_d'][j], pr['Cr'], pr['Ci'])
            acc('s5_d', j, dd)
            gr, gi, dabr, dabi = _s5_scan_bwd(gdr, gdi, sv['xr'], sv['xi'], pr['abr'], pr['abi'])
            dhn = _s5_in_bwd(gr, gi, pr['Br'], pr['Bi'], du0)
            dBr, dBi, dCr, dCi = _s5_wgrad(sv['hn'], gr, gi, sv['xr'], sv['xi'], dy)
            cg, P, G = pr['cg'], pr['P'], pr['G']
            acc('s5_c_re', j, _blockdiag_out_extract(dCr, cg, P))
            acc('s5_c_im', j, _blockdiag_out_extract(dCi, cg, P))
            dar, dai, dldt, dbr, dbi = _s5_prep_bwd(pr['ar'], pr['ai'], pr['ldt'], pr['br'], pr['bi'], dabr, dabi,
                                                    _blockdiag_in_extract(dBr, cg, P), _blockdiag_in_extract(dBi, cg, P))
            acc('s5_a_re', j, dar.reshape(G, P))
            acc('s5_a_im', j, dai.reshape(G, P))
            acc('s5_log_dt', j, dldt.reshape(G, P).sum(axis=1))
            acc('s5_b_re', j, dbr.reshape(cg, G, P).transpose(1, 2, 0))
            acc('s5_b_im', j, dbi.reshape(cg, G, P).transpose(1, 2, 0))
        dh, dg = _rms_bwd(sv['h'], full['mix_norm_g'][i], dhn, dh1)
        acc('mix_norm_g', i, dg)
    grad_x = dh[None]

    gfull = {}
    for n in WEIGHTS:
        gfull[n] = grads[n] if n == 'final_norm_g' else jnp.stack([grads[n][k] for k in range(len(grads[n]))])

    gw = {}
    gw.update(_reduce_scatter_grads(gfull, {n: w[n].shape for n in SHARD_AXIS}))
    gw.update(_all_reduce_small(gfull, REPLICATED))

    deltas, new_m, new_v = {}, {}, {}
    for n in WEIGHTS:
        shp = w[n].shape
        d_, m_, v_ = _adamw(_as2d(w[n]), _as2d(gw[n]), _as2d(mom[n]), _as2d(var[n]))
        deltas[n], new_m[n], new_v[n] = d_.reshape(shp), m_.reshape(shp), v_.reshape(shp)

    return (loss, grad_x, *[gw[n] for n in WEIGHTS], *[deltas[n] for n in WEIGHTS],
            *[new_m[n] for n in WEIGHTS], *[new_v[n] for n in WEIGHTS])
```

```python
import functools
import math

import jax
import jax.numpy as jnp
from jax import lax
from jax.experimental import pallas as pl
from jax.experimental.pallas import tpu as pltpu

f32 = jnp.float32
_MXU = jnp.bfloat16
_WIRE = _MXU
_VMEM_LIMIT = 48 * 1024 * 1024
_LANES = 128
_PACK_COLS = 1024

NDEV = 8
EPS = 1e-6
POOL_WINDOWS = (2, 4, 8, 16)
SB_HEAD_DIM = 64
XA_HEADS = 4
S5_GROUP = 16
S5_BLOCK_GROUPS = _LANES // S5_GROUP
ADAM_LR, ADAM_B1, ADAM_B2, ADAM_EPS, ADAM_WD, ADAM_STEP = 0.001, 0.9, 0.999, 1e-08, 0.01, 10

WEIGHTS = ['mix_norm_g', 'pool_w', 'pool_scale', 'sb_w_qkv', 'sb_w_o', 's5_a_re', 's5_a_im', 's5_log_dt',
           's5_b_re', 's5_b_im', 's5_c_re', 's5_c_im', 's5_d', 's5_w_glu', 'xa_norm_g', 'mem_norm_g', 'xa_wq',
           'xa_wkv', 'xa_wo', 'ffn_norm_g', 'ffn_w_up', 'ffn_conv_w', 'ffn_conv_b', 'ffn_w_down', 'final_norm_g']
SHARD_AXIS = {'pool_w': 2, 'pool_scale': 1, 'sb_w_qkv': 2, 'sb_w_o': 1, 's5_d': 1, 's5_w_glu': 2, 'xa_wq': 1,
              'xa_wkv': 2, 'xa_wo': 1, 'ffn_w_up': 2, 'ffn_conv_w': 2, 'ffn_w_down': 1}
MXU_WEIGHTS = ['pool_w', 'sb_w_qkv', 'sb_w_o', 's5_w_glu', 'xa_wq', 'xa_wkv', 'xa_wo', 'ffn_w_up', 'ffn_w_down']
VEC_WEIGHTS = ['pool_scale', 's5_d', 'ffn_conv_w']
REPLICATED = [n for n in WEIGHTS if n not in SHARD_AXIS]

_NN = (((1,), (0,)), ((), ()))
_NT = (((1,), (1,)), ((), ()))
_TN = (((0,), (0,)), ((), ()))
MESH_ID = pl.DeviceIdType.MESH
ANY = pl.BlockSpec(memory_space=pl.ANY)


def _call(body, **kw):
    return pl.pallas_call(body, **kw)


def _cp(*sem):
    return pltpu.CompilerParams(dimension_semantics=sem, vmem_limit_bytes=_VMEM_LIMIT)


def _tile(n, target, mult=_LANES):
    if n <= target:
        return n
    t = (target // mult) * mult
    while t >= mult:
        if n % t == 0:
            return t
        t -= mult
    return n


def _dot(a, b, dims=_NN):
    return lax.dot_general(a, b, dims, preferred_element_type=f32)


def _split(a):
    hi = a.astype(_MXU)
    lo = (a - hi.astype(f32)).astype(_MXU)
    return hi, lo


def _dot_hilo(a, u, dims=_NN):
    hi, lo = _split(a)
    return _dot(hi, u, dims) + _dot(lo, u, dims)


def _dot3(a, b, dims=_NN):
    ah, al = _split(a)
    bh, bl = _split(b)
    return _dot(ah, bh, dims) + _dot(al, bh, dims) + _dot(ah, bl, dims)


def _sigmoid(x):
    return 1.0 / (1.0 + jnp.exp(-x))


_GELU_C = math.sqrt(2.0 / math.pi)


def _gelu(x):
    return x * (0.5 * (1.0 + jnp.tanh(_GELU_C * (x + 0.044715 * (x * x * x)))))


def _gelu_grad(x):
    t = jnp.tanh(_GELU_C * (x + 0.044715 * (x * x * x)))
    return 0.5 * (1.0 + t) + x * 0.5 * (1.0 - t * t) * _GELU_C * (1.0 + 3.0 * 0.044715 * x * x)


def _shift_down(x, k, rows):
    return jnp.where(rows >= k, pltpu.roll(x, k, 0), 0.0)


def _shift_up(x, k, rows, n):
    return jnp.where(rows < n - k, pltpu.roll(x, n - k, 0), 0.0)


def _mm(a, b, *, ta=False, tb=False, out_dtype=f32):
    M, K = (a.shape[1], a.shape[0]) if ta else a.shape
    N = b.shape[0] if tb else b.shape[1]
    tm, tn, tk = _tile(M, 1408), _tile(N, 1536), _tile(K, 1408)
    nk = K // tk
    a_spec = pl.BlockSpec((tk, tm), lambda i, j, k: (k, i)) if ta else pl.BlockSpec((tm, tk), lambda i, j, k: (i, k))
    b_spec = pl.BlockSpec((tn, tk), lambda i, j, k: (j, k)) if tb else pl.BlockSpec((tk, tn), lambda i, j, k: (k, j))
    dims = (((0 if ta else 1,), (1 if tb else 0,)), ((), ()))

    def body(a_ref, b_ref, o_ref, acc_ref):
        k = pl.program_id(2)

        @pl.when(k == 0)
        def _():
            acc_ref[...] = jnp.zeros_like(acc_ref)

        acc_ref[...] += _dot(a_ref[...].astype(_MXU), b_ref[...].astype(_MXU), dims)

        @pl.when(k == nk - 1)
        def _():
            o_ref[...] = acc_ref[...].astype(out_dtype)

    return _call(
        body, name=f"mm_{'t' if ta else 'n'}{'t' if tb else 'n'}_{M}x{K}x{N}",
        grid=(M // tm, N // tn, nk), in_specs=[a_spec, b_spec],
        out_specs=pl.BlockSpec((tm, tn), lambda i, j, k: (i, j)),
        out_shape=jax.ShapeDtypeStruct((M, N), out_dtype),
        scratch_shapes=[pltpu.VMEM((tm, tn), f32)],
        compiler_params=_cp("parallel", "parallel", "arbitrary"))(a, b)


def _rms_fwd(x, g, delta=None, *, out_dtype=f32):
    S, D = x.shape
    ts = _tile(S, 512, 8)
    row = pl.BlockSpec((ts, D), lambda i: (i, 0))
    vec = pl.BlockSpec((1, D), lambda i: (0, 0))
    g2 = g.reshape(1, D)

    def norm(xv, g_ref):
        r = lax.rsqrt(jnp.mean(xv * xv, axis=-1, keepdims=True) + EPS)
        return ((xv * r) * g_ref[...]).astype(out_dtype)

    if delta is None:
        def body(x_ref, g_ref, y_ref):
            y_ref[...] = norm(x_ref[...], g_ref)

        return _call(body, name=f"rms_fwd_{S}", grid=(S // ts,), in_specs=[row, vec], out_specs=row,
                     out_shape=jax.ShapeDtypeStruct((S, D), out_dtype), compiler_params=_cp("parallel"))(x, g2)

    def body(x_ref, d_ref, g_ref, s_ref, y_ref):
        xv = x_ref[...] + d_ref[...]
        s_ref[...] = xv
        y_ref[...] = norm(xv, g_ref)

    return _call(body, name=f"add_rms_fwd_{S}", grid=(S // ts,), in_specs=[row, row, vec], out_specs=[row, row],
                 out_shape=[jax.ShapeDtypeStruct((S, D), f32), jax.ShapeDtypeStruct((S, D), out_dtype)],
                 compiler_params=_cp("parallel"))(x, delta, g2)


def _rms_bwd(x, g, dy, dres=None):
    S, D = x.shape
    ts = _tile(S, 512, 8)
    row = pl.BlockSpec((ts, D), lambda i: (i, 0))
    vec = pl.BlockSpec((1, D), lambda i: (0, 0))
    has_res = dres is not None

    def body(*refs):
        if has_res:
            x_ref, g_ref, dy_ref, dr_ref, dx_ref, dg_ref = refs
        else:
            x_ref, g_ref, dy_ref, dx_ref, dg_ref = refs
        xv = x_ref[...]
        r = lax.rsqrt(jnp.mean(xv * xv, axis=-1, keepdims=True) + EPS)
        xh = xv * r
        dyv = dy_ref[...].astype(f32)

        @pl.when(pl.program_id(0) == 0)
        def _():
            dg_ref[...] = jnp.zeros_like(dg_ref)

        dg_ref[...] += jnp.sum(dyv * xh, axis=0, keepdims=True)
        dxh = dyv * g_ref[...]
        dx = r * (dxh - xh * jnp.mean(dxh * xh, axis=-1, keepdims=True))
        if has_res:
            dx = dx + dr_ref[...]
        dx_ref[...] = dx

    ins = [x, g.reshape(1, D), dy] + ([dres] if has_res else [])
    dx, dg = _call(body, name=f"rms_bwd_{S}_{int(has_res)}", grid=(S // ts,),
                   in_specs=[row, vec, row] + ([row] if has_res else []), out_specs=[row, vec],
                   out_shape=[jax.ShapeDtypeStruct((S, D), f32), jax.ShapeDtypeStruct((1, D), f32)],
                   compiler_params=_cp("arbitrary"))(*ins)
    return dx, dg.reshape(D)


def _pool_windows_sum(x, win, rows):
    s, k = x, 1
    while k < win:
        s = s + _shift_down(s, k, rows)
        k *= 2
    return s


def _pool_windows_sum_up(x, win, rows, n):
    s, k = x, 1
    while k < win:
        s = s + _shift_up(s, k, rows, n)
        k *= 2
    return s


def _pool_fwd(hn):
    S, D = hn.shape
    cg = D // len(POOL_WINDOWS)
    tc = _tile(cg, 128)
    nb = cg // tc
    blk = pl.BlockSpec((S, tc), lambda gi, j: (0, gi * nb + j))

    def body(x_ref, p_ref):
        gi = pl.program_id(0)
        rows = lax.broadcasted_iota(jnp.int32, (S, 1), 0)
        cnt = (rows + 1).astype(f32)
        for k, win in enumerate(POOL_WINDOWS):
            @pl.when(gi == k)
            def _(win=win):
                x = x_ref[...]
                s = _pool_windows_sum(x, win, rows)
                p_ref[...] = (s / jnp.minimum(cnt, float(win)) - x).astype(p_ref.dtype)

    return _call(body, name=f"pool_fwd_{S}", grid=(len(POOL_WINDOWS), nb), in_specs=[blk], out_specs=blk,
                 out_shape=jax.ShapeDtypeStruct((S, D), _MXU), compiler_params=_cp("parallel", "parallel"))(hn)


def _pool_bwd(dp):
    S, D = dp.shape
    cg = D // len(POOL_WINDOWS)
    tc = _tile(cg, 128)
    nb = cg // tc
    blk = pl.BlockSpec((S, tc), lambda gi, j: (0, gi * nb + j))

    def body(dp_ref, dx_ref):
        gi = pl.program_id(0)
        rows = lax.broadcasted_iota(jnp.int32, (S, 1), 0)
        cnt = (rows + 1).astype(f32)
        for k, win in enumerate(POOL_WINDOWS):
            @pl.when(gi == k)
            def _(win=win):
                d = dp_ref[...]
                e = d / jnp.minimum(cnt, float(win))
                dx_ref[...] = _pool_windows_sum_up(e, win, rows, S) - d

    return _call(body, name=f"pool_bwd_{S}", grid=(len(POOL_WINDOWS), nb), in_specs=[blk], out_specs=blk,
                 out_shape=jax.ShapeDtypeStruct((S, D), f32), compiler_params=_cp("parallel", "parallel"))(dp)


def _pool_mix_fwd(p, w, scale):
    S, D = p.shape
    G, cg, _ = w.shape
    ts = _tile(S, 1024, 8)

    def body(p_ref, w_ref, s_ref, y_ref):
        y_ref[...] = _dot(p_ref[...], w_ref[0]) * s_ref[...]

    return _call(body, name=f"pool_mix_fwd_{S}", grid=(S // ts, G),
                 in_specs=[pl.BlockSpec((ts, cg), lambda i, g: (i, g)), pl.BlockSpec((1, cg, cg), lambda i, g: (g, 0, 0)),
                           pl.BlockSpec((1, cg), lambda i, g: (0, g))],
                 out_specs=pl.BlockSpec((ts, cg), lambda i, g: (i, g)),
                 out_shape=jax.ShapeDtypeStruct((S, D), f32), compiler_params=_cp("parallel", "parallel"))(
                     p, w, scale.reshape(1, D))


def _pool_mix_bwd(p, w, scale, dy):
    S, D = p.shape
    G, cg, _ = w.shape
    ts = _tile(S, 1024, 8)

    def body(p_ref, w_ref, s_ref, dy_ref, dp_ref, dw_ref, ds_ref):
        @pl.when(pl.program_id(1) == 0)
        def _():
            dw_ref[...] = jnp.zeros_like(dw_ref)
            ds_ref[...] = jnp.zeros_like(ds_ref)

        pv, wv, dyv = p_ref[...], w_ref[0], dy_ref[...]
        ypre = _dot(pv, wv)
        ds_ref[...] += jnp.sum(dyv * ypre, axis=0, keepdims=True)
        dyp = (dyv * s_ref[...]).astype(_MXU)
        dp_ref[...] = _dot(dyp, wv, _NT)
        dw_ref[0] += _dot(pv, dyp, _TN)

    dp, dw, ds = _call(
        body, name=f"pool_mix_bwd_{S}", grid=(G, S // ts),
        in_specs=[pl.BlockSpec((ts, cg), lambda g, i: (i, g)), pl.BlockSpec((1, cg, cg), lambda g, i: (g, 0, 0)),
                  pl.BlockSpec((1, cg), lambda g, i: (0, g)), pl.BlockSpec((ts, cg), lambda g, i: (i, g))],
        out_specs=[pl.BlockSpec((ts, cg), lambda g, i: (i, g)), pl.BlockSpec((1, cg, cg), lambda g, i: (g, 0, 0)),
                   pl.BlockSpec((1, cg), lambda g, i: (0, g))],
        out_shape=[jax.ShapeDtypeStruct((S, D), f32), jax.ShapeDtypeStruct((G, cg, cg), f32),
                   jax.ShapeDtypeStruct((1, D), f32)],
        compiler_params=_cp("parallel", "arbitrary"))(p, w, scale.reshape(1, D), dy)
    return dp, dw, ds.reshape(D)


def _sb_tile(S):
    return min(256, max(128, S // 4))


def _sb_scores(qm, ks, tri, R, U, scale):
    z = _dot(qm, ks, _NT) * scale
    soft = jnp.log(1.0 + jnp.exp(-jnp.abs(z)))
    lm = jnp.minimum(-z, 0.0) - soft
    lb = jnp.minimum(z, 0.0) - soft
    if tri is not None:
        lm = jnp.where(tri, lm, 0.0)
    c = _dot(lm.astype(_MXU), U)
    a = jnp.exp(lb + c + R)
    if tri is not None:
        a = jnp.where(tri, a, 0.0)
    return lm, lb, a


def _sb_consts(T):
    lane = lax.broadcasted_iota(jnp.int32, (1, _LANES), 1)
    row, col = lax.broadcasted_iota(jnp.int32, (T, T), 0), lax.broadcasted_iota(jnp.int32, (T, T), 1)
    U = jnp.where(row > col, 1.0, 0.0).astype(_MXU)
    heads = [(lane >= SB_HEAD_DIM * h) & (lane < SB_HEAD_DIM * (h + 1)) for h in range(_LANES // SB_HEAD_DIM)]
    return heads, col < row, U


def _sb_fwd(qkv):
    S, D3 = qkv.shape
    D = D3 // 3
    HP = D // _LANES
    T = _sb_tile(S)
    scale = SB_HEAD_DIM ** -0.5

    def body(q_ref, k_ref, v_ref, o_ref):
        i = pl.program_id(1)
        q = q_ref[...]
        heads, tri, U = _sb_consts(T)
        qms = [jnp.where(hm, q, jnp.zeros_like(q)) for hm in heads]

        def tile(kb, carry, mask):
            off = pl.multiple_of(kb * T, T)
            ks, vs = k_ref[pl.ds(off, T), :], v_ref[pl.ds(off, T), :]
            new = []
            for h, qm in enumerate(qms):
                R, acc = carry[2 * h], carry[2 * h + 1]
                lm, _, a = _sb_scores(qm, ks, mask, R, U, scale)
                new += [R + jnp.sum(lm, axis=1, keepdims=True), acc + _dot(a.astype(_MXU), vs)]
            return tuple(new)

        zero = (jnp.zeros((T, 1), f32), jnp.zeros((T, _LANES), f32)) * len(heads)
        carry = tile(i, zero, tri)
        carry = lax.fori_loop(1, i + 1, lambda j, c: tile(i - j, c, None), carry)
        out = jnp.zeros((T, _LANES), f32)
        for h, hm in enumerate(heads):
            out = out + jnp.where(hm, carry[2 * h + 1], 0.0)
        o_ref[...] = out

    return _call(
        body, name=f"sb_fwd_{S}", grid=(HP, S // T),
        in_specs=[pl.BlockSpec((T, _LANES), lambda hp, i: (i, hp)), pl.BlockSpec((S, _LANES), lambda hp, i: (0, HP + hp)),
                  pl.BlockSpec((S, _LANES), lambda hp, i: (0, 2 * HP + hp))],
        out_specs=pl.BlockSpec((T, _LANES), lambda hp, i: (i, hp)),
        out_shape=jax.ShapeDtypeStruct((S, D), f32), compiler_params=_cp("parallel", "parallel"))(qkv, qkv, qkv)


def _sb_bwd(qkv, o, do):
    S, D3 = qkv.shape
    D = D3 // 3
    HP = D // _LANES
    T = _sb_tile(S)
    nq = S // T
    scale = SB_HEAD_DIM ** -0.5

    def body(q_ref, k_ref, v_ref, o_ref, do_ref, dq_ref, dk_ref, dv_ref, dk_acc, dv_acc):
        i = pl.program_id(1)

        @pl.when(i == 0)
        def _():
            dk_acc[...] = jnp.zeros_like(dk_acc)
            dv_acc[...] = jnp.zeros_like(dv_acc)

        q = q_ref[...]
        dob = do_ref[...].astype(_MXU)
        prod = dob.astype(f32) * o_ref[...]
        heads, tri, U = _sb_consts(T)
        qms = [jnp.where(hm, q, jnp.zeros_like(q)) for hm in heads]
        doms = [jnp.where(hm, dob, jnp.zeros_like(dob)) for hm in heads]
        totals = [jnp.sum(jnp.where(hm, prod, 0.0), axis=1, keepdims=True) for hm in heads]

        def tile(kb, carry, mask):
            off = pl.multiple_of(kb * T, T)
            ks, vs = k_ref[pl.ds(off, T), :], v_ref[pl.ds(off, T), :]
            new = []
            dk_t = jnp.zeros((T, _LANES), f32)
            dv_t = jnp.zeros((T, _LANES), f32)
            for h, (qm, dom, total) in enumerate(zip(qms, doms, totals)):
                R, Gs, dq = carry[3 * h:3 * h + 3]
                lm, lb, a = _sb_scores(qm, ks, mask, R, U, scale)
                ab = a.astype(_MXU)
                g = ab.astype(f32) * _dot(dom, vs, _NT)
                before = total - (g + _dot_hilo(g, U) + Gs)
                beta = jnp.exp(lb)
                dz = (g * (1.0 - beta) - before * beta) * scale
                if mask is not None:
                    dz = jnp.where(mask, dz, 0.0)
                dzb = dz.astype(_MXU)
                dk_t = dk_t + _dot(dzb, qm, _TN)
                dv_t = dv_t + _dot(ab, dom, _TN)
                new += [R + jnp.sum(lm, axis=1, keepdims=True), Gs + jnp.sum(g, axis=1, keepdims=True), dq + _dot(dzb, ks)]
            dk_acc[pl.ds(off, T), :] += dk_t
            dv_acc[pl.ds(off, T), :] += dv_t
            return tuple(new)

        zero1 = jnp.zeros((T, 1), f32)
        carry = tile(i, (zero1, zero1, jnp.zeros((T, _LANES), f32)) * len(heads), tri)
        carry = lax.fori_loop(1, i + 1, lambda j, c: tile(i - j, c, None), carry)
        dq_out = jnp.zeros((T, _LANES), f32)
        for h, hm in enumerate(heads):
            dq_out = dq_out + jnp.where(hm, carry[3 * h + 2], 0.0)
        dq_ref[...] = dq_out.astype(dq_ref.dtype)

        @pl.when(i == nq - 1)
        def _():
            dk_ref[...] = dk_acc[...].astype(dk_ref.dtype)
            dv_ref[...] = dv_acc[...].astype(dv_ref.dtype)

    qb = pl.BlockSpec((T, _LANES), lambda hp, i: (i, hp))
    col = pl.BlockSpec((S, _LANES), lambda hp, i: (0, hp))
    out = jax.ShapeDtypeStruct((S, D), _MXU)
    return _call(
        body, name=f"sb_bwd_{S}", grid=(HP, nq),
        in_specs=[qb, pl.BlockSpec((S, _LANES), lambda hp, i: (0, HP + hp)),
                  pl.BlockSpec((S, _LANES), lambda hp, i: (0, 2 * HP + hp)), qb, qb],
        out_specs=[qb, col, col], out_shape=[out, out, out],
        scratch_shapes=[pltpu.VMEM((S, _LANES), f32), pltpu.VMEM((S, _LANES), f32)],
        compiler_params=_cp("parallel", "arbitrary"))(qkv, qkv, qkv, o, do)


def _cmul(ar, ai, br, bi):
    return ar * br - ai * bi, ar * bi + ai * br


def _s5_coef(ar, ai, ldt):
    dt = jnp.exp(ldt)
    e = jnp.exp(ar * dt)
    abr, abi = e * jnp.cos(ai * dt), e * jnp.sin(ai * dt)
    inv = 1.0 / (ar * ar + ai * ai)
    cr, ci = _cmul(abr - 1.0, abi, ar * inv, -ai * inv)
    return dt, abr, abi, inv, cr, ci


def _s5_prep_fwd(ar, ai, ldt, br, bi):
    N = ar.shape[1]
    cg = br.shape[0]

    def body(ar_ref, ai_ref, ldt_ref, br_ref, bi_ref, abr_ref, abi_ref, bbr_ref, bbi_ref):
        _, abr, abi, _, cr, ci = _s5_coef(ar_ref[...], ai_ref[...], ldt_ref[...])
        abr_ref[...], abi_ref[...] = abr, abi
        bbr_ref[...], bbi_ref[...] = _cmul(cr, ci, br_ref[...], bi_ref[...])

    v, m = jax.ShapeDtypeStruct((1, N), f32), jax.ShapeDtypeStruct((cg, N), f32)
    return _call(body, name="s5_prep_fwd", out_shape=[v, v, m, m])(ar, ai, ldt, br, bi)


def _s5_prep_bwd(ar, ai, ldt, br, bi, dabr, dabi, dbbr, dbbi):
    N = ar.shape[1]
    cg = br.shape[0]

    def body(ar_ref, ai_ref, ldt_ref, br_ref, bi_ref, dabr_ref, dabi_ref, dbbr_ref, dbbi_ref,
             dar_ref, dai_ref, dldt_ref, dbr_ref, dbi_ref):
        a_r, a_i = ar_ref[...], ai_ref[...]
        dt, abr, abi, inv, cr, ci = _s5_coef(a_r, a_i, ldt_ref[...])
        b_r, b_i, gr, gi = br_ref[...], bi_ref[...], dbbr_ref[...], dbbi_ref[...]
        dbr_ref[...], dbi_ref[...] = _cmul(cr, -ci, gr, gi)
        dcr = jnp.sum(gr * b_r + gi * b_i, axis=0, keepdims=True)
        dci = jnp.sum(gi * b_r - gr * b_i, axis=0, keepdims=True)
        ilr, ili = a_r * inv, -a_i * inv
        dwr, dwi = _cmul(ilr, -ili, dcr, dci)
        qr, qi = _cmul(cr, ci, ilr, ili)
        dl1r, dl1i = _cmul(-qr, qi, dcr, dci)
        tr, ti = dabr_ref[...] + dwr, dabi_ref[...] + dwi
        ddlr, ddli = _cmul(abr, -abi, tr, ti)
        dar_ref[...] = dl1r + ddlr * dt
        dai_ref[...] = dl1i + ddli * dt
        dldt_ref[...] = (a_r * ddlr + a_i * ddli) * dt

    v, m = jax.ShapeDtypeStruct((1, N), f32), jax.ShapeDtypeStruct((cg, N), f32)
    return _call(body, name="s5_prep_bwd", out_shape=[v, v, v, m, m])(ar, ai, ldt, br, bi, dabr, dabi, dbbr, dbbi)


def _s5_in_fwd(u, Br, Bi):
    S, D = u.shape
    nb, _, nw = Br.shape
    ts = _tile(S, 512, 8)

    def body(u_ref, br_ref, bi_ref, or_ref, oi_ref):
        uv = u_ref[...]
        or_ref[...] = _dot3(uv, br_ref[0])
        oi_ref[...] = _dot3(uv, bi_ref[0])

    ub = pl.BlockSpec((ts, _LANES), lambda i, k: (i, k))
    wb = pl.BlockSpec((1, _LANES, nw), lambda i, k: (k, 0, 0))
    ob = pl.BlockSpec((ts, nw), lambda i, k: (i, k))
    o = jax.ShapeDtypeStruct((S, nb * nw), f32)
    return _call(body, name=f"s5_in_fwd_{S}", grid=(S // ts, nb), in_specs=[ub, wb, wb], out_specs=[ob, ob],
                 out_shape=[o, o], compiler_params=_cp("parallel", "parallel"))(u, Br, Bi)


def _s5_scan_fwd(bur, bui, abr, abi):
    S, N = bur.shape
    tt, tn = _tile(S, 256, 8), _tile(N, 1024)

    def body(br_ref, bi_ref, ar_ref, ai_ref, xr_ref, xi_ref, sr, si):
        @pl.when(pl.program_id(1) == 0)
        def _():
            sr[...] = jnp.zeros_like(sr)
            si[...] = jnp.zeros_like(si)

        a_r, a_i = ar_ref[...], ai_ref[...]

        def step(t, carry):
            xr, xi = carry
            nr = a_r * xr - a_i * xi + br_ref[pl.ds(t, 1), :]
            ni = a_r * xi + a_i * xr + bi_ref[pl.ds(t, 1), :]
            xr_ref[pl.ds(t, 1), :] = nr
            xi_ref[pl.ds(t, 1), :] = ni
            return nr, ni

        xr, xi = lax.fori_loop(0, tt, step, (sr[...], si[...]))
        sr[...], si[...] = xr, xi

    blk = pl.BlockSpec((tt, tn), lambda n, i: (i, n))
    vec = pl.BlockSpec((1, tn), lambda n, i: (0, n))
    o = jax.ShapeDtypeStruct((S, N), f32)
    return _call(body, name=f"s5_scan_fwd_{S}", grid=(N // tn, S // tt), in_specs=[blk, blk, vec, vec],
                 out_specs=[blk, blk], out_shape=[o, o],
                 scratch_shapes=[pltpu.VMEM((1, tn), f32), pltpu.VMEM((1, tn), f32)],
                 compiler_params=_cp("parallel", "arbitrary"))(bur, bui, abr, abi)


def _s5_scan_bwd(dr, di, xr, xi, abr, abi):
    S, N = dr.shape
    tt, tn = _tile(S, 256, 8), _tile(N, 1024)
    nt = S // tt

    def body(dr_ref, di_ref, xr_ref, xi_ref, ar_ref, ai_ref, gr_ref, gi_ref, dar_ref, dai_ref, sr, si):
        @pl.when(pl.program_id(1) == 0)
        def _():
            sr[...] = jnp.zeros_like(sr)
            si[...] = jnp.zeros_like(si)
            dar_ref[...] = jnp.zeros_like(dar_ref)
            dai_ref[...] = jnp.zeros_like(dai_ref)

        a_r, a_i = ar_ref[...], ai_ref[...]

        def step(j, carry):
            gr, gi, accr, acci = carry
            t = tt - 1 - j
            xr_t, xi_t = xr_ref[pl.ds(t, 1), :], xi_ref[pl.ds(t, 1), :]
            accr = accr + gr * xr_t + gi * xi_t
            acci = acci + gi * xr_t - gr * xi_t
            nr = dr_ref[pl.ds(t, 1), :] + a_r * gr + a_i * gi
            ni = di_ref[pl.ds(t, 1), :] + a_r * gi - a_i * gr
            gr_ref[pl.ds(t, 1), :] = nr
            gi_ref[pl.ds(t, 1), :] = ni
            return nr, ni, accr, acci

        gr, gi, accr, acci = lax.fori_loop(0, tt, step, (sr[...], si[...], dar_ref[...], dai_ref[...]))
        sr[...], si[...] = gr, gi
        dar_ref[...], dai_ref[...] = accr, acci

    blk = pl.BlockSpec((tt, tn), lambda n, i: (nt - 1 - i, n))
    vec = pl.BlockSpec((1, tn), lambda n, i: (0, n))
    o, v = jax.ShapeDtypeStruct((S, N), f32), jax.ShapeDtypeStruct((1, N), f32)
    return _call(body, name=f"s5_scan_bwd_{S}", grid=(N // tn, nt), in_specs=[blk, blk, blk, blk, vec, vec],
                 out_specs=[blk, blk, vec, vec], out_shape=[o, o, v, v],
                 scratch_shapes=[pltpu.VMEM((1, tn), f32), pltpu.VMEM((1, tn), f32)],
                 compiler_params=_cp("parallel", "arbitrary"))(dr, di, xr, xi, abr, abi)


def _s5_out_fwd(xr, xi, Cr, Ci, u, d):
    S, N = xr.shape
    nb, nw, _ = Cr.shape
    D = u.shape[1]
    ts = _tile(S, 512, 8)

    def body(xr_ref, xi_ref, cr_ref, ci_ref, u_ref, d_ref, y_ref, z_ref):
        y = _dot3(xr_ref[...], cr_ref[0]) - _dot3(xi_ref[...], ci_ref[0]) + d_ref[...] * u_ref[...]
        y_ref[...] = y
        z_ref[...] = _gelu(y).astype(z_ref.dtype)

    xb = pl.BlockSpec((ts, nw), lambda i, k: (i, k))
    cb = pl.BlockSpec((1, nw, _LANES), lambda i, k: (k, 0, 0))
    ub = pl.BlockSpec((ts, _LANES), lambda i, k: (i, k))
    db = pl.BlockSpec((1, _LANES), lambda i, k: (0, k))
    return _call(body, name=f"s5_out_fwd_{S}", grid=(S // ts, nb), in_specs=[xb, xb, cb, cb, ub, db],
                 out_specs=[ub, ub], out_shape=[jax.ShapeDtypeStruct((S, D), f32), jax.ShapeDtypeStruct((S, D), _MXU)],
                 compiler_params=_cp("parallel", "parallel"))(xr, xi, Cr, Ci, u, d.reshape(1, D))


def _s5_out_bwd(dz, y, u, d, Cr, Ci):
    S, D = y.shape
    nb, nw, _ = Cr.shape
    ts = _tile(S, 512, 8)

    def body(dz_ref, y_ref, u_ref, d_ref, cr_ref, ci_ref, dy_ref, gr_ref, gi_ref, du_ref, dd_ref):
        @pl.when(pl.program_id(1) == 0)
        def _():
            dd_ref[...] = jnp.zeros_like(dd_ref)

        dy = dz_ref[...] * _gelu_grad(y_ref[...])
        dy_ref[...] = dy
        gr_ref[...] = _dot3(dy, cr_ref[0], _NT)
        gi_ref[...] = -_dot3(dy, ci_ref[0], _NT)
        du_ref[...] = dy * d_ref[...]
        dd_ref[...] += jnp.sum(dy * u_ref[...], axis=0, keepdims=True)

    xb = pl.BlockSpec((ts, nw), lambda k, i: (i, k))
    cb = pl.BlockSpec((1, nw, _LANES), lambda k, i: (k, 0, 0))
    ub = pl.BlockSpec((ts, _LANES), lambda k, i: (i, k))
    db = pl.BlockSpec((1, _LANES), lambda k, i: (0, k))
    a, s = jax.ShapeDtypeStruct((S, D), f32), jax.ShapeDtypeStruct((S, nb * nw), f32)
    dy, gr, gi, du, dd = _call(
        body, name=f"s5_out_bwd_{S}", grid=(nb, S // ts), in_specs=[ub, ub, ub, db, cb, cb],
        out_specs=[ub, xb, xb, ub, db], out_shape=[a, s, s, a, jax.ShapeDtypeStruct((1, D), f32)],
        compiler_params=_cp("parallel", "arbitrary"))(dz, y, u, d.reshape(1, D), Cr, Ci)
    return dy, gr, gi, du, dd.reshape(D)


def _s5_in_bwd(gr, gi, Br, Bi, du0):
    S, N = gr.shape
    nb, _, nw = Br.shape
    ts = _tile(S, 512, 8)

    def body(gr_ref, gi_ref, br_ref, bi_ref, d0_ref, du_ref):
        du_ref[...] = d0_ref[...] + _dot3(gr_ref[...], br_ref[0], _NT) + _dot3(gi_ref[...], bi_ref[0], _NT)

    xb = pl.BlockSpec((ts, nw), lambda i, k: (i, k))
    wb = pl.BlockSpec((1, _LANES, nw), lambda i, k: (k, 0, 0))
    ub = pl.BlockSpec((ts, _LANES), lambda i, k: (i, k))
    return _call(body, name=f"s5_in_bwd_{S}", grid=(S // ts, nb), in_specs=[xb, xb, wb, wb, ub], out_specs=ub,
                 out_shape=jax.ShapeDtypeStruct((S, nb * _LANES), f32), compiler_params=_cp("parallel", "parallel"))(
                     gr, gi, Br, Bi, du0)


def _s5_wgrad(u, gr, gi, xr, xi, dy):
    S, D = u.shape
    nb = D // _LANES
    nw = gr.shape[1] // nb
    ts = _tile(S, 512, 8)

    def body(u_ref, gr_ref, gi_ref, xr_ref, xi_ref, dy_ref, dbr_ref, dbi_ref, dcr_ref, dci_ref):
        @pl.when(pl.program_id(1) == 0)
        def _():
            for r in (dbr_ref, dbi_ref, dcr_ref, dci_ref):
                r[...] = jnp.zeros_like(r)

        uv, dyv = u_ref[...], dy_ref[...]
        dbr_ref[0] += _dot3(uv, gr_ref[...], _TN)
        dbi_ref[0] += _dot3(uv, gi_ref[...], _TN)
        dcr_ref[0] += _dot3(xr_ref[...], dyv, _TN)
        dci_ref[0] -= _dot3(xi_ref[...], dyv, _TN)

    xb = pl.BlockSpec((ts, nw), lambda k, i: (i, k))
    ub = pl.BlockSpec((ts, _LANES), lambda k, i: (i, k))
    wb = pl.BlockSpec((1, _LANES, nw), lambda k, i: (k, 0, 0))
    cb = pl.BlockSpec((1, nw, _LANES), lambda k, i: (k, 0, 0))
    w, c = jax.ShapeDtypeStruct((nb, _LANES, nw), f32), jax.ShapeDtypeStruct((nb, nw, _LANES), f32)
    return _call(body, name=f"s5_wgrad_{S}", grid=(nb, S // ts), in_specs=[ub, xb, xb, xb, xb, ub],
                 out_specs=[wb, wb, cb, cb], out_shape=[w, w, c, c],
                 compiler_params=_cp("parallel", "arbitrary"))(u, gr, gi, xr, xi, dy)


def _glu_fwd(vg):
    S, D2 = vg.shape
    D = D2 // 2
    ts, tc = _tile(S, 1024, 8), _tile(D, 512)
    nc = D // tc

    def body(v_ref, g_ref, o_ref):
        o_ref[...] = v_ref[...] * _sigmoid(g_ref[...])

    return _call(body, name=f"glu_fwd_{S}", grid=(S // ts, nc),
                 in_specs=[pl.BlockSpec((ts, tc), lambda i, j: (i, j)), pl.BlockSpec((ts, tc), lambda i, j: (i, nc + j))],
                 out_specs=pl.BlockSpec((ts, tc), lambda i, j: (i, j)), out_shape=jax.ShapeDtypeStruct((S, D), f32),
                 compiler_params=_cp("parallel", "parallel"))(vg, vg)


def _glu_bwd(vg, dout):
    S, D2 = vg.shape
    D = D2 // 2
    ts, tc = _tile(S, 1024, 8), _tile(D, 512)
    nc = D // tc

    def body(v_ref, g_ref, do_ref, dv_ref, dg_ref):
        sg = _sigmoid(g_ref[...])
        do = do_ref[...]
        dv_ref[...] = (do * sg).astype(dv_ref.dtype)
        dg_ref[...] = (do * v_ref[...] * sg * (1.0 - sg)).astype(dg_ref.dtype)

    blk = pl.BlockSpec((ts, tc), lambda i, j: (i, j))
    o = jax.ShapeDtypeStruct((S, D), _MXU)
    return _call(body, name=f"glu_bwd_{S}", grid=(S // ts, nc),
                 in_specs=[blk, pl.BlockSpec((ts, tc), lambda i, j: (i, nc + j)), blk], out_specs=[blk, blk],
                 out_shape=[o, o], compiler_params=_cp("parallel", "parallel"))(vg, vg, dout)


def _blockdiag_in(b2, nb, P):
    cg = b2.shape[0]
    gb = S5_BLOCK_GROUPS
    t = b2.reshape(cg, nb, gb, P).transpose(1, 0, 2, 3)
    eye = jnp.eye(gb, dtype=b2.dtype)
    return (eye[None, :, None, :, None] * t[:, None]).reshape(nb, gb * cg, gb * P)


def _blockdiag_in_extract(db, cg, P):
    nb = db.shape[0]
    gb = S5_BLOCK_GROUPS
    eye = jnp.eye(gb, dtype=db.dtype)
    t = (db.reshape(nb, gb, cg, gb, P) * eye[None, :, None, :, None]).sum(3)
    return t.transpose(2, 0, 1, 3).reshape(cg, nb * gb * P)


def _blockdiag_out(c, nb):
    G, cg, P = c.shape
    gb = S5_BLOCK_GROUPS
    t = c.reshape(nb, gb, cg, P).transpose(0, 3, 1, 2)
    eye = jnp.eye(gb, dtype=c.dtype)
    return (eye[None, :, None, :, None] * t[:, None]).reshape(nb, gb * P, gb * cg)


def _blockdiag_out_extract(dc, cg, P):
    nb = dc.shape[0]
    gb = S5_BLOCK_GROUPS
    eye = jnp.eye(gb, dtype=dc.dtype)
    t = (dc.reshape(nb, gb, P, gb, cg) * eye[None, :, None, :, None]).sum(1)
    return t.transpose(0, 2, 3, 1).reshape(nb * gb, cg, P)


def _xa_fwd(q, kv):
    S, D = q.shape
    M = kv.shape[0]
    dh = D // XA_HEADS
    ts = _tile(S, 512, 8)
    scale = dh ** -0.5

    def body(q_ref, k_ref, v_ref, o_ref):
        s = _dot(q_ref[...], k_ref[...], _NT) * scale
        e = jnp.exp(s - jnp.max(s, axis=-1, keepdims=True))
        p = e / jnp.sum(e, axis=-1, keepdims=True)
        o_ref[...] = _dot(p.astype(_MXU), v_ref[...]).astype(o_ref.dtype)

    return _call(body, name=f"xa_fwd_{S}", grid=(S // ts, XA_HEADS),
                 in_specs=[pl.BlockSpec((ts, dh), lambda i, h: (i, h)), pl.BlockSpec((M, dh), lambda i, h: (0, h)),
                           pl.BlockSpec((M, dh), lambda i, h: (0, XA_HEADS + h))],
                 out_specs=pl.BlockSpec((ts, dh), lambda i, h: (i, h)), out_shape=jax.ShapeDtypeStruct((S, D), _MXU),
                 compiler_params=_cp("parallel", "parallel"))(q, kv, kv)


def _xa_bwd(q, kv, do):
    S, D = q.shape
    M = kv.shape[0]
    dh = D // XA_HEADS
    ts = _tile(S, 512, 8)
    scale = dh ** -0.5

    def body(q_ref, k_ref, v_ref, do_ref, dq_ref, dk_ref, dv_ref):
        @pl.when(pl.program_id(1) == 0)
        def _():
            dk_ref[...] = jnp.zeros_like(dk_ref)
            dv_ref[...] = jnp.zeros_like(dv_ref)

        qv, kv_, vv, dov = q_ref[...], k_ref[...], v_ref[...], do_ref[...]
        s = _dot(qv, kv_, _NT) * scale
        e = jnp.exp(s - jnp.max(s, axis=-1, keepdims=True))
        p = e / jnp.sum(e, axis=-1, keepdims=True)
        dp = _dot(dov, vv, _NT)
        dv_ref[...] += _dot(p.astype(_MXU), dov, _TN)
        ds = (p * (dp - jnp.sum(dp * p, axis=-1, keepdims=True)) * scale).astype(_MXU)
        dq_ref[...] = _dot(ds, kv_).astype(dq_ref.dtype)
        dk_ref[...] += _dot(ds, qv, _TN)

    qb = pl.BlockSpec((ts, dh), lambda h, i: (i, h))
    mb = pl.BlockSpec((M, dh), lambda h, i: (0, h))
    m = jax.ShapeDtypeStruct((M, D), f32)
    return _call(body, name=f"xa_bwd_{S}", grid=(XA_HEADS, S // ts),
                 in_specs=[qb, mb, pl.BlockSpec((M, dh), lambda h, i: (0, XA_HEADS + h)), qb],
                 out_specs=[qb, mb, mb], out_shape=[jax.ShapeDtypeStruct((S, D), _MXU), m, m],
                 compiler_params=_cp("parallel", "arbitrary"))(q, kv, kv, do)


def _conv(u, cw_ref, cb_ref, rows):
    return cw_ref[2:3, :] * u + cw_ref[1:2, :] * _shift_down(u, 1, rows) + cw_ref[0:1, :] * _shift_down(u, 2, rows) + cb_ref[...]


def _convglu_fwd(u, cw, cb):
    S, F2 = u.shape
    F = F2 // 2
    tc = _tile(F, 128)
    nc = F // tc

    def body(uv_ref, ug_ref, cwv_ref, cwg_ref, cbv_ref, cbg_ref, o_ref):
        rows = lax.broadcasted_iota(jnp.int32, (S, 1), 0)
        val = _conv(uv_ref[...], cwv_ref, cbv_ref, rows)
        gate = _conv(ug_ref[...], cwg_ref, cbg_ref, rows)
        o_ref[...] = (gate * _sigmoid(gate) * val).astype(o_ref.dtype)

    def col(r, off):
        return pl.BlockSpec((r, tc), lambda j: (0, off + j))

    return _call(body, name=f"convglu_fwd_{S}", grid=(nc,),
                 in_specs=[col(S, 0), col(S, nc), col(3, 0), col(3, nc), col(1, 0), col(1, nc)], out_specs=col(S, 0),
                 out_shape=jax.ShapeDtypeStruct((S, F), _MXU), compiler_params=_cp("parallel"))(
                     u, u, cw, cw, cb.reshape(1, F2), cb.reshape(1, F2))


def _convglu_bwd(u, cw, cb, dact):
    S, F2 = u.shape
    F = F2 // 2
    tc = _tile(F, 128)
    nc = F // tc

    def body(uv_ref, ug_ref, cwv_ref, cwg_ref, cbv_ref, cbg_ref, da_ref, duv_ref, dug_ref, dcwv_ref, dcwg_ref, dcbv_ref, dcbg_ref):
        rows = lax.broadcasted_iota(jnp.int32, (S, 1), 0)
        uv, ug = uv_ref[...], ug_ref[...]
        val = _conv(uv, cwv_ref, cbv_ref, rows)
        gate = _conv(ug, cwg_ref, cbg_ref, rows)
        sg = _sigmoid(gate)
        da = da_ref[...]
        dval = da * (gate * sg)
        dgate = da * val * (sg * (1.0 + gate * (1.0 - sg)))
        for uu, d, cw_ref, du_ref, dcw_ref, dcb_ref in ((uv, dval, cwv_ref, duv_ref, dcwv_ref, dcbv_ref),
                                                        (ug, dgate, cwg_ref, dug_ref, dcwg_ref, dcbg_ref)):
            dcb_ref[...] = jnp.sum(d, axis=0, keepdims=True)
            dcw_ref[2:3, :] = jnp.sum(d * uu, axis=0, keepdims=True)
            dcw_ref[1:2, :] = jnp.sum(d * _shift_down(uu, 1, rows), axis=0, keepdims=True)
            dcw_ref[0:1, :] = jnp.sum(d * _shift_down(uu, 2, rows), axis=0, keepdims=True)
            du = cw_ref[2:3, :] * d + cw_ref[1:2, :] * _shift_up(d, 1, rows, S) + cw_ref[0:1, :] * _shift_up(d, 2, rows, S)
            du_ref[...] = du.astype(du_ref.dtype)

    def col(r, off):
        return pl.BlockSpec((r, tc), lambda j: (0, off + j))

    o = jax.ShapeDtypeStruct((S, F), _MXU)
    w, b = jax.ShapeDtypeStruct((3, F), f32), jax.ShapeDtypeStruct((1, F), f32)
    duv, dug, dcwv, dcwg, dcbv, dcbg = _call(
        body, name=f"convglu_bwd_{S}", grid=(nc,),
        in_specs=[col(S, 0), col(S, nc), col(3, 0), col(3, nc), col(1, 0), col(1, nc), col(S, 0)],
        out_specs=[col(S, 0), col(S, 0), col(3, 0), col(3, 0), col(1, 0), col(1, 0)],
        out_shape=[o, o, w, w, b, b], compiler_params=_cp("parallel"))(
            u, u, cw, cw, cb.reshape(1, F2), cb.reshape(1, F2), dact)
    return (jnp.concatenate([duv, dug], axis=1), jnp.concatenate([dcwv, dcwg], axis=1),
            jnp.concatenate([dcbv, dcbg], axis=1).reshape(F2))


def _final_loss(h, delta, g, target):
    S, D = h.shape
    ts = _tile(S, 512, 8)
    row = pl.BlockSpec((ts, D), lambda i: (i, 0))
    vec = pl.BlockSpec((1, D), lambda i: (0, 0))
    one = pl.BlockSpec((1, _LANES), lambda i: (0, 0))

    def body(h_ref, d_ref, g_ref, t_ref, l_ref, dh_ref, dg_ref):
        @pl.when(pl.program_id(0) == 0)
        def _():
            l_ref[...] = jnp.zeros_like(l_ref)
            dg_ref[...] = jnp.zeros_like(dg_ref)

        xv = h_ref[...] + d_ref[...]
        gv = g_ref[...]
        r = lax.rsqrt(jnp.mean(xv * xv, axis=-1, keepdims=True) + EPS)
        xh = xv * r
        err = xh * gv - t_ref[...]
        l_ref[...] += 0.5 * jnp.sum(jnp.mean(err * err, axis=-1, keepdims=True), axis=0, keepdims=True)
        dy = err * (1.0 / D)
        dg_ref[...] += jnp.sum(dy * xh, axis=0, keepdims=True)
        dxh = dy * gv
        dh_ref[...] = r * (dxh - xh * jnp.mean(dxh * xh, axis=-1, keepdims=True))

    loss, dh, dg = _call(body, name=f"final_loss_{S}", grid=(S // ts,), in_specs=[row, row, vec, row],
                         out_specs=[one, row, vec],
                         out_shape=[jax.ShapeDtypeStruct((1, _LANES), f32), jax.ShapeDtypeStruct((S, D), f32),
                                    jax.ShapeDtypeStruct((1, D), f32)],
                         compiler_params=_cp("arbitrary"))(h, delta, g.reshape(1, D), target)
    return loss[0, 0], dh, dg.reshape(D)


def _adamw(w, g, m, v):
    R, C = w.shape
    tr = _tile(R, max(8, (1 << 19) // C // 8 * 8), 8)
    blk = pl.BlockSpec((tr, C), lambda i: (i, 0))

    def body(w_ref, g_ref, m_ref, v_ref, d_ref, nm_ref, nv_ref):
        gv = g_ref[...]
        m_new = ADAM_B1 * m_ref[...] + (1.0 - ADAM_B1) * gv
        v_new = ADAM_B2 * v_ref[...] + (1.0 - ADAM_B2) * (gv * gv)
        m_hat = m_new / (1.0 - ADAM_B1 ** ADAM_STEP)
        v_hat = v_new / (1.0 - ADAM_B2 ** ADAM_STEP)
        d_ref[...] = -ADAM_LR * (m_hat / (jnp.sqrt(v_hat) + ADAM_EPS) + ADAM_WD * w_ref[...])
        nm_ref[...] = m_new
        nv_ref[...] = v_new

    o = jax.ShapeDtypeStruct((R, C), f32)
    return _call(body, name=f"adamw_{R}x{C}", grid=(R // tr,), in_specs=[blk] * 4, out_specs=[blk] * 3,
                 out_shape=[o, o, o], compiler_params=_cp("parallel"))(w, g, m, v)


def _sum_slabs(xs):
    K, R, C = xs.shape
    tr = _tile(R, 256, 8)

    def body(x_ref, o_ref):
        s = x_ref[0]
        for k in range(1, K):
            s = s + x_ref[k]
        o_ref[...] = s

    return _call(body, name=f"sum_slabs_{K}x{R}", grid=(R // tr,),
                 in_specs=[pl.BlockSpec((K, tr, C), lambda i: (0, i, 0))], out_specs=pl.BlockSpec((tr, C), lambda i: (i, 0)),
                 out_shape=jax.ShapeDtypeStruct((R, C), f32), compiler_params=_cp("parallel"))(xs)


def _slab_rows(R, C):
    return _tile(R, max(16, (1 << 19) // C // 16 * 16), 16)


def _add_partial(g, recv, core):
    _, R, C = g.shape
    tr = _slab_rows(R, C)

    def body(c_ref, a_ref, b_ref, o_ref, w_ref):
        s = a_ref[...] + b_ref[...]
        o_ref[...] = s
        w_ref[...] = s.astype(w_ref.dtype)

    blk = pl.BlockSpec((1, tr, C), lambda k, i, c: (k, i, 0))
    return _call(
        body, name=f"add_partial_{R}x{C}",
        grid_spec=pltpu.PrefetchScalarGridSpec(
            num_scalar_prefetch=1, grid=(4, R // tr),
            in_specs=[pl.BlockSpec((1, tr, C), lambda k, i, c: (2 * k + c[0], i, 0)), blk], out_specs=[blk, blk]),
        out_shape=[jax.ShapeDtypeStruct((4, R, C), f32), jax.ShapeDtypeStruct((4, R, C), _WIRE)],
        compiler_params=_cp("parallel", "parallel"))(core, g, recv)


def _sum_final(p, recv, chip):
    _, R, C = p.shape
    tr = _slab_rows(R, C)

    def body(q_ref, p_ref, r_ref, o_ref):
        o_ref[...] = ((p_ref[0] + r_ref[0].astype(f32)) + r_ref[1].astype(f32)) + r_ref[2].astype(f32)

    return _call(
        body, name=f"sum_final_{R}x{C}",
        grid_spec=pltpu.PrefetchScalarGridSpec(
            num_scalar_prefetch=1, grid=(R // tr,),
            in_specs=[pl.BlockSpec((1, tr, C), lambda i, q: (q[0], i, 0)), pl.BlockSpec((3, tr, C), lambda i, q: (0, i, 0))],
            out_specs=pl.BlockSpec((tr, C), lambda i, q: (i, 0))),
        out_shape=jax.ShapeDtypeStruct((R, C), f32), compiler_params=_cp("parallel"))(chip, p, recv)


def _my_pos():
    return lax.axis_index("x"), lax.axis_index("y"), lax.axis_index("c")


def _all_gather(shards):
    n = len(shards)

    def body(*refs):
        x_refs, out_refs = refs[:n], refs[n:2 * n]
        send_sems, recv_sems, local_sems = refs[2 * n:]
        x, y, c = _my_pos()
        me, sibling = (x, y, c), (x, y, 1 - c)
        chips = [(1 - x, y), (x, 1 - y), (1 - x, 1 - y)]

        def slab(t, px, py, pc):
            return out_refs[t].at[4 * px + 2 * py + pc]

        def copy(k, t, block, to, from_input=False):
            return pltpu.make_async_remote_copy(
                src_ref=x_refs[t] if from_input else slab(t, *block), dst_ref=slab(t, *block),
                send_sem=send_sems.at[k, t], recv_sem=recv_sems.at[k, t], device_id=to, device_id_type=MESH_ID)

        mine = [pltpu.make_async_copy(x_refs[t], slab(t, *me), local_sems.at[t]) for t in range(n)]
        first = [copy(0, t, me, sibling, True) for t in range(n)]
        first += [copy(1 + j, t, me, (*chip, c), True) for j, chip in enumerate(chips) for t in range(n)]
        for cp in mine + first:
            cp.start()
        passed = []
        for j, chip in enumerate(chips):
            for t in range(n):
                copy(1 + j, t, (*chip, c), me).wait_recv()
                passed.append(copy(4 + j, t, (*chip, c), sibling))
                passed[-1].start()
        for t in range(n):
            copy(0, t, sibling, me).wait_recv()
        for j, chip in enumerate(chips):
            for t in range(n):
                copy(4 + j, t, (*chip, 1 - c), me).wait_recv()
        for cp in first + passed:
            cp.wait_send()
        for cp in mine:
            cp.wait()

    tag = "_".join(f"{s.shape[0]}x{s.shape[1]}" for s in shards)
    return _call(body, name=f"all_gather_{tag}", in_specs=[ANY] * n, out_specs=[ANY] * n,
                 out_shape=[jax.ShapeDtypeStruct((NDEV,) + s.shape, s.dtype) for s in shards],
                 scratch_shapes=[pltpu.SemaphoreType.DMA((7, n)), pltpu.SemaphoreType.DMA((7, n)),
                                 pltpu.SemaphoreType.DMA((n,))])(*shards)


def _exchange_cores(gs):
    n = len(gs)

    def body(*refs):
        g_refs, out_refs = refs[:n], refs[n:2 * n]
        send_sems, recv_sems = refs[2 * n:]
        x, y, c = _my_pos()
        cps = [pltpu.make_async_remote_copy(src_ref=g_refs[t].at[2 * q + (1 - c)], dst_ref=out_refs[t].at[q],
                                            send_sem=send_sems.at[q, t], recv_sem=recv_sems.at[q, t],
                                            device_id=(x, y, 1 - c), device_id_type=MESH_ID)
               for t in range(n) for q in range(4)]
        for cp in cps:
            cp.start()
        for cp in cps:
            cp.wait()

    tag = "_".join(f"{g.shape[1]}x{g.shape[2]}" for g in gs)
    return _call(body, name=f"exchange_cores_{tag}", in_specs=[ANY] * n, out_specs=[ANY] * n,
                 out_shape=[jax.ShapeDtypeStruct((4,) + g.shape[1:], g.dtype) for g in gs],
                 scratch_shapes=[pltpu.SemaphoreType.DMA((4, n)), pltpu.SemaphoreType.DMA((4, n))])(*gs)


def _exchange_chips(ps):
    n = len(ps)

    def body(*refs):
        p_refs, out_refs = refs[:n], refs[n:2 * n]
        send_sems, recv_sems = refs[2 * n:]
        x, y, c = _my_pos()
        chips = [(x, 1 - y), (1 - x, y), (1 - x, 1 - y)]
        cps = [pltpu.make_async_remote_copy(src_ref=p_refs[t].at[2 * px + py], dst_ref=out_refs[t].at[r],
                                            send_sem=send_sems.at[r, t], recv_sem=recv_sems.at[r, t],
                                            device_id=(px, py, c), device_id_type=MESH_ID)
               for t in range(n) for r, (px, py) in enumerate(chips)]
        for cp in cps:
            cp.start()
        for cp in cps:
            cp.wait()

    tag = "_".join(f"{p.shape[1]}x{p.shape[2]}" for p in ps)
    return _call(body, name=f"exchange_chips_{tag}", in_specs=[ANY] * n, out_specs=[ANY] * n,
                 out_shape=[jax.ShapeDtypeStruct((3,) + p.shape[1:], p.dtype) for p in ps],
                 scratch_shapes=[pltpu.SemaphoreType.DMA((3, n)), pltpu.SemaphoreType.DMA((3, n))])(*ps)


def _pack(arrs, dtype, rows_mult):
    flat = jnp.concatenate([a.astype(dtype).reshape(-1) for a in arrs])
    q = rows_mult * _PACK_COLS
    tot = -(-flat.shape[0] // q) * q
    return jnp.pad(flat, (0, tot - flat.shape[0])).reshape(tot // _PACK_COLS, _PACK_COLS)


def _unpack(flat, shapes):
    out, off = [], 0
    for s in shapes:
        n = math.prod(s)
        out.append(flat[..., off:off + n].reshape(flat.shape[:-1] + tuple(s)))
        off += n
    return out


def _width_groups(names, shapes):
    groups = {}
    for n in names:
        groups.setdefault(shapes[n][-1], []).append(n)
    return list(groups.values())


def _to_full(piece, ax):
    t = jnp.moveaxis(piece, 0, ax)
    return t.reshape(t.shape[:ax] + (t.shape[ax] * t.shape[ax + 1],) + t.shape[ax + 2:])


def _to_shards(g, ax):
    t = g.reshape(g.shape[:ax] + (NDEV, g.shape[ax] // NDEV) + g.shape[ax + 1:])
    return jnp.moveaxis(t, ax, 0)


def _split_rows(buf, shapes):
    out, off = [], 0
    for s in shapes:
        r = math.prod(s[:-1])
        out.append(buf[..., off:off + r, :].reshape(buf.shape[:-2] + tuple(s)))
        off += r
    return out


def _gather_weights(shards):
    shapes = {n: shards[n].shape for n in SHARD_AXIS}
    groups = _width_groups(MXU_WEIGHTS, shapes)
    bufs = [jnp.concatenate([shards[n].astype(_MXU).reshape(-1, shapes[n][-1]) for n in grp], axis=0) for grp in groups]
    outs = _all_gather(bufs + [_pack([shards[n] for n in VEC_WEIGHTS], f32, 8)])
    full = {}
    for grp, g in zip(groups, outs):
        for n, piece in zip(grp, _split_rows(g, [shapes[n] for n in grp])):
            full[n] = _to_full(piece, SHARD_AXIS[n])
    for n, piece in zip(VEC_WEIGHTS, _unpack(outs[-1].reshape(NDEV, -1), [shapes[n] for n in VEC_WEIGHTS])):
        full[n] = _to_full(piece, SHARD_AXIS[n])
    return full


def _reduce_scatter_grads(grads, shapes):
    groups = _width_groups(MXU_WEIGHTS, shapes)
    bufs = [jnp.concatenate([_to_shards(grads[n], SHARD_AXIS[n]).reshape(NDEV, -1, shapes[n][-1]) for n in grp], axis=1)
            for grp in groups]
    small = jnp.concatenate([_to_shards(grads[n], SHARD_AXIS[n]).reshape(NDEV, -1) for n in VEC_WEIGHTS], axis=1)
    q = 8 * _PACK_COLS
    tot = -(-small.shape[1] // q) * q
    bufs.append(jnp.pad(small, ((0, 0), (0, tot - small.shape[1]))).reshape(NDEV, tot // _PACK_COLS, _PACK_COLS))
    core = jnp.reshape(lax.axis_index("c"), (1,)).astype(jnp.int32)
    chip = jnp.reshape(2 * lax.axis_index("x") + lax.axis_index("y"), (1,)).astype(jnp.int32)
    from_sibling = _exchange_cores(bufs)
    chip_sums = [_add_partial(b, r, core) for b, r in zip(bufs, from_sibling)]
    from_chips = _exchange_chips([wire for _, wire in chip_sums])
    totals = [_sum_final(p, r, chip) for (p, _), r in zip(chip_sums, from_chips)]
    out = {}
    for grp, t in zip(groups, totals):
        out.update(zip(grp, _split_rows(t, [shapes[n] for n in grp])))
    out.update(zip(VEC_WEIGHTS, _unpack(totals[-1].reshape(-1), [shapes[n] for n in VEC_WEIGHTS])))
    return out


def _all_reduce_small(grads, names):
    shapes = [grads[n].shape for n in names]
    packed = _pack([grads[n] for n in names], f32, 8)
    total = _sum_slabs(_all_gather([packed])[0])
    return dict(zip(names, _unpack(total.reshape(-1), shapes)))


def _as2d(a):
    if a.ndim == 1:
        return a.reshape(1, -1)
    return a.reshape(-1, a.shape[-1])


def kernel(x, mem, mix_norm_g, pool_w, pool_scale, sb_w_qkv, sb_w_o, s5_a_re, s5_a_im, s5_log_dt, s5_b_re, s5_b_im, s5_c_re, s5_c_im, s5_d, s5_w_glu, xa_norm_g, mem_norm_g, xa_wq, xa_wkv, xa_wo, ffn_norm_g, ffn_w_up, ffn_conv_w, ffn_conv_b, ffn_w_down, final_norm_g, loss_target, m_mix_norm_g, m_pool_w, m_pool_scale, m_sb_w_qkv, m_sb_w_o, m_s5_a_re, m_s5_a_im, m_s5_log_dt, m_s5_b_re, m_s5_b_im, m_s5_c_re, m_s5_c_im, m_s5_d, m_s5_w_glu, m_xa_norm_g, m_mem_norm_g, m_xa_wq, m_xa_wkv, m_xa_wo, m_ffn_norm_g, m_ffn_w_up, m_ffn_conv_w, m_ffn_conv_b, m_ffn_w_down, m_final_norm_g, v_mix_norm_g, v_pool_w, v_pool_scale, v_sb_w_qkv, v_sb_w_o, v_s5_a_re, v_s5_a_im, v_s5_log_dt, v_s5_b_re, v_s5_b_im, v_s5_c_re, v_s5_c_im, v_s5_d, v_s5_w_glu, v_xa_norm_g, v_mem_norm_g, v_xa_wq, v_xa_wkv, v_xa_wo, v_ffn_norm_g, v_ffn_w_up, v_ffn_conv_w, v_ffn_conv_b, v_ffn_w_down, v_final_norm_g):
    args = locals()
    w = {n: args[n] for n in WEIGHTS}
    mom = {n: args["m_" + n] for n in WEIGHTS}
    var = {n: args["v_" + n] for n in WEIGHTS}
    h0, memv, target = x[0], mem[0], loss_target[0]
    S, D = h0.shape
    depth = mix_norm_g.shape[0]
    n_mix = 3

    full = dict(w)
    full.update(_gather_weights(w))

    grads = {}

    def acc(name, j, val):
        grads.setdefault(name, {})[j] = val

    s5 = []
    for j in range(s5_a_re.shape[0]):
        G, P = s5_a_re.shape[1:]
        cg = s5_b_re.shape[3]
        N = G * P
        nb = D // _LANES
        ar, ai = s5_a_re[j].reshape(1, N), s5_a_im[j].reshape(1, N)
        ldt = jnp.repeat(s5_log_dt[j], P).reshape(1, N)
        br = s5_b_re[j].transpose(2, 0, 1).reshape(cg, N)
        bi = s5_b_im[j].transpose(2, 0, 1).reshape(cg, N)
        abr, abi, bbr, bbi = _s5_prep_fwd(ar, ai, ldt, br, bi)
        s5.append(dict(ar=ar, ai=ai, ldt=ldt, br=br, bi=bi, abr=abr, abi=abi, G=G, P=P, cg=cg, N=N, nb=nb,
                       Br=_blockdiag_in(bbr, nb, P), Bi=_blockdiag_in(bbi, nb, P),
                       Cr=_blockdiag_out(s5_c_re[j], nb), Ci=_blockdiag_out(s5_c_im[j], nb)))

    saved = []
    h, delta = h0, None
    for i in range(depth):
        kind, j = i % n_mix, i // n_mix
        sv = dict(kind=kind, j=j)
        if delta is None:
            hn = _rms_fwd(h, full['mix_norm_g'][i])
        else:
            h, hn = _rms_fwd(h, full['mix_norm_g'][i], delta)
        sv['h'] = h
        if kind == 0:
            p = _pool_fwd(hn)
            t = _pool_mix_fwd(p, full['pool_w'][j], full['pool_scale'][j])
            sv.update(p=p)
        elif kind == 1:
            qkv = _mm(hn, full['sb_w_qkv'][j], out_dtype=_MXU)
            o = _sb_fwd(qkv)
            t = _mm(o, full['sb_w_o'][j])
            sv.update(hn=hn, qkv=qkv, o=o)
        else:
            pr = s5[j]
            bur, bui = _s5_in_fwd(hn, pr['Br'], pr['Bi'])
            xr, xi = _s5_scan_fwd(bur, bui, pr['abr'], pr['abi'])
            y, z = _s5_out_fwd(xr, xi, pr['Cr'], pr['Ci'], hn, full['s5_d'][j])
            vg = _mm(z, full['s5_w_glu'][j])
            t = _glu_fwd(vg)
            sv.update(hn=hn, xr=xr, xi=xi, y=y, z=z, vg=vg)
        h1, a = _rms_fwd(h, full['xa_norm_g'][i], t)
        memn = _rms_fwd(memv, full['mem_norm_g'][i])
        q = _mm(a, full['xa_wq'][i], out_dtype=_MXU)
        kv = _mm(memn, full['xa_wkv'][i], out_dtype=_MXU)
        o2 = _xa_fwd(q, kv)
        mo = _mm(o2, full['xa_wo'][i])
        h2, b = _rms_fwd(h1, full['ffn_norm_g'][i], mo)
        u = _mm(b, full['ffn_w_up'][i])
        act = _convglu_fwd(u, full['ffn_conv_w'][i], full['ffn_conv_b'][i])
        delta = _mm(act, full['ffn_w_down'][i])
        sv.update(h1=h1, a=a, memn=memn, q=q, kv=kv, o2=o2, h2=h2, b=b, u=u, act=act)
        saved.append(sv)
        h = h2

    loss_local, dh, dg_final = _final_loss(h, delta, full['final_norm_g'], target)
    loss = lax.psum(loss_local, ("x", "y", "c"))
    grads['final_norm_g'] = dg_final

    for i in reversed(range(depth)):
        sv = saved[i]
        kind, j = sv['kind'], sv['j']
        dact = _mm(dh, full['ffn_w_down'][i], tb=True)
        acc('ffn_w_down', i, _mm(sv['act'], dh, ta=True))
        du, dcw, dcb = _convglu_bwd(sv['u'], full['ffn_conv_w'][i], full['ffn_conv_b'][i], dact)
        acc('ffn_conv_w', i, dcw)
        acc('ffn_conv_b', i, dcb)
        db = _mm(du, full['ffn_w_up'][i], tb=True)
        acc('ffn_w_up', i, _mm(sv['b'], du, ta=True))
        dh2, dg = _rms_bwd(sv['h2'], full['ffn_norm_g'][i], db, dh)
        acc('ffn_norm_g', i, dg)
        do2 = _mm(dh2, full['xa_wo'][i], tb=True, out_dtype=_MXU)
        acc('xa_wo', i, _mm(sv['o2'], dh2, ta=True))
        dq, dk, dv = _xa_bwd(sv['q'], sv['kv'], do2)
        dkv = jnp.concatenate([dk, dv], axis=1)
        da = _mm(dq, full['xa_wq'][i], tb=True)
        acc('xa_wq', i, _mm(sv['a'], dq, ta=True))
        dmemn = _mm(dkv, full['xa_wkv'][i], tb=True)
        acc('xa_wkv', i, _mm(sv['memn'], dkv, ta=True))
        _, dg = _rms_bwd(memv, full['mem_norm_g'][i], dmemn)
        acc('mem_norm_g', i, dg)
        dh1, dg = _rms_bwd(sv['h1'], full['xa_norm_g'][i], da, dh2)
        acc('xa_norm_g', i, dg)
        if kind == 0:
            dp, dw, ds = _pool_mix_bwd(sv['p'], full['pool_w'][j], full['pool_scale'][j], dh1)
            acc('pool_w', j, dw)
            acc('pool_scale', j, ds)
            dhn = _pool_bwd(dp)
        elif kind == 1:
            do = _mm(dh1, full['sb_w_o'][j], tb=True)
            acc('sb_w_o', j, _mm(sv['o'], dh1, ta=True))
            dqs, dks, dvs = _sb_bwd(sv['qkv'], sv['o'], do)
            dqkv = jnp.concatenate([dqs, dks, dvs], axis=1)
            dhn = _mm(dqkv, full['sb_w_qkv'][j], tb=True)
            acc('sb_w_qkv', j, _mm(sv['hn'], dqkv, ta=True))
        else:
            pr = s5[j]
            dval, dgate = _glu_bwd(sv['vg'], dh1)
            dvg = jnp.concatenate([dval, dgate], axis=1)
            dz = _mm(dvg, full['s5_w_glu'][j], tb=True)
            acc('s5_w_glu', j, _mm(sv['z'], dvg, ta=True))
            dy, gdr, gdi, du0, dd = _s5_out_bwd(dz, sv['y'], sv['hn'], full['s5_d'][j], pr['Cr'], pr['Ci'])
            acc('s5_d', j, dd)
            gr, gi, dabr, dabi = _s5_scan_bwd(gdr, gdi, sv['xr'], sv['xi'], pr['abr'], pr['abi'])
            dhn = _s5_in_bwd(gr, gi, pr['Br'], pr['Bi'], du0)
            dBr, dBi, dCr, dCi = _s5_wgrad(sv['hn'], gr, gi, sv['xr'], sv['xi'], dy)
            cg, P, G = pr['cg'], pr['P'], pr['G']
            acc('s5_c_re', j, _blockdiag_out_extract(dCr, cg, P))
            acc('s5_c_im', j, _blockdiag_out_extract(dCi, cg, P))
            dar, dai, dldt, dbr, dbi = _s5_prep_bwd(pr['ar'], pr['ai'], pr['ldt'], pr['br'], pr['bi'], dabr, dabi,
                                                    _blockdiag_in_extract(dBr, cg, P), _blockdiag_in_extract(dBi, cg, P))
            acc('s5_a_re', j, dar.reshape(G, P))
            acc('s5_a_im', j, dai.reshape(G, P))
            acc('s5_log_dt', j, dldt.reshape(G, P).sum(axis=1))
            acc('s5_b_re', j, dbr.reshape(cg, G, P).transpose(1, 2, 0))
            acc('s5_b_im', j, dbi.reshape(cg, G, P).transpose(1, 2, 0))
        dh, dg = _rms_bwd(sv['h'], full['mix_norm_g'][i], dhn, dh1)
        acc('mix_norm_g', i, dg)
    grad_x = dh[None]

    gfull = {}
    for n in WEIGHTS:
        gfull[n] = grads[n] if n == 'final_norm_g' else jnp.stack([grads[n][k] for k in range(len(grads[n]))])

    gw = {}
    gw.update(_reduce_scatter_grads(gfull, {n: w[n].shape for n in SHARD_AXIS}))
    gw.update(_all_reduce_small(gfull, REPLICATED))

    deltas, new_m, new_v = {}, {}, {}
    for n in WEIGHTS:
        shp = w[n].shape
        d_, m_, v_ = _adamw(_as2d(w[n]), _as2d(gw[n]), _as2d(mom[n]), _as2d(var[n]))
        deltas[n], new_m[n], new_v[n] = d_.reshape(shp), m_.reshape(shp), v_.reshape(shp)

    return (loss, grad_x, *[gw[n] for n in WEIGHTS], *[deltas[n] for n in WEIGHTS],
            *[new_m[n] for n in WEIGHTS], *[new_v[n] for n in WEIGHTS])
```

```python
import functools
import math

import jax
import jax.numpy as jnp
from jax import lax
from jax.experimental import pallas as pl
from jax.experimental.pallas import tpu as pltpu

f32 = jnp.float32
_MXU = jnp.bfloat16
_WIRE = _MXU
_VMEM_LIMIT = 48 * 1024 * 1024
_LANES = 128
_PACK_COLS = 1024

NDEV = 8
EPS = 1e-6
POOL_WINDOWS = (2, 4, 8, 16)
SB_HEAD_DIM = 64
XA_HEADS = 4
S5_GROUP = 16
S5_BLOCK_GROUPS = _LANES // S5_GROUP
ADAM_LR, ADAM_B1, ADAM_B2, ADAM_EPS, ADAM_WD, ADAM_STEP = 0.001, 0.9, 0.999, 1e-08, 0.01, 10

WEIGHTS = ['mix_norm_g', 'pool_w', 'pool_scale', 'sb_w_qkv', 'sb_w_o', 's5_a_re', 's5_a_im', 's5_log_dt',
           's5_b_re', 's5_b_im', 's5_c_re', 's5_c_im', 's5_d', 's5_w_glu', 'xa_norm_g', 'mem_norm_g', 'xa_wq',
           'xa_wkv', 'xa_wo', 'ffn_norm_g', 'ffn_w_up', 'ffn_conv_w', 'ffn_conv_b', 'ffn_w_down', 'final_norm_g']
SHARD_AXIS = {'pool_w': 2, 'pool_scale': 1, 'sb_w_qkv': 2, 'sb_w_o': 1, 's5_d': 1, 's5_w_glu': 2, 'xa_wq': 1,
              'xa_wkv': 2, 'xa_wo': 1, 'ffn_w_up': 2, 'ffn_conv_w': 2, 'ffn_w_down': 1}
MXU_WEIGHTS = ['pool_w', 'sb_w_qkv', 'sb_w_o', 's5_w_glu', 'xa_wq', 'xa_wkv', 'xa_wo', 'ffn_w_up', 'ffn_w_down']
VEC_WEIGHTS = ['pool_scale', 's5_d', 'ffn_conv_w']
REPLICATED = [n for n in WEIGHTS if n not in SHARD_AXIS]

_NN = (((1,), (0,)), ((), ()))
_NT = (((1,), (1,)), ((), ()))
_TN = (((0,), (0,)), ((), ()))
MESH_ID = pl.DeviceIdType.MESH
ANY = pl.BlockSpec(memory_space=pl.ANY)


def _call(body, **kw):
    return pl.pallas_call(body, **kw)


def _cp(*sem):
    return pltpu.CompilerParams(dimension_semantics=sem, vmem_limit_bytes=_VMEM_LIMIT)


def _tile(n, target, mult=_LANES):
    if n <= target:
        return n
    t = (target // mult) * mult
    while t >= mult:
        if n % t == 0:
            return t
        t -= mult
    return n


def _dot(a, b, dims=_NN):
    return lax.dot_general(a, b, dims, preferred_element_type=f32)


def _split(a):
    hi = a.astype(_MXU)
    lo = (a - hi.astype(f32)).astype(_MXU)
    return hi, lo


def _dot_hilo(a, u, dims=_NN):
    hi, lo = _split(a)
    return _dot(hi, u, dims) + _dot(lo, u, dims)


def _dot3(a, b, dims=_NN):
    ah, al = _split(a)
    bh, bl = _split(b)
    return _dot(ah, bh, dims) + _dot(al, bh, dims) + _dot(ah, bl, dims)


def _sigmoid(x):
    return 1.0 / (1.0 + jnp.exp(-x))


_GELU_C = math.sqrt(2.0 / math.pi)


def _gelu(x):
    return x * (0.5 * (1.0 + jnp.tanh(_GELU_C * (x + 0.044715 * (x * x * x)))))


def _gelu_grad(x):
    t = jnp.tanh(_GELU_C * (x + 0.044715 * (x * x * x)))
    return 0.5 * (1.0 + t) + x * 0.5 * (1.0 - t * t) * _GELU_C * (1.0 + 3.0 * 0.044715 * x * x)


def _shift_down(x, k, rows):
    return jnp.where(rows >= k, pltpu.roll(x, k, 0), 0.0)


def _shift_up(x, k, rows, n):
    return jnp.where(rows < n - k, pltpu.roll(x, n - k, 0), 0.0)


def _mm(a, b, *, ta=False, tb=False, out_dtype=f32):
    M, K = (a.shape[1], a.shape[0]) if ta else a.shape
    N = b.shape[0] if tb else b.shape[1]
    tm, tn, tk = _tile(M, 1408), _tile(N, 1536), _tile(K, 1408)
    nk = K // tk
    a_spec = pl.BlockSpec((tk, tm), lambda i, j, k: (k, i)) if ta else pl.BlockSpec((tm, tk), lambda i, j, k: (i, k))
    b_spec = pl.BlockSpec((tn, tk), lambda i, j, k: (j, k)) if tb else pl.BlockSpec((tk, tn), lambda i, j, k: (k, j))
    dims = (((0 if ta else 1,), (1 if tb else 0,)), ((), ()))

    def body(a_ref, b_ref, o_ref, acc_ref):
        k = pl.program_id(2)

        @pl.when(k == 0)
        def _():
            acc_ref[...] = jnp.zeros_like(acc_ref)

        acc_ref[...] += _dot(a_ref[...].astype(_MXU), b_ref[...].astype(_MXU), dims)

        @pl.when(k == nk - 1)
        def _():
            o_ref[...] = acc_ref[...].astype(out_dtype)

    return _call(
        body, name=f"mm_{'t' if ta else 'n'}{'t' if tb else 'n'}_{M}x{K}x{N}",
        grid=(M // tm, N // tn, nk), in_specs=[a_spec, b_spec],
        out_specs=pl.BlockSpec((tm, tn), lambda i, j, k: (i, j)),
        out_shape=jax.ShapeDtypeStruct((M, N), out_dtype),
        scratch_shapes=[pltpu.VMEM((tm, tn), f32)],
        compiler_params=_cp("parallel", "parallel", "arbitrary"))(a, b)


def _rms_fwd(x, g, delta=None, *, out_dtype=f32):
    S, D = x.shape
    ts = _tile(S, 512, 8)
    row = pl.BlockSpec((ts, D), lambda i: (i, 0))
    vec = pl.BlockSpec((1, D), lambda i: (0, 0))
    g2 = g.reshape(1, D)

    def norm(xv, g_ref):
        r = lax.rsqrt(jnp.mean(xv * xv, axis=-1, keepdims=True) + EPS)
        return ((xv * r) * g_ref[...]).astype(out_dtype)

    if delta is None:
        def body(x_ref, g_ref, y_ref):
            y_ref[...] = norm(x_ref[...], g_ref)

        return _call(body, name=f"rms_fwd_{S}", grid=(S // ts,), in_specs=[row, vec], out_specs=row,
                     out_shape=jax.ShapeDtypeStruct((S, D), out_dtype), compiler_params=_cp("parallel"))(x, g2)

    def body(x_ref, d_ref, g_ref, s_ref, y_ref):
        xv = x_ref[...] + d_ref[...]
        s_ref[...] = xv
        y_ref[...] = norm(xv, g_ref)

    return _call(body, name=f"add_rms_fwd_{S}", grid=(S // ts,), in_specs=[row, row, vec], out_specs=[row, row],
                 out_shape=[jax.ShapeDtypeStruct((S, D), f32), jax.ShapeDtypeStruct((S, D), out_dtype)],
                 compiler_params=_cp("parallel"))(x, delta, g2)


def _rms_bwd(x, g, dy, dres=None):
    S, D = x.shape
    ts = _tile(S, 512, 8)
    row = pl.BlockSpec((ts, D), lambda i: (i, 0))
    vec = pl.BlockSpec((1, D), lambda i: (0, 0))
    has_res = dres is not None

    def body(*refs):
        if has_res:
            x_ref, g_ref, dy_ref, dr_ref, dx_ref, dg_ref = refs
        else:
            x_ref, g_ref, dy_ref, dx_ref, dg_ref = refs
        xv = x_ref[...]
        r = lax.rsqrt(jnp.mean(xv * xv, axis=-1, keepdims=True) + EPS)
        xh = xv * r
        dyv = dy_ref[...].astype(f32)

        @pl.when(pl.program_id(0) == 0)
        def _():
            dg_ref[...] = jnp.zeros_like(dg_ref)

        dg_ref[...] += jnp.sum(dyv * xh, axis=0, keepdims=True)
        dxh = dyv * g_ref[...]
        dx = r * (dxh - xh * jnp.mean(dxh * xh, axis=-1, keepdims=True))
        if has_res:
            dx = dx + dr_ref[...]
        dx_ref[...] = dx

    ins = [x, g.reshape(1, D), dy] + ([dres] if has_res else [])
    dx, dg = _call(body, name=f"rms_bwd_{S}_{int(has_res)}", grid=(S // ts,),
                   in_specs=[row, vec, row] + ([row] if has_res else []), out_specs=[row, vec],
                   out_shape=[jax.ShapeDtypeStruct((S, D), f32), jax.ShapeDtypeStruct((1, D), f32)],
                   compiler_params=_cp("arbitrary"))(*ins)
    return dx, dg.reshape(D)


def _pool_windows_sum(x, win, rows):
    s, k = x, 1
    while k < win:
        s = s + _shift_down(s, k, rows)
        k *= 2
    return s


def _pool_windows_sum_up(x, win, rows, n):
    s, k = x, 1
    while k < win:
        s = s + _shift_up(s, k, rows, n)
        k *= 2
    return s


def _pool_fwd(hn):
    S, D = hn.shape
    cg = D // len(POOL_WINDOWS)
    tc = _tile(cg, 128)
    nb = cg // tc
    blk = pl.BlockSpec((S, tc), lambda gi, j: (0, gi * nb + j))

    def body(x_ref, p_ref):
        gi = pl.program_id(0)
        rows = lax.broadcasted_iota(jnp.int32, (S, 1), 0)
        cnt = (rows + 1).astype(f32)
        for k, win in enumerate(POOL_WINDOWS):
            @pl.when(gi == k)
            def _(win=win):
                x = x_ref[...]
                s = _pool_windows_sum(x, win, rows)
                p_ref[...] = (s / jnp.minimum(cnt, float(win)) - x).astype(p_ref.dtype)

    return _call(body, name=f"pool_fwd_{S}", grid=(len(POOL_WINDOWS), nb), in_specs=[blk], out_specs=blk,
                 out_shape=jax.ShapeDtypeStruct((S, D), _MXU), compiler_params=_cp("parallel", "parallel"))(hn)


def _pool_bwd(dp):
    S, D = dp.shape
    cg = D // len(POOL_WINDOWS)
    tc = _tile(cg, 128)
    nb = cg // tc
    blk = pl.BlockSpec((S, tc), lambda gi, j: (0, gi * nb + j))

    def body(dp_ref, dx_ref):
        gi = pl.program_id(0)
        rows = lax.broadcasted_iota(jnp.int32, (S, 1), 0)
        cnt = (rows + 1).astype(f32)
        for k, win in enumerate(POOL_WINDOWS):
            @pl.when(gi == k)
            def _(win=win):
                d = dp_ref[...]
                e = d / jnp.minimum(cnt, float(win))
                dx_ref[...] = _pool_windows_sum_up(e, win, rows, S) - d

    return _call(body, name=f"pool_bwd_{S}", grid=(len(POOL_WINDOWS), nb), in_specs=[blk], out_specs=blk,
                 out_shape=jax.ShapeDtypeStruct((S, D), f32), compiler_params=_cp("parallel", "parallel"))(dp)


def _pool_mix_fwd(p, w, scale):
    S, D = p.shape
    G, cg, _ = w.shape
    ts = _tile(S, 1024, 8)

    def body(p_ref, w_ref, s_ref, y_ref):
        y_ref[...] = _dot(p_ref[...], w_ref[0]) * s_ref[...]

    return _call(body, name=f"pool_mix_fwd_{S}", grid=(S // ts, G),
                 in_specs=[pl.BlockSpec((ts, cg), lambda i, g: (i, g)), pl.BlockSpec((1, cg, cg), lambda i, g: (g, 0, 0)),
                           pl.BlockSpec((1, cg), lambda i, g: (0, g))],
                 out_specs=pl.BlockSpec((ts, cg), lambda i, g: (i, g)),
                 out_shape=jax.ShapeDtypeStruct((S, D), f32), compiler_params=_cp("parallel", "parallel"))(
                     p, w, scale.reshape(1, D))


def _pool_mix_bwd(p, w, scale, dy):
    S, D = p.shape
    G, cg, _ = w.shape
    ts = _tile(S, 1024, 8)

    def body(p_ref, w_ref, s_ref, dy_ref, dp_ref, dw_ref, ds_ref):
        @pl.when(pl.program_id(1) == 0)
        def _():
            dw_ref[...] = jnp.zeros_like(dw_ref)
            ds_ref[...] = jnp.zeros_like(ds_ref)

        pv, wv, dyv = p_ref[...], w_ref[0], dy_ref[...]
        ypre = _dot(pv, wv)
        ds_ref[...] += jnp.sum(dyv * ypre, axis=0, keepdims=True)
        dyp = (dyv * s_ref[...]).astype(_MXU)
        dp_ref[...] = _dot(dyp, wv, _NT)
        dw_ref[0] += _dot(pv, dyp, _TN)

    dp, dw, ds = _call(
        body, name=f"pool_mix_bwd_{S}", grid=(G, S // ts),
        in_specs=[pl.BlockSpec((ts, cg), lambda g, i: (i, g)), pl.BlockSpec((1, cg, cg), lambda g, i: (g, 0, 0)),
                  pl.BlockSpec((1, cg), lambda g, i: (0, g)), pl.BlockSpec((ts, cg), lambda g, i: (i, g))],
        out_specs=[pl.BlockSpec((ts, cg), lambda g, i: (i, g)), pl.BlockSpec((1, cg, cg), lambda g, i: (g, 0, 0)),
                   pl.BlockSpec((1, cg), lambda g, i: (0, g))],
        out_shape=[jax.ShapeDtypeStruct((S, D), f32), jax.ShapeDtypeStruct((G, cg, cg), f32),
                   jax.ShapeDtypeStruct((1, D), f32)],
        compiler_params=_cp("parallel", "arbitrary"))(p, w, scale.reshape(1, D), dy)
    return dp, dw, ds.reshape(D)


def _sb_tile(S):
    return min(256, max(128, S // 4))


def _sb_scores(qm, ks, tri, R, U, scale):
    z = _dot(qm, ks, _NT) * scale
    soft = jnp.log(1.0 + jnp.exp(-jnp.abs(z)))
    lm = jnp.minimum(-z, 0.0) - soft
    lb = jnp.minimum(z, 0.0) - soft
    if tri is not None:
        lm = jnp.where(tri, lm, 0.0)
    c = _dot(lm.astype(_MXU), U)
    a = jnp.exp(lb + c + R)
    if tri is not None:
        a = jnp.where(tri, a, 0.0)
    return lm, lb, a


def _sb_consts(T):
    lane = lax.broadcasted_iota(jnp.int32, (1, _LANES), 1)
    row, col = lax.broadcasted_iota(jnp.int32, (T, T), 0), lax.broadcasted_iota(jnp.int32, (T, T), 1)
    U = jnp.where(row > col, 1.0, 0.0).astype(_MXU)
    heads = [(lane >= SB_HEAD_DIM * h) & (lane < SB_HEAD_DIM * (h + 1)) for h in range(_LANES // SB_HEAD_DIM)]
    return heads, col < row, U


def _sb_fwd(qkv, gather=()):
    S, D3 = qkv.shape
    D = D3 // 3
    HP = D // _LANES
    T = _sb_tile(S)
    nq = S // T
    n = len(gather)
    scale = SB_HEAD_DIM ** -0.5

    def body(*refs):
        q_ref, k_ref, v_ref = refs[:3]
        o_ref = refs[3 + n]
        i = pl.program_id(1)
        if n:
            step = pl.program_id(0) * nq + i
            start, forward, finish = _all_gather_stages(refs[3:3 + n], refs[4 + n:4 + 2 * n], *refs[4 + 2 * n:])
            pl.when(step == 0)(start)
            pl.when(step == (HP // 2) * nq)(forward)
        q = q_ref[...]
        heads, tri, U = _sb_consts(T)
        qms = [jnp.where(hm, q, jnp.zeros_like(q)) for hm in heads]

        def tile(kb, carry, mask):
            off = pl.multiple_of(kb * T, T)
            ks, vs = k_ref[pl.ds(off, T), :], v_ref[pl.ds(off, T), :]
            new = []
            for h, qm in enumerate(qms):
                R, acc = carry[2 * h], carry[2 * h + 1]
                lm, _, a = _sb_scores(qm, ks, mask, R, U, scale)
                new += [R + jnp.sum(lm, axis=1, keepdims=True), acc + _dot(a.astype(_MXU), vs)]
            return tuple(new)

        zero = (jnp.zeros((T, 1), f32), jnp.zeros((T, _LANES), f32)) * len(heads)
        carry = tile(i, zero, tri)
        carry = lax.fori_loop(1, i + 1, lambda j, c: tile(i - j, c, None), carry)
        out = jnp.zeros((T, _LANES), f32)
        for h, hm in enumerate(heads):
            out = out + jnp.where(hm, carry[2 * h + 1], 0.0)
        o_ref[...] = out
        if n:
            pl.when(step == HP * nq - 1)(finish)

    outs = _call(
        body, name=f"sb_fwd_{S}_{n}", grid=(HP, nq),
        in_specs=[pl.BlockSpec((T, _LANES), lambda hp, i: (i, hp)), pl.BlockSpec((S, _LANES), lambda hp, i: (0, HP + hp)),
                  pl.BlockSpec((S, _LANES), lambda hp, i: (0, 2 * HP + hp))] + [ANY] * n,
        out_specs=[pl.BlockSpec((T, _LANES), lambda hp, i: (i, hp))] + [ANY] * n,
        out_shape=[jax.ShapeDtypeStruct((S, D), f32)] + _all_gather_shapes(gather),
        scratch_shapes=_all_gather_sems(n) if n else [],
        compiler_params=_cp("arbitrary", "arbitrary"))(qkv, qkv, qkv, *gather)
    return outs[0], list(outs[1:])


def _sb_bwd(qkv, o, do, send=()):
    S, D3 = qkv.shape
    D = D3 // 3
    HP = D // _LANES
    T = _sb_tile(S)
    nq = S // T
    n = len(send)
    scale = SB_HEAD_DIM ** -0.5

    def body(*refs):
        q_ref, k_ref, v_ref, o_ref, do_ref = refs[:5]
        dq_ref, dk_ref, dv_ref = refs[5 + n:8 + n]
        dk_acc, dv_acc = refs[8 + 2 * n:10 + 2 * n]
        i = pl.program_id(1)
        if n:
            step = pl.program_id(0) * nq + i
            start, finish = _exchange_chips_stages(refs[5:5 + n], refs[8 + n:8 + 2 * n], *refs[10 + 2 * n:])
            pl.when(step == 0)(start)

        @pl.when(i == 0)
        def _():
            dk_acc[...] = jnp.zeros_like(dk_acc)
            dv_acc[...] = jnp.zeros_like(dv_acc)

        q = q_ref[...]
        dob = do_ref[...].astype(_MXU)
        prod = dob.astype(f32) * o_ref[...]
        heads, tri, U = _sb_consts(T)
        qms = [jnp.where(hm, q, jnp.zeros_like(q)) for hm in heads]
        doms = [jnp.where(hm, dob, jnp.zeros_like(dob)) for hm in heads]
        totals = [jnp.sum(jnp.where(hm, prod, 0.0), axis=1, keepdims=True) for hm in heads]

        def tile(kb, carry, mask):
            off = pl.multiple_of(kb * T, T)
            ks, vs = k_ref[pl.ds(off, T), :], v_ref[pl.ds(off, T), :]
            new = []
            dk_t = jnp.zeros((T, _LANES), f32)
            dv_t = jnp.zeros((T, _LANES), f32)
            for h, (qm, dom, total) in enumerate(zip(qms, doms, totals)):
                R, Gs, dq = carry[3 * h:3 * h + 3]
                lm, lb, a = _sb_scores(qm, ks, mask, R, U, scale)
                ab = a.astype(_MXU)
                g = ab.astype(f32) * _dot(dom, vs, _NT)
                before = total - (g + _dot_hilo(g, U) + Gs)
                beta = jnp.exp(lb)
                dz = (g * (1.0 - beta) - before * beta) * scale
                if mask is not None:
                    dz = jnp.where(mask, dz, 0.0)
                dzb = dz.astype(_MXU)
                dk_t = dk_t + _dot(dzb, qm, _TN)
                dv_t = dv_t + _dot(ab, dom, _TN)
                new += [R + jnp.sum(lm, axis=1, keepdims=True), Gs + jnp.sum(g, axis=1, keepdims=True), dq + _dot(dzb, ks)]
            dk_acc[pl.ds(off, T), :] += dk_t
            dv_acc[pl.ds(off, T), :] += dv_t
            return tuple(new)

        zero1 = jnp.zeros((T, 1), f32)
        carry = tile(i, (zero1, zero1, jnp.zeros((T, _LANES), f32)) * len(heads), tri)
        carry = lax.fori_loop(1, i + 1, lambda j, c: tile(i - j, c, None), carry)
        dq_out = jnp.zeros((T, _LANES), f32)
        for h, hm in enumerate(heads):
            dq_out = dq_out + jnp.where(hm, carry[3 * h + 2], 0.0)
        dq_ref[...] = dq_out.astype(dq_ref.dtype)

        @pl.when(i == nq - 1)
        def _():
            dk_ref[...] = dk_acc[...].astype(dk_ref.dtype)
            dv_ref[...] = dv_acc[...].astype(dv_ref.dtype)

        if n:
            pl.when(step == HP * nq - 1)(finish)

    qb = pl.BlockSpec((T, _LANES), lambda hp, i: (i, hp))
    col = pl.BlockSpec((S, _LANES), lambda hp, i: (0, hp))
    out = jax.ShapeDtypeStruct((S, D), _MXU)
    outs = _call(
        body, name=f"sb_bwd_{S}_{n}", grid=(HP, nq),
        in_specs=[qb, pl.BlockSpec((S, _LANES), lambda hp, i: (0, HP + hp)),
                  pl.BlockSpec((S, _LANES), lambda hp, i: (0, 2 * HP + hp)), qb, qb] + [ANY] * n,
        out_specs=[qb, col, col] + [ANY] * n, out_shape=[out, out, out] + _exchange_chips_shapes(send),
        scratch_shapes=[pltpu.VMEM((S, _LANES), f32), pltpu.VMEM((S, _LANES), f32)] + (_exchange_chips_sems(n) if n else []),
        compiler_params=_cp("arbitrary", "arbitrary"))(qkv, qkv, qkv, o, do, *send)
    return outs[0], outs[1], outs[2], list(outs[3:])


def _cmul(ar, ai, br, bi):
    return ar * br - ai * bi, ar * bi + ai * br


def _s5_coef(ar, ai, ldt):
    dt = jnp.exp(ldt)
    e = jnp.exp(ar * dt)
    abr, abi = e * jnp.cos(ai * dt), e * jnp.sin(ai * dt)
    inv = 1.0 / (ar * ar + ai * ai)
    cr, ci = _cmul(abr - 1.0, abi, ar * inv, -ai * inv)
    return dt, abr, abi, inv, cr, ci


def _s5_prep_fwd(ar, ai, ldt, br, bi):
    N = ar.shape[1]
    cg = br.shape[0]

    def body(ar_ref, ai_ref, ldt_ref, br_ref, bi_ref, abr_ref, abi_ref, bbr_ref, bbi_ref):
        _, abr, abi, _, cr, ci = _s5_coef(ar_ref[...], ai_ref[...], ldt_ref[...])
        abr_ref[...], abi_ref[...] = abr, abi
        bbr_ref[...], bbi_ref[...] = _cmul(cr, ci, br_ref[...], bi_ref[...])

    v, m = jax.ShapeDtypeStruct((1, N), f32), jax.ShapeDtypeStruct((cg, N), f32)
    return _call(body, name="s5_prep_fwd", out_shape=[v, v, m, m])(ar, ai, ldt, br, bi)


def _s5_prep_bwd(ar, ai, ldt, br, bi, dabr, dabi, dbbr, dbbi):
    N = ar.shape[1]
    cg = br.shape[0]

    def body(ar_ref, ai_ref, ldt_ref, br_ref, bi_ref, dabr_ref, dabi_ref, dbbr_ref, dbbi_ref,
             dar_ref, dai_ref, dldt_ref, dbr_ref, dbi_ref):
        a_r, a_i = ar_ref[...], ai_ref[...]
        dt, abr, abi, inv, cr, ci = _s5_coef(a_r, a_i, ldt_ref[...])
        b_r, b_i, gr, gi = br_ref[...], bi_ref[...], dbbr_ref[...], dbbi_ref[...]
        dbr_ref[...], dbi_ref[...] = _cmul(cr, -ci, gr, gi)
        dcr = jnp.sum(gr * b_r + gi * b_i, axis=0, keepdims=True)
        dci = jnp.sum(gi * b_r - gr * b_i, axis=0, keepdims=True)
        ilr, ili = a_r * inv, -a_i * inv
        dwr, dwi = _cmul(ilr, -ili, dcr, dci)
        qr, qi = _cmul(cr, ci, ilr, ili)
        dl1r, dl1i = _cmul(-qr, qi, dcr, dci)
        tr, ti = dabr_ref[...] + dwr, dabi_ref[...] + dwi
        ddlr, ddli = _cmul(abr, -abi, tr, ti)
        dar_ref[...] = dl1r + ddlr * dt
        dai_ref[...] = dl1i + ddli * dt
        dldt_ref[...] = (a_r * ddlr + a_i * ddli) * dt

    v, m = jax.ShapeDtypeStruct((1, N), f32), jax.ShapeDtypeStruct((cg, N), f32)
    return _call(body, name="s5_prep_bwd", out_shape=[v, v, v, m, m])(ar, ai, ldt, br, bi, dabr, dabi, dbbr, dbbi)


def _s5_in_fwd(u, Br, Bi):
    S, D = u.shape
    nb, _, nw = Br.shape
    ts = _tile(S, 512, 8)

    def body(u_ref, br_ref, bi_ref, or_ref, oi_ref):
        uv = u_ref[...]
        or_ref[...] = _dot3(uv, br_ref[0])
        oi_ref[...] = _dot3(uv, bi_ref[0])

    ub = pl.BlockSpec((ts, _LANES), lambda i, k: (i, k))
    wb = pl.BlockSpec((1, _LANES, nw), lambda i, k: (k, 0, 0))
    ob = pl.BlockSpec((ts, nw), lambda i, k: (i, k))
    o = jax.ShapeDtypeStruct((S, nb * nw), f32)
    return _call(body, name=f"s5_in_fwd_{S}", grid=(S // ts, nb), in_specs=[ub, wb, wb], out_specs=[ob, ob],
                 out_shape=[o, o], compiler_params=_cp("parallel", "parallel"))(u, Br, Bi)


def _s5_scan_fwd(bur, bui, abr, abi):
    S, N = bur.shape
    tt, tn = _tile(S, 256, 8), _tile(N, 1024)

    def body(br_ref, bi_ref, ar_ref, ai_ref, xr_ref, xi_ref, sr, si):
        @pl.when(pl.program_id(1) == 0)
        def _():
            sr[...] = jnp.zeros_like(sr)
            si[...] = jnp.zeros_like(si)

        a_r, a_i = ar_ref[...], ai_ref[...]

        def step(t, carry):
            xr, xi = carry
            nr = a_r * xr - a_i * xi + br_ref[pl.ds(t, 1), :]
            ni = a_r * xi + a_i * xr + bi_ref[pl.ds(t, 1), :]
            xr_ref[pl.ds(t, 1), :] = nr
            xi_ref[pl.ds(t, 1), :] = ni
            return nr, ni

        xr, xi = lax.fori_loop(0, tt, step, (sr[...], si[...]))
        sr[...], si[...] = xr, xi

    blk = pl.BlockSpec((tt, tn), lambda n, i: (i, n))
    vec = pl.BlockSpec((1, tn), lambda n, i: (0, n))
    o = jax.ShapeDtypeStruct((S, N), f32)
    return _call(body, name=f"s5_scan_fwd_{S}", grid=(N // tn, S // tt), in_specs=[blk, blk, vec, vec],
                 out_specs=[blk, blk], out_shape=[o, o],
                 scratch_shapes=[pltpu.VMEM((1, tn), f32), pltpu.VMEM((1, tn), f32)],
                 compiler_params=_cp("parallel", "arbitrary"))(bur, bui, abr, abi)


def _s5_scan_bwd(dr, di, xr, xi, abr, abi):
    S, N = dr.shape
    tt, tn = _tile(S, 256, 8), _tile(N, 1024)
    nt = S // tt

    def body(dr_ref, di_ref, xr_ref, xi_ref, ar_ref, ai_ref, gr_ref, gi_ref, dar_ref, dai_ref, sr, si):
        @pl.when(pl.program_id(1) == 0)
        def _():
            sr[...] = jnp.zeros_like(sr)
            si[...] = jnp.zeros_like(si)
            dar_ref[...] = jnp.zeros_like(dar_ref)
            dai_ref[...] = jnp.zeros_like(dai_ref)

        a_r, a_i = ar_ref[...], ai_ref[...]

        def step(j, carry):
            gr, gi, accr, acci = carry
            t = tt - 1 - j
            xr_t, xi_t = xr_ref[pl.ds(t, 1), :], xi_ref[pl.ds(t, 1), :]
            accr = accr + gr * xr_t + gi * xi_t
            acci = acci + gi * xr_t - gr * xi_t
            nr = dr_ref[pl.ds(t, 1), :] + a_r * gr + a_i * gi
            ni = di_ref[pl.ds(t, 1), :] + a_r * gi - a_i * gr
            gr_ref[pl.ds(t, 1), :] = nr
            gi_ref[pl.ds(t, 1), :] = ni
            return nr, ni, accr, acci

        gr, gi, accr, acci = lax.fori_loop(0, tt, step, (sr[...], si[...], dar_ref[...], dai_ref[...]))
        sr[...], si[...] = gr, gi
        dar_ref[...], dai_ref[...] = accr, acci

    blk = pl.BlockSpec((tt, tn), lambda n, i: (nt - 1 - i, n))
    vec = pl.BlockSpec((1, tn), lambda n, i: (0, n))
    o, v = jax.ShapeDtypeStruct((S, N), f32), jax.ShapeDtypeStruct((1, N), f32)
    return _call(body, name=f"s5_scan_bwd_{S}", grid=(N // tn, nt), in_specs=[blk, blk, blk, blk, vec, vec],
                 out_specs=[blk, blk, vec, vec], out_shape=[o, o, v, v],
                 scratch_shapes=[pltpu.VMEM((1, tn), f32), pltpu.VMEM((1, tn), f32)],
                 compiler_params=_cp("parallel", "arbitrary"))(dr, di, xr, xi, abr, abi)


def _s5_out_fwd(xr, xi, Cr, Ci, u, d):
    S, N = xr.shape
    nb, nw, _ = Cr.shape
    D = u.shape[1]
    ts = _tile(S, 512, 8)

    def body(xr_ref, xi_ref, cr_ref, ci_ref, u_ref, d_ref, y_ref, z_ref):
        y = _dot3(xr_ref[...], cr_ref[0]) - _dot3(xi_ref[...], ci_ref[0]) + d_ref[...] * u_ref[...]
        y_ref[...] = y
        z_ref[...] = _gelu(y).astype(z_ref.dtype)

    xb = pl.BlockSpec((ts, nw), lambda i, k: (i, k))
    cb = pl.BlockSpec((1, nw, _LANES), lambda i, k: (k, 0, 0))
    ub = pl.BlockSpec((ts, _LANES), lambda i, k: (i, k))
    db = pl.BlockSpec((1, _LANES), lambda i, k: (0, k))
    return _call(body, name=f"s5_out_fwd_{S}", grid=(S // ts, nb), in_specs=[xb, xb, cb, cb, ub, db],
                 out_specs=[ub, ub], out_shape=[jax.ShapeDtypeStruct((S, D), f32), jax.ShapeDtypeStruct((S, D), _MXU)],
                 compiler_params=_cp("parallel", "parallel"))(xr, xi, Cr, Ci, u, d.reshape(1, D))


def _s5_out_bwd(dz, y, u, d, Cr, Ci):
    S, D = y.shape
    nb, nw, _ = Cr.shape
    ts = _tile(S, 512, 8)

    def body(dz_ref, y_ref, u_ref, d_ref, cr_ref, ci_ref, dy_ref, gr_ref, gi_ref, du_ref, dd_ref):
        @pl.when(pl.program_id(1) == 0)
        def _():
            dd_ref[...] = jnp.zeros_like(dd_ref)

        dy = dz_ref[...] * _gelu_grad(y_ref[...])
        dy_ref[...] = dy
        gr_ref[...] = _dot3(dy, cr_ref[0], _NT)
        gi_ref[...] = -_dot3(dy, ci_ref[0], _NT)
        du_ref[...] = dy * d_ref[...]
        dd_ref[...] += jnp.sum(dy * u_ref[...], axis=0, keepdims=True)

    xb = pl.BlockSpec((ts, nw), lambda k, i: (i, k))
    cb = pl.BlockSpec((1, nw, _LANES), lambda k, i: (k, 0, 0))
    ub = pl.BlockSpec((ts, _LANES), lambda k, i: (i, k))
    db = pl.BlockSpec((1, _LANES), lambda k, i: (0, k))
    a, s = jax.ShapeDtypeStruct((S, D), f32), jax.ShapeDtypeStruct((S, nb * nw), f32)
    dy, gr, gi, du, dd = _call(
        body, name=f"s5_out_bwd_{S}", grid=(nb, S // ts), in_specs=[ub, ub, ub, db, cb, cb],
        out_specs=[ub, xb, xb, ub, db], out_shape=[a, s, s, a, jax.ShapeDtypeStruct((1, D), f32)],
        compiler_params=_cp("parallel", "arbitrary"))(dz, y, u, d.reshape(1, D), Cr, Ci)
    return dy, gr, gi, du, dd.reshape(D)


def _s5_in_bwd(gr, gi, Br, Bi, du0):
    S, N = gr.shape
    nb, _, nw = Br.shape
    ts = _tile(S, 512, 8)

    def body(gr_ref, gi_ref, br_ref, bi_ref, d0_ref, du_ref):
        du_ref[...] = d0_ref[...] + _dot3(gr_ref[...], br_ref[0], _NT) + _dot3(gi_ref[...], bi_ref[0], _NT)

    xb = pl.BlockSpec((ts, nw), lambda i, k: (i, k))
    wb = pl.BlockSpec((1, _LANES, nw), lambda i, k: (k, 0, 0))
    ub = pl.BlockSpec((ts, _LANES), lambda i, k: (i, k))
    return _call(body, name=f"s5_in_bwd_{S}", grid=(S // ts, nb), in_specs=[xb, xb, wb, wb, ub], out_specs=ub,
                 out_shape=jax.ShapeDtypeStruct((S, nb * _LANES), f32), compiler_params=_cp("parallel", "parallel"))(
                     gr, gi, Br, Bi, du0)


def _s5_wgrad(u, gr, gi, xr, xi, dy):
    S, D = u.shape
    nb = D // _LANES
    nw = gr.shape[1] // nb
    ts = _tile(S, 512, 8)

    def body(u_ref, gr_ref, gi_ref, xr_ref, xi_ref, dy_ref, dbr_ref, dbi_ref, dcr_ref, dci_ref):
        @pl.when(pl.program_id(1) == 0)
        def _():
            for r in (dbr_ref, dbi_ref, dcr_ref, dci_ref):
                r[...] = jnp.zeros_like(r)

        uv, dyv = u_ref[...], dy_ref[...]
        dbr_ref[0] += _dot3(uv, gr_ref[...], _TN)
        dbi_ref[0] += _dot3(uv, gi_ref[...], _TN)
        dcr_ref[0] += _dot3(xr_ref[...], dyv, _TN)
        dci_ref[0] -= _dot3(xi_ref[...], dyv, _TN)

    xb = pl.BlockSpec((ts, nw), lambda k, i: (i, k))
    ub = pl.BlockSpec((ts, _LANES), lambda k, i: (i, k))
    wb = pl.BlockSpec((1, _LANES, nw), lambda k, i: (k, 0, 0))
    cb = pl.BlockSpec((1, nw, _LANES), lambda k, i: (k, 0, 0))
    w, c = jax.ShapeDtypeStruct((nb, _LANES, nw), f32), jax.ShapeDtypeStruct((nb, nw, _LANES), f32)
    return _call(body, name=f"s5_wgrad_{S}", grid=(nb, S // ts), in_specs=[ub, xb, xb, xb, xb, ub],
                 out_specs=[wb, wb, cb, cb], out_shape=[w, w, c, c],
                 compiler_params=_cp("parallel", "arbitrary"))(u, gr, gi, xr, xi, dy)


def _glu_fwd(vg):
    S, D2 = vg.shape
    D = D2 // 2
    ts, tc = _tile(S, 1024, 8), _tile(D, 512)
    nc = D // tc

    def body(v_ref, g_ref, o_ref):
        o_ref[...] = v_ref[...] * _sigmoid(g_ref[...])

    return _call(body, name=f"glu_fwd_{S}", grid=(S // ts, nc),
                 in_specs=[pl.BlockSpec((ts, tc), lambda i, j: (i, j)), pl.BlockSpec((ts, tc), lambda i, j: (i, nc + j))],
                 out_specs=pl.BlockSpec((ts, tc), lambda i, j: (i, j)), out_shape=jax.ShapeDtypeStruct((S, D), f32),
                 compiler_params=_cp("parallel", "parallel"))(vg, vg)


def _glu_bwd(vg, dout):
    S, D2 = vg.shape
    D = D2 // 2
    ts, tc = _tile(S, 1024, 8), _tile(D, 512)
    nc = D // tc

    def body(v_ref, g_ref, do_ref, dv_ref, dg_ref):
        sg = _sigmoid(g_ref[...])
        do = do_ref[...]
        dv_ref[...] = (do * sg).astype(dv_ref.dtype)
        dg_ref[...] = (do * v_ref[...] * sg * (1.0 - sg)).astype(dg_ref.dtype)

    blk = pl.BlockSpec((ts, tc), lambda i, j: (i, j))
    o = jax.ShapeDtypeStruct((S, D), _MXU)
    return _call(body, name=f"glu_bwd_{S}", grid=(S // ts, nc),
                 in_specs=[blk, pl.BlockSpec((ts, tc), lambda i, j: (i, nc + j)), blk], out_specs=[blk, blk],
                 out_shape=[o, o], compiler_params=_cp("parallel", "parallel"))(vg, vg, dout)


def _blockdiag_in(b2, nb, P):
    cg = b2.shape[0]
    gb = S5_BLOCK_GROUPS
    t = b2.reshape(cg, nb, gb, P).transpose(1, 0, 2, 3)
    eye = jnp.eye(gb, dtype=b2.dtype)
    return (eye[None, :, None, :, None] * t[:, None]).reshape(nb, gb * cg, gb * P)


def _blockdiag_in_extract(db, cg, P):
    nb = db.shape[0]
    gb = S5_BLOCK_GROUPS
    eye = jnp.eye(gb, dtype=db.dtype)
    t = (db.reshape(nb, gb, cg, gb, P) * eye[None, :, None, :, None]).sum(3)
    return t.transpose(2, 0, 1, 3).reshape(cg, nb * gb * P)


def _blockdiag_out(c, nb):
    G, cg, P = c.shape
    gb = S5_BLOCK_GROUPS
    t = c.reshape(nb, gb, cg, P).transpose(0, 3, 1, 2)
    eye = jnp.eye(gb, dtype=c.dtype)
    return (eye[None, :, None, :, None] * t[:, None]).reshape(nb, gb * P, gb * cg)


def _blockdiag_out_extract(dc, cg, P):
    nb = dc.shape[0]
    gb = S5_BLOCK_GROUPS
    eye = jnp.eye(gb, dtype=dc.dtype)
    t = (dc.reshape(nb, gb, P, gb, cg) * eye[None, :, None, :, None]).sum(1)
    return t.transpose(0, 2, 3, 1).reshape(nb * gb, cg, P)


def _xa_fwd(q, kv):
    S, D = q.shape
    M = kv.shape[0]
    dh = D // XA_HEADS
    ts = _tile(S, 512, 8)
    scale = dh ** -0.5

    def body(q_ref, k_ref, v_ref, o_ref):
        s = _dot(q_ref[...], k_ref[...], _NT) * scale
        e = jnp.exp(s - jnp.max(s, axis=-1, keepdims=True))
        p = e / jnp.sum(e, axis=-1, keepdims=True)
        o_ref[...] = _dot(p.astype(_MXU), v_ref[...]).astype(o_ref.dtype)

    return _call(body, name=f"xa_fwd_{S}", grid=(S // ts, XA_HEADS),
                 in_specs=[pl.BlockSpec((ts, dh), lambda i, h: (i, h)), pl.BlockSpec((M, dh), lambda i, h: (0, h)),
                           pl.BlockSpec((M, dh), lambda i, h: (0, XA_HEADS + h))],
                 out_specs=pl.BlockSpec((ts, dh), lambda i, h: (i, h)), out_shape=jax.ShapeDtypeStruct((S, D), _MXU),
                 compiler_params=_cp("parallel", "parallel"))(q, kv, kv)


def _xa_bwd(q, kv, do):
    S, D = q.shape
    M = kv.shape[0]
    dh = D // XA_HEADS
    ts = _tile(S, 512, 8)
    scale = dh ** -0.5

    def body(q_ref, k_ref, v_ref, do_ref, dq_ref, dk_ref, dv_ref):
        @pl.when(pl.program_id(1) == 0)
        def _():
            dk_ref[...] = jnp.zeros_like(dk_ref)
            dv_ref[...] = jnp.zeros_like(dv_ref)

        qv, kv_, vv, dov = q_ref[...], k_ref[...], v_ref[...], do_ref[...]
        s = _dot(qv, kv_, _NT) * scale
        e = jnp.exp(s - jnp.max(s, axis=-1, keepdims=True))
        p = e / jnp.sum(e, axis=-1, keepdims=True)
        dp = _dot(dov, vv, _NT)
        dv_ref[...] += _dot(p.astype(_MXU), dov, _TN)
        ds = (p * (dp - jnp.sum(dp * p, axis=-1, keepdims=True)) * scale).astype(_MXU)
        dq_ref[...] = _dot(ds, kv_).astype(dq_ref.dtype)
        dk_ref[...] += _dot(ds, qv, _TN)

    qb = pl.BlockSpec((ts, dh), lambda h, i: (i, h))
    mb = pl.BlockSpec((M, dh), lambda h, i: (0, h))
    m = jax.ShapeDtypeStruct((M, D), f32)
    return _call(body, name=f"xa_bwd_{S}", grid=(XA_HEADS, S // ts),
                 in_specs=[qb, mb, pl.BlockSpec((M, dh), lambda h, i: (0, XA_HEADS + h)), qb],
                 out_specs=[qb, mb, mb], out_shape=[jax.ShapeDtypeStruct((S, D), _MXU), m, m],
                 compiler_params=_cp("parallel", "arbitrary"))(q, kv, kv, do)


def _conv(u, cw_ref, cb_ref, rows):
    return cw_ref[2:3, :] * u + cw_ref[1:2, :] * _shift_down(u, 1, rows) + cw_ref[0:1, :] * _shift_down(u, 2, rows) + cb_ref[...]


def _convglu_fwd(u, cw, cb):
    S, F2 = u.shape
    F = F2 // 2
    tc = _tile(F, 128)
    nc = F // tc

    def body(uv_ref, ug_ref, cwv_ref, cwg_ref, cbv_ref, cbg_ref, o_ref):
        rows = lax.broadcasted_iota(jnp.int32, (S, 1), 0)
        val = _conv(uv_ref[...], cwv_ref, cbv_ref, rows)
        gate = _conv(ug_ref[...], cwg_ref, cbg_ref, rows)
        o_ref[...] = (gate * _sigmoid(gate) * val).astype(o_ref.dtype)

    def col(r, off):
        return pl.BlockSpec((r, tc), lambda j: (0, off + j))

    return _call(body, name=f"convglu_fwd_{S}", grid=(nc,),
                 in_specs=[col(S, 0), col(S, nc), col(3, 0), col(3, nc), col(1, 0), col(1, nc)], out_specs=col(S, 0),
                 out_shape=jax.ShapeDtypeStruct((S, F), _MXU), compiler_params=_cp("parallel"))(
                     u, u, cw, cw, cb.reshape(1, F2), cb.reshape(1, F2))


def _convglu_bwd(u, cw, cb, dact):
    S, F2 = u.shape
    F = F2 // 2
    tc = _tile(F, 128)
    nc = F // tc

    def body(uv_ref, ug_ref, cwv_ref, cwg_ref, cbv_ref, cbg_ref, da_ref, duv_ref, dug_ref, dcwv_ref, dcwg_ref, dcbv_ref, dcbg_ref):
        rows = lax.broadcasted_iota(jnp.int32, (S, 1), 0)
        uv, ug = uv_ref[...], ug_ref[...]
        val = _conv(uv, cwv_ref, cbv_ref, rows)
        gate = _conv(ug, cwg_ref, cbg_ref, rows)
        sg = _sigmoid(gate)
        da = da_ref[...]
        dval = da * (gate * sg)
        dgate = da * val * (sg * (1.0 + gate * (1.0 - sg)))
        for uu, d, cw_ref, du_ref, dcw_ref, dcb_ref in ((uv, dval, cwv_ref, duv_ref, dcwv_ref, dcbv_ref),
                                                        (ug, dgate, cwg_ref, dug_ref, dcwg_ref, dcbg_ref)):
            dcb_ref[...] = jnp.sum(d, axis=0, keepdims=True)
            dcw_ref[2:3, :] = jnp.sum(d * uu, axis=0, keepdims=True)
            dcw_ref[1:2, :] = jnp.sum(d * _shift_down(uu, 1, rows), axis=0, keepdims=True)
            dcw_ref[0:1, :] = jnp.sum(d * _shift_down(uu, 2, rows), axis=0, keepdims=True)
            du = cw_ref[2:3, :] * d + cw_ref[1:2, :] * _shift_up(d, 1, rows, S) + cw_ref[0:1, :] * _shift_up(d, 2, rows, S)
            du_ref[...] = du.astype(du_ref.dtype)

    def col(r, off):
        return pl.BlockSpec((r, tc), lambda j: (0, off + j))

    o = jax.ShapeDtypeStruct((S, F), _MXU)
    w, b = jax.ShapeDtypeStruct((3, F), f32), jax.ShapeDtypeStruct((1, F), f32)
    duv, dug, dcwv, dcwg, dcbv, dcbg = _call(
        body, name=f"convglu_bwd_{S}", grid=(nc,),
        in_specs=[col(S, 0), col(S, nc), col(3, 0), col(3, nc), col(1, 0), col(1, nc), col(S, 0)],
        out_specs=[col(S, 0), col(S, 0), col(3, 0), col(3, 0), col(1, 0), col(1, 0)],
        out_shape=[o, o, w, w, b, b], compiler_params=_cp("parallel"))(
            u, u, cw, cw, cb.reshape(1, F2), cb.reshape(1, F2), dact)
    return (jnp.concatenate([duv, dug], axis=1), jnp.concatenate([dcwv, dcwg], axis=1),
            jnp.concatenate([dcbv, dcbg], axis=1).reshape(F2))


def _final_loss(h, delta, g, target):
    S, D = h.shape
    ts = _tile(S, 512, 8)
    row = pl.BlockSpec((ts, D), lambda i: (i, 0))
    vec = pl.BlockSpec((1, D), lambda i: (0, 0))
    one = pl.BlockSpec((1, _LANES), lambda i: (0, 0))

    def body(h_ref, d_ref, g_ref, t_ref, l_ref, dh_ref, dg_ref):
        @pl.when(pl.program_id(0) == 0)
        def _():
            l_ref[...] = jnp.zeros_like(l_ref)
            dg_ref[...] = jnp.zeros_like(dg_ref)

        xv = h_ref[...] + d_ref[...]
        gv = g_ref[...]
        r = lax.rsqrt(jnp.mean(xv * xv, axis=-1, keepdims=True) + EPS)
        xh = xv * r
        err = xh * gv - t_ref[...]
        l_ref[...] += 0.5 * jnp.sum(jnp.mean(err * err, axis=-1, keepdims=True), axis=0, keepdims=True)
        dy = err * (1.0 / D)
        dg_ref[...] += jnp.sum(dy * xh, axis=0, keepdims=True)
        dxh = dy * gv
        dh_ref[...] = r * (dxh - xh * jnp.mean(dxh * xh, axis=-1, keepdims=True))

    loss, dh, dg = _call(body, name=f"final_loss_{S}", grid=(S // ts,), in_specs=[row, row, vec, row],
                         out_specs=[one, row, vec],
                         out_shape=[jax.ShapeDtypeStruct((1, _LANES), f32), jax.ShapeDtypeStruct((S, D), f32),
                                    jax.ShapeDtypeStruct((1, D), f32)],
                         compiler_params=_cp("arbitrary"))(h, delta, g.reshape(1, D), target)
    return loss[0, 0], dh, dg.reshape(D)


def _adamw(w, g, m, v):
    R, C = w.shape
    tr = _tile(R, max(8, (1 << 19) // C // 8 * 8), 8)
    blk = pl.BlockSpec((tr, C), lambda i: (i, 0))

    def body(w_ref, g_ref, m_ref, v_ref, d_ref, nm_ref, nv_ref):
        gv = g_ref[...]
        m_new = ADAM_B1 * m_ref[...] + (1.0 - ADAM_B1) * gv
        v_new = ADAM_B2 * v_ref[...] + (1.0 - ADAM_B2) * (gv * gv)
        m_hat = m_new / (1.0 - ADAM_B1 ** ADAM_STEP)
        v_hat = v_new / (1.0 - ADAM_B2 ** ADAM_STEP)
        d_ref[...] = -ADAM_LR * (m_hat / (jnp.sqrt(v_hat) + ADAM_EPS) + ADAM_WD * w_ref[...])
        nm_ref[...] = m_new
        nv_ref[...] = v_new

    o = jax.ShapeDtypeStruct((R, C), f32)
    return _call(body, name=f"adamw_{R}x{C}", grid=(R // tr,), in_specs=[blk] * 4, out_specs=[blk] * 3,
                 out_shape=[o, o, o], compiler_params=_cp("parallel"))(w, g, m, v)


def _sum_slabs(xs):
    K, R, C = xs.shape
    tr = _tile(R, 256, 8)

    def body(x_ref, o_ref):
        s = x_ref[0]
        for k in range(1, K):
            s = s + x_ref[k]
        o_ref[...] = s

    return _call(body, name=f"sum_slabs_{K}x{R}", grid=(R // tr,),
                 in_specs=[pl.BlockSpec((K, tr, C), lambda i: (0, i, 0))], out_specs=pl.BlockSpec((tr, C), lambda i: (i, 0)),
                 out_shape=jax.ShapeDtypeStruct((R, C), f32), compiler_params=_cp("parallel"))(xs)


def _slab_rows(R, C):
    return _tile(R, max(16, (1 << 19) // C // 16 * 16), 16)


def _add_partial(g, recv, core):
    _, R, C = g.shape
    tr = _slab_rows(R, C)

    def body(c_ref, a_ref, b_ref, o_ref, w_ref):
        s = a_ref[...] + b_ref[...]
        o_ref[...] = s
        w_ref[...] = s.astype(w_ref.dtype)

    blk = pl.BlockSpec((1, tr, C), lambda k, i, c: (k, i, 0))
    return _call(
        body, name=f"add_partial_{R}x{C}",
        grid_spec=pltpu.PrefetchScalarGridSpec(
            num_scalar_prefetch=1, grid=(4, R // tr),
            in_specs=[pl.BlockSpec((1, tr, C), lambda k, i, c: (2 * k + c[0], i, 0)), blk], out_specs=[blk, blk]),
        out_shape=[jax.ShapeDtypeStruct((4, R, C), f32), jax.ShapeDtypeStruct((4, R, C), _WIRE)],
        compiler_params=_cp("parallel", "parallel"))(core, g, recv)


def _sum_final(p, recv, chip):
    _, R, C = p.shape
    tr = _slab_rows(R, C)

    def body(q_ref, p_ref, r_ref, o_ref):
        o_ref[...] = ((p_ref[0] + r_ref[0].astype(f32)) + r_ref[1].astype(f32)) + r_ref[2].astype(f32)

    return _call(
        body, name=f"sum_final_{R}x{C}",
        grid_spec=pltpu.PrefetchScalarGridSpec(
            num_scalar_prefetch=1, grid=(R // tr,),
            in_specs=[pl.BlockSpec((1, tr, C), lambda i, q: (q[0], i, 0)), pl.BlockSpec((3, tr, C), lambda i, q: (0, i, 0))],
            out_specs=pl.BlockSpec((tr, C), lambda i, q: (i, 0))),
        out_shape=jax.ShapeDtypeStruct((R, C), f32), compiler_params=_cp("parallel"))(chip, p, recv)


def _my_pos():
    return lax.axis_index("x"), lax.axis_index("y"), lax.axis_index("c")


def _all_gather(shards):
    n = len(shards)

    def body(*refs):
        start, forward, finish = _all_gather_stages(refs[:n], refs[n:2 * n], *refs[2 * n:])
        start()
        forward()
        finish()

    tag = "_".join(f"{s.shape[0]}x{s.shape[1]}" for s in shards)
    return _call(body, name=f"all_gather_{tag}", in_specs=[ANY] * n, out_specs=[ANY] * n,
                 out_shape=_all_gather_shapes(shards), scratch_shapes=_all_gather_sems(n))(*shards)


def _all_gather_shapes(shards):
    return [jax.ShapeDtypeStruct((NDEV,) + s.shape, s.dtype) for s in shards]


def _all_gather_sems(n):
    return [pltpu.SemaphoreType.DMA((7, n)), pltpu.SemaphoreType.DMA((7, n)), pltpu.SemaphoreType.DMA((n,))]


def _all_gather_stages(x_refs, out_refs, send_sems, recv_sems, local_sems):
    n = len(x_refs)
    x, y, c = _my_pos()
    me, sibling = (x, y, c), (x, y, 1 - c)
    chips = [(1 - x, y), (x, 1 - y), (1 - x, 1 - y)]

    def slab(t, px, py, pc):
        return out_refs[t].at[4 * px + 2 * py + pc]

    def copy(k, t, block, to, from_input=False):
        return pltpu.make_async_remote_copy(
            src_ref=x_refs[t] if from_input else slab(t, *block), dst_ref=slab(t, *block),
            send_sem=send_sems.at[k, t], recv_sem=recv_sems.at[k, t], device_id=to, device_id_type=MESH_ID)

    mine = [pltpu.make_async_copy(x_refs[t], slab(t, *me), local_sems.at[t]) for t in range(n)]
    first = [copy(0, t, me, sibling, True) for t in range(n)]
    first += [copy(1 + j, t, me, (*chip, c), True) for j, chip in enumerate(chips) for t in range(n)]
    passed = [copy(4 + j, t, (*chip, c), sibling) for j, chip in enumerate(chips) for t in range(n)]

    def start():
        for cp in mine + first:
            cp.start()

    def forward():
        for j, chip in enumerate(chips):
            for t in range(n):
                copy(1 + j, t, (*chip, c), me).wait_recv()
                passed[j * n + t].start()

    def finish():
        for t in range(n):
            copy(0, t, sibling, me).wait_recv()
        for j, chip in enumerate(chips):
            for t in range(n):
                copy(4 + j, t, (*chip, 1 - c), me).wait_recv()
        for cp in first + passed:
            cp.wait_send()
        for cp in mine:
            cp.wait()

    return start, forward, finish


def _exchange_cores(gs):
    n = len(gs)

    def body(*refs):
        g_refs, out_refs = refs[:n], refs[n:2 * n]
        send_sems, recv_sems = refs[2 * n:]
        x, y, c = _my_pos()
        cps = [pltpu.make_async_remote_copy(src_ref=g_refs[t].at[2 * q + (1 - c)], dst_ref=out_refs[t].at[q],
                                            send_sem=send_sems.at[q, t], recv_sem=recv_sems.at[q, t],
                                            device_id=(x, y, 1 - c), device_id_type=MESH_ID)
               for t in range(n) for q in range(4)]
        for cp in cps:
            cp.start()
        for cp in cps:
            cp.wait()

    tag = "_".join(f"{g.shape[1]}x{g.shape[2]}" for g in gs)
    return _call(body, name=f"exchange_cores_{tag}", in_specs=[ANY] * n, out_specs=[ANY] * n,
                 out_shape=[jax.ShapeDtypeStruct((4,) + g.shape[1:], g.dtype) for g in gs],
                 scratch_shapes=[pltpu.SemaphoreType.DMA((4, n)), pltpu.SemaphoreType.DMA((4, n))])(*gs)


def _exchange_chips(ps):
    n = len(ps)

    def body(*refs):
        start, finish = _exchange_chips_stages(refs[:n], refs[n:2 * n], *refs[2 * n:])
        start()
        finish()

    tag = "_".join(f"{p.shape[1]}x{p.shape[2]}" for p in ps)
    return _call(body, name=f"exchange_chips_{tag}", in_specs=[ANY] * n, out_specs=[ANY] * n,
                 out_shape=_exchange_chips_shapes(ps), scratch_shapes=_exchange_chips_sems(n))(*ps)


def _exchange_chips_shapes(ps):
    return [jax.ShapeDtypeStruct((3,) + p.shape[1:], p.dtype) for p in ps]


def _exchange_chips_sems(n):
    return [pltpu.SemaphoreType.DMA((3, n)), pltpu.SemaphoreType.DMA((3, n))]


def _exchange_chips_stages(p_refs, out_refs, send_sems, recv_sems):
    n = len(p_refs)
    x, y, c = _my_pos()
    chips = [(x, 1 - y), (1 - x, y), (1 - x, 1 - y)]
    cps = [pltpu.make_async_remote_copy(src_ref=p_refs[t].at[2 * px + py], dst_ref=out_refs[t].at[r],
                                        send_sem=send_sems.at[r, t], recv_sem=recv_sems.at[r, t],
                                        device_id=(px, py, c), device_id_type=MESH_ID)
           for t in range(n) for r, (px, py) in enumerate(chips)]

    def start():
        for cp in cps:
            cp.start()

    def finish():
        for cp in cps:
            cp.wait()

    return start, finish


def _pack(arrs, dtype, rows_mult):
    flat = jnp.concatenate([a.astype(dtype).reshape(-1) for a in arrs])
    q = rows_mult * _PACK_COLS
    tot = -(-flat.shape[0] // q) * q
    return jnp.pad(flat, (0, tot - flat.shape[0])).reshape(tot // _PACK_COLS, _PACK_COLS)


def _unpack(flat, shapes):
    out, off = [], 0
    for s in shapes:
        n = math.prod(s)
        out.append(flat[..., off:off + n].reshape(flat.shape[:-1] + tuple(s)))
        off += n
    return out


def _width_groups(names, shapes):
    groups = {}
    for n in names:
        groups.setdefault(shapes[n][-1], []).append(n)
    return list(groups.values())


def _to_full(piece, ax):
    t = jnp.moveaxis(piece, 0, ax)
    return t.reshape(t.shape[:ax] + (t.shape[ax] * t.shape[ax + 1],) + t.shape[ax + 2:])


def _to_shards(g, ax):
    t = g.reshape(g.shape[:ax] + (NDEV, g.shape[ax] // NDEV) + g.shape[ax + 1:])
    return jnp.moveaxis(t, ax, 0)


def _split_rows(buf, shapes):
    out, off = [], 0
    for s in shapes:
        r = math.prod(s[:-1])
        out.append(buf[..., off:off + r, :].reshape(buf.shape[:-2] + tuple(s)))
        off += r
    return out


def _item_ax(item):
    return SHARD_AXIS[item[0]] - 1


def _gather_bufs(shards, items):
    shapes = {it: shards[it[0]].shape[1:] for it in items}
    groups = _width_groups(items, shapes)
    bufs = [jnp.concatenate([shards[n][l].astype(_MXU).reshape(-1, shapes[(n, l)][-1]) for n, l in grp], axis=0)
            for grp in groups]
    return groups, bufs


def _gather_unpack(shards, groups, outs):
    full = {}
    for grp, g in zip(groups, outs):
        for it, piece in zip(grp, _split_rows(g, [shards[it[0]].shape[1:] for it in grp])):
            full[it] = _to_full(piece, _item_ax(it))
    return full


def _scatter_bufs(grads, shards, items):
    shapes = {it: shards[it[0]].shape[1:] for it in items}
    groups = _width_groups(items, shapes)
    bufs = [jnp.concatenate([_to_shards(grads[it], _item_ax(it)).reshape(NDEV, -1, shapes[it][-1]) for it in grp], axis=1)
            for grp in groups]
    bufs = [jnp.pad(b, ((0, 0), (0, -b.shape[1] % 256), (0, 0))) for b in bufs]
    return groups, bufs


def _mesh_scalars():
    core = jnp.reshape(lax.axis_index("c"), (1,)).astype(jnp.int32)
    chip = jnp.reshape(2 * lax.axis_index("x") + lax.axis_index("y"), (1,)).astype(jnp.int32)
    return core, chip


def _chip_sums(bufs):
    core, _ = _mesh_scalars()
    return [_add_partial(b, r, core) for b, r in zip(bufs, _exchange_cores(bufs))]


def _device_sums(chip_sums, from_chips):
    _, chip = _mesh_scalars()
    return [_sum_final(p, r, chip) for (p, _), r in zip(chip_sums, from_chips)]


def _scatter_unpack(shards, groups, totals):
    out = {}
    for grp, t in zip(groups, totals):
        out.update(zip(grp, _split_rows(t, [shards[it[0]].shape[1:] for it in grp])))
    return out


def _small_scatter_buf(grads, names):
    small = jnp.concatenate([_to_shards(grads[n], SHARD_AXIS[n]).reshape(NDEV, -1) for n in names], axis=1)
    q = 16 * _PACK_COLS
    tot = -(-small.shape[1] // q) * q
    return jnp.pad(small, ((0, 0), (0, tot - small.shape[1]))).reshape(NDEV, tot // _PACK_COLS, _PACK_COLS)


def _all_reduce_small(grads, names):
    shapes = [grads[n].shape for n in names]
    packed = _pack([grads[n] for n in names], f32, 8)
    total = _sum_slabs(_all_gather([packed])[0])
    return dict(zip(names, _unpack(total.reshape(-1), shapes)))


def _as2d(a):
    if a.ndim == 1:
        return a.reshape(1, -1)
    return a.reshape(-1, a.shape[-1])


def kernel(x, mem, mix_norm_g, pool_w, pool_scale, sb_w_qkv, sb_w_o, s5_a_re, s5_a_im, s5_log_dt, s5_b_re, s5_b_im, s5_c_re, s5_c_im, s5_d, s5_w_glu, xa_norm_g, mem_norm_g, xa_wq, xa_wkv, xa_wo, ffn_norm_g, ffn_w_up, ffn_conv_w, ffn_conv_b, ffn_w_down, final_norm_g, loss_target, m_mix_norm_g, m_pool_w, m_pool_scale, m_sb_w_qkv, m_sb_w_o, m_s5_a_re, m_s5_a_im, m_s5_log_dt, m_s5_b_re, m_s5_b_im, m_s5_c_re, m_s5_c_im, m_s5_d, m_s5_w_glu, m_xa_norm_g, m_mem_norm_g, m_xa_wq, m_xa_wkv, m_xa_wo, m_ffn_norm_g, m_ffn_w_up, m_ffn_conv_w, m_ffn_conv_b, m_ffn_w_down, m_final_norm_g, v_mix_norm_g, v_pool_w, v_pool_scale, v_sb_w_qkv, v_sb_w_o, v_s5_a_re, v_s5_a_im, v_s5_log_dt, v_s5_b_re, v_s5_b_im, v_s5_c_re, v_s5_c_im, v_s5_d, v_s5_w_glu, v_xa_norm_g, v_mem_norm_g, v_xa_wq, v_xa_wkv, v_xa_wo, v_ffn_norm_g, v_ffn_w_up, v_ffn_conv_w, v_ffn_conv_b, v_ffn_w_down, v_final_norm_g):
    args = locals()
    w = {n: args[n] for n in WEIGHTS}
    mom = {n: args["m_" + n] for n in WEIGHTS}
    var = {n: args["v_" + n] for n in WEIGHTS}
    h0, memv, target = x[0], mem[0], loss_target[0]
    S, D = h0.shape
    depth = mix_norm_g.shape[0]
    n_mix = 3

    first_layer = {'pool_w': 0, 'sb_w_qkv': 1, 'sb_w_o': 1, 's5_w_glu': 2}

    def layer_of(item):
        return first_layer[item[0]] + n_mix * item[1] if item[0] in first_layer else item[1]

    sb_layer = 1 if depth > 1 else None
    items = [(n, l) for n in MXU_WEIGHTS for l in range(w[n].shape[0])]
    early_w = [it for it in items if sb_layer is None or layer_of(it) < sb_layer or it == ('sb_w_qkv', 0)]
    late_w = [it for it in items if it not in early_w]
    groups_a, bufs_a = _gather_bufs(w, early_w)
    outs_a = _all_gather(bufs_a + [_pack([w[n] for n in VEC_WEIGHTS], f32, 8)])
    wfull = _gather_unpack(w, groups_a, outs_a[:-1])
    full = dict(w)
    for n, piece in zip(VEC_WEIGHTS, _unpack(outs_a[-1].reshape(NDEV, -1), [w[n].shape for n in VEC_WEIGHTS])):
        full[n] = _to_full(piece, SHARD_AXIS[n])
    groups_b, bufs_b = _gather_bufs(w, late_w)

    grads = {}

    def acc(name, j, val):
        grads.setdefault(name, {})[j] = val

    s5 = []
    for j in range(s5_a_re.shape[0]):
        G, P = s5_a_re.shape[1:]
        cg = s5_b_re.shape[3]
        N = G * P
        nb = D // _LANES
        ar, ai = s5_a_re[j].reshape(1, N), s5_a_im[j].reshape(1, N)
        ldt = jnp.repeat(s5_log_dt[j], P).reshape(1, N)
        br = s5_b_re[j].transpose(2, 0, 1).reshape(cg, N)
        bi = s5_b_im[j].transpose(2, 0, 1).reshape(cg, N)
        abr, abi, bbr, bbi = _s5_prep_fwd(ar, ai, ldt, br, bi)
        s5.append(dict(ar=ar, ai=ai, ldt=ldt, br=br, bi=bi, abr=abr, abi=abi, G=G, P=P, cg=cg, N=N, nb=nb,
                       Br=_blockdiag_in(bbr, nb, P), Bi=_blockdiag_in(bbi, nb, P),
                       Cr=_blockdiag_out(s5_c_re[j], nb), Ci=_blockdiag_out(s5_c_im[j], nb)))

    saved = []
    h, delta = h0, None
    for i in range(depth):
        kind, j = i % n_mix, i // n_mix
        sv = dict(kind=kind, j=j)
        if delta is None:
            hn = _rms_fwd(h, full['mix_norm_g'][i])
        else:
            h, hn = _rms_fwd(h, full['mix_norm_g'][i], delta)
        sv['h'] = h
        if kind == 0:
            p = _pool_fwd(hn)
            t = _pool_mix_fwd(p, wfull['pool_w', j], full['pool_scale'][j])
            sv.update(p=p)
        elif kind == 1:
            qkv = _mm(hn, wfull['sb_w_qkv', j], out_dtype=_MXU)
            if i == sb_layer:
                o, outs_b = _sb_fwd(qkv, bufs_b)
                wfull.update(_gather_unpack(w, groups_b, outs_b))
            else:
                o, _ = _sb_fwd(qkv)
            t = _mm(o, wfull['sb_w_o', j])
            sv.update(hn=hn, qkv=qkv, o=o)
        else:
            pr = s5[j]
            bur, bui = _s5_in_fwd(hn, pr['Br'], pr['Bi'])
            xr, xi = _s5_scan_fwd(bur, bui, pr['abr'], pr['abi'])
            y, z = _s5_out_fwd(xr, xi, pr['Cr'], pr['Ci'], hn, full['s5_d'][j])
            vg = _mm(z, wfull['s5_w_glu', j])
            t = _glu_fwd(vg)
            sv.update(hn=hn, xr=xr, xi=xi, y=y, z=z, vg=vg)
        h1, a = _rms_fwd(h, full['xa_norm_g'][i], t)
        memn = _rms_fwd(memv, full['mem_norm_g'][i])
        q = _mm(a, wfull['xa_wq', i], out_dtype=_MXU)
        kv = _mm(memn, wfull['xa_wkv', i], out_dtype=_MXU)
        o2 = _xa_fwd(q, kv)
        mo = _mm(o2, wfull['xa_wo', i])
        h2, b = _rms_fwd(h1, full['ffn_norm_g'][i], mo)
        u = _mm(b, wfull['ffn_w_up', i])
        act = _convglu_fwd(u, full['ffn_conv_w'][i], full['ffn_conv_b'][i])
        delta = _mm(act, wfull['ffn_w_down', i])
        sv.update(h1=h1, a=a, memn=memn, q=q, kv=kv, o2=o2, h2=h2, b=b, u=u, act=act)
        saved.append(sv)
        h = h2

    loss_local, dh, dg_final = _final_loss(h, delta, full['final_norm_g'], target)
    loss = lax.psum(loss_local, ("x", "y", "c"))
    grads['final_norm_g'] = dg_final

    for i in reversed(range(depth)):
        sv = saved[i]
        kind, j = sv['kind'], sv['j']
        dact = _mm(dh, wfull['ffn_w_down', i], tb=True)
        acc('ffn_w_down', i, _mm(sv['act'], dh, ta=True))
        du, dcw, dcb = _convglu_bwd(sv['u'], full['ffn_conv_w'][i], full['ffn_conv_b'][i], dact)
        acc('ffn_conv_w', i, dcw)
        acc('ffn_conv_b', i, dcb)
        db = _mm(du, wfull['ffn_w_up', i], tb=True)
        acc('ffn_w_up', i, _mm(sv['b'], du, ta=True))
        dh2, dg = _rms_bwd(sv['h2'], full['ffn_norm_g'][i], db, dh)
        acc('ffn_norm_g', i, dg)
        do2 = _mm(dh2, wfull['xa_wo', i], tb=True, out_dtype=_MXU)
        acc('xa_wo', i, _mm(sv['o2'], dh2, ta=True))
        dq, dk, dv = _xa_bwd(sv['q'], sv['kv'], do2)
        dkv = jnp.concatenate([dk, dv], axis=1)
        da = _mm(dq, wfull['xa_wq', i], tb=True)
        acc('xa_wq', i, _mm(sv['a'], dq, ta=True))
        dmemn = _mm(dkv, wfull['xa_wkv', i], tb=True)
        acc('xa_wkv', i, _mm(sv['memn'], dkv, ta=True))
        _, dg = _rms_bwd(memv, full['mem_norm_g'][i], dmemn)
        acc('mem_norm_g', i, dg)
        dh1, dg = _rms_bwd(sv['h1'], full['xa_norm_g'][i], da, dh2)
        acc('xa_norm_g', i, dg)
        if kind == 0:
            dp, dw, ds = _pool_mix_bwd(sv['p'], wfull['pool_w', j], full['pool_scale'][j], dh1)
            acc('pool_w', j, dw)
            acc('pool_scale', j, ds)
            dhn = _pool_bwd(dp)
        elif kind == 1:
            do = _mm(dh1, wfull['sb_w_o', j], tb=True)
            acc('sb_w_o', j, _mm(sv['o'], dh1, ta=True))
            if i == sb_layer:
                early_g = [it for it in items if it != ('sb_w_qkv', j) and layer_of(it) >= sb_layer]
                groups_e, bufs_e = _scatter_bufs({it: grads[it[0]][it[1]] for it in early_g}, w, early_g)
                sums_e = _chip_sums(bufs_e)
                dqs, dks, dvs, recv_e = _sb_bwd(sv['qkv'], sv['o'], do, [wire for _, wire in sums_e])
            else:
                dqs, dks, dvs, _ = _sb_bwd(sv['qkv'], sv['o'], do)
            dqkv = jnp.concatenate([dqs, dks, dvs], axis=1)
            dhn = _mm(dqkv, wfull['sb_w_qkv', j], tb=True)
            acc('sb_w_qkv', j, _mm(sv['hn'], dqkv, ta=True))
        else:
            pr = s5[j]
            dval, dgate = _glu_bwd(sv['vg'], dh1)
            dvg = jnp.concatenate([dval, dgate], axis=1)
            dz = _mm(dvg, wfull['s5_w_glu', j], tb=True)
            acc('s5_w_glu', j, _mm(sv['z'], dvg, ta=True))
            dy, gdr, gdi, du0, dd = _s5_out_bwd(dz, sv['y'], sv['hn'], full['s5_d'][j], pr['Cr'], pr['Ci'])
            acc('s5_d', j, dd)
            gr, gi, dabr, dabi = _s5_scan_bwd(gdr, gdi, sv['xr'], sv['xi'], pr['abr'], pr['abi'])
            dhn = _s5_in_bwd(gr, gi, pr['Br'], pr['Bi'], du0)
            dBr, dBi, dCr, dCi = _s5_wgrad(sv['hn'], gr, gi, sv['xr'], sv['xi'], dy)
            cg, P, G = pr['cg'], pr['P'], pr['G']
            acc('s5_c_re', j, _blockdiag_out_extract(dCr, cg, P))
            acc('s5_c_im', j, _blockdiag_out_extract(dCi, cg, P))
            dar, dai, dldt, dbr, dbi = _s5_prep_bwd(pr['ar'], pr['ai'], pr['ldt'], pr['br'], pr['bi'], dabr, dabi,
                                                    _blockdiag_in_extract(dBr, cg, P), _blockdiag_in_extract(dBi, cg, P))
            acc('s5_a_re', j, dar.reshape(G, P))
            acc('s5_a_im', j, dai.reshape(G, P))
            acc('s5_log_dt', j, dldt.reshape(G, P).sum(axis=1))
            acc('s5_b_re', j, dbr.reshape(cg, G, P).transpose(1, 2, 0))
            acc('s5_b_im', j, dbi.reshape(cg, G, P).transpose(1, 2, 0))
        dh, dg = _rms_bwd(sv['h'], full['mix_norm_g'][i], dhn, dh1)
        acc('mix_norm_g', i, dg)
    grad_x = dh[None]

    gfull = {}
    for n in VEC_WEIGHTS + REPLICATED:
        gfull[n] = grads[n] if n == 'final_norm_g' else jnp.stack([grads[n][k] for k in range(len(grads[n]))])
    if sb_layer is None:
        early_g, local = [], {}
    else:
        local = _scatter_unpack(w, groups_e, _device_sums(sums_e, recv_e))
    late_g = [it for it in items if it not in early_g]
    groups_l, bufs_l = _scatter_bufs({it: grads[it[0]][it[1]] for it in late_g}, w, late_g)
    sums_l = _chip_sums(bufs_l + [_small_scatter_buf(gfull, VEC_WEIGHTS)])
    totals_l = _device_sums(sums_l, _exchange_chips([wire for _, wire in sums_l]))
    local.update(_scatter_unpack(w, groups_l, totals_l[:-1]))
    gw = {n: jnp.stack([local[(n, l)] for l in range(w[n].shape[0])]) for n in MXU_WEIGHTS}
    gw.update(zip(VEC_WEIGHTS, _unpack(totals_l[-1].reshape(-1), [w[n].shape for n in VEC_WEIGHTS])))
    gw.update(_all_reduce_small(gfull, REPLICATED))

    deltas, new_m, new_v = {}, {}, {}
    for n in WEIGHTS:
        shp = w[n].shape
        d_, m_, v_ = _adamw(_as2d(w[n]), _as2d(gw[n]), _as2d(mom[n]), _as2d(var[n]))
        deltas[n], new_m[n], new_v[n] = d_.reshape(shp), m_.reshape(shp), v_.reshape(shp)

    return (loss, grad_x, *[gw[n] for n in WEIGHTS], *[deltas[n] for n in WEIGHTS],
            *[new_m[n] for n in WEIGHTS], *[new_v[n] for n in WEIGHTS])
```

```python
import functools
import math

import jax
import jax.numpy as jnp
from jax import lax
from jax.experimental import pallas as pl
from jax.experimental.pallas import tpu as pltpu

f32 = jnp.float32
_MXU = jnp.bfloat16
_WIRE = _MXU
_VMEM_LIMIT = 48 * 1024 * 1024
_LANES = 128
_PACK_COLS = 1024

NDEV = 8
EPS = 1e-6
POOL_WINDOWS = (2, 4, 8, 16)
SB_HEAD_DIM = 64
XA_HEADS = 4
S5_GROUP = 16
S5_BLOCK_GROUPS = _LANES // S5_GROUP
ADAM_LR, ADAM_B1, ADAM_B2, ADAM_EPS, ADAM_WD, ADAM_STEP = 0.001, 0.9, 0.999, 1e-08, 0.01, 10

WEIGHTS = ['mix_norm_g', 'pool_w', 'pool_scale', 'sb_w_qkv', 'sb_w_o', 's5_a_re', 's5_a_im', 's5_log_dt',
           's5_b_re', 's5_b_im', 's5_c_re', 's5_c_im', 's5_d', 's5_w_glu', 'xa_norm_g', 'mem_norm_g', 'xa_wq',
           'xa_wkv', 'xa_wo', 'ffn_norm_g', 'ffn_w_up', 'ffn_conv_w', 'ffn_conv_b', 'ffn_w_down', 'final_norm_g']
SHARD_AXIS = {'pool_w': 2, 'pool_scale': 1, 'sb_w_qkv': 2, 'sb_w_o': 1, 's5_d': 1, 's5_w_glu': 2, 'xa_wq': 1,
              'xa_wkv': 2, 'xa_wo': 1, 'ffn_w_up': 2, 'ffn_conv_w': 2, 'ffn_w_down': 1}
MXU_WEIGHTS = ['pool_w', 'sb_w_qkv', 'sb_w_o', 's5_w_glu', 'xa_wq', 'xa_wkv', 'xa_wo', 'ffn_w_up', 'ffn_w_down']
VEC_WEIGHTS = ['pool_scale', 's5_d', 'ffn_conv_w']
REPLICATED = [n for n in WEIGHTS if n not in SHARD_AXIS]

_NN = (((1,), (0,)), ((), ()))
_NT = (((1,), (1,)), ((), ()))
_TN = (((0,), (0,)), ((), ()))
MESH_ID = pl.DeviceIdType.MESH
ANY = pl.BlockSpec(memory_space=pl.ANY)


def _call(body, **kw):
    return pl.pallas_call(body, **kw)


def _cp(*sem):
    return pltpu.CompilerParams(dimension_semantics=sem, vmem_limit_bytes=_VMEM_LIMIT)


def _tile(n, target, mult=_LANES):
    if n <= target:
        return n
    t = (target // mult) * mult
    while t >= mult:
        if n % t == 0:
            return t
        t -= mult
    return n


def _dot(a, b, dims=_NN):
    return lax.dot_general(a, b, dims, preferred_element_type=f32)


def _split(a):
    hi = a.astype(_MXU)
    lo = (a - hi.astype(f32)).astype(_MXU)
    return hi, lo


def _dot_hilo(a, u, dims=_NN):
    hi, lo = _split(a)
    return _dot(hi, u, dims) + _dot(lo, u, dims)


def _dot3(a, b, dims=_NN):
    ah, al = _split(a)
    bh, bl = _split(b)
    return _dot(ah, bh, dims) + _dot(al, bh, dims) + _dot(ah, bl, dims)


def _sigmoid(x):
    return 0.5 * jnp.tanh(0.5 * x) + 0.5


_GELU_C = math.sqrt(2.0 / math.pi)


def _gelu(x):
    return x * (0.5 * (1.0 + jnp.tanh(_GELU_C * (x + 0.044715 * (x * x * x)))))


def _gelu_grad(x):
    t = jnp.tanh(_GELU_C * (x + 0.044715 * (x * x * x)))
    return 0.5 * (1.0 + t) + x * 0.5 * (1.0 - t * t) * _GELU_C * (1.0 + 3.0 * 0.044715 * x * x)


def _shift_down(x, k, rows):
    return jnp.where(rows >= k, pltpu.roll(x, k, 0), 0.0)


def _shift_up(x, k, rows, n):
    return jnp.where(rows < n - k, pltpu.roll(x, n - k, 0), 0.0)


def _mm(a, b, *, ta=False, tb=False, out_dtype=f32):
    M, K = (a.shape[1], a.shape[0]) if ta else a.shape
    N = b.shape[0] if tb else b.shape[1]
    tm, tn, tk = _tile(M, 1408), _tile(N, 1536), _tile(K, 1408)
    nk = K // tk
    a_spec = pl.BlockSpec((tk, tm), lambda i, j, k: (k, i)) if ta else pl.BlockSpec((tm, tk), lambda i, j, k: (i, k))
    b_spec = pl.BlockSpec((tn, tk), lambda i, j, k: (j, k)) if tb else pl.BlockSpec((tk, tn), lambda i, j, k: (k, j))
    dims = (((0 if ta else 1,), (1 if tb else 0,)), ((), ()))

    def body(a_ref, b_ref, o_ref, acc_ref):
        k = pl.program_id(2)

        @pl.when(k == 0)
        def _():
            acc_ref[...] = jnp.zeros_like(acc_ref)

        acc_ref[...] += _dot(a_ref[...].astype(_MXU), b_ref[...].astype(_MXU), dims)

        @pl.when(k == nk - 1)
        def _():
            o_ref[...] = acc_ref[...].astype(out_dtype)

    return _call(
        body, name=f"mm_{'t' if ta else 'n'}{'t' if tb else 'n'}_{M}x{K}x{N}",
        grid=(M // tm, N // tn, nk), in_specs=[a_spec, b_spec],
        out_specs=pl.BlockSpec((tm, tn), lambda i, j, k: (i, j)),
        out_shape=jax.ShapeDtypeStruct((M, N), out_dtype),
        scratch_shapes=[pltpu.VMEM((tm, tn), f32)],
        compiler_params=_cp("parallel", "parallel", "arbitrary"))(a, b)


def _rms_fwd(x, g, delta=None, *, out_dtype=f32):
    S, D = x.shape
    ts = _tile(S, 512, 8)
    row = pl.BlockSpec((ts, D), lambda i: (i, 0))
    vec = pl.BlockSpec((1, D), lambda i: (0, 0))
    g2 = g.reshape(1, D)

    def norm(xv, g_ref):
        r = lax.rsqrt(jnp.mean(xv * xv, axis=-1, keepdims=True) + EPS)
        return ((xv * r) * g_ref[...]).astype(out_dtype)

    if delta is None:
        def body(x_ref, g_ref, y_ref):
            y_ref[...] = norm(x_ref[...], g_ref)

        return _call(body, name=f"rms_fwd_{S}", grid=(S // ts,), in_specs=[row, vec], out_specs=row,
                     out_shape=jax.ShapeDtypeStruct((S, D), out_dtype), compiler_params=_cp("parallel"))(x, g2)

    def body(x_ref, d_ref, g_ref, s_ref, y_ref):
        xv = x_ref[...] + d_ref[...]
        s_ref[...] = xv
        y_ref[...] = norm(xv, g_ref)

    return _call(body, name=f"add_rms_fwd_{S}", grid=(S // ts,), in_specs=[row, row, vec], out_specs=[row, row],
                 out_shape=[jax.ShapeDtypeStruct((S, D), f32), jax.ShapeDtypeStruct((S, D), out_dtype)],
                 compiler_params=_cp("parallel"))(x, delta, g2)


def _rms_bwd(x, g, dy, dres=None):
    S, D = x.shape
    ts = _tile(S, 512, 8)
    row = pl.BlockSpec((ts, D), lambda i: (i, 0))
    vec = pl.BlockSpec((1, D), lambda i: (0, 0))
    has_res = dres is not None

    def body(*refs):
        if has_res:
            x_ref, g_ref, dy_ref, dr_ref, dx_ref, dg_ref = refs
        else:
            x_ref, g_ref, dy_ref, dx_ref, dg_ref = refs
        xv = x_ref[...]
        r = lax.rsqrt(jnp.mean(xv * xv, axis=-1, keepdims=True) + EPS)
        xh = xv * r
        dyv = dy_ref[...].astype(f32)

        @pl.when(pl.program_id(0) == 0)
        def _():
            dg_ref[...] = jnp.zeros_like(dg_ref)

        dg_ref[...] += jnp.sum(dyv * xh, axis=0, keepdims=True)
        dxh = dyv * g_ref[...]
        dx = r * (dxh - xh * jnp.mean(dxh * xh, axis=-1, keepdims=True))
        if has_res:
            dx = dx + dr_ref[...]
        dx_ref[...] = dx

    ins = [x, g.reshape(1, D), dy] + ([dres] if has_res else [])
    dx, dg = _call(body, name=f"rms_bwd_{S}_{int(has_res)}", grid=(S // ts,),
                   in_specs=[row, vec, row] + ([row] if has_res else []), out_specs=[row, vec],
                   out_shape=[jax.ShapeDtypeStruct((S, D), f32), jax.ShapeDtypeStruct((1, D), f32)],
                   compiler_params=_cp("arbitrary"))(*ins)
    return dx, dg.reshape(D)


def _pool_windows_sum(x, win, rows):
    s, k = x, 1
    while k < win:
        s = s + _shift_down(s, k, rows)
        k *= 2
    return s


def _pool_windows_sum_up(x, win, rows, n):
    s, k = x, 1
    while k < win:
        s = s + _shift_up(s, k, rows, n)
        k *= 2
    return s


def _pool_fwd(hn):
    S, D = hn.shape
    cg = D // len(POOL_WINDOWS)
    tc = _tile(cg, 128)
    nb = cg // tc
    blk = pl.BlockSpec((S, tc), lambda gi, j: (0, gi * nb + j))

    def body(x_ref, p_ref):
        gi = pl.program_id(0)
        rows = lax.broadcasted_iota(jnp.int32, (S, 1), 0)
        cnt = (rows + 1).astype(f32)
        for k, win in enumerate(POOL_WINDOWS):
            @pl.when(gi == k)
            def _(win=win):
                x = x_ref[...]
                s = _pool_windows_sum(x, win, rows)
                p_ref[...] = (s / jnp.minimum(cnt, float(win)) - x).astype(p_ref.dtype)

    return _call(body, name=f"pool_fwd_{S}", grid=(len(POOL_WINDOWS), nb), in_specs=[blk], out_specs=blk,
                 out_shape=jax.ShapeDtypeStruct((S, D), _MXU), compiler_params=_cp("parallel", "parallel"))(hn)


def _pool_bwd(dp):
    S, D = dp.shape
    cg = D // len(POOL_WINDOWS)
    tc = _tile(cg, 128)
    nb = cg // tc
    blk = pl.BlockSpec((S, tc), lambda gi, j: (0, gi * nb + j))

    def body(dp_ref, dx_ref):
        gi = pl.program_id(0)
        rows = lax.broadcasted_iota(jnp.int32, (S, 1), 0)
        cnt = (rows + 1).astype(f32)
        for k, win in enumerate(POOL_WINDOWS):
            @pl.when(gi == k)
            def _(win=win):
                d = dp_ref[...]
                e = d / jnp.minimum(cnt, float(win))
                dx_ref[...] = _pool_windows_sum_up(e, win, rows, S) - d

    return _call(body, name=f"pool_bwd_{S}", grid=(len(POOL_WINDOWS), nb), in_specs=[blk], out_specs=blk,
                 out_shape=jax.ShapeDtypeStruct((S, D), f32), compiler_params=_cp("parallel", "parallel"))(dp)


def _pool_mix_fwd(p, w, scale):
    S, D = p.shape
    G, cg, _ = w.shape
    ts = _tile(S, 1024, 8)

    def body(p_ref, w_ref, s_ref, y_ref):
        y_ref[...] = _dot(p_ref[...], w_ref[0]) * s_ref[...]

    return _call(body, name=f"pool_mix_fwd_{S}", grid=(S // ts, G),
                 in_specs=[pl.BlockSpec((ts, cg), lambda i, g: (i, g)), pl.BlockSpec((1, cg, cg), lambda i, g: (g, 0, 0)),
                           pl.BlockSpec((1, cg), lambda i, g: (0, g))],
                 out_specs=pl.BlockSpec((ts, cg), lambda i, g: (i, g)),
                 out_shape=jax.ShapeDtypeStruct((S, D), f32), compiler_params=_cp("parallel", "parallel"))(
                     p, w, scale.reshape(1, D))


def _pool_mix_bwd(p, w, scale, dy):
    S, D = p.shape
    G, cg, _ = w.shape
    ts = _tile(S, 1024, 8)

    def body(p_ref, w_ref, s_ref, dy_ref, dp_ref, dw_ref, ds_ref):
        @pl.when(pl.program_id(1) == 0)
        def _():
            dw_ref[...] = jnp.zeros_like(dw_ref)
            ds_ref[...] = jnp.zeros_like(ds_ref)

        pv, wv, dyv = p_ref[...], w_ref[0], dy_ref[...]
        ypre = _dot(pv, wv)
        ds_ref[...] += jnp.sum(dyv * ypre, axis=0, keepdims=True)
        dyp = (dyv * s_ref[...]).astype(_MXU)
        dp_ref[...] = _dot(dyp, wv, _NT)
        dw_ref[0] += _dot(pv, dyp, _TN)

    dp, dw, ds = _call(
        body, name=f"pool_mix_bwd_{S}", grid=(G, S // ts),
        in_specs=[pl.BlockSpec((ts, cg), lambda g, i: (i, g)), pl.BlockSpec((1, cg, cg), lambda g, i: (g, 0, 0)),
                  pl.BlockSpec((1, cg), lambda g, i: (0, g)), pl.BlockSpec((ts, cg), lambda g, i: (i, g))],
        out_specs=[pl.BlockSpec((ts, cg), lambda g, i: (i, g)), pl.BlockSpec((1, cg, cg), lambda g, i: (g, 0, 0)),
                   pl.BlockSpec((1, cg), lambda g, i: (0, g))],
        out_shape=[jax.ShapeDtypeStruct((S, D), f32), jax.ShapeDtypeStruct((G, cg, cg), f32),
                   jax.ShapeDtypeStruct((1, D), f32)],
        compiler_params=_cp("parallel", "arbitrary"))(p, w, scale.reshape(1, D), dy)
    return dp, dw, ds.reshape(D)


def _sb_tile(S):
    return min(256, max(128, S // 4))


_LOG2E = math.log2(math.e)


def _sb_scores(qs, ks, tri, R, U):
    z = _dot(qs, ks, _NT)
    sp = jnp.maximum(z, 0.0) + jnp.log(1.0 + jnp.exp2(jnp.abs(z) * -_LOG2E))
    lb = z - sp
    if tri is not None:
        sp = jnp.where(tri, sp, 0.0)
    c = _dot(sp.astype(_MXU), U)
    a = jnp.exp(lb - c - R)
    if tri is not None:
        a = jnp.where(tri, a, 0.0)
    return sp, lb, a


def _sb_sweep(tile, i, carry):
    def two(p, c):
        kb = i - 1 - 2 * p
        return tile(kb - 1, tile(kb, c, None), None)

    carry = lax.fori_loop(0, i // 2, two, carry)
    return lax.cond(i % 2 == 1, lambda c: tile(i * 0, c, None), lambda c: c, carry)


def _sb_consts(T):
    lane = lax.broadcasted_iota(jnp.int32, (1, _LANES), 1)
    row, col = lax.broadcasted_iota(jnp.int32, (T, T), 0), lax.broadcasted_iota(jnp.int32, (T, T), 1)
    U = jnp.where(row > col, 1.0, 0.0).astype(_MXU)
    heads = [(lane >= SB_HEAD_DIM * h) & (lane < SB_HEAD_DIM * (h + 1)) for h in range(_LANES // SB_HEAD_DIM)]
    return heads, col < row, U


def _sb_fwd(qkv, gather=()):
    S, D3 = qkv.shape
    D = D3 // 3
    HP = D // _LANES
    T = _sb_tile(S)
    nq = S // T
    n = len(gather)
    scale = SB_HEAD_DIM ** -0.5

    def body(*refs):
        q_ref, k_ref, v_ref = refs[:3]
        o_ref = refs[3 + n]
        i = pl.program_id(1)
        if n:
            step = pl.program_id(0) * nq + i
            start, forward, finish = _all_gather_stages(refs[3:3 + n], refs[4 + n:4 + 2 * n], *refs[4 + 2 * n:])
            pl.when(step == 0)(start)
            pl.when(step == (HP // 2) * nq)(forward)
        q = q_ref[...] * scale
        heads, tri, U = _sb_consts(T)
        qms = [jnp.where(hm, q, jnp.zeros_like(q)) for hm in heads]

        def tile(kb, carry, mask):
            off = pl.multiple_of(kb * T, T)
            ks, vs = k_ref[pl.ds(off, T), :], v_ref[pl.ds(off, T), :]
            new = []
            for h, qm in enumerate(qms):
                R, acc = carry[2 * h], carry[2 * h + 1]
                sp, _, a = _sb_scores(qm, ks, mask, R, U)
                new += [R + jnp.sum(sp, axis=1, keepdims=True), acc + _dot(a.astype(_MXU), vs)]
            return tuple(new)

        zero = (jnp.zeros((T, 1), f32), jnp.zeros((T, _LANES), f32)) * len(heads)
        carry = _sb_sweep(tile, i, tile(i, zero, tri))
        out = jnp.zeros((T, _LANES), f32)
        for h, hm in enumerate(heads):
            out = out + jnp.where(hm, carry[2 * h + 1], 0.0)
        o_ref[...] = out
        if n:
            pl.when(step == HP * nq - 1)(finish)

    outs = _call(
        body, name=f"sb_fwd_{S}_{n}", grid=(HP, nq),
        in_specs=[pl.BlockSpec((T, _LANES), lambda hp, i: (i, hp)), pl.BlockSpec((S, _LANES), lambda hp, i: (0, HP + hp)),
                  pl.BlockSpec((S, _LANES), lambda hp, i: (0, 2 * HP + hp))] + [ANY] * n,
        out_specs=[pl.BlockSpec((T, _LANES), lambda hp, i: (i, hp))] + [ANY] * n,
        out_shape=[jax.ShapeDtypeStruct((S, D), f32)] + _all_gather_shapes(gather),
        scratch_shapes=_all_gather_sems(n) if n else [],
        compiler_params=_cp("arbitrary", "arbitrary"))(qkv, qkv, qkv, *gather)
    return outs[0], list(outs[1:])


def _sb_bwd(qkv, o, do, send=()):
    S, D3 = qkv.shape
    D = D3 // 3
    HP = D // _LANES
    T = _sb_tile(S)
    nq = S // T
    n = len(send)
    scale = SB_HEAD_DIM ** -0.5

    def body(*refs):
        q_ref, k_ref, v_ref, o_ref, do_ref = refs[:5]
        dq_ref, dk_ref, dv_ref = refs[5 + n:8 + n]
        dk_acc, dv_acc = refs[8 + 2 * n:10 + 2 * n]
        i = pl.program_id(1)
        if n:
            step = pl.program_id(0) * nq + i
            start, finish = _exchange_chips_stages(refs[5:5 + n], refs[8 + n:8 + 2 * n], *refs[10 + 2 * n:])
            pl.when(step == 0)(start)

        @pl.when(i == 0)
        def _():
            dk_acc[...] = jnp.zeros_like(dk_acc)
            dv_acc[...] = jnp.zeros_like(dv_acc)

        q = q_ref[...] * scale
        dob = do_ref[...].astype(_MXU)
        prod = dob.astype(f32) * o_ref[...]
        heads, tri, U = _sb_consts(T)
        qms = [jnp.where(hm, q, jnp.zeros_like(q)) for hm in heads]
        doms = [jnp.where(hm, dob, jnp.zeros_like(dob)) for hm in heads]
        totals = [jnp.sum(jnp.where(hm, prod, 0.0), axis=1, keepdims=True) for hm in heads]

        def tile(kb, carry, mask):
            off = pl.multiple_of(kb * T, T)
            ks, vs = k_ref[pl.ds(off, T), :], v_ref[pl.ds(off, T), :]
            new = []
            dk_t = jnp.zeros((T, _LANES), f32)
            dv_t = jnp.zeros((T, _LANES), f32)
            for h, (qm, dom, total) in enumerate(zip(qms, doms, totals)):
                R, Gs, dq = carry[3 * h:3 * h + 3]
                sp, lb, a = _sb_scores(qm, ks, mask, R, U)
                ab = a.astype(_MXU)
                g = ab.astype(f32) * _dot(dom, vs, _NT)
                before = total - (g + _dot_hilo(g, U) + Gs)
                beta = jnp.exp(lb)
                dz = g - (g + before) * beta
                if mask is not None:
                    dz = jnp.where(mask, dz, 0.0)
                dzb = dz.astype(_MXU)
                dk_t = dk_t + _dot(dzb, qm, _TN)
                dv_t = dv_t + _dot(ab, dom, _TN)
                new += [R + jnp.sum(sp, axis=1, keepdims=True), Gs + jnp.sum(g, axis=1, keepdims=True), dq + _dot(dzb, ks)]
            dk_acc[pl.ds(off, T), :] += dk_t
            dv_acc[pl.ds(off, T), :] += dv_t
            return tuple(new)

        zero1 = jnp.zeros((T, 1), f32)
        carry = _sb_sweep(tile, i, tile(i, (zero1, zero1, jnp.zeros((T, _LANES), f32)) * len(heads), tri))
        dq_out = jnp.zeros((T, _LANES), f32)
        for h, hm in enumerate(heads):
            dq_out = dq_out + jnp.where(hm, carry[3 * h + 2], 0.0)
        dq_ref[...] = (dq_out * scale).astype(dq_ref.dtype)

        @pl.when(i == nq - 1)
        def _():
            dk_ref[...] = dk_acc[...].astype(dk_ref.dtype)
            dv_ref[...] = dv_acc[...].astype(dv_ref.dtype)

        if n:
            pl.when(step == HP * nq - 1)(finish)

    qb = pl.BlockSpec((T, _LANES), lambda hp, i: (i, hp))
    col = pl.BlockSpec((S, _LANES), lambda hp, i: (0, hp))
    out = jax.ShapeDtypeStruct((S, D), _MXU)
    outs = _call(
        body, name=f"sb_bwd_{S}_{n}", grid=(HP, nq),
        in_specs=[qb, pl.BlockSpec((S, _LANES), lambda hp, i: (0, HP + hp)),
                  pl.BlockSpec((S, _LANES), lambda hp, i: (0, 2 * HP + hp)), qb, qb] + [ANY] * n,
        out_specs=[qb, col, col] + [ANY] * n, out_shape=[out, out, out] + _exchange_chips_shapes(send),
        scratch_shapes=[pltpu.VMEM((S, _LANES), f32), pltpu.VMEM((S, _LANES), f32)] + (_exchange_chips_sems(n) if n else []),
        compiler_params=_cp("arbitrary", "arbitrary"))(qkv, qkv, qkv, o, do, *send)
    return outs[0], outs[1], outs[2], list(outs[3:])


def _cmul(ar, ai, br, bi):
    return ar * br - ai * bi, ar * bi + ai * br


def _s5_coef(ar, ai, ldt):
    dt = jnp.exp(ldt)
    e = jnp.exp(ar * dt)
    abr, abi = e * jnp.cos(ai * dt), e * jnp.sin(ai * dt)
    inv = 1.0 / (ar * ar + ai * ai)
    cr, ci = _cmul(abr - 1.0, abi, ar * inv, -ai * inv)
    return dt, abr, abi, inv, cr, ci


def _s5_prep_fwd(ar, ai, ldt, br, bi):
    N = ar.shape[1]
    cg = br.shape[0]

    def body(ar_ref, ai_ref, ldt_ref, br_ref, bi_ref, abr_ref, abi_ref, bbr_ref, bbi_ref):
        _, abr, abi, _, cr, ci = _s5_coef(ar_ref[...], ai_ref[...], ldt_ref[...])
        abr_ref[...], abi_ref[...] = abr, abi
        bbr_ref[...], bbi_ref[...] = _cmul(cr, ci, br_ref[...], bi_ref[...])

    v, m = jax.ShapeDtypeStruct((1, N), f32), jax.ShapeDtypeStruct((cg, N), f32)
    return _call(body, name="s5_prep_fwd", out_shape=[v, v, m, m])(ar, ai, ldt, br, bi)


def _s5_prep_bwd(ar, ai, ldt, br, bi, dabr, dabi, dbbr, dbbi):
    N = ar.shape[1]
    cg = br.shape[0]

    def body(ar_ref, ai_ref, ldt_ref, br_ref, bi_ref, dabr_ref, dabi_ref, dbbr_ref, dbbi_ref,
             dar_ref, dai_ref, dldt_ref, dbr_ref, dbi_ref):
        a_r, a_i = ar_ref[...], ai_ref[...]
        dt, abr, abi, inv, cr, ci = _s5_coef(a_r, a_i, ldt_ref[...])
        b_r, b_i, gr, gi = br_ref[...], bi_ref[...], dbbr_ref[...], dbbi_ref[...]
        dbr_ref[...], dbi_ref[...] = _cmul(cr, -ci, gr, gi)
        dcr = jnp.sum(gr * b_r + gi * b_i, axis=0, keepdims=True)
        dci = jnp.sum(gi * b_r - gr * b_i, axis=0, keepdims=True)
        ilr, ili = a_r * inv, -a_i * inv
        dwr, dwi = _cmul(ilr, -ili, dcr, dci)
        qr, qi = _cmul(cr, ci, ilr, ili)
        dl1r, dl1i = _cmul(-qr, qi, dcr, dci)
        tr, ti = dabr_ref[...] + dwr, dabi_ref[...] + dwi
        ddlr, ddli = _cmul(abr, -abi, tr, ti)
        dar_ref[...] = dl1r + ddlr * dt
        dai_ref[...] = dl1i + ddli * dt
        dldt_ref[...] = (a_r * ddlr + a_i * ddli) * dt

    v, m = jax.ShapeDtypeStruct((1, N), f32), jax.ShapeDtypeStruct((cg, N), f32)
    return _call(body, name="s5_prep_bwd", out_shape=[v, v, v, m, m])(ar, ai, ldt, br, bi, dabr, dabi, dbbr, dbbi)


def _s5_in_fwd(u, Br, Bi):
    S, D = u.shape
    nb, _, nw = Br.shape
    ts = _tile(S, 512, 8)

    def body(u_ref, br_ref, bi_ref, or_ref, oi_ref):
        uv = u_ref[...]
        or_ref[...] = _dot3(uv, br_ref[0])
        oi_ref[...] = _dot3(uv, bi_ref[0])

    ub = pl.BlockSpec((ts, _LANES), lambda i, k: (i, k))
    wb = pl.BlockSpec((1, _LANES, nw), lambda i, k: (k, 0, 0))
    ob = pl.BlockSpec((ts, nw), lambda i, k: (i, k))
    o = jax.ShapeDtypeStruct((S, nb * nw), f32)
    return _call(body, name=f"s5_in_fwd_{S}", grid=(S // ts, nb), in_specs=[ub, wb, wb], out_specs=[ob, ob],
                 out_shape=[o, o], compiler_params=_cp("parallel", "parallel"))(u, Br, Bi)


def _s5_scan_fwd(bur, bui, abr, abi):
    S, N = bur.shape
    tt, tn = _tile(S, 128, 8), _tile(N, 4096)

    def body(br_ref, bi_ref, ar_ref, ai_ref, xr_ref, xi_ref, sr, si):
        @pl.when(pl.program_id(1) == 0)
        def _():
            sr[...] = jnp.zeros_like(sr)
            si[...] = jnp.zeros_like(si)

        a_r, a_i = ar_ref[...], ai_ref[...]

        def step(t, carry):
            xr, xi = carry
            nr = a_r * xr - a_i * xi + br_ref[pl.ds(t, 1), :]
            ni = a_r * xi + a_i * xr + bi_ref[pl.ds(t, 1), :]
            xr_ref[pl.ds(t, 1), :] = nr
            xi_ref[pl.ds(t, 1), :] = ni
            return nr, ni

        xr, xi = lax.fori_loop(0, tt, step, (sr[...], si[...]))
        sr[...], si[...] = xr, xi

    blk = pl.BlockSpec((tt, tn), lambda n, i: (i, n))
    vec = pl.BlockSpec((1, tn), lambda n, i: (0, n))
    o = jax.ShapeDtypeStruct((S, N), f32)
    return _call(body, name=f"s5_scan_fwd_{S}", grid=(N // tn, S // tt), in_specs=[blk, blk, vec, vec],
                 out_specs=[blk, blk], out_shape=[o, o],
                 scratch_shapes=[pltpu.VMEM((1, tn), f32), pltpu.VMEM((1, tn), f32)],
                 compiler_params=_cp("parallel", "arbitrary"))(bur, bui, abr, abi)


def _s5_scan_bwd(dr, di, xr, xi, abr, abi):
    S, N = dr.shape
    tt, tn = _tile(S, 128, 8), _tile(N, 4096)
    nt = S // tt

    def body(dr_ref, di_ref, xr_ref, xi_ref, ar_ref, ai_ref, gr_ref, gi_ref, dar_ref, dai_ref, sr, si):
        @pl.when(pl.program_id(1) == 0)
        def _():
            sr[...] = jnp.zeros_like(sr)
            si[...] = jnp.zeros_like(si)
            dar_ref[...] = jnp.zeros_like(dar_ref)
            dai_ref[...] = jnp.zeros_like(dai_ref)

        a_r, a_i = ar_ref[...], ai_ref[...]

        def step(j, carry):
            gr, gi, accr, acci = carry
            t = tt - 1 - j
            xr_t, xi_t = xr_ref[pl.ds(t, 1), :], xi_ref[pl.ds(t, 1), :]
            accr = accr + gr * xr_t + gi * xi_t
            acci = acci + gi * xr_t - gr * xi_t
            nr = dr_ref[pl.ds(t, 1), :] + a_r * gr + a_i * gi
            ni = di_ref[pl.ds(t, 1), :] + a_r * gi - a_i * gr
            gr_ref[pl.ds(t, 1), :] = nr
            gi_ref[pl.ds(t, 1), :] = ni
            return nr, ni, accr, acci

        gr, gi, accr, acci = lax.fori_loop(0, tt, step, (sr[...], si[...], dar_ref[...], dai_ref[...]))
        sr[...], si[...] = gr, gi
        dar_ref[...], dai_ref[...] = accr, acci

    blk = pl.BlockSpec((tt, tn), lambda n, i: (nt - 1 - i, n))
    vec = pl.BlockSpec((1, tn), lambda n, i: (0, n))
    o, v = jax.ShapeDtypeStruct((S, N), f32), jax.ShapeDtypeStruct((1, N), f32)
    return _call(body, name=f"s5_scan_bwd_{S}", grid=(N // tn, nt), in_specs=[blk, blk, blk, blk, vec, vec],
                 out_specs=[blk, blk, vec, vec], out_shape=[o, o, v, v],
                 scratch_shapes=[pltpu.VMEM((1, tn), f32), pltpu.VMEM((1, tn), f32)],
                 compiler_params=_cp("parallel", "arbitrary"))(dr, di, xr, xi, abr, abi)


def _s5_out_fwd(xr, xi, Cr, Ci, u, d):
    S, N = xr.shape
    nb, nw, _ = Cr.shape
    D = u.shape[1]
    ts = _tile(S, 512, 8)

    def body(xr_ref, xi_ref, cr_ref, ci_ref, u_ref, d_ref, y_ref, z_ref):
        y = _dot3(xr_ref[...], cr_ref[0]) - _dot3(xi_ref[...], ci_ref[0]) + d_ref[...] * u_ref[...]
        y_ref[...] = y
        z_ref[...] = _gelu(y).astype(z_ref.dtype)

    xb = pl.BlockSpec((ts, nw), lambda i, k: (i, k))
    cb = pl.BlockSpec((1, nw, _LANES), lambda i, k: (k, 0, 0))
    ub = pl.BlockSpec((ts, _LANES), lambda i, k: (i, k))
    db = pl.BlockSpec((1, _LANES), lambda i, k: (0, k))
    return _call(body, name=f"s5_out_fwd_{S}", grid=(S // ts, nb), in_specs=[xb, xb, cb, cb, ub, db],
                 out_specs=[ub, ub], out_shape=[jax.ShapeDtypeStruct((S, D), f32), jax.ShapeDtypeStruct((S, D), _MXU)],
                 compiler_params=_cp("parallel", "parallel"))(xr, xi, Cr, Ci, u, d.reshape(1, D))


def _s5_out_bwd(dz, y, u, d, Cr, Ci):
    S, D = y.shape
    nb, nw, _ = Cr.shape
    ts = _tile(S, 512, 8)

    def body(dz_ref, y_ref, u_ref, d_ref, cr_ref, ci_ref, dy_ref, gr_ref, gi_ref, du_ref, dd_ref):
        @pl.when(pl.program_id(1) == 0)
        def _():
            dd_ref[...] = jnp.zeros_like(dd_ref)

        dy = dz_ref[...] * _gelu_grad(y_ref[...])
        dy_ref[...] = dy
        gr_ref[...] = _dot3(dy, cr_ref[0], _NT)
        gi_ref[...] = -_dot3(dy, ci_ref[0], _NT)
        du_ref[...] = dy * d_ref[...]
        dd_ref[...] += jnp.sum(dy * u_ref[...], axis=0, keepdims=True)

    xb = pl.BlockSpec((ts, nw), lambda k, i: (i, k))
    cb = pl.BlockSpec((1, nw, _LANES), lambda k, i: (k, 0, 0))
    ub = pl.BlockSpec((ts, _LANES), lambda k, i: (i, k))
    db = pl.BlockSpec((1, _LANES), lambda k, i: (0, k))
    a, s = jax.ShapeDtypeStruct((S, D), f32), jax.ShapeDtypeStruct((S, nb * nw), f32)
    dy, gr, gi, du, dd = _call(
        body, name=f"s5_out_bwd_{S}", grid=(nb, S // ts), in_specs=[ub, ub, ub, db, cb, cb],
        out_specs=[ub, xb, xb, ub, db], out_shape=[a, s, s, a, jax.ShapeDtypeStruct((1, D), f32)],
        compiler_params=_cp("parallel", "arbitrary"))(dz, y, u, d.reshape(1, D), Cr, Ci)
    return dy, gr, gi, du, dd.reshape(D)


def _s5_in_bwd(gr, gi, Br, Bi, du0):
    S, N = gr.shape
    nb, _, nw = Br.shape
    ts = _tile(S, 512, 8)

    def body(gr_ref, gi_ref, br_ref, bi_ref, d0_ref, du_ref):
        du_ref[...] = d0_ref[...] + _dot3(gr_ref[...], br_ref[0], _NT) + _dot3(gi_ref[...], bi_ref[0], _NT)

    xb = pl.BlockSpec((ts, nw), lambda i, k: (i, k))
    wb = pl.BlockSpec((1, _LANES, nw), lambda i, k: (k, 0, 0))
    ub = pl.BlockSpec((ts, _LANES), lambda i, k: (i, k))
    return _call(body, name=f"s5_in_bwd_{S}", grid=(S // ts, nb), in_specs=[xb, xb, wb, wb, ub], out_specs=ub,
                 out_shape=jax.ShapeDtypeStruct((S, nb * _LANES), f32), compiler_params=_cp("parallel", "parallel"))(
                     gr, gi, Br, Bi, du0)


def _s5_wgrad(u, gr, gi, xr, xi, dy):
    S, D = u.shape
    nb = D // _LANES
    nw = gr.shape[1] // nb
    ts = _tile(S, 512, 8)

    def body(u_ref, gr_ref, gi_ref, xr_ref, xi_ref, dy_ref, dbr_ref, dbi_ref, dcr_ref, dci_ref):
        @pl.when(pl.program_id(1) == 0)
        def _():
            for r in (dbr_ref, dbi_ref, dcr_ref, dci_ref):
                r[...] = jnp.zeros_like(r)

        uv, dyv = u_ref[...], dy_ref[...]
        dbr_ref[0] += _dot3(uv, gr_ref[...], _TN)
        dbi_ref[0] += _dot3(uv, gi_ref[...], _TN)
        dcr_ref[0] += _dot3(xr_ref[...], dyv, _TN)
        dci_ref[0] -= _dot3(xi_ref[...], dyv, _TN)

    xb = pl.BlockSpec((ts, nw), lambda k, i: (i, k))
    ub = pl.BlockSpec((ts, _LANES), lambda k, i: (i, k))
    wb = pl.BlockSpec((1, _LANES, nw), lambda k, i: (k, 0, 0))
    cb = pl.BlockSpec((1, nw, _LANES), lambda k, i: (k, 0, 0))
    w, c = jax.ShapeDtypeStruct((nb, _LANES, nw), f32), jax.ShapeDtypeStruct((nb, nw, _LANES), f32)
    return _call(body, name=f"s5_wgrad_{S}", grid=(nb, S // ts), in_specs=[ub, xb, xb, xb, xb, ub],
                 out_specs=[wb, wb, cb, cb], out_shape=[w, w, c, c],
                 compiler_params=_cp("parallel", "arbitrary"))(u, gr, gi, xr, xi, dy)


def _glu_fwd(vg):
    S, D2 = vg.shape
    D = D2 // 2
    ts, tc = _tile(S, 1024, 8), _tile(D, 512)
    nc = D // tc

    def body(v_ref, g_ref, o_ref):
        o_ref[...] = v_ref[...] * _sigmoid(g_ref[...])

    return _call(body, name=f"glu_fwd_{S}", grid=(S // ts, nc),
                 in_specs=[pl.BlockSpec((ts, tc), lambda i, j: (i, j)), pl.BlockSpec((ts, tc), lambda i, j: (i, nc + j))],
                 out_specs=pl.BlockSpec((ts, tc), lambda i, j: (i, j)), out_shape=jax.ShapeDtypeStruct((S, D), f32),
                 compiler_params=_cp("parallel", "parallel"))(vg, vg)


def _glu_bwd(vg, dout):
    S, D2 = vg.shape
    D = D2 // 2
    ts, tc = _tile(S, 1024, 8), _tile(D, 512)
    nc = D // tc

    def body(v_ref, g_ref, do_ref, dv_ref, dg_ref):
        sg = _sigmoid(g_ref[...])
        do = do_ref[...]
        dv_ref[...] = (do * sg).astype(dv_ref.dtype)
        dg_ref[...] = (do * v_ref[...] * sg * (1.0 - sg)).astype(dg_ref.dtype)

    blk = pl.BlockSpec((ts, tc), lambda i, j: (i, j))
    o = jax.ShapeDtypeStruct((S, D), _MXU)
    return _call(body, name=f"glu_bwd_{S}", grid=(S // ts, nc),
                 in_specs=[blk, pl.BlockSpec((ts, tc), lambda i, j: (i, nc + j)), blk], out_specs=[blk, blk],
                 out_shape=[o, o], compiler_params=_cp("parallel", "parallel"))(vg, vg, dout)


def _blockdiag_in(b2, nb, P):
    cg = b2.shape[0]
    gb = S5_BLOCK_GROUPS
    t = b2.reshape(cg, nb, gb, P).transpose(1, 0, 2, 3)
    eye = jnp.eye(gb, dtype=b2.dtype)
    return (eye[None, :, None, :, None] * t[:, None]).reshape(nb, gb * cg, gb * P)


def _blockdiag_in_extract(db, cg, P):
    nb = db.shape[0]
    gb = S5_BLOCK_GROUPS
    eye = jnp.eye(gb, dtype=db.dtype)
    t = (db.reshape(nb, gb, cg, gb, P) * eye[None, :, None, :, None]).sum(3)
    return t.transpose(2, 0, 1, 3).reshape(cg, nb * gb * P)


def _blockdiag_out(c, nb):
    G, cg, P = c.shape
    gb = S5_BLOCK_GROUPS
    t = c.reshape(nb, gb, cg, P).transpose(0, 3, 1, 2)
    eye = jnp.eye(gb, dtype=c.dtype)
    return (eye[None, :, None, :, None] * t[:, None]).reshape(nb, gb * P, gb * cg)


def _blockdiag_out_extract(dc, cg, P):
    nb = dc.shape[0]
    gb = S5_BLOCK_GROUPS
    eye = jnp.eye(gb, dtype=dc.dtype)
    t = (dc.reshape(nb, gb, P, gb, cg) * eye[None, :, None, :, None]).sum(1)
    return t.transpose(0, 2, 3, 1).reshape(nb * gb, cg, P)


def _xa_fwd(q, kv):
    S, D = q.shape
    M = kv.shape[0]
    dh = D // XA_HEADS
    ts = _tile(S, 512, 8)
    scale = dh ** -0.5

    def body(q_ref, k_ref, v_ref, o_ref):
        s = _dot(q_ref[...], k_ref[...], _NT) * scale
        e = jnp.exp(s - jnp.max(s, axis=-1, keepdims=True))
        p = e / jnp.sum(e, axis=-1, keepdims=True)
        o_ref[...] = _dot(p.astype(_MXU), v_ref[...]).astype(o_ref.dtype)

    return _call(body, name=f"xa_fwd_{S}", grid=(S // ts, XA_HEADS),
                 in_specs=[pl.BlockSpec((ts, dh), lambda i, h: (i, h)), pl.BlockSpec((M, dh), lambda i, h: (0, h)),
                           pl.BlockSpec((M, dh), lambda i, h: (0, XA_HEADS + h))],
                 out_specs=pl.BlockSpec((ts, dh), lambda i, h: (i, h)), out_shape=jax.ShapeDtypeStruct((S, D), _MXU),
                 compiler_params=_cp("parallel", "parallel"))(q, kv, kv)


def _xa_bwd(q, kv, do):
    S, D = q.shape
    M = kv.shape[0]
    dh = D // XA_HEADS
    ts = _tile(S, 512, 8)
    scale = dh ** -0.5

    def body(q_ref, k_ref, v_ref, do_ref, dq_ref, dk_ref, dv_ref):
        @pl.when(pl.program_id(1) == 0)
        def _():
            dk_ref[...] = jnp.zeros_like(dk_ref)
            dv_ref[...] = jnp.zeros_like(dv_ref)

        qv, kv_, vv, dov = q_ref[...], k_ref[...], v_ref[...], do_ref[...]
        s = _dot(qv, kv_, _NT) * scale
        e = jnp.exp(s - jnp.max(s, axis=-1, keepdims=True))
        p = e / jnp.sum(e, axis=-1, keepdims=True)
        dp = _dot(dov, vv, _NT)
        dv_ref[...] += _dot(p.astype(_MXU), dov, _TN)
        ds = (p * (dp - jnp.sum(dp * p, axis=-1, keepdims=True)) * scale).astype(_MXU)
        dq_ref[...] = _dot(ds, kv_).astype(dq_ref.dtype)
        dk_ref[...] += _dot(ds, qv, _TN)

    qb = pl.BlockSpec((ts, dh), lambda h, i: (i, h))
    mb = pl.BlockSpec((M, dh), lambda h, i: (0, h))
    m = jax.ShapeDtypeStruct((M, D), f32)
    return _call(body, name=f"xa_bwd_{S}", grid=(XA_HEADS, S // ts),
                 in_specs=[qb, mb, pl.BlockSpec((M, dh), lambda h, i: (0, XA_HEADS + h)), qb],
                 out_specs=[qb, mb, mb], out_shape=[jax.ShapeDtypeStruct((S, D), _MXU), m, m],
                 compiler_params=_cp("parallel", "arbitrary"))(q, kv, kv, do)


def _conv(u, cw_ref, cb_ref, rows):
    return cw_ref[2:3, :] * u + cw_ref[1:2, :] * _shift_down(u, 1, rows) + cw_ref[0:1, :] * _shift_down(u, 2, rows) + cb_ref[...]


def _convglu_fwd(u, cw, cb):
    S, F2 = u.shape
    F = F2 // 2
    tc = _tile(F, 128)
    nc = F // tc

    def body(uv_ref, ug_ref, cwv_ref, cwg_ref, cbv_ref, cbg_ref, o_ref):
        rows = lax.broadcasted_iota(jnp.int32, (S, 1), 0)
        val = _conv(uv_ref[...], cwv_ref, cbv_ref, rows)
        gate = _conv(ug_ref[...], cwg_ref, cbg_ref, rows)
        o_ref[...] = (gate * _sigmoid(gate) * val).astype(o_ref.dtype)

    def col(r, off):
        return pl.BlockSpec((r, tc), lambda j: (0, off + j))

    return _call(body, name=f"convglu_fwd_{S}", grid=(nc,),
                 in_specs=[col(S, 0), col(S, nc), col(3, 0), col(3, nc), col(1, 0), col(1, nc)], out_specs=col(S, 0),
                 out_shape=jax.ShapeDtypeStruct((S, F), _MXU), compiler_params=_cp("parallel"))(
                     u, u, cw, cw, cb.reshape(1, F2), cb.reshape(1, F2))


def _convglu_bwd(u, cw, cb, dact):
    S, F2 = u.shape
    F = F2 // 2
    tc = _tile(F, 128)
    nc = F // tc

    def body(uv_ref, ug_ref, cwv_ref, cwg_ref, cbv_ref, cbg_ref, da_ref, duv_ref, dug_ref, dcwv_ref, dcwg_ref, dcbv_ref, dcbg_ref):
        rows = lax.broadcasted_iota(jnp.int32, (S, 1), 0)
        uv, ug = uv_ref[...], ug_ref[...]
        val = _conv(uv, cwv_ref, cbv_ref, rows)
        gate = _conv(ug, cwg_ref, cbg_ref, rows)
        sg = _sigmoid(gate)
        da = da_ref[...]
        dval = da * (gate * sg)
        dgate = da * val * (sg * (1.0 + gate * (1.0 - sg)))
        for uu, d, cw_ref, du_ref, dcw_ref, dcb_ref in ((uv, dval, cwv_ref, duv_ref, dcwv_ref, dcbv_ref),
                                                        (ug, dgate, cwg_ref, dug_ref, dcwg_ref, dcbg_ref)):
            dcb_ref[...] = jnp.sum(d, axis=0, keepdims=True)
            dcw_ref[2:3, :] = jnp.sum(d * uu, axis=0, keepdims=True)
            dcw_ref[1:2, :] = jnp.sum(d * _shift_down(uu, 1, rows), axis=0, keepdims=True)
            dcw_ref[0:1, :] = jnp.sum(d * _shift_down(uu, 2, rows), axis=0, keepdims=True)
            du = cw_ref[2:3, :] * d + cw_ref[1:2, :] * _shift_up(d, 1, rows, S) + cw_ref[0:1, :] * _shift_up(d, 2, rows, S)
            du_ref[...] = du.astype(du_ref.dtype)

    def col(r, off):
        return pl.BlockSpec((r, tc), lambda j: (0, off + j))

    o = jax.ShapeDtypeStruct((S, F), _MXU)
    w, b = jax.ShapeDtypeStruct((3, F), f32), jax.ShapeDtypeStruct((1, F), f32)
    duv, dug, dcwv, dcwg, dcbv, dcbg = _call(
        body, name=f"convglu_bwd_{S}", grid=(nc,),
        in_specs=[col(S, 0), col(S, nc), col(3, 0), col(3, nc), col(1, 0), col(1, nc), col(S, 0)],
        out_specs=[col(S, 0), col(S, 0), col(3, 0), col(3, 0), col(1, 0), col(1, 0)],
        out_shape=[o, o, w, w, b, b], compiler_params=_cp("parallel"))(
            u, u, cw, cw, cb.reshape(1, F2), cb.reshape(1, F2), dact)
    return (jnp.concatenate([duv, dug], axis=1), jnp.concatenate([dcwv, dcwg], axis=1),
            jnp.concatenate([dcbv, dcbg], axis=1).reshape(F2))


def _final_loss(h, delta, g, target):
    S, D = h.shape
    ts = _tile(S, 512, 8)
    row = pl.BlockSpec((ts, D), lambda i: (i, 0))
    vec = pl.BlockSpec((1, D), lambda i: (0, 0))
    one = pl.BlockSpec((1, _LANES), lambda i: (0, 0))

    def body(h_ref, d_ref, g_ref, t_ref, l_ref, dh_ref, dg_ref):
        @pl.when(pl.program_id(0) == 0)
        def _():
            l_ref[...] = jnp.zeros_like(l_ref)
            dg_ref[...] = jnp.zeros_like(dg_ref)

        xv = h_ref[...] + d_ref[...]
        gv = g_ref[...]
        r = lax.rsqrt(jnp.mean(xv * xv, axis=-1, keepdims=True) + EPS)
        xh = xv * r
        err = xh * gv - t_ref[...]
        l_ref[...] += 0.5 * jnp.sum(jnp.mean(err * err, axis=-1, keepdims=True), axis=0, keepdims=True)
        dy = err * (1.0 / D)
        dg_ref[...] += jnp.sum(dy * xh, axis=0, keepdims=True)
        dxh = dy * gv
        dh_ref[...] = r * (dxh - xh * jnp.mean(dxh * xh, axis=-1, keepdims=True))

    loss, dh, dg = _call(body, name=f"final_loss_{S}", grid=(S // ts,), in_specs=[row, row, vec, row],
                         out_specs=[one, row, vec],
                         out_shape=[jax.ShapeDtypeStruct((1, _LANES), f32), jax.ShapeDtypeStruct((S, D), f32),
                                    jax.ShapeDtypeStruct((1, D), f32)],
                         compiler_params=_cp("arbitrary"))(h, delta, g.reshape(1, D), target)
    return loss[0, 0], dh, dg.reshape(D)


def _adamw(w, g, m, v):
    R, C = w.shape
    tr = _tile(R, max(8, (1 << 19) // C // 8 * 8), 8)
    blk = pl.BlockSpec((tr, C), lambda i: (i, 0))

    def body(w_ref, g_ref, m_ref, v_ref, d_ref, nm_ref, nv_ref):
        gv = g_ref[...]
        m_new = ADAM_B1 * m_ref[...] + (1.0 - ADAM_B1) * gv
        v_new = ADAM_B2 * v_ref[...] + (1.0 - ADAM_B2) * (gv * gv)
        m_hat = m_new / (1.0 - ADAM_B1 ** ADAM_STEP)
        v_hat = v_new / (1.0 - ADAM_B2 ** ADAM_STEP)
        d_ref[...] = -ADAM_LR * (m_hat / (jnp.sqrt(v_hat) + ADAM_EPS) + ADAM_WD * w_ref[...])
        nm_ref[...] = m_new
        nv_ref[...] = v_new

    o = jax.ShapeDtypeStruct((R, C), f32)
    return _call(body, name=f"adamw_{R}x{C}", grid=(R // tr,), in_specs=[blk] * 4, out_specs=[blk] * 3,
                 out_shape=[o, o, o], compiler_params=_cp("parallel"))(w, g, m, v)


def _sum_slabs(xs):
    K, R, C = xs.shape
    tr = _tile(R, 256, 8)

    def body(x_ref, o_ref):
        s = x_ref[0]
        for k in range(1, K):
            s = s + x_ref[k]
        o_ref[...] = s

    return _call(body, name=f"sum_slabs_{K}x{R}", grid=(R // tr,),
                 in_specs=[pl.BlockSpec((K, tr, C), lambda i: (0, i, 0))], out_specs=pl.BlockSpec((tr, C), lambda i: (i, 0)),
                 out_shape=jax.ShapeDtypeStruct((R, C), f32), compiler_params=_cp("parallel"))(xs)


def _slab_rows(R, C):
    return _tile(R, max(16, (1 << 19) // C // 16 * 16), 16)


def _add_partial(g, recv, core):
    _, R, C = g.shape
    tr = _slab_rows(R, C)

    def body(c_ref, a_ref, b_ref, o_ref, w_ref):
        s = a_ref[...] + b_ref[...]
        o_ref[...] = s
        w_ref[...] = s.astype(w_ref.dtype)

    blk = pl.BlockSpec((1, tr, C), lambda k, i, c: (k, i, 0))
    return _call(
        body, name=f"add_partial_{R}x{C}",
        grid_spec=pltpu.PrefetchScalarGridSpec(
            num_scalar_prefetch=1, grid=(4, R // tr),
            in_specs=[pl.BlockSpec((1, tr, C), lambda k, i, c: (2 * k + c[0], i, 0)), blk], out_specs=[blk, blk]),
        out_shape=[jax.ShapeDtypeStruct((4, R, C), f32), jax.ShapeDtypeStruct((4, R, C), _WIRE)],
        compiler_params=_cp("parallel", "parallel"))(core, g, recv)


def _sum_final(p, recv, chip):
    _, R, C = p.shape
    tr = _slab_rows(R, C)

    def body(q_ref, p_ref, r_ref, o_ref):
        o_ref[...] = ((p_ref[0] + r_ref[0].astype(f32)) + r_ref[1].astype(f32)) + r_ref[2].astype(f32)

    return _call(
        body, name=f"sum_final_{R}x{C}",
        grid_spec=pltpu.PrefetchScalarGridSpec(
            num_scalar_prefetch=1, grid=(R // tr,),
            in_specs=[pl.BlockSpec((1, tr, C), lambda i, q: (q[0], i, 0)), pl.BlockSpec((3, tr, C), lambda i, q: (0, i, 0))],
            out_specs=pl.BlockSpec((tr, C), lambda i, q: (i, 0))),
        out_shape=jax.ShapeDtypeStruct((R, C), f32), compiler_params=_cp("parallel"))(chip, p, recv)


def _my_pos():
    return lax.axis_index("x"), lax.axis_index("y"), lax.axis_index("c")


def _all_gather(shards):
    n = len(shards)

    def body(*refs):
        start, forward, finish = _all_gather_stages(refs[:n], refs[n:2 * n], *refs[2 * n:])
        start()
        forward()
        finish()

    tag = "_".join(f"{s.shape[0]}x{s.shape[1]}" for s in shards)
    return _call(body, name=f"all_gather_{tag}", in_specs=[ANY] * n, out_specs=[ANY] * n,
                 out_shape=_all_gather_shapes(shards), scratch_shapes=_all_gather_sems(n))(*shards)


def _all_gather_shapes(shards):
    return [jax.ShapeDtypeStruct((NDEV,) + s.shape, s.dtype) for s in shards]


def _all_gather_sems(n):
    return [pltpu.SemaphoreType.DMA((7, n)), pltpu.SemaphoreType.DMA((7, n)), pltpu.SemaphoreType.DMA((n,))]


def _all_gather_stages(x_refs, out_refs, send_sems, recv_sems, local_sems):
    n = len(x_refs)
    x, y, c = _my_pos()
    me, sibling = (x, y, c), (x, y, 1 - c)
    chips = [(1 - x, y), (x, 1 - y), (1 - x, 1 - y)]

    def slab(t, px, py, pc):
        return out_refs[t].at[4 * px + 2 * py + pc]

    def copy(k, t, block, to, from_input=False):
        return pltpu.make_async_remote_copy(
            src_ref=x_refs[t] if from_input else slab(t, *block), dst_ref=slab(t, *block),
            send_sem=send_sems.at[k, t], recv_sem=recv_sems.at[k, t], device_id=to, device_id_type=MESH_ID)

    mine = [pltpu.make_async_copy(x_refs[t], slab(t, *me), local_sems.at[t]) for t in range(n)]
    first = [copy(0, t, me, sibling, True) for t in range(n)]
    first += [copy(1 + j, t, me, (*chip, c), True) for j, chip in enumerate(chips) for t in range(n)]
    passed = [copy(4 + j, t, (*chip, c), sibling) for j, chip in enumerate(chips) for t in range(n)]

    def start():
        for cp in mine + first:
            cp.start()

    def forward():
        for j, chip in enumerate(chips):
            for t in range(n):
                copy(1 + j, t, (*chip, c), me).wait_recv()
                passed[j * n + t].start()

    def finish():
        for t in range(n):
            copy(0, t, sibling, me).wait_recv()
        for j, chip in enumerate(chips):
            for t in range(n):
                copy(4 + j, t, (*chip, 1 - c), me).wait_recv()
        for cp in first + passed:
            cp.wait_send()
        for cp in mine:
            cp.wait()

    return start, forward, finish


def _exchange_cores(gs):
    n = len(gs)

    def body(*refs):
        g_refs, out_refs = refs[:n], refs[n:2 * n]
        send_sems, recv_sems = refs[2 * n:]
        x, y, c = _my_pos()
        cps = [pltpu.make_async_remote_copy(src_ref=g_refs[t].at[2 * q + (1 - c)], dst_ref=out_refs[t].at[q],
                                            send_sem=send_sems.at[q, t], recv_sem=recv_sems.at[q, t],
                                            device_id=(x, y, 1 - c), device_id_type=MESH_ID)
               for t in range(n) for q in range(4)]
        for cp in cps:
            cp.start()
        for cp in cps:
            cp.wait()

    tag = "_".join(f"{g.shape[1]}x{g.shape[2]}" for g in gs)
    return _call(body, name=f"exchange_cores_{tag}", in_specs=[ANY] * n, out_specs=[ANY] * n,
                 out_shape=[jax.ShapeDtypeStruct((4,) + g.shape[1:], g.dtype) for g in gs],
                 scratch_shapes=[pltpu.SemaphoreType.DMA((4, n)), pltpu.SemaphoreType.DMA((4, n))])(*gs)


def _exchange_chips(ps):
    n = len(ps)

    def body(*refs):
        start, finish = _exchange_chips_stages(refs[:n], refs[n:2 * n], *refs[2 * n:])
        start()
        finish()

    tag = "_".join(f"{p.shape[1]}x{p.shape[2]}" for p in ps)
    return _call(body, name=f"exchange_chips_{tag}", in_specs=[ANY] * n, out_specs=[ANY] * n,
                 out_shape=_exchange_chips_shapes(ps), scratch_shapes=_exchange_chips_sems(n))(*ps)


def _exchange_chips_shapes(ps):
    return [jax.ShapeDtypeStruct((3,) + p.shape[1:], p.dtype) for p in ps]


def _exchange_chips_sems(n):
    return [pltpu.SemaphoreType.DMA((3, n)), pltpu.SemaphoreType.DMA((3, n))]


def _exchange_chips_stages(p_refs, out_refs, send_sems, recv_sems):
    n = len(p_refs)
    x, y, c = _my_pos()
    chips = [(x, 1 - y), (1 - x, y), (1 - x, 1 - y)]
    cps = [pltpu.make_async_remote_copy(src_ref=p_refs[t].at[2 * px + py], dst_ref=out_refs[t].at[r],
                                        send_sem=send_sems.at[r, t], recv_sem=recv_sems.at[r, t],
                                        device_id=(px, py, c), device_id_type=MESH_ID)
           for t in range(n) for r, (px, py) in enumerate(chips)]

    def start():
        for cp in cps:
            cp.start()

    def finish():
        for cp in cps:
            cp.wait()

    return start, finish


def _pack(arrs, dtype, rows_mult):
    flat = jnp.concatenate([a.astype(dtype).reshape(-1) for a in arrs])
    q = rows_mult * _PACK_COLS
    tot = -(-flat.shape[0] // q) * q
    return jnp.pad(flat, (0, tot - flat.shape[0])).reshape(tot // _PACK_COLS, _PACK_COLS)


def _unpack(flat, shapes):
    out, off = [], 0
    for s in shapes:
        n = math.prod(s)
        out.append(flat[..., off:off + n].reshape(flat.shape[:-1] + tuple(s)))
        off += n
    return out


def _width_groups(names, shapes):
    groups = {}
    for n in names:
        groups.setdefault(shapes[n][-1], []).append(n)
    return list(groups.values())


def _to_full(piece, ax):
    t = jnp.moveaxis(piece, 0, ax)
    return t.reshape(t.shape[:ax] + (t.shape[ax] * t.shape[ax + 1],) + t.shape[ax + 2:])


def _to_shards(g, ax):
    t = g.reshape(g.shape[:ax] + (NDEV, g.shape[ax] // NDEV) + g.shape[ax + 1:])
    return jnp.moveaxis(t, ax, 0)


def _split_rows(buf, shapes):
    out, off = [], 0
    for s in shapes:
        r = math.prod(s[:-1])
        out.append(buf[..., off:off + r, :].reshape(buf.shape[:-2] + tuple(s)))
        off += r
    return out


def _item_ax(item):
    return SHARD_AXIS[item[0]] - 1


def _gather_bufs(shards, items):
    shapes = {it: shards[it[0]].shape[1:] for it in items}
    groups = _width_groups(items, shapes)
    bufs = [jnp.concatenate([shards[n][l].astype(_MXU).reshape(-1, shapes[(n, l)][-1]) for n, l in grp], axis=0)
            for grp in groups]
    return groups, bufs


def _gather_unpack(shards, groups, outs):
    full = {}
    for grp, g in zip(groups, outs):
        for it, piece in zip(grp, _split_rows(g, [shards[it[0]].shape[1:] for it in grp])):
            full[it] = _to_full(piece, _item_ax(it))
    return full


def _scatter_bufs(grads, shards, items):
    shapes = {it: shards[it[0]].shape[1:] for it in items}
    groups = _width_groups(items, shapes)
    bufs = [jnp.concatenate([_to_shards(grads[it], _item_ax(it)).reshape(NDEV, -1, shapes[it][-1]) for it in grp], axis=1)
            for grp in groups]
    bufs = [jnp.pad(b, ((0, 0), (0, -b.shape[1] % 256), (0, 0))) for b in bufs]
    return groups, bufs


def _mesh_scalars():
    core = jnp.reshape(lax.axis_index("c"), (1,)).astype(jnp.int32)
    chip = jnp.reshape(2 * lax.axis_index("x") + lax.axis_index("y"), (1,)).astype(jnp.int32)
    return core, chip


def _chip_sums(bufs):
    core, _ = _mesh_scalars()
    return [_add_partial(b, r, core) for b, r in zip(bufs, _exchange_cores(bufs))]


def _device_sums(chip_sums, from_chips):
    _, chip = _mesh_scalars()
    return [_sum_final(p, r, chip) for (p, _), r in zip(chip_sums, from_chips)]


def _scatter_unpack(shards, groups, totals):
    out = {}
    for grp, t in zip(groups, totals):
        out.update(zip(grp, _split_rows(t, [shards[it[0]].shape[1:] for it in grp])))
    return out


def _small_scatter_buf(grads, names):
    small = jnp.concatenate([_to_shards(grads[n], SHARD_AXIS[n]).reshape(NDEV, -1) for n in names], axis=1)
    q = 16 * _PACK_COLS
    tot = -(-small.shape[1] // q) * q
    return jnp.pad(small, ((0, 0), (0, tot - small.shape[1]))).reshape(NDEV, tot // _PACK_COLS, _PACK_COLS)


def _all_reduce_small(grads, names):
    shapes = [grads[n].shape for n in names]
    packed = _pack([grads[n] for n in names], f32, 8)
    total = _sum_slabs(_all_gather([packed])[0])
    return dict(zip(names, _unpack(total.reshape(-1), shapes)))


def _as2d(a):
    if a.ndim == 1:
        return a.reshape(1, -1)
    return a.reshape(-1, a.shape[-1])


def kernel(x, mem, mix_norm_g, pool_w, pool_scale, sb_w_qkv, sb_w_o, s5_a_re, s5_a_im, s5_log_dt, s5_b_re, s5_b_im, s5_c_re, s5_c_im, s5_d, s5_w_glu, xa_norm_g, mem_norm_g, xa_wq, xa_wkv, xa_wo, ffn_norm_g, ffn_w_up, ffn_conv_w, ffn_conv_b, ffn_w_down, final_norm_g, loss_target, m_mix_norm_g, m_pool_w, m_pool_scale, m_sb_w_qkv, m_sb_w_o, m_s5_a_re, m_s5_a_im, m_s5_log_dt, m_s5_b_re, m_s5_b_im, m_s5_c_re, m_s5_c_im, m_s5_d, m_s5_w_glu, m_xa_norm_g, m_mem_norm_g, m_xa_wq, m_xa_wkv, m_xa_wo, m_ffn_norm_g, m_ffn_w_up, m_ffn_conv_w, m_ffn_conv_b, m_ffn_w_down, m_final_norm_g, v_mix_norm_g, v_pool_w, v_pool_scale, v_sb_w_qkv, v_sb_w_o, v_s5_a_re, v_s5_a_im, v_s5_log_dt, v_s5_b_re, v_s5_b_im, v_s5_c_re, v_s5_c_im, v_s5_d, v_s5_w_glu, v_xa_norm_g, v_mem_norm_g, v_xa_wq, v_xa_wkv, v_xa_wo, v_ffn_norm_g, v_ffn_w_up, v_ffn_conv_w, v_ffn_conv_b, v_ffn_w_down, v_final_norm_g):
    args = locals()
    w = {n: args[n] for n in WEIGHTS}
    mom = {n: args["m_" + n] for n in WEIGHTS}
    var = {n: args["v_" + n] for n in WEIGHTS}
    h0, memv, target = x[0], mem[0], loss_target[0]
    S, D = h0.shape
    depth = mix_norm_g.shape[0]
    n_mix = 3

    first_layer = {'pool_w': 0, 'sb_w_qkv': 1, 'sb_w_o': 1, 's5_w_glu': 2}

    def layer_of(item):
        return first_layer[item[0]] + n_mix * item[1] if item[0] in first_layer else item[1]

    sb_layer = 1 if depth > 1 else None
    items = [(n, l) for n in MXU_WEIGHTS for l in range(w[n].shape[0])]
    early_w = [it for it in items if sb_layer is None or layer_of(it) < sb_layer or it == ('sb_w_qkv', 0)]
    late_w = [it for it in items if it not in early_w]
    groups_a, bufs_a = _gather_bufs(w, early_w)
    outs_a = _all_gather(bufs_a + [_pack([w[n] for n in VEC_WEIGHTS], f32, 8)])
    wfull = _gather_unpack(w, groups_a, outs_a[:-1])
    full = dict(w)
    for n, piece in zip(VEC_WEIGHTS, _unpack(outs_a[-1].reshape(NDEV, -1), [w[n].shape for n in VEC_WEIGHTS])):
        full[n] = _to_full(piece, SHARD_AXIS[n])
    groups_b, bufs_b = _gather_bufs(w, late_w)

    grads = {}

    def acc(name, j, val):
        grads.setdefault(name, {})[j] = val

    s5 = []
    for j in range(s5_a_re.shape[0]):
        G, P = s5_a_re.shape[1:]
        cg = s5_b_re.shape[3]
        N = G * P
        nb = D // _LANES
        ar, ai = s5_a_re[j].reshape(1, N), s5_a_im[j].reshape(1, N)
        ldt = jnp.repeat(s5_log_dt[j], P).reshape(1, N)
        br = s5_b_re[j].transpose(2, 0, 1).reshape(cg, N)
        bi = s5_b_im[j].transpose(2, 0, 1).reshape(cg, N)
        abr, abi, bbr, bbi = _s5_prep_fwd(ar, ai, ldt, br, bi)
        s5.append(dict(ar=ar, ai=ai, ldt=ldt, br=br, bi=bi, abr=abr, abi=abi, G=G, P=P, cg=cg, N=N, nb=nb,
                       Br=_blockdiag_in(bbr, nb, P), Bi=_blockdiag_in(bbi, nb, P),
                       Cr=_blockdiag_out(s5_c_re[j], nb), Ci=_blockdiag_out(s5_c_im[j], nb)))

    saved = []
    h, delta = h0, None
    for i in range(depth):
        kind, j = i % n_mix, i // n_mix
        sv = dict(kind=kind, j=j)
        if delta is None:
            hn = _rms_fwd(h, full['mix_norm_g'][i])
        else:
            h, hn = _rms_fwd(h, full['mix_norm_g'][i], delta)
        sv['h'] = h
        if kind == 0:
            p = _pool_fwd(hn)
            t = _pool_mix_fwd(p, wfull['pool_w', j], full['pool_scale'][j])
            sv.update(p=p)
        elif kind == 1:
            qkv = _mm(hn, wfull['sb_w_qkv', j], out_dtype=_MXU)
            if i == sb_layer:
                o, outs_b = _sb_fwd(qkv, bufs_b)
                wfull.update(_gather_unpack(w, groups_b, outs_b))
            else:
                o, _ = _sb_fwd(qkv)
            t = _mm(o, wfull['sb_w_o', j])
            sv.update(hn=hn, qkv=qkv, o=o)
        else:
            pr = s5[j]
            bur, bui = _s5_in_fwd(hn, pr['Br'], pr['Bi'])
            xr, xi = _s5_scan_fwd(bur, bui, pr['abr'], pr['abi'])
            y, z = _s5_out_fwd(xr, xi, pr['Cr'], pr['Ci'], hn, full['s5_d'][j])
            vg = _mm(z, wfull['s5_w_glu', j])
            t = _glu_fwd(vg)
            sv.update(hn=hn, xr=xr, xi=xi, y=y, z=z, vg=vg)
        h1, a = _rms_fwd(h, full['xa_norm_g'][i], t)
        memn = _rms_fwd(memv, full['mem_norm_g'][i])
        q = _mm(a, wfull['xa_wq', i], out_dtype=_MXU)
        kv = _mm(memn, wfull['xa_wkv', i], out_dtype=_MXU)
        o2 = _xa_fwd(q, kv)
        mo = _mm(o2, wfull['xa_wo', i])
        h2, b = _rms_fwd(h1, full['ffn_norm_g'][i], mo)
        u = _mm(b, wfull['ffn_w_up', i])
        act = _convglu_fwd(u, full['ffn_conv_w'][i], full['ffn_conv_b'][i])
        delta = _mm(act, wfull['ffn_w_down', i])
        sv.update(h1=h1, a=a, memn=memn, q=q, kv=kv, o2=o2, h2=h2, b=b, u=u, act=act)
        saved.append(sv)
        h = h2

    loss_local, dh, dg_final = _final_loss(h, delta, full['final_norm_g'], target)
    loss = lax.psum(loss_local, ("x", "y", "c"))
    grads['final_norm_g'] = dg_final

    for i in reversed(range(depth)):
        sv = saved[i]
        kind, j = sv['kind'], sv['j']
        dact = _mm(dh, wfull['ffn_w_down', i], tb=True)
        acc('ffn_w_down', i, _mm(sv['act'], dh, ta=True))
        du, dcw, dcb = _convglu_bwd(sv['u'], full['ffn_conv_w'][i], full['ffn_conv_b'][i], dact)
        acc('ffn_conv_w', i, dcw)
        acc('ffn_conv_b', i, dcb)
        db = _mm(du, wfull['ffn_w_up', i], tb=True)
        acc('ffn_w_up', i, _mm(sv['b'], du, ta=True))
        dh2, dg = _rms_bwd(sv['h2'], full['ffn_norm_g'][i], db, dh)
        acc('ffn_norm_g', i, dg)
        do2 = _mm(dh2, wfull['xa_wo', i], tb=True, out_dtype=_MXU)
        acc('xa_wo', i, _mm(sv['o2'], dh2, ta=True))
        dq, dk, dv = _xa_bwd(sv['q'], sv['kv'], do2)
        dkv = jnp.concatenate([dk, dv], axis=1)
        da = _mm(dq, wfull['xa_wq', i], tb=True)
        acc('xa_wq', i, _mm(sv['a'], dq, ta=True))
        dmemn = _mm(dkv, wfull['xa_wkv', i], tb=True)
        acc('xa_wkv', i, _mm(sv['memn'], dkv, ta=True))
        _, dg = _rms_bwd(memv, full['mem_norm_g'][i], dmemn)
        acc('mem_norm_g', i, dg)
        dh1, dg = _rms_bwd(sv['h1'], full['xa_norm_g'][i], da, dh2)
        acc('xa_norm_g', i, dg)
        if kind == 0:
            dp, dw, ds = _pool_mix_bwd(sv['p'], wfull['pool_w', j], full['pool_scale'][j], dh1)
            acc('pool_w', j, dw)
            acc('pool_scale', j, ds)
            dhn = _pool_bwd(dp)
        elif kind == 1:
            do = _mm(dh1, wfull['sb_w_o', j], tb=True)
            acc('sb_w_o', j, _mm(sv['o'], dh1, ta=True))
            if i == sb_layer:
                early_g = [it for it in items if it != ('sb_w_qkv', j) and layer_of(it) >= sb_layer]
                groups_e, bufs_e = _scatter_bufs({it: grads[it[0]][it[1]] for it in early_g}, w, early_g)
                sums_e = _chip_sums(bufs_e)
                dqs, dks, dvs, recv_e = _sb_bwd(sv['qkv'], sv['o'], do, [wire for _, wire in sums_e])
            else:
                dqs, dks, dvs, _ = _sb_bwd(sv['qkv'], sv['o'], do)
            dqkv = jnp.concatenate([dqs, dks, dvs], axis=1)
            dhn = _mm(dqkv, wfull['sb_w_qkv', j], tb=True)
            acc('sb_w_qkv', j, _mm(sv['hn'], dqkv, ta=True))
        else:
            pr = s5[j]
            dval, dgate = _glu_bwd(sv['vg'], dh1)
            dvg = jnp.concatenate([dval, dgate], axis=1)
            dz = _mm(dvg, wfull['s5_w_glu', j], tb=True)
            acc('s5_w_glu', j, _mm(sv['z'], dvg, ta=True))
            dy, gdr, gdi, du0, dd = _s5_out_bwd(dz, sv['y'], sv['hn'], full['s5_d'][j], pr['Cr'], pr['Ci'])
            acc('s5_d', j, dd)
            gr, gi, dabr, dabi = _s5_scan_bwd(gdr, gdi, sv['xr'], sv['xi'], pr['abr'], pr['abi'])
            dhn = _s5_in_bwd(gr, gi, pr['Br'], pr['Bi'], du0)
            dBr, dBi, dCr, dCi = _s5_wgrad(sv['hn'], gr, gi, sv['xr'], sv['xi'], dy)
            cg, P, G = pr['cg'], pr['P'], pr['G']
            acc('s5_c_re', j, _blockdiag_out_extract(dCr, cg, P))
            acc('s5_c_im', j, _blockdiag_out_extract(dCi, cg, P))
            dar, dai, dldt, dbr, dbi = _s5_prep_bwd(pr['ar'], pr['ai'], pr['ldt'], pr['br'], pr['bi'], dabr, dabi,
                                                    _blockdiag_in_extract(dBr, cg, P), _blockdiag_in_extract(dBi, cg, P))
            acc('s5_a_re', j, dar.reshape(G, P))
            acc('s5_a_im', j, dai.reshape(G, P))
            acc('s5_log_dt', j, dldt.reshape(G, P).sum(axis=1))
            acc('s5_b_re', j, dbr.reshape(cg, G, P).transpose(1, 2, 0))
            acc('s5_b_im', j, dbi.reshape(cg, G, P).transpose(1, 2, 0))
        dh, dg = _rms_bwd(sv['h'], full['mix_norm_g'][i], dhn, dh1)
        acc('mix_norm_g', i, dg)
    grad_x = dh[None]

    gfull = {}
    for n in VEC_WEIGHTS + REPLICATED:
        gfull[n] = grads[n] if n == 'final_norm_g' else jnp.stack([grads[n][k] for k in range(len(grads[n]))])
    if sb_layer is None:
        early_g, local = [], {}
    else:
        local = _scatter_unpack(w, groups_e, _device_sums(sums_e, recv_e))
    late_g = [it for it in items if it not in early_g]
    groups_l, bufs_l = _scatter_bufs({it: grads[it[0]][it[1]] for it in late_g}, w, late_g)
    sums_l = _chip_sums(bufs_l + [_small_scatter_buf(gfull, VEC_WEIGHTS)])
    totals_l = _device_sums(sums_l, _exchange_chips([wire for _, wire in sums_l]))
    local.update(_scatter_unpack(w, groups_l, totals_l[:-1]))
    gw = {n: jnp.stack([local[(n, l)] for l in range(w[n].shape[0])]) for n in MXU_WEIGHTS}
    gw.update(zip(VEC_WEIGHTS, _unpack(totals_l[-1].reshape(-1), [w[n].shape for n in VEC_WEIGHTS])))
    gw.update(_all_reduce_small(gfull, REPLICATED))

    deltas, new_m, new_v = {}, {}, {}
    for n in WEIGHTS:
        shp = w[n].shape
        d_, m_, v_ = _adamw(_as2d(w[n]), _as2d(gw[n]), _as2d(mom[n]), _as2d(var[n]))
        deltas[n], new_m[n], new_v[n] = d_.reshape(shp), m_.reshape(shp), v_.reshape(shp)

    return (loss, grad_x, *[gw[n] for n in WEIGHTS], *[deltas[n] for n in WEIGHTS],
            *[new_m[n] for n in WEIGHTS], *[new_v[n] for n in WEIGHTS])
```

```python
import functools
import math

import jax
import jax.numpy as jnp
from jax import lax
from jax.experimental import pallas as pl
from jax.experimental.pallas import tpu as pltpu

f32 = jnp.float32
_MXU = jnp.bfloat16
_WIRE = _MXU
_VMEM_LIMIT = 48 * 1024 * 1024
_LANES = 128
_PACK_COLS = 1024

NDEV = 8
EPS = 1e-6
POOL_WINDOWS = (2, 4, 8, 16)
SB_HEAD_DIM = 64
XA_HEADS = 4
S5_GROUP = 16
S5_BLOCK_GROUPS = _LANES // S5_GROUP
ADAM_LR, ADAM_B1, ADAM_B2, ADAM_EPS, ADAM_WD, ADAM_STEP = 0.001, 0.9, 0.999, 1e-08, 0.01, 10

WEIGHTS = ['mix_norm_g', 'pool_w', 'pool_scale', 'sb_w_qkv', 'sb_w_o', 's5_a_re', 's5_a_im', 's5_log_dt',
           's5_b_re', 's5_b_im', 's5_c_re', 's5_c_im', 's5_d', 's5_w_glu', 'xa_norm_g', 'mem_norm_g', 'xa_wq',
           'xa_wkv', 'xa_wo', 'ffn_norm_g', 'ffn_w_up', 'ffn_conv_w', 'ffn_conv_b', 'ffn_w_down', 'final_norm_g']
SHARD_AXIS = {'pool_w': 2, 'pool_scale': 1, 'sb_w_qkv': 2, 'sb_w_o': 1, 's5_d': 1, 's5_w_glu': 2, 'xa_wq': 1,
              'xa_wkv': 2, 'xa_wo': 1, 'ffn_w_up': 2, 'ffn_conv_w': 2, 'ffn_w_down': 1}
MXU_WEIGHTS = ['pool_w', 'sb_w_qkv', 'sb_w_o', 's5_w_glu', 'xa_wq', 'xa_wkv', 'xa_wo', 'ffn_w_up', 'ffn_w_down']
VEC_WEIGHTS = ['pool_scale', 's5_d', 'ffn_conv_w']
REPLICATED = [n for n in WEIGHTS if n not in SHARD_AXIS]

_NN = (((1,), (0,)), ((), ()))
_NT = (((1,), (1,)), ((), ()))
_TN = (((0,), (0,)), ((), ()))
MESH_ID = pl.DeviceIdType.MESH
ANY = pl.BlockSpec(memory_space=pl.ANY)


def _call(body, **kw):
    return pl.pallas_call(body, **kw)


def _cp(*sem):
    return pltpu.CompilerParams(dimension_semantics=sem, vmem_limit_bytes=_VMEM_LIMIT)


def _tile(n, target, mult=_LANES):
    if n <= target:
        return n
    t = (target // mult) * mult
    while t >= mult:
        if n % t == 0:
            return t
        t -= mult
    return n


def _dot(a, b, dims=_NN):
    return lax.dot_general(a, b, dims, preferred_element_type=f32)


def _split(a):
    hi = a.astype(_MXU)
    lo = (a - hi.astype(f32)).astype(_MXU)
    return hi, lo


def _dot_hilo(a, u, dims=_NN):
    hi, lo = _split(a)
    return _dot(hi, u, dims) + _dot(lo, u, dims)


def _dot3(a, b, dims=_NN):
    ah, al = _split(a)
    bh, bl = _split(b)
    return _dot(ah, bh, dims) + _dot(al, bh, dims) + _dot(ah, bl, dims)


def _sigmoid(x):
    return 0.5 * jnp.tanh(0.5 * x) + 0.5


_GELU_C = math.sqrt(2.0 / math.pi)


def _gelu(x):
    return x * (0.5 * (1.0 + jnp.tanh(_GELU_C * (x + 0.044715 * (x * x * x)))))


def _gelu_grad(x):
    t = jnp.tanh(_GELU_C * (x + 0.044715 * (x * x * x)))
    return 0.5 * (1.0 + t) + x * 0.5 * (1.0 - t * t) * _GELU_C * (1.0 + 3.0 * 0.044715 * x * x)


_SUBLANES = 8


def _shift_down(x, k, rows=None):
    r = pltpu.roll(x, k, 0)
    head = jnp.where(lax.broadcasted_iota(jnp.int32, (_SUBLANES, 1), 0) >= k, r[:_SUBLANES], 0.0)
    return jnp.concatenate([head, r[_SUBLANES:]], axis=0)


def _shift_up(x, k, rows, n):
    r = pltpu.roll(x, n - k, 0)
    tail = jnp.where(lax.broadcasted_iota(jnp.int32, (_SUBLANES, 1), 0) < _SUBLANES - k, r[n - _SUBLANES:], 0.0)
    return jnp.concatenate([r[:n - _SUBLANES], tail], axis=0)


def _mm(a, b, *, ta=False, tb=False, out_dtype=f32):
    M, K = (a.shape[1], a.shape[0]) if ta else a.shape
    N = b.shape[0] if tb else b.shape[1]
    tm, tn, tk = _tile(M, 1408), _tile(N, 1536), _tile(K, 1408)
    nk = K // tk
    a_spec = pl.BlockSpec((tk, tm), lambda i, j, k: (k, i)) if ta else pl.BlockSpec((tm, tk), lambda i, j, k: (i, k))
    b_spec = pl.BlockSpec((tn, tk), lambda i, j, k: (j, k)) if tb else pl.BlockSpec((tk, tn), lambda i, j, k: (k, j))
    dims = (((0 if ta else 1,), (1 if tb else 0,)), ((), ()))

    def body(a_ref, b_ref, o_ref, acc_ref):
        k = pl.program_id(2)

        @pl.when(k == 0)
        def _():
            acc_ref[...] = jnp.zeros_like(acc_ref)

        acc_ref[...] += _dot(a_ref[...].astype(_MXU), b_ref[...].astype(_MXU), dims)

        @pl.when(k == nk - 1)
        def _():
            o_ref[...] = acc_ref[...].astype(out_dtype)

    return _call(
        body, name=f"mm_{'t' if ta else 'n'}{'t' if tb else 'n'}_{M}x{K}x{N}",
        grid=(M // tm, N // tn, nk), in_specs=[a_spec, b_spec],
        out_specs=pl.BlockSpec((tm, tn), lambda i, j, k: (i, j)),
        out_shape=jax.ShapeDtypeStruct((M, N), out_dtype),
        scratch_shapes=[pltpu.VMEM((tm, tn), f32)],
        compiler_params=_cp("parallel", "parallel", "arbitrary"))(a, b)


def _rms_fwd(x, g, delta=None, *, out_dtype=f32):
    S, D = x.shape
    ts = _tile(S, 512, 8)
    row = pl.BlockSpec((ts, D), lambda i: (i, 0))
    vec = pl.BlockSpec((1, D), lambda i: (0, 0))
    g2 = g.reshape(1, D)

    def norm(xv, g_ref):
        r = lax.rsqrt(jnp.mean(xv * xv, axis=-1, keepdims=True) + EPS)
        return ((xv * r) * g_ref[...]).astype(out_dtype)

    if delta is None:
        def body(x_ref, g_ref, y_ref):
            y_ref[...] = norm(x_ref[...], g_ref)

        return _call(body, name=f"rms_fwd_{S}", grid=(S // ts,), in_specs=[row, vec], out_specs=row,
                     out_shape=jax.ShapeDtypeStruct((S, D), out_dtype), compiler_params=_cp("parallel"))(x, g2)

    def body(x_ref, d_ref, g_ref, s_ref, y_ref):
        xv = x_ref[...] + d_ref[...]
        s_ref[...] = xv
        y_ref[...] = norm(xv, g_ref)

    return _call(body, name=f"add_rms_fwd_{S}", grid=(S // ts,), in_specs=[row, row, vec], out_specs=[row, row],
                 out_shape=[jax.ShapeDtypeStruct((S, D), f32), jax.ShapeDtypeStruct((S, D), out_dtype)],
                 compiler_params=_cp("parallel"))(x, delta, g2)


def _rms_bwd(x, g, dy, dres=None):
    S, D = x.shape
    ts = _tile(S, 512, 8)
    row = pl.BlockSpec((ts, D), lambda i: (i, 0))
    vec = pl.BlockSpec((1, D), lambda i: (0, 0))
    has_res = dres is not None

    def body(*refs):
        if has_res:
            x_ref, g_ref, dy_ref, dr_ref, dx_ref, dg_ref = refs
        else:
            x_ref, g_ref, dy_ref, dx_ref, dg_ref = refs
        xv = x_ref[...]
        r = lax.rsqrt(jnp.mean(xv * xv, axis=-1, keepdims=True) + EPS)
        xh = xv * r
        dyv = dy_ref[...].astype(f32)

        @pl.when(pl.program_id(0) == 0)
        def _():
            dg_ref[...] = jnp.zeros_like(dg_ref)

        dg_ref[...] += jnp.sum(dyv * xh, axis=0, keepdims=True)
        dxh = dyv * g_ref[...]
        dx = r * (dxh - xh * jnp.mean(dxh * xh, axis=-1, keepdims=True))
        if has_res:
            dx = dx + dr_ref[...]
        dx_ref[...] = dx

    ins = [x, g.reshape(1, D), dy] + ([dres] if has_res else [])
    dx, dg = _call(body, name=f"rms_bwd_{S}_{int(has_res)}", grid=(S // ts,),
                   in_specs=[row, vec, row] + ([row] if has_res else []), out_specs=[row, vec],
                   out_shape=[jax.ShapeDtypeStruct((S, D), f32), jax.ShapeDtypeStruct((1, D), f32)],
                   compiler_params=_cp("arbitrary"))(*ins)
    return dx, dg.reshape(D)


def _pool_windows_sum(x, win, rows):
    s, k = x, 1
    while k < win:
        s = s + _shift_down(s, k, rows)
        k *= 2
    return s


def _pool_windows_sum_up(x, win, rows, n):
    s, k = x, 1
    while k < win:
        s = s + _shift_up(s, k, rows, n)
        k *= 2
    return s


def _pool_fwd(hn):
    S, D = hn.shape
    cg = D // len(POOL_WINDOWS)
    tc = _tile(cg, 128)
    nb = cg // tc
    blk = pl.BlockSpec((S, tc), lambda gi, j: (0, gi * nb + j))

    def body(x_ref, p_ref):
        gi = pl.program_id(0)
        rows = lax.broadcasted_iota(jnp.int32, (S, 1), 0)
        cnt = (rows + 1).astype(f32)
        for k, win in enumerate(POOL_WINDOWS):
            @pl.when(gi == k)
            def _(win=win):
                x = x_ref[...]
                s = _pool_windows_sum(x, win, rows)
                p_ref[...] = (s / jnp.minimum(cnt, float(win)) - x).astype(p_ref.dtype)

    return _call(body, name=f"pool_fwd_{S}", grid=(len(POOL_WINDOWS), nb), in_specs=[blk], out_specs=blk,
                 out_shape=jax.ShapeDtypeStruct((S, D), _MXU), compiler_params=_cp("parallel", "parallel"))(hn)


def _pool_bwd(dp):
    S, D = dp.shape
    cg = D // len(POOL_WINDOWS)
    tc = _tile(cg, 128)
    nb = cg // tc
    blk = pl.BlockSpec((S, tc), lambda gi, j: (0, gi * nb + j))

    def body(dp_ref, dx_ref):
        gi = pl.program_id(0)
        rows = lax.broadcasted_iota(jnp.int32, (S, 1), 0)
        cnt = (rows + 1).astype(f32)
        for k, win in enumerate(POOL_WINDOWS):
            @pl.when(gi == k)
            def _(win=win):
                d = dp_ref[...]
                e = d / jnp.minimum(cnt, float(win))
                dx_ref[...] = _pool_windows_sum_up(e, win, rows, S) - d

    return _call(body, name=f"pool_bwd_{S}", grid=(len(POOL_WINDOWS), nb), in_specs=[blk], out_specs=blk,
                 out_shape=jax.ShapeDtypeStruct((S, D), f32), compiler_params=_cp("parallel", "parallel"))(dp)


def _pool_mix_fwd(p, w, scale):
    S, D = p.shape
    G, cg, _ = w.shape
    ts = _tile(S, 1024, 8)

    def body(p_ref, w_ref, s_ref, y_ref):
        y_ref[...] = _dot(p_ref[...], w_ref[0]) * s_ref[...]

    return _call(body, name=f"pool_mix_fwd_{S}", grid=(S // ts, G),
                 in_specs=[pl.BlockSpec((ts, cg), lambda i, g: (i, g)), pl.BlockSpec((1, cg, cg), lambda i, g: (g, 0, 0)),
                           pl.BlockSpec((1, cg), lambda i, g: (0, g))],
                 out_specs=pl.BlockSpec((ts, cg), lambda i, g: (i, g)),
                 out_shape=jax.ShapeDtypeStruct((S, D), f32), compiler_params=_cp("parallel", "parallel"))(
                     p, w, scale.reshape(1, D))


def _pool_mix_bwd(p, w, scale, dy):
    S, D = p.shape
    G, cg, _ = w.shape
    ts = _tile(S, 1024, 8)

    def body(p_ref, w_ref, s_ref, dy_ref, dp_ref, dw_ref, ds_ref):
        @pl.when(pl.program_id(1) == 0)
        def _():
            dw_ref[...] = jnp.zeros_like(dw_ref)
            ds_ref[...] = jnp.zeros_like(ds_ref)

        pv, wv, dyv = p_ref[...], w_ref[0], dy_ref[...]
        ypre = _dot(pv, wv)
        ds_ref[...] += jnp.sum(dyv * ypre, axis=0, keepdims=True)
        dyp = (dyv * s_ref[...]).astype(_MXU)
        dp_ref[...] = _dot(dyp, wv, _NT)
        dw_ref[0] += _dot(pv, dyp, _TN)

    dp, dw, ds = _call(
        body, name=f"pool_mix_bwd_{S}", grid=(G, S // ts),
        in_specs=[pl.BlockSpec((ts, cg), lambda g, i: (i, g)), pl.BlockSpec((1, cg, cg), lambda g, i: (g, 0, 0)),
                  pl.BlockSpec((1, cg), lambda g, i: (0, g)), pl.BlockSpec((ts, cg), lambda g, i: (i, g))],
        out_specs=[pl.BlockSpec((ts, cg), lambda g, i: (i, g)), pl.BlockSpec((1, cg, cg), lambda g, i: (g, 0, 0)),
                   pl.BlockSpec((1, cg), lambda g, i: (0, g))],
        out_shape=[jax.ShapeDtypeStruct((S, D), f32), jax.ShapeDtypeStruct((G, cg, cg), f32),
                   jax.ShapeDtypeStruct((1, D), f32)],
        compiler_params=_cp("parallel", "arbitrary"))(p, w, scale.reshape(1, D), dy)
    return dp, dw, ds.reshape(D)


def _sb_tile(S):
    return min(256, max(128, S // 4))


_LOG2E = math.log2(math.e)


def _sb_scores(qs, ks, tri, R, U):
    z = _dot(qs, ks, _NT)
    sp = jnp.maximum(z, 0.0) + jnp.log(1.0 + jnp.exp2(jnp.abs(z) * -_LOG2E))
    lb = z - sp
    if tri is not None:
        sp = jnp.where(tri, sp, 0.0)
    c = _dot(sp.astype(_MXU), U)
    a = jnp.exp(lb - c - R)
    if tri is not None:
        a = jnp.where(tri, a, 0.0)
    return sp, lb, a


def _sb_sweep(tile, i, carry):
    def two(p, c):
        kb = i - 1 - 2 * p
        return tile(kb - 1, tile(kb, c, None), None)

    carry = lax.fori_loop(0, i // 2, two, carry)
    return lax.cond(i % 2 == 1, lambda c: tile(i * 0, c, None), lambda c: c, carry)


def _sb_consts(T):
    lane = lax.broadcasted_iota(jnp.int32, (1, _LANES), 1)
    row, col = lax.broadcasted_iota(jnp.int32, (T, T), 0), lax.broadcasted_iota(jnp.int32, (T, T), 1)
    U = jnp.where(row > col, 1.0, 0.0).astype(_MXU)
    heads = [(lane >= SB_HEAD_DIM * h) & (lane < SB_HEAD_DIM * (h + 1)) for h in range(_LANES // SB_HEAD_DIM)]
    return heads, col < row, U


def _sb_fwd(qkv, gather=()):
    S, D3 = qkv.shape
    D = D3 // 3
    HP = D // _LANES
    T = _sb_tile(S)
    nq = S // T
    n = len(gather)
    scale = SB_HEAD_DIM ** -0.5

    def body(*refs):
        q_ref, k_ref, v_ref = refs[:3]
        o_ref = refs[3 + n]
        i = pl.program_id(1)
        if n:
            step = pl.program_id(0) * nq + i
            start, forward, finish = _all_gather_stages(refs[3:3 + n], refs[4 + n:4 + 2 * n], *refs[4 + 2 * n:])
            pl.when(step == 0)(start)
            pl.when(step == (HP // 2) * nq)(forward)
        q = q_ref[...] * scale
        heads, tri, U = _sb_consts(T)
        qms = [jnp.where(hm, q, jnp.zeros_like(q)) for hm in heads]

        def tile(kb, carry, mask):
            off = pl.multiple_of(kb * T, T)
            ks, vs = k_ref[pl.ds(off, T), :], v_ref[pl.ds(off, T), :]
            new = []
            for h, qm in enumerate(qms):
                R, acc = carry[2 * h], carry[2 * h + 1]
                sp, _, a = _sb_scores(qm, ks, mask, R, U)
                new += [R + jnp.sum(sp, axis=1, keepdims=True), acc + _dot(a.astype(_MXU), vs)]
            return tuple(new)

        zero = (jnp.zeros((T, 1), f32), jnp.zeros((T, _LANES), f32)) * len(heads)
        carry = _sb_sweep(tile, i, tile(i, zero, tri))
        out = jnp.zeros((T, _LANES), f32)
        for h, hm in enumerate(heads):
            out = out + jnp.where(hm, carry[2 * h + 1], 0.0)
        o_ref[...] = out
        if n:
            pl.when(step == HP * nq - 1)(finish)

    outs = _call(
        body, name=f"sb_fwd_{S}_{n}", grid=(HP, nq),
        in_specs=[pl.BlockSpec((T, _LANES), lambda hp, i: (i, hp)), pl.BlockSpec((S, _LANES), lambda hp, i: (0, HP + hp)),
                  pl.BlockSpec((S, _LANES), lambda hp, i: (0, 2 * HP + hp))] + [ANY] * n,
        out_specs=[pl.BlockSpec((T, _LANES), lambda hp, i: (i, hp))] + [ANY] * n,
        out_shape=[jax.ShapeDtypeStruct((S, D), f32)] + _all_gather_shapes(gather),
        scratch_shapes=_all_gather_sems(n) if n else [],
        compiler_params=_cp("arbitrary", "arbitrary"))(qkv, qkv, qkv, *gather)
    return outs[0], list(outs[1:])


def _sb_bwd(qkv, o, do, send=()):
    S, D3 = qkv.shape
    D = D3 // 3
    HP = D // _LANES
    T = _sb_tile(S)
    nq = S // T
    n = len(send)
    scale = SB_HEAD_DIM ** -0.5

    def body(*refs):
        q_ref, k_ref, v_ref, o_ref, do_ref = refs[:5]
        dq_ref, dk_ref, dv_ref = refs[5 + n:8 + n]
        dk_acc, dv_acc = refs[8 + 2 * n:10 + 2 * n]
        i = pl.program_id(1)
        if n:
            step = pl.program_id(0) * nq + i
            start, finish = _exchange_all_stages(refs[5:5 + n], refs[8 + n:8 + 2 * n], *refs[10 + 2 * n:])
            pl.when(step == 0)(start)

        @pl.when(i == 0)
        def _():
            dk_acc[...] = jnp.zeros_like(dk_acc)
            dv_acc[...] = jnp.zeros_like(dv_acc)

        q = q_ref[...] * scale
        dob = do_ref[...].astype(_MXU)
        prod = dob.astype(f32) * o_ref[...]
        heads, tri, U = _sb_consts(T)
        qms = [jnp.where(hm, q, jnp.zeros_like(q)) for hm in heads]
        doms = [jnp.where(hm, dob, jnp.zeros_like(dob)) for hm in heads]
        totals = [jnp.sum(jnp.where(hm, prod, 0.0), axis=1, keepdims=True) for hm in heads]

        def tile(kb, carry, mask):
            off = pl.multiple_of(kb * T, T)
            ks, vs = k_ref[pl.ds(off, T), :], v_ref[pl.ds(off, T), :]
            new = []
            dk_t = jnp.zeros((T, _LANES), f32)
            dv_t = jnp.zeros((T, _LANES), f32)
            for h, (qm, dom, total) in enumerate(zip(qms, doms, totals)):
                R, Gs, dq = carry[3 * h:3 * h + 3]
                sp, lb, a = _sb_scores(qm, ks, mask, R, U)
                ab = a.astype(_MXU)
                g = ab.astype(f32) * _dot(dom, vs, _NT)
                before = total - (g + _dot_hilo(g, U) + Gs)
                beta = jnp.exp(lb)
                dz = g - (g + before) * beta
                if mask is not None:
                    dz = jnp.where(mask, dz, 0.0)
                dzb = dz.astype(_MXU)
                dk_t = dk_t + _dot(dzb, qm, _TN)
                dv_t = dv_t + _dot(ab, dom, _TN)
                new += [R + jnp.sum(sp, axis=1, keepdims=True), Gs + jnp.sum(g, axis=1, keepdims=True), dq + _dot(dzb, ks)]
            dk_acc[pl.ds(off, T), :] += dk_t
            dv_acc[pl.ds(off, T), :] += dv_t
            return tuple(new)

        zero1 = jnp.zeros((T, 1), f32)
        carry = _sb_sweep(tile, i, tile(i, (zero1, zero1, jnp.zeros((T, _LANES), f32)) * len(heads), tri))
        dq_out = jnp.zeros((T, _LANES), f32)
        for h, hm in enumerate(heads):
            dq_out = dq_out + jnp.where(hm, carry[3 * h + 2], 0.0)
        dq_ref[...] = (dq_out * scale).astype(dq_ref.dtype)

        @pl.when(i == nq - 1)
        def _():
            dk_ref[...] = dk_acc[...].astype(dk_ref.dtype)
            dv_ref[...] = dv_acc[...].astype(dv_ref.dtype)

        if n:
            pl.when(step == HP * nq - 1)(finish)

    qb = pl.BlockSpec((T, _LANES), lambda hp, i: (i, hp))
    col = pl.BlockSpec((S, _LANES), lambda hp, i: (0, hp))
    out = jax.ShapeDtypeStruct((S, D), _MXU)
    outs = _call(
        body, name=f"sb_bwd_{S}_{n}", grid=(HP, nq),
        in_specs=[qb, pl.BlockSpec((S, _LANES), lambda hp, i: (0, HP + hp)),
                  pl.BlockSpec((S, _LANES), lambda hp, i: (0, 2 * HP + hp)), qb, qb] + [ANY] * n,
        out_specs=[qb, col, col] + [ANY] * n, out_shape=[out, out, out] + _exchange_all_shapes(send),
        scratch_shapes=[pltpu.VMEM((S, _LANES), f32), pltpu.VMEM((S, _LANES), f32)] + (_exchange_all_sems(n) if n else []),
        compiler_params=_cp("arbitrary", "arbitrary"))(qkv, qkv, qkv, o, do, *send)
    return outs[0], outs[1], outs[2], list(outs[3:])


def _cmul(ar, ai, br, bi):
    return ar * br - ai * bi, ar * bi + ai * br


def _s5_coef(ar, ai, ldt):
    dt = jnp.exp(ldt)
    e = jnp.exp(ar * dt)
    abr, abi = e * jnp.cos(ai * dt), e * jnp.sin(ai * dt)
    inv = 1.0 / (ar * ar + ai * ai)
    cr, ci = _cmul(abr - 1.0, abi, ar * inv, -ai * inv)
    return dt, abr, abi, inv, cr, ci


def _s5_prep_fwd(ar, ai, ldt, br, bi):
    N = ar.shape[1]
    cg = br.shape[0]

    def body(ar_ref, ai_ref, ldt_ref, br_ref, bi_ref, abr_ref, abi_ref, bbr_ref, bbi_ref):
        _, abr, abi, _, cr, ci = _s5_coef(ar_ref[...], ai_ref[...], ldt_ref[...])
        abr_ref[...], abi_ref[...] = abr, abi
        bbr_ref[...], bbi_ref[...] = _cmul(cr, ci, br_ref[...], bi_ref[...])

    v, m = jax.ShapeDtypeStruct((1, N), f32), jax.ShapeDtypeStruct((cg, N), f32)
    return _call(body, name="s5_prep_fwd", out_shape=[v, v, m, m])(ar, ai, ldt, br, bi)


def _s5_prep_bwd(ar, ai, ldt, br, bi, dabr, dabi, dbbr, dbbi):
    N = ar.shape[1]
    cg = br.shape[0]

    def body(ar_ref, ai_ref, ldt_ref, br_ref, bi_ref, dabr_ref, dabi_ref, dbbr_ref, dbbi_ref,
             dar_ref, dai_ref, dldt_ref, dbr_ref, dbi_ref):
        a_r, a_i = ar_ref[...], ai_ref[...]
        dt, abr, abi, inv, cr, ci = _s5_coef(a_r, a_i, ldt_ref[...])
        b_r, b_i, gr, gi = br_ref[...], bi_ref[...], dbbr_ref[...], dbbi_ref[...]
        dbr_ref[...], dbi_ref[...] = _cmul(cr, -ci, gr, gi)
        dcr = jnp.sum(gr * b_r + gi * b_i, axis=0, keepdims=True)
        dci = jnp.sum(gi * b_r - gr * b_i, axis=0, keepdims=True)
        ilr, ili = a_r * inv, -a_i * inv
        dwr, dwi = _cmul(ilr, -ili, dcr, dci)
        qr, qi = _cmul(cr, ci, ilr, ili)
        dl1r, dl1i = _cmul(-qr, qi, dcr, dci)
        tr, ti = dabr_ref[...] + dwr, dabi_ref[...] + dwi
        ddlr, ddli = _cmul(abr, -abi, tr, ti)
        dar_ref[...] = dl1r + ddlr * dt
        dai_ref[...] = dl1i + ddli * dt
        dldt_ref[...] = (a_r * ddlr + a_i * ddli) * dt

    v, m = jax.ShapeDtypeStruct((1, N), f32), jax.ShapeDtypeStruct((cg, N), f32)
    return _call(body, name="s5_prep_bwd", out_shape=[v, v, v, m, m])(ar, ai, ldt, br, bi, dabr, dabi, dbbr, dbbi)


def _s5_in_fwd(u, Br, Bi):
    S, D = u.shape
    nb, _, nw = Br.shape
    ts = _tile(S, 512, 8)

    def body(u_ref, br_ref, bi_ref, or_ref, oi_ref):
        uv = u_ref[...]
        or_ref[...] = _dot3(uv, br_ref[0])
        oi_ref[...] = _dot3(uv, bi_ref[0])

    ub = pl.BlockSpec((ts, _LANES), lambda i, k: (i, k))
    wb = pl.BlockSpec((1, _LANES, nw), lambda i, k: (k, 0, 0))
    ob = pl.BlockSpec((ts, nw), lambda i, k: (i, k))
    o = jax.ShapeDtypeStruct((S, nb * nw), f32)
    return _call(body, name=f"s5_in_fwd_{S}", grid=(S // ts, nb), in_specs=[ub, wb, wb], out_specs=[ob, ob],
                 out_shape=[o, o], compiler_params=_cp("parallel", "parallel"))(u, Br, Bi)


def _s5_scan_fwd(bur, bui, abr, abi):
    S, N = bur.shape
    tt, tn = _tile(S, 128, 8), _tile(N, 4096)

    def body(br_ref, bi_ref, ar_ref, ai_ref, xr_ref, xi_ref, sr, si):
        @pl.when(pl.program_id(1) == 0)
        def _():
            sr[...] = jnp.zeros_like(sr)
            si[...] = jnp.zeros_like(si)

        a_r, a_i = ar_ref[...], ai_ref[...]

        def step(t, carry):
            xr, xi = carry
            nr = a_r * xr - a_i * xi + br_ref[pl.ds(t, 1), :]
            ni = a_r * xi + a_i * xr + bi_ref[pl.ds(t, 1), :]
            xr_ref[pl.ds(t, 1), :] = nr
            xi_ref[pl.ds(t, 1), :] = ni
            return nr, ni

        xr, xi = lax.fori_loop(0, tt, step, (sr[...], si[...]))
        sr[...], si[...] = xr, xi

    blk = pl.BlockSpec((tt, tn), lambda n, i: (i, n))
    vec = pl.BlockSpec((1, tn), lambda n, i: (0, n))
    o = jax.ShapeDtypeStruct((S, N), f32)
    return _call(body, name=f"s5_scan_fwd_{S}", grid=(N // tn, S // tt), in_specs=[blk, blk, vec, vec],
                 out_specs=[blk, blk], out_shape=[o, o],
                 scratch_shapes=[pltpu.VMEM((1, tn), f32), pltpu.VMEM((1, tn), f32)],
                 compiler_params=_cp("parallel", "arbitrary"))(bur, bui, abr, abi)


def _s5_scan_bwd(dr, di, xr, xi, abr, abi):
    S, N = dr.shape
    tt, tn = _tile(S, 128, 8), _tile(N, 4096)
    nt = S // tt

    def body(dr_ref, di_ref, xr_ref, xi_ref, ar_ref, ai_ref, gr_ref, gi_ref, dar_ref, dai_ref, sr, si):
        @pl.when(pl.program_id(1) == 0)
        def _():
            sr[...] = jnp.zeros_like(sr)
            si[...] = jnp.zeros_like(si)
            dar_ref[...] = jnp.zeros_like(dar_ref)
            dai_ref[...] = jnp.zeros_like(dai_ref)

        a_r, a_i = ar_ref[...], ai_ref[...]

        def step(j, carry):
            gr, gi, accr, acci = carry
            t = tt - 1 - j
            xr_t, xi_t = xr_ref[pl.ds(t, 1), :], xi_ref[pl.ds(t, 1), :]
            accr = accr + gr * xr_t + gi * xi_t
            acci = acci + gi * xr_t - gr * xi_t
            nr = dr_ref[pl.ds(t, 1), :] + a_r * gr + a_i * gi
            ni = di_ref[pl.ds(t, 1), :] + a_r * gi - a_i * gr
            gr_ref[pl.ds(t, 1), :] = nr
            gi_ref[pl.ds(t, 1), :] = ni
            return nr, ni, accr, acci

        gr, gi, accr, acci = lax.fori_loop(0, tt, step, (sr[...], si[...], dar_ref[...], dai_ref[...]))
        sr[...], si[...] = gr, gi
        dar_ref[...], dai_ref[...] = accr, acci

    blk = pl.BlockSpec((tt, tn), lambda n, i: (nt - 1 - i, n))
    vec = pl.BlockSpec((1, tn), lambda n, i: (0, n))
    o, v = jax.ShapeDtypeStruct((S, N), f32), jax.ShapeDtypeStruct((1, N), f32)
    return _call(body, name=f"s5_scan_bwd_{S}", grid=(N // tn, nt), in_specs=[blk, blk, blk, blk, vec, vec],
                 out_specs=[blk, blk, vec, vec], out_shape=[o, o, v, v],
                 scratch_shapes=[pltpu.VMEM((1, tn), f32), pltpu.VMEM((1, tn), f32)],
                 compiler_params=_cp("parallel", "arbitrary"))(dr, di, xr, xi, abr, abi)


def _s5_out_fwd(xr, xi, Cr, Ci, u, d):
    S, N = xr.shape
    nb, nw, _ = Cr.shape
    D = u.shape[1]
    ts = _tile(S, 512, 8)

    def body(xr_ref, xi_ref, cr_ref, ci_ref, u_ref, d_ref, y_ref, z_ref):
        y = _dot3(xr_ref[...], cr_ref[0]) - _dot3(xi_ref[...], ci_ref[0]) + d_ref[...] * u_ref[...]
        y_ref[...] = y
        z_ref[...] = _gelu(y).astype(z_ref.dtype)

    xb = pl.BlockSpec((ts, nw), lambda i, k: (i, k))
    cb = pl.BlockSpec((1, nw, _LANES), lambda i, k: (k, 0, 0))
    ub = pl.BlockSpec((ts, _LANES), lambda i, k: (i, k))
    db = pl.BlockSpec((1, _LANES), lambda i, k: (0, k))
    return _call(body, name=f"s5_out_fwd_{S}", grid=(S // ts, nb), in_specs=[xb, xb, cb, cb, ub, db],
                 out_specs=[ub, ub], out_shape=[jax.ShapeDtypeStruct((S, D), f32), jax.ShapeDtypeStruct((S, D), _MXU)],
                 compiler_params=_cp("parallel", "parallel"))(xr, xi, Cr, Ci, u, d.reshape(1, D))


def _s5_out_bwd(dz, y, u, d, Cr, Ci):
    S, D = y.shape
    nb, nw, _ = Cr.shape
    ts = _tile(S, 512, 8)

    def body(dz_ref, y_ref, u_ref, d_ref, cr_ref, ci_ref, dy_ref, gr_ref, gi_ref, du_ref, dd_ref):
        @pl.when(pl.program_id(1) == 0)
        def _():
            dd_ref[...] = jnp.zeros_like(dd_ref)

        dy = dz_ref[...] * _gelu_grad(y_ref[...])
        dy_ref[...] = dy
        gr_ref[...] = _dot3(dy, cr_ref[0], _NT)
        gi_ref[...] = -_dot3(dy, ci_ref[0], _NT)
        du_ref[...] = dy * d_ref[...]
        dd_ref[...] += jnp.sum(dy * u_ref[...], axis=0, keepdims=True)

    xb = pl.BlockSpec((ts, nw), lambda k, i: (i, k))
    cb = pl.BlockSpec((1, nw, _LANES), lambda k, i: (k, 0, 0))
    ub = pl.BlockSpec((ts, _LANES), lambda k, i: (i, k))
    db = pl.BlockSpec((1, _LANES), lambda k, i: (0, k))
    a, s = jax.ShapeDtypeStruct((S, D), f32), jax.ShapeDtypeStruct((S, nb * nw), f32)
    dy, gr, gi, du, dd = _call(
        body, name=f"s5_out_bwd_{S}", grid=(nb, S // ts), in_specs=[ub, ub, ub, db, cb, cb],
        out_specs=[ub, xb, xb, ub, db], out_shape=[a, s, s, a, jax.ShapeDtypeStruct((1, D), f32)],
        compiler_params=_cp("parallel", "arbitrary"))(dz, y, u, d.reshape(1, D), Cr, Ci)
    return dy, gr, gi, du, dd.reshape(D)


def _s5_in_bwd(gr, gi, Br, Bi, du0):
    S, N = gr.shape
    nb, _, nw = Br.shape
    ts = _tile(S, 512, 8)

    def body(gr_ref, gi_ref, br_ref, bi_ref, d0_ref, du_ref):
        du_ref[...] = d0_ref[...] + _dot3(gr_ref[...], br_ref[0], _NT) + _dot3(gi_ref[...], bi_ref[0], _NT)

    xb = pl.BlockSpec((ts, nw), lambda i, k: (i, k))
    wb = pl.BlockSpec((1, _LANES, nw), lambda i, k: (k, 0, 0))
    ub = pl.BlockSpec((ts, _LANES), lambda i, k: (i, k))
    return _call(body, name=f"s5_in_bwd_{S}", grid=(S // ts, nb), in_specs=[xb, xb, wb, wb, ub], out_specs=ub,
                 out_shape=jax.ShapeDtypeStruct((S, nb * _LANES), f32), compiler_params=_cp("parallel", "parallel"))(
                     gr, gi, Br, Bi, du0)


def _s5_wgrad(u, gr, gi, xr, xi, dy):
    S, D = u.shape
    nb = D // _LANES
    nw = gr.shape[1] // nb
    ts = _tile(S, 512, 8)

    def body(u_ref, gr_ref, gi_ref, xr_ref, xi_ref, dy_ref, dbr_ref, dbi_ref, dcr_ref, dci_ref):
        @pl.when(pl.program_id(1) == 0)
        def _():
            for r in (dbr_ref, dbi_ref, dcr_ref, dci_ref):
                r[...] = jnp.zeros_like(r)

        uv, dyv = u_ref[...], dy_ref[...]
        dbr_ref[0] += _dot3(uv, gr_ref[...], _TN)
        dbi_ref[0] += _dot3(uv, gi_ref[...], _TN)
        dcr_ref[0] += _dot3(xr_ref[...], dyv, _TN)
        dci_ref[0] -= _dot3(xi_ref[...], dyv, _TN)

    xb = pl.BlockSpec((ts, nw), lambda k, i: (i, k))
    ub = pl.BlockSpec((ts, _LANES), lambda k, i: (i, k))
    wb = pl.BlockSpec((1, _LANES, nw), lambda k, i: (k, 0, 0))
    cb = pl.BlockSpec((1, nw, _LANES), lambda k, i: (k, 0, 0))
    w, c = jax.ShapeDtypeStruct((nb, _LANES, nw), f32), jax.ShapeDtypeStruct((nb, nw, _LANES), f32)
    return _call(body, name=f"s5_wgrad_{S}", grid=(nb, S // ts), in_specs=[ub, xb, xb, xb, xb, ub],
                 out_specs=[wb, wb, cb, cb], out_shape=[w, w, c, c],
                 compiler_params=_cp("parallel", "arbitrary"))(u, gr, gi, xr, xi, dy)


def _glu_fwd(vg):
    S, D2 = vg.shape
    D = D2 // 2
    ts, tc = _tile(S, 1024, 8), _tile(D, 512)
    nc = D // tc

    def body(v_ref, g_ref, o_ref):
        o_ref[...] = v_ref[...] * _sigmoid(g_ref[...])

    return _call(body, name=f"glu_fwd_{S}", grid=(S // ts, nc),
                 in_specs=[pl.BlockSpec((ts, tc), lambda i, j: (i, j)), pl.BlockSpec((ts, tc), lambda i, j: (i, nc + j))],
                 out_specs=pl.BlockSpec((ts, tc), lambda i, j: (i, j)), out_shape=jax.ShapeDtypeStruct((S, D), f32),
                 compiler_params=_cp("parallel", "parallel"))(vg, vg)


def _glu_bwd(vg, dout):
    S, D2 = vg.shape
    D = D2 // 2
    ts, tc = _tile(S, 1024, 8), _tile(D, 512)
    nc = D // tc

    def body(v_ref, g_ref, do_ref, dv_ref, dg_ref):
        sg = _sigmoid(g_ref[...])
        do = do_ref[...]
        dv_ref[...] = (do * sg).astype(dv_ref.dtype)
        dg_ref[...] = (do * v_ref[...] * sg * (1.0 - sg)).astype(dg_ref.dtype)

    blk = pl.BlockSpec((ts, tc), lambda i, j: (i, j))
    o = jax.ShapeDtypeStruct((S, D), _MXU)
    return _call(body, name=f"glu_bwd_{S}", grid=(S // ts, nc),
                 in_specs=[blk, pl.BlockSpec((ts, tc), lambda i, j: (i, nc + j)), blk], out_specs=[blk, blk],
                 out_shape=[o, o], compiler_params=_cp("parallel", "parallel"))(vg, vg, dout)


def _blockdiag_in(b2, nb, P):
    cg = b2.shape[0]
    gb = S5_BLOCK_GROUPS
    t = b2.reshape(cg, nb, gb, P).transpose(1, 0, 2, 3)
    eye = jnp.eye(gb, dtype=b2.dtype)
    return (eye[None, :, None, :, None] * t[:, None]).reshape(nb, gb * cg, gb * P)


def _blockdiag_in_extract(db, cg, P):
    nb = db.shape[0]
    gb = S5_BLOCK_GROUPS
    eye = jnp.eye(gb, dtype=db.dtype)
    t = (db.reshape(nb, gb, cg, gb, P) * eye[None, :, None, :, None]).sum(3)
    return t.transpose(2, 0, 1, 3).reshape(cg, nb * gb * P)


def _blockdiag_out(c, nb):
    G, cg, P = c.shape
    gb = S5_BLOCK_GROUPS
    t = c.reshape(nb, gb, cg, P).transpose(0, 3, 1, 2)
    eye = jnp.eye(gb, dtype=c.dtype)
    return (eye[None, :, None, :, None] * t[:, None]).reshape(nb, gb * P, gb * cg)


def _blockdiag_out_extract(dc, cg, P):
    nb = dc.shape[0]
    gb = S5_BLOCK_GROUPS
    eye = jnp.eye(gb, dtype=dc.dtype)
    t = (dc.reshape(nb, gb, P, gb, cg) * eye[None, :, None, :, None]).sum(1)
    return t.transpose(0, 2, 3, 1).reshape(nb * gb, cg, P)


def _xa_fwd(q, kv):
    S, D = q.shape
    M = kv.shape[0]
    dh = D // XA_HEADS
    ts = _tile(S, 512, 8)
    scale = dh ** -0.5

    def body(q_ref, k_ref, v_ref, o_ref):
        s = _dot(q_ref[...], k_ref[...], _NT) * scale
        e = jnp.exp(s - jnp.max(s, axis=-1, keepdims=True))
        p = e / jnp.sum(e, axis=-1, keepdims=True)
        o_ref[...] = _dot(p.astype(_MXU), v_ref[...]).astype(o_ref.dtype)

    return _call(body, name=f"xa_fwd_{S}", grid=(S // ts, XA_HEADS),
                 in_specs=[pl.BlockSpec((ts, dh), lambda i, h: (i, h)), pl.BlockSpec((M, dh), lambda i, h: (0, h)),
                           pl.BlockSpec((M, dh), lambda i, h: (0, XA_HEADS + h))],
                 out_specs=pl.BlockSpec((ts, dh), lambda i, h: (i, h)), out_shape=jax.ShapeDtypeStruct((S, D), _MXU),
                 compiler_params=_cp("parallel", "parallel"))(q, kv, kv)


def _xa_bwd(q, kv, do):
    S, D = q.shape
    M = kv.shape[0]
    dh = D // XA_HEADS
    ts = _tile(S, 512, 8)
    scale = dh ** -0.5

    def body(q_ref, k_ref, v_ref, do_ref, dq_ref, dk_ref, dv_ref):
        @pl.when(pl.program_id(1) == 0)
        def _():
            dk_ref[...] = jnp.zeros_like(dk_ref)
            dv_ref[...] = jnp.zeros_like(dv_ref)

        qv, kv_, vv, dov = q_ref[...], k_ref[...], v_ref[...], do_ref[...]
        s = _dot(qv, kv_, _NT) * scale
        e = jnp.exp(s - jnp.max(s, axis=-1, keepdims=True))
        p = e / jnp.sum(e, axis=-1, keepdims=True)
        dp = _dot(dov, vv, _NT)
        dv_ref[...] += _dot(p.astype(_MXU), dov, _TN)
        ds = (p * (dp - jnp.sum(dp * p, axis=-1, keepdims=True)) * scale).astype(_MXU)
        dq_ref[...] = _dot(ds, kv_).astype(dq_ref.dtype)
        dk_ref[...] += _dot(ds, qv, _TN)

    qb = pl.BlockSpec((ts, dh), lambda h, i: (i, h))
    mb = pl.BlockSpec((M, dh), lambda h, i: (0, h))
    m = jax.ShapeDtypeStruct((M, D), f32)
    return _call(body, name=f"xa_bwd_{S}", grid=(XA_HEADS, S // ts),
                 in_specs=[qb, mb, pl.BlockSpec((M, dh), lambda h, i: (0, XA_HEADS + h)), qb],
                 out_specs=[qb, mb, mb], out_shape=[jax.ShapeDtypeStruct((S, D), _MXU), m, m],
                 compiler_params=_cp("parallel", "arbitrary"))(q, kv, kv, do)


def _conv(u, cw_ref, cb_ref, rows):
    return _conv_taps(u, cw_ref, cb_ref)[0]


def _conv_taps(u, cw_ref, cb_ref):
    u1, u2 = _shift_down(u, 1), _shift_down(u, 2)
    return cw_ref[2:3, :] * u + cw_ref[1:2, :] * u1 + cw_ref[0:1, :] * u2 + cb_ref[...], u1, u2


def _convglu_fwd(u, cw, cb):
    S, F2 = u.shape
    F = F2 // 2
    tc = _tile(F, 128)
    nc = F // tc

    def body(uv_ref, ug_ref, cwv_ref, cwg_ref, cbv_ref, cbg_ref, o_ref):
        rows = lax.broadcasted_iota(jnp.int32, (S, 1), 0)
        val = _conv(uv_ref[...], cwv_ref, cbv_ref, rows)
        gate = _conv(ug_ref[...], cwg_ref, cbg_ref, rows)
        o_ref[...] = (gate * _sigmoid(gate) * val).astype(o_ref.dtype)

    def col(r, off):
        return pl.BlockSpec((r, tc), lambda j: (0, off + j))

    return _call(body, name=f"convglu_fwd_{S}", grid=(nc,),
                 in_specs=[col(S, 0), col(S, nc), col(3, 0), col(3, nc), col(1, 0), col(1, nc)], out_specs=col(S, 0),
                 out_shape=jax.ShapeDtypeStruct((S, F), _MXU), compiler_params=_cp("parallel"))(
                     u, u, cw, cw, cb.reshape(1, F2), cb.reshape(1, F2))


def _convglu_bwd(u, cw, cb, dact):
    S, F2 = u.shape
    F = F2 // 2
    tc = _tile(F, 128)
    nc = F // tc

    def body(uv_ref, ug_ref, cwv_ref, cwg_ref, cbv_ref, cbg_ref, da_ref, duv_ref, dug_ref, dcwv_ref, dcwg_ref, dcbv_ref, dcbg_ref):
        rows = lax.broadcasted_iota(jnp.int32, (S, 1), 0)
        uv, ug = uv_ref[...], ug_ref[...]
        val, uv1, uv2 = _conv_taps(uv, cwv_ref, cbv_ref)
        gate, ug1, ug2 = _conv_taps(ug, cwg_ref, cbg_ref)
        sg = _sigmoid(gate)
        da = da_ref[...]
        dval = da * (gate * sg)
        dgate = da * val * (sg * (1.0 + gate * (1.0 - sg)))
        for (uu, uu1, uu2), d, cw_ref, du_ref, dcw_ref, dcb_ref in (((uv, uv1, uv2), dval, cwv_ref, duv_ref, dcwv_ref, dcbv_ref),
                                                                    ((ug, ug1, ug2), dgate, cwg_ref, dug_ref, dcwg_ref, dcbg_ref)):
            dcb_ref[...] = jnp.sum(d, axis=0, keepdims=True)
            dcw_ref[2:3, :] = jnp.sum(d * uu, axis=0, keepdims=True)
            dcw_ref[1:2, :] = jnp.sum(d * uu1, axis=0, keepdims=True)
            dcw_ref[0:1, :] = jnp.sum(d * uu2, axis=0, keepdims=True)
            du = cw_ref[2:3, :] * d + cw_ref[1:2, :] * _shift_up(d, 1, rows, S) + cw_ref[0:1, :] * _shift_up(d, 2, rows, S)
            du_ref[...] = du.astype(du_ref.dtype)

    def col(r, off):
        return pl.BlockSpec((r, tc), lambda j: (0, off + j))

    o = jax.ShapeDtypeStruct((S, F), _MXU)
    w, b = jax.ShapeDtypeStruct((3, F), f32), jax.ShapeDtypeStruct((1, F), f32)
    duv, dug, dcwv, dcwg, dcbv, dcbg = _call(
        body, name=f"convglu_bwd_{S}", grid=(nc,),
        in_specs=[col(S, 0), col(S, nc), col(3, 0), col(3, nc), col(1, 0), col(1, nc), col(S, 0)],
        out_specs=[col(S, 0), col(S, 0), col(3, 0), col(3, 0), col(1, 0), col(1, 0)],
        out_shape=[o, o, w, w, b, b], compiler_params=_cp("parallel"))(
            u, u, cw, cw, cb.reshape(1, F2), cb.reshape(1, F2), dact)
    return (jnp.concatenate([duv, dug], axis=1), jnp.concatenate([dcwv, dcwg], axis=1),
            jnp.concatenate([dcbv, dcbg], axis=1).reshape(F2))


def _final_loss(h, delta, g, target):
    S, D = h.shape
    ts = _tile(S, 512, 8)
    row = pl.BlockSpec((ts, D), lambda i: (i, 0))
    vec = pl.BlockSpec((1, D), lambda i: (0, 0))
    one = pl.BlockSpec((1, _LANES), lambda i: (0, 0))

    def body(h_ref, d_ref, g_ref, t_ref, l_ref, dh_ref, dg_ref):
        @pl.when(pl.program_id(0) == 0)
        def _():
            l_ref[...] = jnp.zeros_like(l_ref)
            dg_ref[...] = jnp.zeros_like(dg_ref)

        xv = h_ref[...] + d_ref[...]
        gv = g_ref[...]
        r = lax.rsqrt(jnp.mean(xv * xv, axis=-1, keepdims=True) + EPS)
        xh = xv * r
        err = xh * gv - t_ref[...]
        l_ref[...] += 0.5 * jnp.sum(jnp.mean(err * err, axis=-1, keepdims=True), axis=0, keepdims=True)
        dy = err * (1.0 / D)
        dg_ref[...] += jnp.sum(dy * xh, axis=0, keepdims=True)
        dxh = dy * gv
        dh_ref[...] = r * (dxh - xh * jnp.mean(dxh * xh, axis=-1, keepdims=True))

    loss, dh, dg = _call(body, name=f"final_loss_{S}", grid=(S // ts,), in_specs=[row, row, vec, row],
                         out_specs=[one, row, vec],
                         out_shape=[jax.ShapeDtypeStruct((1, _LANES), f32), jax.ShapeDtypeStruct((S, D), f32),
                                    jax.ShapeDtypeStruct((1, D), f32)],
                         compiler_params=_cp("arbitrary"))(h, delta, g.reshape(1, D), target)
    return loss[0, 0], dh, dg.reshape(D)


def _adamw(w, g, m, v):
    R, C = w.shape
    tr = _tile(R, max(8, (1 << 19) // C // 8 * 8), 8)
    blk = pl.BlockSpec((tr, C), lambda i: (i, 0))

    def body(w_ref, g_ref, m_ref, v_ref, d_ref, nm_ref, nv_ref):
        gv = g_ref[...]
        m_new = ADAM_B1 * m_ref[...] + (1.0 - ADAM_B1) * gv
        v_new = ADAM_B2 * v_ref[...] + (1.0 - ADAM_B2) * (gv * gv)
        m_hat = m_new / (1.0 - ADAM_B1 ** ADAM_STEP)
        v_hat = v_new / (1.0 - ADAM_B2 ** ADAM_STEP)
        d_ref[...] = -ADAM_LR * (m_hat / (jnp.sqrt(v_hat) + ADAM_EPS) + ADAM_WD * w_ref[...])
        nm_ref[...] = m_new
        nv_ref[...] = v_new

    o = jax.ShapeDtypeStruct((R, C), f32)
    return _call(body, name=f"adamw_{R}x{C}", grid=(R // tr,), in_specs=[blk] * 4, out_specs=[blk] * 3,
                 out_shape=[o, o, o], compiler_params=_cp("parallel"))(w, g, m, v)


def _sum_slabs(xs):
    K, R, C = xs.shape
    tr = _tile(R, 256, 8)

    def body(x_ref, o_ref):
        s = x_ref[0]
        for k in range(1, K):
            s = s + x_ref[k]
        o_ref[...] = s

    return _call(body, name=f"sum_slabs_{K}x{R}", grid=(R // tr,),
                 in_specs=[pl.BlockSpec((K, tr, C), lambda i: (0, i, 0))], out_specs=pl.BlockSpec((tr, C), lambda i: (i, 0)),
                 out_shape=jax.ShapeDtypeStruct((R, C), f32), compiler_params=_cp("parallel"))(xs)


def _slab_rows(R, C):
    return _tile(R, max(16, (1 << 19) // C // 16 * 16), 16)


def _add_partial(g, recv, core):
    _, R, C = g.shape
    tr = _slab_rows(R, C)

    def body(c_ref, a_ref, b_ref, o_ref, w_ref):
        s = a_ref[...] + b_ref[...]
        o_ref[...] = s
        w_ref[...] = s.astype(w_ref.dtype)

    blk = pl.BlockSpec((1, tr, C), lambda k, i, c: (k, i, 0))
    return _call(
        body, name=f"add_partial_{R}x{C}",
        grid_spec=pltpu.PrefetchScalarGridSpec(
            num_scalar_prefetch=1, grid=(4, R // tr),
            in_specs=[pl.BlockSpec((1, tr, C), lambda k, i, c: (2 * k + c[0], i, 0)), blk], out_specs=[blk, blk]),
        out_shape=[jax.ShapeDtypeStruct((4, R, C), f32), jax.ShapeDtypeStruct((4, R, C), _WIRE)],
        compiler_params=_cp("parallel", "parallel"))(core, g, recv)


def _sum_final(p, recv, chip):
    _, R, C = p.shape
    tr = _slab_rows(R, C)

    def body(q_ref, p_ref, r_ref, o_ref):
        o_ref[...] = ((p_ref[0] + r_ref[0].astype(f32)) + r_ref[1].astype(f32)) + r_ref[2].astype(f32)

    return _call(
        body, name=f"sum_final_{R}x{C}",
        grid_spec=pltpu.PrefetchScalarGridSpec(
            num_scalar_prefetch=1, grid=(R // tr,),
            in_specs=[pl.BlockSpec((1, tr, C), lambda i, q: (q[0], i, 0)), pl.BlockSpec((3, tr, C), lambda i, q: (0, i, 0))],
            out_specs=pl.BlockSpec((tr, C), lambda i, q: (i, 0))),
        out_shape=jax.ShapeDtypeStruct((R, C), f32), compiler_params=_cp("parallel"))(chip, p, recv)


def _my_pos():
    return lax.axis_index("x"), lax.axis_index("y"), lax.axis_index("c")


def _all_gather(shards):
    n = len(shards)

    def body(*refs):
        start, forward, finish = _all_gather_stages(refs[:n], refs[n:2 * n], *refs[2 * n:])
        start()
        forward()
        finish()

    tag = "_".join(f"{s.shape[0]}x{s.shape[1]}" for s in shards)
    return _call(body, name=f"all_gather_{tag}", in_specs=[ANY] * n, out_specs=[ANY] * n,
                 out_shape=_all_gather_shapes(shards), scratch_shapes=_all_gather_sems(n))(*shards)


def _all_gather_shapes(shards):
    return [jax.ShapeDtypeStruct((NDEV,) + s.shape, s.dtype) for s in shards]


def _all_gather_sems(n):
    return [pltpu.SemaphoreType.DMA((7, n)), pltpu.SemaphoreType.DMA((7, n)), pltpu.SemaphoreType.DMA((n,))]


def _all_gather_stages(x_refs, out_refs, send_sems, recv_sems, local_sems):
    n = len(x_refs)
    x, y, c = _my_pos()
    me, sibling = (x, y, c), (x, y, 1 - c)
    chips = [(1 - x, y), (x, 1 - y), (1 - x, 1 - y)]

    def slab(t, px, py, pc):
        return out_refs[t].at[4 * px + 2 * py + pc]

    def copy(k, t, block, to, from_input=False):
        return pltpu.make_async_remote_copy(
            src_ref=x_refs[t] if from_input else slab(t, *block), dst_ref=slab(t, *block),
            send_sem=send_sems.at[k, t], recv_sem=recv_sems.at[k, t], device_id=to, device_id_type=MESH_ID)

    mine = [pltpu.make_async_copy(x_refs[t], slab(t, *me), local_sems.at[t]) for t in range(n)]
    first = [copy(0, t, me, sibling, True) for t in range(n)]
    first += [copy(1 + j, t, me, (*chip, c), True) for j, chip in enumerate(chips) for t in range(n)]
    passed = [copy(4 + j, t, (*chip, c), sibling) for j, chip in enumerate(chips) for t in range(n)]

    def start():
        for cp in mine + first:
            cp.start()

    def forward():
        for j, chip in enumerate(chips):
            for t in range(n):
                copy(1 + j, t, (*chip, c), me).wait_recv()
                passed[j * n + t].start()

    def finish():
        for t in range(n):
            copy(0, t, sibling, me).wait_recv()
        for j, chip in enumerate(chips):
            for t in range(n):
                copy(4 + j, t, (*chip, 1 - c), me).wait_recv()
        for cp in first + passed:
            cp.wait_send()
        for cp in mine:
            cp.wait()

    return start, forward, finish


def _exchange_cores(gs):
    n = len(gs)

    def body(*refs):
        g_refs, out_refs = refs[:n], refs[n:2 * n]
        send_sems, recv_sems = refs[2 * n:]
        x, y, c = _my_pos()
        cps = [pltpu.make_async_remote_copy(src_ref=g_refs[t].at[2 * q + (1 - c)], dst_ref=out_refs[t].at[q],
                                            send_sem=send_sems.at[q, t], recv_sem=recv_sems.at[q, t],
                                            device_id=(x, y, 1 - c), device_id_type=MESH_ID)
               for t in range(n) for q in range(4)]
        for cp in cps:
            cp.start()
        for cp in cps:
            cp.wait()

    tag = "_".join(f"{g.shape[1]}x{g.shape[2]}" for g in gs)
    return _call(body, name=f"exchange_cores_{tag}", in_specs=[ANY] * n, out_specs=[ANY] * n,
                 out_shape=[jax.ShapeDtypeStruct((4,) + g.shape[1:], g.dtype) for g in gs],
                 scratch_shapes=[pltpu.SemaphoreType.DMA((4, n)), pltpu.SemaphoreType.DMA((4, n))])(*gs)


def _exchange_chips(ps):
    n = len(ps)

    def body(*refs):
        start, finish = _exchange_chips_stages(refs[:n], refs[n:2 * n], *refs[2 * n:])
        start()
        finish()

    tag = "_".join(f"{p.shape[1]}x{p.shape[2]}" for p in ps)
    return _call(body, name=f"exchange_chips_{tag}", in_specs=[ANY] * n, out_specs=[ANY] * n,
                 out_shape=_exchange_chips_shapes(ps), scratch_shapes=_exchange_chips_sems(n))(*ps)


def _exchange_chips_shapes(ps):
    return [jax.ShapeDtypeStruct((3,) + p.shape[1:], p.dtype) for p in ps]


def _exchange_chips_sems(n):
    return [pltpu.SemaphoreType.DMA((3, n)), pltpu.SemaphoreType.DMA((3, n))]


def _exchange_chips_stages(p_refs, out_refs, send_sems, recv_sems):
    n = len(p_refs)
    x, y, c = _my_pos()
    chips = [(x, 1 - y), (1 - x, y), (1 - x, 1 - y)]
    cps = [pltpu.make_async_remote_copy(src_ref=p_refs[t].at[2 * px + py], dst_ref=out_refs[t].at[r],
                                        send_sem=send_sems.at[r, t], recv_sem=recv_sems.at[r, t],
                                        device_id=(px, py, c), device_id_type=MESH_ID)
           for t in range(n) for r, (px, py) in enumerate(chips)]

    def start():
        for cp in cps:
            cp.start()

    def finish():
        for cp in cps:
            cp.wait()

    return start, finish


def _exchange_all_shapes(gs):
    return [jax.ShapeDtypeStruct((NDEV - 1,) + g.shape[1:], g.dtype) for g in gs]


def _exchange_all_sems(n):
    return [pltpu.SemaphoreType.DMA((NDEV - 1, n)), pltpu.SemaphoreType.DMA((NDEV - 1, n))]


def _exchange_all_stages(g_refs, out_refs, send_sems, recv_sems):
    n = len(g_refs)
    x, y, c = _my_pos()
    cps = []
    for t in range(n):
        for r in range(1, NDEV):
            px, py, pc = (1 - x if r & 4 else x), (1 - y if r & 2 else y), (1 - c if r & 1 else c)
            cps.append(pltpu.make_async_remote_copy(
                src_ref=g_refs[t].at[4 * px + 2 * py + pc], dst_ref=out_refs[t].at[r - 1],
                send_sem=send_sems.at[r - 1, t], recv_sem=recv_sems.at[r - 1, t],
                device_id=(px, py, pc), device_id_type=MESH_ID))

    def start():
        for cp in cps:
            cp.start()

    def finish():
        for cp in cps:
            cp.wait()

    return start, finish


def _sum_all(g, recv, dev):
    _, R, C = g.shape
    tr = _slab_rows(R, C)

    def body(d_ref, g_ref, r_ref, o_ref):
        s = g_ref[0].astype(f32)
        for k in range(NDEV - 1):
            s = s + r_ref[k].astype(f32)
        o_ref[...] = s

    return _call(
        body, name=f"sum_all_{R}x{C}",
        grid_spec=pltpu.PrefetchScalarGridSpec(
            num_scalar_prefetch=1, grid=(R // tr,),
            in_specs=[pl.BlockSpec((1, tr, C), lambda i, d: (d[0], i, 0)),
                      pl.BlockSpec((NDEV - 1, tr, C), lambda i, d: (0, i, 0))],
            out_specs=pl.BlockSpec((tr, C), lambda i, d: (i, 0))),
        out_shape=jax.ShapeDtypeStruct((R, C), f32), compiler_params=_cp("parallel"))(dev, g, recv)


def _pack(arrs, dtype, rows_mult):
    flat = jnp.concatenate([a.astype(dtype).reshape(-1) for a in arrs])
    q = rows_mult * _PACK_COLS
    tot = -(-flat.shape[0] // q) * q
    return jnp.pad(flat, (0, tot - flat.shape[0])).reshape(tot // _PACK_COLS, _PACK_COLS)


def _unpack(flat, shapes):
    out, off = [], 0
    for s in shapes:
        n = math.prod(s)
        out.append(flat[..., off:off + n].reshape(flat.shape[:-1] + tuple(s)))
        off += n
    return out


def _width_groups(names, shapes):
    groups = {}
    for n in names:
        groups.setdefault(shapes[n][-1], []).append(n)
    return list(groups.values())


def _to_full(piece, ax):
    t = jnp.moveaxis(piece, 0, ax)
    return t.reshape(t.shape[:ax] + (t.shape[ax] * t.shape[ax + 1],) + t.shape[ax + 2:])


def _to_shards(g, ax):
    t = g.reshape(g.shape[:ax] + (NDEV, g.shape[ax] // NDEV) + g.shape[ax + 1:])
    return jnp.moveaxis(t, ax, 0)


def _split_rows(buf, shapes):
    out, off = [], 0
    for s in shapes:
        r = math.prod(s[:-1])
        out.append(buf[..., off:off + r, :].reshape(buf.shape[:-2] + tuple(s)))
        off += r
    return out


def _item_ax(item):
    return SHARD_AXIS[item[0]] - 1


def _gather_bufs(shards, items):
    shapes = {it: shards[it[0]].shape[1:] for it in items}
    groups = _width_groups(items, shapes)
    bufs = [jnp.concatenate([shards[n][l].astype(_MXU).reshape(-1, shapes[(n, l)][-1]) for n, l in grp], axis=0)
            for grp in groups]
    return groups, bufs


def _gather_unpack(shards, groups, outs):
    full = {}
    for grp, g in zip(groups, outs):
        for it, piece in zip(grp, _split_rows(g, [shards[it[0]].shape[1:] for it in grp])):
            full[it] = _to_full(piece, _item_ax(it))
    return full


def _scatter_bufs(grads, shards, items):
    shapes = {it: shards[it[0]].shape[1:] for it in items}
    groups = _width_groups(items, shapes)
    bufs = [jnp.concatenate([_to_shards(grads[it], _item_ax(it)).reshape(NDEV, -1, shapes[it][-1]) for it in grp], axis=1)
            for grp in groups]
    bufs = [jnp.pad(b, ((0, 0), (0, -b.shape[1] % 256), (0, 0))) for b in bufs]
    return groups, bufs


def _mesh_scalars():
    core = jnp.reshape(lax.axis_index("c"), (1,)).astype(jnp.int32)
    chip = jnp.reshape(2 * lax.axis_index("x") + lax.axis_index("y"), (1,)).astype(jnp.int32)
    return core, chip


def _chip_sums(bufs):
    core, _ = _mesh_scalars()
    return [_add_partial(b, r, core) for b, r in zip(bufs, _exchange_cores(bufs))]


def _device_sums(chip_sums, from_chips):
    _, chip = _mesh_scalars()
    return [_sum_final(p, r, chip) for (p, _), r in zip(chip_sums, from_chips)]


def _scatter_unpack(shards, groups, totals):
    out = {}
    for grp, t in zip(groups, totals):
        out.update(zip(grp, _split_rows(t, [shards[it[0]].shape[1:] for it in grp])))
    return out


def _small_scatter_buf(grads, names):
    small = jnp.concatenate([_to_shards(grads[n], SHARD_AXIS[n]).reshape(NDEV, -1) for n in names], axis=1)
    q = 16 * _PACK_COLS
    tot = -(-small.shape[1] // q) * q
    return jnp.pad(small, ((0, 0), (0, tot - small.shape[1]))).reshape(NDEV, tot // _PACK_COLS, _PACK_COLS)


def _all_reduce_small(grads, names):
    shapes = [grads[n].shape for n in names]
    packed = _pack([grads[n] for n in names], f32, 8)
    total = _sum_slabs(_all_gather([packed])[0])
    return dict(zip(names, _unpack(total.reshape(-1), shapes)))


def _as2d(a):
    if a.ndim == 1:
        return a.reshape(1, -1)
    return a.reshape(-1, a.shape[-1])


def kernel(x, mem, mix_norm_g, pool_w, pool_scale, sb_w_qkv, sb_w_o, s5_a_re, s5_a_im, s5_log_dt, s5_b_re, s5_b_im, s5_c_re, s5_c_im, s5_d, s5_w_glu, xa_norm_g, mem_norm_g, xa_wq, xa_wkv, xa_wo, ffn_norm_g, ffn_w_up, ffn_conv_w, ffn_conv_b, ffn_w_down, final_norm_g, loss_target, m_mix_norm_g, m_pool_w, m_pool_scale, m_sb_w_qkv, m_sb_w_o, m_s5_a_re, m_s5_a_im, m_s5_log_dt, m_s5_b_re, m_s5_b_im, m_s5_c_re, m_s5_c_im, m_s5_d, m_s5_w_glu, m_xa_norm_g, m_mem_norm_g, m_xa_wq, m_xa_wkv, m_xa_wo, m_ffn_norm_g, m_ffn_w_up, m_ffn_conv_w, m_ffn_conv_b, m_ffn_w_down, m_final_norm_g, v_mix_norm_g, v_pool_w, v_pool_scale, v_sb_w_qkv, v_sb_w_o, v_s5_a_re, v_s5_a_im, v_s5_log_dt, v_s5_b_re, v_s5_b_im, v_s5_c_re, v_s5_c_im, v_s5_d, v_s5_w_glu, v_xa_norm_g, v_mem_norm_g, v_xa_wq, v_xa_wkv, v_xa_wo, v_ffn_norm_g, v_ffn_w_up, v_ffn_conv_w, v_ffn_conv_b, v_ffn_w_down, v_final_norm_g):
    args = locals()
    w = {n: args[n] for n in WEIGHTS}
    mom = {n: args["m_" + n] for n in WEIGHTS}
    var = {n: args["v_" + n] for n in WEIGHTS}
    h0, memv, target = x[0], mem[0], loss_target[0]
    S, D = h0.shape
    depth = mix_norm_g.shape[0]
    n_mix = 3

    first_layer = {'pool_w': 0, 'sb_w_qkv': 1, 'sb_w_o': 1, 's5_w_glu': 2}

    def layer_of(item):
        return first_layer[item[0]] + n_mix * item[1] if item[0] in first_layer else item[1]

    sb_layer = 1 if depth > 1 else None
    items = [(n, l) for n in MXU_WEIGHTS for l in range(w[n].shape[0])]
    early_w = [it for it in items if sb_layer is None or layer_of(it) < sb_layer or it == ('sb_w_qkv', 0)]
    late_w = [it for it in items if it not in early_w]
    groups_a, bufs_a = _gather_bufs(w, early_w)
    outs_a = _all_gather(bufs_a + [_pack([w[n] for n in VEC_WEIGHTS], f32, 8)])
    wfull = _gather_unpack(w, groups_a, outs_a[:-1])
    full = dict(w)
    for n, piece in zip(VEC_WEIGHTS, _unpack(outs_a[-1].reshape(NDEV, -1), [w[n].shape for n in VEC_WEIGHTS])):
        full[n] = _to_full(piece, SHARD_AXIS[n])
    groups_b, bufs_b = _gather_bufs(w, late_w)

    grads = {}

    def acc(name, j, val):
        grads.setdefault(name, {})[j] = val

    s5 = []
    for j in range(s5_a_re.shape[0]):
        G, P = s5_a_re.shape[1:]
        cg = s5_b_re.shape[3]
        N = G * P
        nb = D // _LANES
        ar, ai = s5_a_re[j].reshape(1, N), s5_a_im[j].reshape(1, N)
        ldt = jnp.repeat(s5_log_dt[j], P).reshape(1, N)
        br = s5_b_re[j].transpose(2, 0, 1).reshape(cg, N)
        bi = s5_b_im[j].transpose(2, 0, 1).reshape(cg, N)
        abr, abi, bbr, bbi = _s5_prep_fwd(ar, ai, ldt, br, bi)
        s5.append(dict(ar=ar, ai=ai, ldt=ldt, br=br, bi=bi, abr=abr, abi=abi, G=G, P=P, cg=cg, N=N, nb=nb,
                       Br=_blockdiag_in(bbr, nb, P), Bi=_blockdiag_in(bbi, nb, P),
                       Cr=_blockdiag_out(s5_c_re[j], nb), Ci=_blockdiag_out(s5_c_im[j], nb)))

    saved = []
    h, delta = h0, None
    for i in range(depth):
        kind, j = i % n_mix, i // n_mix
        sv = dict(kind=kind, j=j)
        mix_dtype = _MXU if kind == 1 else f32
        if delta is None:
            hn = _rms_fwd(h, full['mix_norm_g'][i], out_dtype=mix_dtype)
        else:
            h, hn = _rms_fwd(h, full['mix_norm_g'][i], delta, out_dtype=mix_dtype)
        sv['h'] = h
        if kind == 0:
            p = _pool_fwd(hn)
            t = _pool_mix_fwd(p, wfull['pool_w', j], full['pool_scale'][j])
            sv.update(p=p)
        elif kind == 1:
            qkv = _mm(hn, wfull['sb_w_qkv', j], out_dtype=_MXU)
            if i == sb_layer:
                o, outs_b = _sb_fwd(qkv, bufs_b)
                wfull.update(_gather_unpack(w, groups_b, outs_b))
            else:
                o, _ = _sb_fwd(qkv)
            t = _mm(o, wfull['sb_w_o', j])
            sv.update(hn=hn, qkv=qkv, o=o)
        else:
            pr = s5[j]
            bur, bui = _s5_in_fwd(hn, pr['Br'], pr['Bi'])
            xr, xi = _s5_scan_fwd(bur, bui, pr['abr'], pr['abi'])
            y, z = _s5_out_fwd(xr, xi, pr['Cr'], pr['Ci'], hn, full['s5_d'][j])
            vg = _mm(z, wfull['s5_w_glu', j])
            t = _glu_fwd(vg)
            sv.update(hn=hn, xr=xr, xi=xi, y=y, z=z, vg=vg)
        h1, a = _rms_fwd(h, full['xa_norm_g'][i], t, out_dtype=_MXU)
        memn = _rms_fwd(memv, full['mem_norm_g'][i], out_dtype=_MXU)
        q = _mm(a, wfull['xa_wq', i], out_dtype=_MXU)
        kv = _mm(memn, wfull['xa_wkv', i], out_dtype=_MXU)
        o2 = _xa_fwd(q, kv)
        mo = _mm(o2, wfull['xa_wo', i])
        h2, b = _rms_fwd(h1, full['ffn_norm_g'][i], mo, out_dtype=_MXU)
        u = _mm(b, wfull['ffn_w_up', i])
        act = _convglu_fwd(u, full['ffn_conv_w'][i], full['ffn_conv_b'][i])
        delta = _mm(act, wfull['ffn_w_down', i])
        sv.update(h1=h1, a=a, memn=memn, q=q, kv=kv, o2=o2, h2=h2, b=b, u=u, act=act)
        saved.append(sv)
        h = h2

    loss_local, dh, dg_final = _final_loss(h, delta, full['final_norm_g'], target)
    loss = lax.psum(loss_local, ("x", "y", "c"))
    grads['final_norm_g'] = dg_final

    for i in reversed(range(depth)):
        sv = saved[i]
        kind, j = sv['kind'], sv['j']
        dact = _mm(dh, wfull['ffn_w_down', i], tb=True)
        acc('ffn_w_down', i, _mm(sv['act'], dh, ta=True))
        du, dcw, dcb = _convglu_bwd(sv['u'], full['ffn_conv_w'][i], full['ffn_conv_b'][i], dact)
        acc('ffn_conv_w', i, dcw)
        acc('ffn_conv_b', i, dcb)
        db = _mm(du, wfull['ffn_w_up', i], tb=True)
        acc('ffn_w_up', i, _mm(sv['b'], du, ta=True))
        dh2, dg = _rms_bwd(sv['h2'], full['ffn_norm_g'][i], db, dh)
        acc('ffn_norm_g', i, dg)
        do2 = _mm(dh2, wfull['xa_wo', i], tb=True, out_dtype=_MXU)
        acc('xa_wo', i, _mm(sv['o2'], dh2, ta=True))
        dq, dk, dv = _xa_bwd(sv['q'], sv['kv'], do2)
        dkv = jnp.concatenate([dk, dv], axis=1)
        da = _mm(dq, wfull['xa_wq', i], tb=True)
        acc('xa_wq', i, _mm(sv['a'], dq, ta=True))
        dmemn = _mm(dkv, wfull['xa_wkv', i], tb=True)
        acc('xa_wkv', i, _mm(sv['memn'], dkv, ta=True))
        _, dg = _rms_bwd(memv, full['mem_norm_g'][i], dmemn)
        acc('mem_norm_g', i, dg)
        dh1, dg = _rms_bwd(sv['h1'], full['xa_norm_g'][i], da, dh2)
        acc('xa_norm_g', i, dg)
        if kind == 0:
            dp, dw, ds = _pool_mix_bwd(sv['p'], wfull['pool_w', j], full['pool_scale'][j], dh1)
            acc('pool_w', j, dw)
            acc('pool_scale', j, ds)
            dhn = _pool_bwd(dp)
        elif kind == 1:
            do = _mm(dh1, wfull['sb_w_o', j], tb=True)
            acc('sb_w_o', j, _mm(sv['o'], dh1, ta=True))
            if i == sb_layer:
                early_g = [it for it in items if it != ('sb_w_qkv', j) and layer_of(it) >= sb_layer]
                groups_e, bufs_e = _scatter_bufs({it: grads[it[0]][it[1]] for it in early_g}, w, early_g)
                wire_e = [b.astype(_WIRE) for b in bufs_e]
                dqs, dks, dvs, recv_e = _sb_bwd(sv['qkv'], sv['o'], do, wire_e)
            else:
                dqs, dks, dvs, _ = _sb_bwd(sv['qkv'], sv['o'], do)
            dqkv = jnp.concatenate([dqs, dks, dvs], axis=1)
            dhn = _mm(dqkv, wfull['sb_w_qkv', j], tb=True)
            acc('sb_w_qkv', j, _mm(sv['hn'], dqkv, ta=True))
        else:
            pr = s5[j]
            dval, dgate = _glu_bwd(sv['vg'], dh1)
            dvg = jnp.concatenate([dval, dgate], axis=1)
            dz = _mm(dvg, wfull['s5_w_glu', j], tb=True)
            acc('s5_w_glu', j, _mm(sv['z'], dvg, ta=True))
            dy, gdr, gdi, du0, dd = _s5_out_bwd(dz, sv['y'], sv['hn'], full['s5_d'][j], pr['Cr'], pr['Ci'])
            acc('s5_d', j, dd)
            gr, gi, dabr, dabi = _s5_scan_bwd(gdr, gdi, sv['xr'], sv['xi'], pr['abr'], pr['abi'])
            dhn = _s5_in_bwd(gr, gi, pr['Br'], pr['Bi'], du0)
            dBr, dBi, dCr, dCi = _s5_wgrad(sv['hn'], gr, gi, sv['xr'], sv['xi'], dy)
            cg, P, G = pr['cg'], pr['P'], pr['G']
            acc('s5_c_re', j, _blockdiag_out_extract(dCr, cg, P))
            acc('s5_c_im', j, _blockdiag_out_extract(dCi, cg, P))
            dar, dai, dldt, dbr, dbi = _s5_prep_bwd(pr['ar'], pr['ai'], pr['ldt'], pr['br'], pr['bi'], dabr, dabi,
                                                    _blockdiag_in_extract(dBr, cg, P), _blockdiag_in_extract(dBi, cg, P))
            acc('s5_a_re', j, dar.reshape(G, P))
            acc('s5_a_im', j, dai.reshape(G, P))
            acc('s5_log_dt', j, dldt.reshape(G, P).sum(axis=1))
            acc('s5_b_re', j, dbr.reshape(cg, G, P).transpose(1, 2, 0))
            acc('s5_b_im', j, dbi.reshape(cg, G, P).transpose(1, 2, 0))
        dh, dg = _rms_bwd(sv['h'], full['mix_norm_g'][i], dhn, dh1)
        acc('mix_norm_g', i, dg)
    grad_x = dh[None]

    gfull = {}
    for n in VEC_WEIGHTS + REPLICATED:
        gfull[n] = grads[n] if n == 'final_norm_g' else jnp.stack([grads[n][k] for k in range(len(grads[n]))])
    if sb_layer is None:
        early_g, local = [], {}
    else:
        dev = jnp.reshape(4 * lax.axis_index("x") + 2 * lax.axis_index("y") + lax.axis_index("c"), (1,)).astype(jnp.int32)
        local = _scatter_unpack(w, groups_e, [_sum_all(g, r, dev) for g, r in zip(wire_e, recv_e)])
    late_g = [it for it in items if it not in early_g]
    groups_l, bufs_l = _scatter_bufs({it: grads[it[0]][it[1]] for it in late_g}, w, late_g)
    sums_l = _chip_sums(bufs_l + [_small_scatter_buf(gfull, VEC_WEIGHTS)])
    totals_l = _device_sums(sums_l, _exchange_chips([wire for _, wire in sums_l]))
    local.update(_scatter_unpack(w, groups_l, totals_l[:-1]))
    gw = {n: jnp.stack([local[(n, l)] for l in range(w[n].shape[0])]) for n in MXU_WEIGHTS}
    gw.update(zip(VEC_WEIGHTS, _unpack(totals_l[-1].reshape(-1), [w[n].shape for n in VEC_WEIGHTS])))
    gw.update(_all_reduce_small(gfull, REPLICATED))

    deltas, new_m, new_v = {}, {}, {}
    for n in WEIGHTS:
        shp = w[n].shape
        d_, m_, v_ = _adamw(_as2d(w[n]), _as2d(gw[n]), _as2d(mom[n]), _as2d(var[n]))
        deltas[n], new_m[n], new_v[n] = d_.reshape(shp), m_.reshape(shp), v_.reshape(shp)

    return (loss, grad_x, *[gw[n] for n in WEIGHTS], *[deltas[n] for n in WEIGHTS],
            *[new_m[n] for n in WEIGHTS], *[new_v[n] for n in WEIGHTS])
```

```python
import functools
import math

import jax
import jax.numpy as jnp
from jax import lax
from jax.experimental import pallas as pl
from jax.experimental.pallas import tpu as pltpu

f32 = jnp.float32
_MXU = jnp.bfloat16
_WIRE = _MXU
_VMEM_LIMIT = 48 * 1024 * 1024
_LANES = 128
_PACK_COLS = 1024

NDEV = 8
EPS = 1e-6
POOL_WINDOWS = (2, 4, 8, 16)
SB_HEAD_DIM = 64
XA_HEADS = 4
S5_GROUP = 16
S5_BLOCK_GROUPS = _LANES // S5_GROUP
ADAM_LR, ADAM_B1, ADAM_B2, ADAM_EPS, ADAM_WD, ADAM_STEP = 0.001, 0.9, 0.999, 1e-08, 0.01, 10

WEIGHTS = ['mix_norm_g', 'pool_w', 'pool_scale', 'sb_w_qkv', 'sb_w_o', 's5_a_re', 's5_a_im', 's5_log_dt',
           's5_b_re', 's5_b_im', 's5_c_re', 's5_c_im', 's5_d', 's5_w_glu', 'xa_norm_g', 'mem_norm_g', 'xa_wq',
           'xa_wkv', 'xa_wo', 'ffn_norm_g', 'ffn_w_up', 'ffn_conv_w', 'ffn_conv_b', 'ffn_w_down', 'final_norm_g']
SHARD_AXIS = {'pool_w': 2, 'pool_scale': 1, 'sb_w_qkv': 2, 'sb_w_o': 1, 's5_d': 1, 's5_w_glu': 2, 'xa_wq': 1,
              'xa_wkv': 2, 'xa_wo': 1, 'ffn_w_up': 2, 'ffn_conv_w': 2, 'ffn_w_down': 1}
MXU_WEIGHTS = ['pool_w', 'sb_w_qkv', 'sb_w_o', 's5_w_glu', 'xa_wq', 'xa_wkv', 'xa_wo', 'ffn_w_up', 'ffn_w_down']
VEC_WEIGHTS = ['pool_scale', 's5_d', 'ffn_conv_w']
REPLICATED = [n for n in WEIGHTS if n not in SHARD_AXIS]

_NN = (((1,), (0,)), ((), ()))
_NT = (((1,), (1,)), ((), ()))
_TN = (((0,), (0,)), ((), ()))
MESH_ID = pl.DeviceIdType.MESH
ANY = pl.BlockSpec(memory_space=pl.ANY)


def _call(body, **kw):
    return pl.pallas_call(body, **kw)


def _cp(*sem):
    return pltpu.CompilerParams(dimension_semantics=sem, vmem_limit_bytes=_VMEM_LIMIT)


def _tile(n, target, mult=_LANES):
    if n <= target:
        return n
    t = (target // mult) * mult
    while t >= mult:
        if n % t == 0:
            return t
        t -= mult
    return n


def _dot(a, b, dims=_NN):
    return lax.dot_general(a, b, dims, preferred_element_type=f32)


def _split(a):
    hi = a.astype(_MXU)
    lo = (a - hi.astype(f32)).astype(_MXU)
    return hi, lo


def _dot_hilo(a, u, dims=_NN):
    hi, lo = _split(a)
    return _dot(hi, u, dims) + _dot(lo, u, dims)


def _dot3(a, b, dims=_NN):
    ah, al = _split(a)
    bh, bl = _split(b)
    return _dot(ah, bh, dims) + _dot(al, bh, dims) + _dot(ah, bl, dims)


def _sigmoid(x):
    return 0.5 * jnp.tanh(0.5 * x) + 0.5


_GELU_C = math.sqrt(2.0 / math.pi)


def _gelu(x):
    return x * (0.5 * (1.0 + jnp.tanh(_GELU_C * (x + 0.044715 * (x * x * x)))))


def _gelu_grad(x):
    t = jnp.tanh(_GELU_C * (x + 0.044715 * (x * x * x)))
    return 0.5 * (1.0 + t) + x * 0.5 * (1.0 - t * t) * _GELU_C * (1.0 + 3.0 * 0.044715 * x * x)


_SUBLANES = 8


def _shift_down(x, k, rows=None):
    r = pltpu.roll(x, k, 0)
    head = jnp.where(lax.broadcasted_iota(jnp.int32, (_SUBLANES, 1), 0) >= k, r[:_SUBLANES], 0.0)
    return jnp.concatenate([head, r[_SUBLANES:]], axis=0)


def _shift_up(x, k, rows, n):
    r = pltpu.roll(x, n - k, 0)
    tail = jnp.where(lax.broadcasted_iota(jnp.int32, (_SUBLANES, 1), 0) < _SUBLANES - k, r[n - _SUBLANES:], 0.0)
    return jnp.concatenate([r[:n - _SUBLANES], tail], axis=0)


def _mm(a, b, *, ta=False, tb=False, out_dtype=f32):
    M, K = (a.shape[1], a.shape[0]) if ta else a.shape
    N = b.shape[0] if tb else b.shape[1]
    tm, tn, tk = _tile(M, 1408), _tile(N, 1536), _tile(K, 1408)
    nk = K // tk
    a_spec = pl.BlockSpec((tk, tm), lambda i, j, k: (k, i)) if ta else pl.BlockSpec((tm, tk), lambda i, j, k: (i, k))
    b_spec = pl.BlockSpec((tn, tk), lambda i, j, k: (j, k)) if tb else pl.BlockSpec((tk, tn), lambda i, j, k: (k, j))
    dims = (((0 if ta else 1,), (1 if tb else 0,)), ((), ()))

    def body(a_ref, b_ref, o_ref, acc_ref):
        k = pl.program_id(2)

        @pl.when(k == 0)
        def _():
            acc_ref[...] = jnp.zeros_like(acc_ref)

        acc_ref[...] += _dot(a_ref[...].astype(_MXU), b_ref[...].astype(_MXU), dims)

        @pl.when(k == nk - 1)
        def _():
            o_ref[...] = acc_ref[...].astype(out_dtype)

    return _call(
        body, name=f"mm_{'t' if ta else 'n'}{'t' if tb else 'n'}_{M}x{K}x{N}",
        grid=(M // tm, N // tn, nk), in_specs=[a_spec, b_spec],
        out_specs=pl.BlockSpec((tm, tn), lambda i, j, k: (i, j)),
        out_shape=jax.ShapeDtypeStruct((M, N), out_dtype),
        scratch_shapes=[pltpu.VMEM((tm, tn), f32)],
        compiler_params=_cp("parallel", "parallel", "arbitrary"))(a, b)


def _rms_fwd(x, g, delta=None, *, out_dtype=f32):
    S, D = x.shape
    ts = _tile(S, 512, 8)
    row = pl.BlockSpec((ts, D), lambda i: (i, 0))
    vec = pl.BlockSpec((1, D), lambda i: (0, 0))
    g2 = g.reshape(1, D)

    def norm(xv, g_ref):
        r = lax.rsqrt(jnp.mean(xv * xv, axis=-1, keepdims=True) + EPS)
        return ((xv * r) * g_ref[...]).astype(out_dtype)

    if delta is None:
        def body(x_ref, g_ref, y_ref):
            y_ref[...] = norm(x_ref[...], g_ref)

        return _call(body, name=f"rms_fwd_{S}", grid=(S // ts,), in_specs=[row, vec], out_specs=row,
                     out_shape=jax.ShapeDtypeStruct((S, D), out_dtype), compiler_params=_cp("parallel"))(x, g2)

    def body(x_ref, d_ref, g_ref, s_ref, y_ref):
        xv = x_ref[...] + d_ref[...]
        s_ref[...] = xv
        y_ref[...] = norm(xv, g_ref)

    return _call(body, name=f"add_rms_fwd_{S}", grid=(S // ts,), in_specs=[row, row, vec], out_specs=[row, row],
                 out_shape=[jax.ShapeDtypeStruct((S, D), f32), jax.ShapeDtypeStruct((S, D), out_dtype)],
                 compiler_params=_cp("parallel"))(x, delta, g2)


def _rms_bwd(x, g, dy, dres=None):
    S, D = x.shape
    ts = _tile(S, 512, 8)
    row = pl.BlockSpec((ts, D), lambda i: (i, 0))
    vec = pl.BlockSpec((1, D), lambda i: (0, 0))
    has_res = dres is not None

    def body(*refs):
        if has_res:
            x_ref, g_ref, dy_ref, dr_ref, dx_ref, dg_ref = refs
        else:
            x_ref, g_ref, dy_ref, dx_ref, dg_ref = refs
        xv = x_ref[...]
        r = lax.rsqrt(jnp.mean(xv * xv, axis=-1, keepdims=True) + EPS)
        xh = xv * r
        dyv = dy_ref[...].astype(f32)

        @pl.when(pl.program_id(0) == 0)
        def _():
            dg_ref[...] = jnp.zeros_like(dg_ref)

        dg_ref[...] += jnp.sum(dyv * xh, axis=0, keepdims=True)
        dxh = dyv * g_ref[...]
        dx = r * (dxh - xh * jnp.mean(dxh * xh, axis=-1, keepdims=True))
        if has_res:
            dx = dx + dr_ref[...]
        dx_ref[...] = dx

    ins = [x, g.reshape(1, D), dy] + ([dres] if has_res else [])
    dx, dg = _call(body, name=f"rms_bwd_{S}_{int(has_res)}", grid=(S // ts,),
                   in_specs=[row, vec, row] + ([row] if has_res else []), out_specs=[row, vec],
                   out_shape=[jax.ShapeDtypeStruct((S, D), f32), jax.ShapeDtypeStruct((1, D), f32)],
                   compiler_params=_cp("arbitrary"))(*ins)
    return dx, dg.reshape(D)


def _pool_windows_sum(x, win, rows):
    s, k = x, 1
    while k < win:
        s = s + _shift_down(s, k, rows)
        k *= 2
    return s


def _pool_windows_sum_up(x, win, rows, n):
    s, k = x, 1
    while k < win:
        s = s + _shift_up(s, k, rows, n)
        k *= 2
    return s


def _pool_fwd(hn):
    S, D = hn.shape
    cg = D // len(POOL_WINDOWS)
    tc = _tile(cg, 128)
    nb = cg // tc
    blk = pl.BlockSpec((S, tc), lambda gi, j: (0, gi * nb + j))

    def body(x_ref, p_ref):
        gi = pl.program_id(0)
        rows = lax.broadcasted_iota(jnp.int32, (S, 1), 0)
        cnt = (rows + 1).astype(f32)
        for k, win in enumerate(POOL_WINDOWS):
            @pl.when(gi == k)
            def _(win=win):
                x = x_ref[...]
                s = _pool_windows_sum(x, win, rows)
                p_ref[...] = (s / jnp.minimum(cnt, float(win)) - x).astype(p_ref.dtype)

    return _call(body, name=f"pool_fwd_{S}", grid=(len(POOL_WINDOWS), nb), in_specs=[blk], out_specs=blk,
                 out_shape=jax.ShapeDtypeStruct((S, D), _MXU), compiler_params=_cp("parallel", "parallel"))(hn)


def _pool_bwd(dp):
    S, D = dp.shape
    cg = D // len(POOL_WINDOWS)
    tc = _tile(cg, 128)
    nb = cg // tc
    blk = pl.BlockSpec((S, tc), lambda gi, j: (0, gi * nb + j))

    def body(dp_ref, dx_ref):
        gi = pl.program_id(0)
        rows = lax.broadcasted_iota(jnp.int32, (S, 1), 0)
        cnt = (rows + 1).astype(f32)
        for k, win in enumerate(POOL_WINDOWS):
            @pl.when(gi == k)
            def _(win=win):
                d = dp_ref[...]
                e = d / jnp.minimum(cnt, float(win))
                dx_ref[...] = _pool_windows_sum_up(e, win, rows, S) - d

    return _call(body, name=f"pool_bwd_{S}", grid=(len(POOL_WINDOWS), nb), in_specs=[blk], out_specs=blk,
                 out_shape=jax.ShapeDtypeStruct((S, D), f32), compiler_params=_cp("parallel", "parallel"))(dp)


def _pool_mix_fwd(p, w, scale):
    S, D = p.shape
    G, cg, _ = w.shape
    ts = _tile(S, 1024, 8)

    def body(p_ref, w_ref, s_ref, y_ref):
        y_ref[...] = _dot(p_ref[...], w_ref[0]) * s_ref[...]

    return _call(body, name=f"pool_mix_fwd_{S}", grid=(S // ts, G),
                 in_specs=[pl.BlockSpec((ts, cg), lambda i, g: (i, g)), pl.BlockSpec((1, cg, cg), lambda i, g: (g, 0, 0)),
                           pl.BlockSpec((1, cg), lambda i, g: (0, g))],
                 out_specs=pl.BlockSpec((ts, cg), lambda i, g: (i, g)),
                 out_shape=jax.ShapeDtypeStruct((S, D), f32), compiler_params=_cp("parallel", "parallel"))(
                     p, w, scale.reshape(1, D))


def _pool_mix_bwd(p, w, scale, dy):
    S, D = p.shape
    G, cg, _ = w.shape
    ts = _tile(S, 1024, 8)

    def body(p_ref, w_ref, s_ref, dy_ref, dp_ref, dw_ref, ds_ref):
        @pl.when(pl.program_id(1) == 0)
        def _():
            dw_ref[...] = jnp.zeros_like(dw_ref)
            ds_ref[...] = jnp.zeros_like(ds_ref)

        pv, wv, dyv = p_ref[...], w_ref[0], dy_ref[...]
        ypre = _dot(pv, wv)
        ds_ref[...] += jnp.sum(dyv * ypre, axis=0, keepdims=True)
        dyp = (dyv * s_ref[...]).astype(_MXU)
        dp_ref[...] = _dot(dyp, wv, _NT)
        dw_ref[0] += _dot(pv, dyp, _TN)

    dp, dw, ds = _call(
        body, name=f"pool_mix_bwd_{S}", grid=(G, S // ts),
        in_specs=[pl.BlockSpec((ts, cg), lambda g, i: (i, g)), pl.BlockSpec((1, cg, cg), lambda g, i: (g, 0, 0)),
                  pl.BlockSpec((1, cg), lambda g, i: (0, g)), pl.BlockSpec((ts, cg), lambda g, i: (i, g))],
        out_specs=[pl.BlockSpec((ts, cg), lambda g, i: (i, g)), pl.BlockSpec((1, cg, cg), lambda g, i: (g, 0, 0)),
                   pl.BlockSpec((1, cg), lambda g, i: (0, g))],
        out_shape=[jax.ShapeDtypeStruct((S, D), f32), jax.ShapeDtypeStruct((G, cg, cg), f32),
                   jax.ShapeDtypeStruct((1, D), f32)],
        compiler_params=_cp("parallel", "arbitrary"))(p, w, scale.reshape(1, D), dy)
    return dp, dw, ds.reshape(D)


def _sb_tile(S):
    return min(256, max(128, S // 4))


_LOG2E = math.log2(math.e)


def _sb_scores(qs, ks, tri, R, U):
    z = _dot(qs, ks, _NT)
    sp = jnp.maximum(z, 0.0) + jnp.log(1.0 + jnp.exp2(jnp.abs(z) * -_LOG2E))
    lb = z - sp
    if tri is not None:
        sp = jnp.where(tri, sp, 0.0)
    c = _dot(sp.astype(_MXU), U)
    a = jnp.exp(lb - c - R)
    if tri is not None:
        a = jnp.where(tri, a, 0.0)
    return sp, lb, a


def _sb_sweep(tile, i, carry):
    def two(p, c):
        kb = i - 1 - 2 * p
        return tile(kb - 1, tile(kb, c, None), None)

    carry = lax.fori_loop(0, i // 2, two, carry)
    return lax.cond(i % 2 == 1, lambda c: tile(i * 0, c, None), lambda c: c, carry)


def _sb_consts(T):
    lane = lax.broadcasted_iota(jnp.int32, (1, _LANES), 1)
    row, col = lax.broadcasted_iota(jnp.int32, (T, T), 0), lax.broadcasted_iota(jnp.int32, (T, T), 1)
    U = jnp.where(row > col, 1.0, 0.0).astype(_MXU)
    heads = [(lane >= SB_HEAD_DIM * h) & (lane < SB_HEAD_DIM * (h + 1)) for h in range(_LANES // SB_HEAD_DIM)]
    return heads, col < row, U


def _sb_fwd(qkv, gather=()):
    S, D3 = qkv.shape
    D = D3 // 3
    HP = D // _LANES
    T = _sb_tile(S)
    nq = S // T
    n = len(gather)
    scale = SB_HEAD_DIM ** -0.5

    def body(*refs):
        q_ref, k_ref, v_ref = refs[:3]
        o_ref = refs[3 + n]
        i = pl.program_id(1)
        if n:
            step = pl.program_id(0) * nq + i
            start, forward, finish = _all_gather_stages(refs[3:3 + n], refs[4 + n:4 + 2 * n], *refs[4 + 2 * n:])
            pl.when(step == 0)(start)
            pl.when(step == (HP // 2) * nq)(forward)
        q = q_ref[...] * scale
        heads, tri, U = _sb_consts(T)
        qms = [jnp.where(hm, q, jnp.zeros_like(q)) for hm in heads]

        def tile(kb, carry, mask):
            off = pl.multiple_of(kb * T, T)
            ks, vs = k_ref[pl.ds(off, T), :], v_ref[pl.ds(off, T), :]
            new = []
            for h, qm in enumerate(qms):
                R, acc = carry[2 * h], carry[2 * h + 1]
                sp, _, a = _sb_scores(qm, ks, mask, R, U)
                new += [R + jnp.sum(sp, axis=1, keepdims=True), acc + _dot(a.astype(_MXU), vs)]
            return tuple(new)

        zero = (jnp.zeros((T, 1), f32), jnp.zeros((T, _LANES), f32)) * len(heads)
        carry = _sb_sweep(tile, i, tile(i, zero, tri))
        out = jnp.zeros((T, _LANES), f32)
        for h, hm in enumerate(heads):
            out = out + jnp.where(hm, carry[2 * h + 1], 0.0)
        o_ref[...] = out
        if n:
            pl.when(step == HP * nq - 1)(finish)

    outs = _call(
        body, name=f"sb_fwd_{S}_{n}", grid=(HP, nq),
        in_specs=[pl.BlockSpec((T, _LANES), lambda hp, i: (i, hp)), pl.BlockSpec((S, _LANES), lambda hp, i: (0, HP + hp)),
                  pl.BlockSpec((S, _LANES), lambda hp, i: (0, 2 * HP + hp))] + [ANY] * n,
        out_specs=[pl.BlockSpec((T, _LANES), lambda hp, i: (i, hp))] + [ANY] * n,
        out_shape=[jax.ShapeDtypeStruct((S, D), f32)] + _all_gather_shapes(gather),
        scratch_shapes=_all_gather_sems(n) if n else [],
        compiler_params=_cp("arbitrary", "arbitrary"))(qkv, qkv, qkv, *gather)
    return outs[0], list(outs[1:])


def _sb_bwd(qkv, o, do, send=()):
    S, D3 = qkv.shape
    D = D3 // 3
    HP = D // _LANES
    T = _sb_tile(S)
    nq = S // T
    n = len(send)
    scale = SB_HEAD_DIM ** -0.5

    def body(*refs):
        q_ref, k_ref, v_ref, o_ref, do_ref = refs[:5]
        dq_ref, dk_ref, dv_ref = refs[5 + n:8 + n]
        dk_acc, dv_acc = refs[8 + 2 * n:10 + 2 * n]
        i = pl.program_id(1)
        if n:
            step = pl.program_id(0) * nq + i
            start, finish = _exchange_all_stages(refs[5:5 + n], refs[8 + n:8 + 2 * n], *refs[10 + 2 * n:])
            pl.when(step == 0)(start)

        @pl.when(i == 0)
        def _():
            dk_acc[...] = jnp.zeros_like(dk_acc)
            dv_acc[...] = jnp.zeros_like(dv_acc)

        q = q_ref[...] * scale
        dob = do_ref[...].astype(_MXU)
        prod = dob.astype(f32) * o_ref[...]
        heads, tri, U = _sb_consts(T)
        qms = [jnp.where(hm, q, jnp.zeros_like(q)) for hm in heads]
        doms = [jnp.where(hm, dob, jnp.zeros_like(dob)) for hm in heads]
        totals = [jnp.sum(jnp.where(hm, prod, 0.0), axis=1, keepdims=True) for hm in heads]

        def tile(kb, carry, mask):
            off = pl.multiple_of(kb * T, T)
            ks, vs = k_ref[pl.ds(off, T), :], v_ref[pl.ds(off, T), :]
            new = []
            dk_t = jnp.zeros((T, _LANES), f32)
            dv_t = jnp.zeros((T, _LANES), f32)
            for h, (qm, dom, total) in enumerate(zip(qms, doms, totals)):
                R, Gs, dq = carry[3 * h:3 * h + 3]
                sp, lb, a = _sb_scores(qm, ks, mask, R, U)
                ab = a.astype(_MXU)
                g = ab.astype(f32) * _dot(dom, vs, _NT)
                before = total - (g + _dot_hilo(g, U) + Gs)
                beta = jnp.exp(lb)
                dz = g - (g + before) * beta
                if mask is not None:
                    dz = jnp.where(mask, dz, 0.0)
                dzb = dz.astype(_MXU)
                dk_t = dk_t + _dot(dzb, qm, _TN)
                dv_t = dv_t + _dot(ab, dom, _TN)
                new += [R + jnp.sum(sp, axis=1, keepdims=True), Gs + jnp.sum(g, axis=1, keepdims=True), dq + _dot(dzb, ks)]
            dk_acc[pl.ds(off, T), :] += dk_t
            dv_acc[pl.ds(off, T), :] += dv_t
            return tuple(new)

        zero1 = jnp.zeros((T, 1), f32)
        carry = _sb_sweep(tile, i, tile(i, (zero1, zero1, jnp.zeros((T, _LANES), f32)) * len(heads), tri))
        dq_out = jnp.zeros((T, _LANES), f32)
        for h, hm in enumerate(heads):
            dq_out = dq_out + jnp.where(hm, carry[3 * h + 2], 0.0)
        dq_ref[...] = (dq_out * scale).astype(dq_ref.dtype)

        @pl.when(i == nq - 1)
        def _():
            dk_ref[...] = dk_acc[...].astype(dk_ref.dtype)
            dv_ref[...] = dv_acc[...].astype(dv_ref.dtype)

        if n:
            pl.when(step == HP * nq - 1)(finish)

    qb = pl.BlockSpec((T, _LANES), lambda hp, i: (i, hp))
    col = pl.BlockSpec((S, _LANES), lambda hp, i: (0, hp))
    out = jax.ShapeDtypeStruct((S, D), _MXU)
    outs = _call(
        body, name=f"sb_bwd_{S}_{n}", grid=(HP, nq),
        in_specs=[qb, pl.BlockSpec((S, _LANES), lambda hp, i: (0, HP + hp)),
                  pl.BlockSpec((S, _LANES), lambda hp, i: (0, 2 * HP + hp)), qb, qb] + [ANY] * n,
        out_specs=[qb, col, col] + [ANY] * n, out_shape=[out, out, out] + _exchange_all_shapes(send),
        scratch_shapes=[pltpu.VMEM((S, _LANES), f32), pltpu.VMEM((S, _LANES), f32)] + (_exchange_all_sems(n) if n else []),
        compiler_params=_cp("arbitrary", "arbitrary"))(qkv, qkv, qkv, o, do, *send)
    return outs[0], outs[1], outs[2], list(outs[3:])


def _cmul(ar, ai, br, bi):
    return ar * br - ai * bi, ar * bi + ai * br


def _s5_coef(ar, ai, ldt):
    dt = jnp.exp(ldt)
    e = jnp.exp(ar * dt)
    abr, abi = e * jnp.cos(ai * dt), e * jnp.sin(ai * dt)
    inv = 1.0 / (ar * ar + ai * ai)
    cr, ci = _cmul(abr - 1.0, abi, ar * inv, -ai * inv)
    return dt, abr, abi, inv, cr, ci


def _s5_prep_fwd(ar, ai, ldt, br, bi):
    N = ar.shape[1]
    cg = br.shape[0]

    def body(ar_ref, ai_ref, ldt_ref, br_ref, bi_ref, abr_ref, abi_ref, bbr_ref, bbi_ref):
        _, abr, abi, _, cr, ci = _s5_coef(ar_ref[...], ai_ref[...], ldt_ref[...])
        abr_ref[...], abi_ref[...] = abr, abi
        bbr_ref[...], bbi_ref[...] = _cmul(cr, ci, br_ref[...], bi_ref[...])

    v, m = jax.ShapeDtypeStruct((1, N), f32), jax.ShapeDtypeStruct((cg, N), f32)
    return _call(body, name="s5_prep_fwd", out_shape=[v, v, m, m])(ar, ai, ldt, br, bi)


def _s5_prep_bwd(ar, ai, ldt, br, bi, dabr, dabi, dbbr, dbbi):
    N = ar.shape[1]
    cg = br.shape[0]

    def body(ar_ref, ai_ref, ldt_ref, br_ref, bi_ref, dabr_ref, dabi_ref, dbbr_ref, dbbi_ref,
             dar_ref, dai_ref, dldt_ref, dbr_ref, dbi_ref):
        a_r, a_i = ar_ref[...], ai_ref[...]
        dt, abr, abi, inv, cr, ci = _s5_coef(a_r, a_i, ldt_ref[...])
        b_r, b_i, gr, gi = br_ref[...], bi_ref[...], dbbr_ref[...], dbbi_ref[...]
        dbr_ref[...], dbi_ref[...] = _cmul(cr, -ci, gr, gi)
        dcr = jnp.sum(gr * b_r + gi * b_i, axis=0, keepdims=True)
        dci = jnp.sum(gi * b_r - gr * b_i, axis=0, keepdims=True)
        ilr, ili = a_r * inv, -a_i * inv
        dwr, dwi = _cmul(ilr, -ili, dcr, dci)
        qr, qi = _cmul(cr, ci, ilr, ili)
        dl1r, dl1i = _cmul(-qr, qi, dcr, dci)
        tr, ti = dabr_ref[...] + dwr, dabi_ref[...] + dwi
        ddlr, ddli = _cmul(abr, -abi, tr, ti)
        dar_ref[...] = dl1r + ddlr * dt
        dai_ref[...] = dl1i + ddli * dt
        dldt_ref[...] = (a_r * ddlr + a_i * ddli) * dt

    v, m = jax.ShapeDtypeStruct((1, N), f32), jax.ShapeDtypeStruct((cg, N), f32)
    return _call(body, name="s5_prep_bwd", out_shape=[v, v, v, m, m])(ar, ai, ldt, br, bi, dabr, dabi, dbbr, dbbi)


def _s5_in_fwd(u, Br, Bi):
    S, D = u.shape
    nb, _, nw = Br.shape
    ts = _tile(S, 512, 8)

    def body(u_ref, br_ref, bi_ref, or_ref, oi_ref):
        uv = u_ref[...]
        or_ref[...] = _dot3(uv, br_ref[0])
        oi_ref[...] = _dot3(uv, bi_ref[0])

    ub = pl.BlockSpec((ts, _LANES), lambda i, k: (i, k))
    wb = pl.BlockSpec((1, _LANES, nw), lambda i, k: (k, 0, 0))
    ob = pl.BlockSpec((ts, nw), lambda i, k: (i, k))
    o = jax.ShapeDtypeStruct((S, nb * nw), f32)
    return _call(body, name=f"s5_in_fwd_{S}", grid=(S // ts, nb), in_specs=[ub, wb, wb], out_specs=[ob, ob],
                 out_shape=[o, o], compiler_params=_cp("parallel", "parallel"))(u, Br, Bi)


def _s5_scan_fwd(bur, bui, abr, abi):
    S, N = bur.shape
    tt, tn = _tile(S, 128, 8), _tile(N, 4096)

    def body(br_ref, bi_ref, ar_ref, ai_ref, xr_ref, xi_ref, sr, si):
        @pl.when(pl.program_id(1) == 0)
        def _():
            sr[...] = jnp.zeros_like(sr)
            si[...] = jnp.zeros_like(si)

        a_r, a_i = ar_ref[...], ai_ref[...]

        def step(t, carry):
            xr, xi = carry
            nr = a_r * xr - a_i * xi + br_ref[pl.ds(t, 1), :]
            ni = a_r * xi + a_i * xr + bi_ref[pl.ds(t, 1), :]
            xr_ref[pl.ds(t, 1), :] = nr
            xi_ref[pl.ds(t, 1), :] = ni
            return nr, ni

        xr, xi = lax.fori_loop(0, tt, step, (sr[...], si[...]))
        sr[...], si[...] = xr, xi

    blk = pl.BlockSpec((tt, tn), lambda n, i: (i, n))
    vec = pl.BlockSpec((1, tn), lambda n, i: (0, n))
    o = jax.ShapeDtypeStruct((S, N), f32)
    return _call(body, name=f"s5_scan_fwd_{S}", grid=(N // tn, S // tt), in_specs=[blk, blk, vec, vec],
                 out_specs=[blk, blk], out_shape=[o, o],
                 scratch_shapes=[pltpu.VMEM((1, tn), f32), pltpu.VMEM((1, tn), f32)],
                 compiler_params=_cp("parallel", "arbitrary"))(bur, bui, abr, abi)


def _s5_scan_bwd(dr, di, xr, xi, abr, abi):
    S, N = dr.shape
    tt, tn = _tile(S, 128, 8), _tile(N, 4096)
    nt = S // tt

    def body(dr_ref, di_ref, xr_ref, xi_ref, ar_ref, ai_ref, gr_ref, gi_ref, dar_ref, dai_ref, sr, si):
        @pl.when(pl.program_id(1) == 0)
        def _():
            sr[...] = jnp.zeros_like(sr)
            si[...] = jnp.zeros_like(si)
            dar_ref[...] = jnp.zeros_like(dar_ref)
            dai_ref[...] = jnp.zeros_like(dai_ref)

        a_r, a_i = ar_ref[...], ai_ref[...]

        def step(j, carry):
            gr, gi, accr, acci = carry
            t = tt - 1 - j
            xr_t, xi_t = xr_ref[pl.ds(t, 1), :], xi_ref[pl.ds(t, 1), :]
            accr = accr + gr * xr_t + gi * xi_t
            acci = acci + gi * xr_t - gr * xi_t
            nr = dr_ref[pl.ds(t, 1), :] + a_r * gr + a_i * gi
            ni = di_ref[pl.ds(t, 1), :] + a_r * gi - a_i * gr
            gr_ref[pl.ds(t, 1), :] = nr
            gi_ref[pl.ds(t, 1), :] = ni
            return nr, ni, accr, acci

        gr, gi, accr, acci = lax.fori_loop(0, tt, step, (sr[...], si[...], dar_ref[...], dai_ref[...]))
        sr[...], si[...] = gr, gi
        dar_ref[...], dai_ref[...] = accr, acci

    blk = pl.BlockSpec((tt, tn), lambda n, i: (nt - 1 - i, n))
    vec = pl.BlockSpec((1, tn), lambda n, i: (0, n))
    o, v = jax.ShapeDtypeStruct((S, N), f32), jax.ShapeDtypeStruct((1, N), f32)
    return _call(body, name=f"s5_scan_bwd_{S}", grid=(N // tn, nt), in_specs=[blk, blk, blk, blk, vec, vec],
                 out_specs=[blk, blk, vec, vec], out_shape=[o, o, v, v],
                 scratch_shapes=[pltpu.VMEM((1, tn), f32), pltpu.VMEM((1, tn), f32)],
                 compiler_params=_cp("parallel", "arbitrary"))(dr, di, xr, xi, abr, abi)


def _s5_out_fwd(xr, xi, Cr, Ci, u, d):
    S, N = xr.shape
    nb, nw, _ = Cr.shape
    D = u.shape[1]
    ts = _tile(S, 512, 8)

    def body(xr_ref, xi_ref, cr_ref, ci_ref, u_ref, d_ref, y_ref, z_ref):
        y = _dot3(xr_ref[...], cr_ref[0]) - _dot3(xi_ref[...], ci_ref[0]) + d_ref[...] * u_ref[...]
        y_ref[...] = y
        z_ref[...] = _gelu(y).astype(z_ref.dtype)

    xb = pl.BlockSpec((ts, nw), lambda i, k: (i, k))
    cb = pl.BlockSpec((1, nw, _LANES), lambda i, k: (k, 0, 0))
    ub = pl.BlockSpec((ts, _LANES), lambda i, k: (i, k))
    db = pl.BlockSpec((1, _LANES), lambda i, k: (0, k))
    return _call(body, name=f"s5_out_fwd_{S}", grid=(S // ts, nb), in_specs=[xb, xb, cb, cb, ub, db],
                 out_specs=[ub, ub], out_shape=[jax.ShapeDtypeStruct((S, D), f32), jax.ShapeDtypeStruct((S, D), _MXU)],
                 compiler_params=_cp("parallel", "parallel"))(xr, xi, Cr, Ci, u, d.reshape(1, D))


def _s5_out_bwd(dz, y, u, d, Cr, Ci):
    S, D = y.shape
    nb, nw, _ = Cr.shape
    ts = _tile(S, 512, 8)

    def body(dz_ref, y_ref, u_ref, d_ref, cr_ref, ci_ref, dy_ref, gr_ref, gi_ref, du_ref, dd_ref):
        @pl.when(pl.program_id(1) == 0)
        def _():
            dd_ref[...] = jnp.zeros_like(dd_ref)

        dy = dz_ref[...] * _gelu_grad(y_ref[...])
        dy_ref[...] = dy
        gr_ref[...] = _dot3(dy, cr_ref[0], _NT)
        gi_ref[...] = -_dot3(dy, ci_ref[0], _NT)
        du_ref[...] = dy * d_ref[...]
        dd_ref[...] += jnp.sum(dy * u_ref[...], axis=0, keepdims=True)

    xb = pl.BlockSpec((ts, nw), lambda k, i: (i, k))
    cb = pl.BlockSpec((1, nw, _LANES), lambda k, i: (k, 0, 0))
    ub = pl.BlockSpec((ts, _LANES), lambda k, i: (i, k))
    db = pl.BlockSpec((1, _LANES), lambda k, i: (0, k))
    a, s = jax.ShapeDtypeStruct((S, D), f32), jax.ShapeDtypeStruct((S, nb * nw), f32)
    dy, gr, gi, du, dd = _call(
        body, name=f"s5_out_bwd_{S}", grid=(nb, S // ts), in_specs=[ub, ub, ub, db, cb, cb],
        out_specs=[ub, xb, xb, ub, db], out_shape=[a, s, s, a, jax.ShapeDtypeStruct((1, D), f32)],
        compiler_params=_cp("parallel", "arbitrary"))(dz, y, u, d.reshape(1, D), Cr, Ci)
    return dy, gr, gi, du, dd.reshape(D)


def _s5_in_bwd(gr, gi, Br, Bi, du0):
    S, N = gr.shape
    nb, _, nw = Br.shape
    ts = _tile(S, 512, 8)

    def body(gr_ref, gi_ref, br_ref, bi_ref, d0_ref, du_ref):
        du_ref[...] = d0_ref[...] + _dot3(gr_ref[...], br_ref[0], _NT) + _dot3(gi_ref[...], bi_ref[0], _NT)

    xb = pl.BlockSpec((ts, nw), lambda i, k: (i, k))
    wb = pl.BlockSpec((1, _LANES, nw), lambda i, k: (k, 0, 0))
    ub = pl.BlockSpec((ts, _LANES), lambda i, k: (i, k))
    return _call(body, name=f"s5_in_bwd_{S}", grid=(S // ts, nb), in_specs=[xb, xb, wb, wb, ub], out_specs=ub,
                 out_shape=jax.ShapeDtypeStruct((S, nb * _LANES), f32), compiler_params=_cp("parallel", "parallel"))(
                     gr, gi, Br, Bi, du0)


def _s5_wgrad(u, gr, gi, xr, xi, dy):
    S, D = u.shape
    nb = D // _LANES
    nw = gr.shape[1] // nb
    ts = _tile(S, 512, 8)

    def body(u_ref, gr_ref, gi_ref, xr_ref, xi_ref, dy_ref, dbr_ref, dbi_ref, dcr_ref, dci_ref):
        @pl.when(pl.program_id(1) == 0)
        def _():
            for r in (dbr_ref, dbi_ref, dcr_ref, dci_ref):
                r[...] = jnp.zeros_like(r)

        uv, dyv = u_ref[...], dy_ref[...]
        dbr_ref[0] += _dot3(uv, gr_ref[...], _TN)
        dbi_ref[0] += _dot3(uv, gi_ref[...], _TN)
        dcr_ref[0] += _dot3(xr_ref[...], dyv, _TN)
        dci_ref[0] -= _dot3(xi_ref[...], dyv, _TN)

    xb = pl.BlockSpec((ts, nw), lambda k, i: (i, k))
    ub = pl.BlockSpec((ts, _LANES), lambda k, i: (i, k))
    wb = pl.BlockSpec((1, _LANES, nw), lambda k, i: (k, 0, 0))
    cb = pl.BlockSpec((1, nw, _LANES), lambda k, i: (k, 0, 0))
    w, c = jax.ShapeDtypeStruct((nb, _LANES, nw), f32), jax.ShapeDtypeStruct((nb, nw, _LANES), f32)
    return _call(body, name=f"s5_wgrad_{S}", grid=(nb, S // ts), in_specs=[ub, xb, xb, xb, xb, ub],
                 out_specs=[wb, wb, cb, cb], out_shape=[w, w, c, c],
                 compiler_params=_cp("parallel", "arbitrary"))(u, gr, gi, xr, xi, dy)


def _glu_fwd(vg):
    S, D2 = vg.shape
    D = D2 // 2
    ts, tc = _tile(S, 1024, 8), _tile(D, 512)
    nc = D // tc

    def body(v_ref, g_ref, o_ref):
        o_ref[...] = v_ref[...] * _sigmoid(g_ref[...])

    return _call(body, name=f"glu_fwd_{S}", grid=(S // ts, nc),
                 in_specs=[pl.BlockSpec((ts, tc), lambda i, j: (i, j)), pl.BlockSpec((ts, tc), lambda i, j: (i, nc + j))],
                 out_specs=pl.BlockSpec((ts, tc), lambda i, j: (i, j)), out_shape=jax.ShapeDtypeStruct((S, D), f32),
                 compiler_params=_cp("parallel", "parallel"))(vg, vg)


def _glu_bwd(vg, dout):
    S, D2 = vg.shape
    D = D2 // 2
    ts, tc = _tile(S, 1024, 8), _tile(D, 512)
    nc = D // tc

    def body(v_ref, g_ref, do_ref, dv_ref, dg_ref):
        sg = _sigmoid(g_ref[...])
        do = do_ref[...]
        dv_ref[...] = (do * sg).astype(dv_ref.dtype)
        dg_ref[...] = (do * v_ref[...] * sg * (1.0 - sg)).astype(dg_ref.dtype)

    blk = pl.BlockSpec((ts, tc), lambda i, j: (i, j))
    o = jax.ShapeDtypeStruct((S, D), _MXU)
    return _call(body, name=f"glu_bwd_{S}", grid=(S // ts, nc),
                 in_specs=[blk, pl.BlockSpec((ts, tc), lambda i, j: (i, nc + j)), blk], out_specs=[blk, blk],
                 out_shape=[o, o], compiler_params=_cp("parallel", "parallel"))(vg, vg, dout)


def _blockdiag_in(b2, nb, P):
    cg = b2.shape[0]
    gb = S5_BLOCK_GROUPS
    t = b2.reshape(cg, nb, gb, P).transpose(1, 0, 2, 3)
    eye = jnp.eye(gb, dtype=b2.dtype)
    return (eye[None, :, None, :, None] * t[:, None]).reshape(nb, gb * cg, gb * P)


def _blockdiag_in_extract(db, cg, P):
    nb = db.shape[0]
    gb = S5_BLOCK_GROUPS
    eye = jnp.eye(gb, dtype=db.dtype)
    t = (db.reshape(nb, gb, cg, gb, P) * eye[None, :, None, :, None]).sum(3)
    return t.transpose(2, 0, 1, 3).reshape(cg, nb * gb * P)


def _blockdiag_out(c, nb):
    G, cg, P = c.shape
    gb = S5_BLOCK_GROUPS
    t = c.reshape(nb, gb, cg, P).transpose(0, 3, 1, 2)
    eye = jnp.eye(gb, dtype=c.dtype)
    return (eye[None, :, None, :, None] * t[:, None]).reshape(nb, gb * P, gb * cg)


def _blockdiag_out_extract(dc, cg, P):
    nb = dc.shape[0]
    gb = S5_BLOCK_GROUPS
    eye = jnp.eye(gb, dtype=dc.dtype)
    t = (dc.reshape(nb, gb, P, gb, cg) * eye[None, :, None, :, None]).sum(1)
    return t.transpose(0, 2, 3, 1).reshape(nb * gb, cg, P)


def _xa_fwd(q, kv):
    S, D = q.shape
    M = kv.shape[0]
    dh = D // XA_HEADS
    ts = _tile(S, 512, 8)
    scale = dh ** -0.5

    def body(q_ref, k_ref, v_ref, o_ref):
        s = _dot(q_ref[...], k_ref[...], _NT) * scale
        e = jnp.exp(s - jnp.max(s, axis=-1, keepdims=True))
        p = e / jnp.sum(e, axis=-1, keepdims=True)
        o_ref[...] = _dot(p.astype(_MXU), v_ref[...]).astype(o_ref.dtype)

    return _call(body, name=f"xa_fwd_{S}", grid=(S // ts, XA_HEADS),
                 in_specs=[pl.BlockSpec((ts, dh), lambda i, h: (i, h)), pl.BlockSpec((M, dh), lambda i, h: (0, h)),
                           pl.BlockSpec((M, dh), lambda i, h: (0, XA_HEADS + h))],
                 out_specs=pl.BlockSpec((ts, dh), lambda i, h: (i, h)), out_shape=jax.ShapeDtypeStruct((S, D), _MXU),
                 compiler_params=_cp("parallel", "parallel"))(q, kv, kv)


def _xa_bwd(q, kv, do):
    S, D = q.shape
    M = kv.shape[0]
    dh = D // XA_HEADS
    ts = _tile(S, 512, 8)
    scale = dh ** -0.5

    def body(q_ref, k_ref, v_ref, do_ref, dq_ref, dk_ref, dv_ref):
        @pl.when(pl.program_id(1) == 0)
        def _():
            dk_ref[...] = jnp.zeros_like(dk_ref)
            dv_ref[...] = jnp.zeros_like(dv_ref)

        qv, kv_, vv, dov = q_ref[...], k_ref[...], v_ref[...], do_ref[...]
        s = _dot(qv, kv_, _NT) * scale
        e = jnp.exp(s - jnp.max(s, axis=-1, keepdims=True))
        p = e / jnp.sum(e, axis=-1, keepdims=True)
        dp = _dot(dov, vv, _NT)
        dv_ref[...] += _dot(p.astype(_MXU), dov, _TN)
        ds = (p * (dp - jnp.sum(dp * p, axis=-1, keepdims=True)) * scale).astype(_MXU)
        dq_ref[...] = _dot(ds, kv_).astype(dq_ref.dtype)
        dk_ref[...] += _dot(ds, qv, _TN)

    qb = pl.BlockSpec((ts, dh), lambda h, i: (i, h))
    mb = pl.BlockSpec((M, dh), lambda h, i: (0, h))
    m = jax.ShapeDtypeStruct((M, D), f32)
    return _call(body, name=f"xa_bwd_{S}", grid=(XA_HEADS, S // ts),
                 in_specs=[qb, mb, pl.BlockSpec((M, dh), lambda h, i: (0, XA_HEADS + h)), qb],
                 out_specs=[qb, mb, mb], out_shape=[jax.ShapeDtypeStruct((S, D), _MXU), m, m],
                 compiler_params=_cp("parallel", "arbitrary"))(q, kv, kv, do)


def _conv(u, cw_ref, cb_ref, rows):
    return _conv_taps(u, cw_ref, cb_ref)[0]


def _conv_taps(u, cw_ref, cb_ref):
    u1, u2 = _shift_down(u, 1), _shift_down(u, 2)
    return cw_ref[2:3, :] * u + cw_ref[1:2, :] * u1 + cw_ref[0:1, :] * u2 + cb_ref[...], u1, u2


def _convglu_fwd(u, cw, cb):
    S, F2 = u.shape
    F = F2 // 2
    tc = _tile(F, 128)
    nc = F // tc

    def body(uv_ref, ug_ref, cwv_ref, cwg_ref, cbv_ref, cbg_ref, o_ref):
        rows = lax.broadcasted_iota(jnp.int32, (S, 1), 0)
        val = _conv(uv_ref[...], cwv_ref, cbv_ref, rows)
        gate = _conv(ug_ref[...], cwg_ref, cbg_ref, rows)
        o_ref[...] = (gate * _sigmoid(gate) * val).astype(o_ref.dtype)

    def col(r, off):
        return pl.BlockSpec((r, tc), lambda j: (0, off + j))

    return _call(body, name=f"convglu_fwd_{S}", grid=(nc,),
                 in_specs=[col(S, 0), col(S, nc), col(3, 0), col(3, nc), col(1, 0), col(1, nc)], out_specs=col(S, 0),
                 out_shape=jax.ShapeDtypeStruct((S, F), _MXU), compiler_params=_cp("parallel"))(
                     u, u, cw, cw, cb.reshape(1, F2), cb.reshape(1, F2))


def _convglu_bwd(u, cw, cb, dact):
    S, F2 = u.shape
    F = F2 // 2
    tc = _tile(F, 128)
    nc = F // tc

    def body(uv_ref, ug_ref, cwv_ref, cwg_ref, cbv_ref, cbg_ref, da_ref, duv_ref, dug_ref, dcwv_ref, dcwg_ref, dcbv_ref, dcbg_ref):
        rows = lax.broadcasted_iota(jnp.int32, (S, 1), 0)
        uv, ug = uv_ref[...], ug_ref[...]
        val, uv1, uv2 = _conv_taps(uv, cwv_ref, cbv_ref)
        gate, ug1, ug2 = _conv_taps(ug, cwg_ref, cbg_ref)
        sg = _sigmoid(gate)
        da = da_ref[...]
        dval = da * (gate * sg)
        dgate = da * val * (sg * (1.0 + gate * (1.0 - sg)))
        for (uu, uu1, uu2), d, cw_ref, du_ref, dcw_ref, dcb_ref in (((uv, uv1, uv2), dval, cwv_ref, duv_ref, dcwv_ref, dcbv_ref),
                                                                    ((ug, ug1, ug2), dgate, cwg_ref, dug_ref, dcwg_ref, dcbg_ref)):
            dcb_ref[...] = jnp.sum(d, axis=0, keepdims=True)
            dcw_ref[2:3, :] = jnp.sum(d * uu, axis=0, keepdims=True)
            dcw_ref[1:2, :] = jnp.sum(d * uu1, axis=0, keepdims=True)
            dcw_ref[0:1, :] = jnp.sum(d * uu2, axis=0, keepdims=True)
            du = cw_ref[2:3, :] * d + cw_ref[1:2, :] * _shift_up(d, 1, rows, S) + cw_ref[0:1, :] * _shift_up(d, 2, rows, S)
            du_ref[...] = du.astype(du_ref.dtype)

    def col(r, off):
        return pl.BlockSpec((r, tc), lambda j: (0, off + j))

    o = jax.ShapeDtypeStruct((S, F), _MXU)
    w, b = jax.ShapeDtypeStruct((3, F), f32), jax.ShapeDtypeStruct((1, F), f32)
    duv, dug, dcwv, dcwg, dcbv, dcbg = _call(
        body, name=f"convglu_bwd_{S}", grid=(nc,),
        in_specs=[col(S, 0), col(S, nc), col(3, 0), col(3, nc), col(1, 0), col(1, nc), col(S, 0)],
        out_specs=[col(S, 0), col(S, 0), col(3, 0), col(3, 0), col(1, 0), col(1, 0)],
        out_shape=[o, o, w, w, b, b], compiler_params=_cp("parallel"))(
            u, u, cw, cw, cb.reshape(1, F2), cb.reshape(1, F2), dact)
    return (jnp.concatenate([duv, dug], axis=1), jnp.concatenate([dcwv, dcwg], axis=1),
            jnp.concatenate([dcbv, dcbg], axis=1).reshape(F2))


def _final_loss(h, delta, g, target):
    S, D = h.shape
    ts = _tile(S, 512, 8)
    row = pl.BlockSpec((ts, D), lambda i: (i, 0))
    vec = pl.BlockSpec((1, D), lambda i: (0, 0))
    one = pl.BlockSpec((1, _LANES), lambda i: (0, 0))

    def body(h_ref, d_ref, g_ref, t_ref, l_ref, dh_ref, dg_ref):
        @pl.when(pl.program_id(0) == 0)
        def _():
            l_ref[...] = jnp.zeros_like(l_ref)
            dg_ref[...] = jnp.zeros_like(dg_ref)

        xv = h_ref[...] + d_ref[...]
        gv = g_ref[...]
        r = lax.rsqrt(jnp.mean(xv * xv, axis=-1, keepdims=True) + EPS)
        xh = xv * r
        err = xh * gv - t_ref[...]
        l_ref[...] += 0.5 * jnp.sum(jnp.mean(err * err, axis=-1, keepdims=True), axis=0, keepdims=True)
        dy = err * (1.0 / D)
        dg_ref[...] += jnp.sum(dy * xh, axis=0, keepdims=True)
        dxh = dy * gv
        dh_ref[...] = r * (dxh - xh * jnp.mean(dxh * xh, axis=-1, keepdims=True))

    loss, dh, dg = _call(body, name=f"final_loss_{S}", grid=(S // ts,), in_specs=[row, row, vec, row],
                         out_specs=[one, row, vec],
                         out_shape=[jax.ShapeDtypeStruct((1, _LANES), f32), jax.ShapeDtypeStruct((S, D), f32),
                                    jax.ShapeDtypeStruct((1, D), f32)],
                         compiler_params=_cp("arbitrary"))(h, delta, g.reshape(1, D), target)
    return loss[0, 0], dh, dg.reshape(D)


def _adamw(w, g, m, v):
    R, C = w.shape
    tr = _tile(R, max(8, (1 << 19) // C // 8 * 8), 8)
    blk = pl.BlockSpec((tr, C), lambda i: (i, 0))

    def body(w_ref, g_ref, m_ref, v_ref, d_ref, nm_ref, nv_ref):
        gv = g_ref[...]
        m_new = ADAM_B1 * m_ref[...] + (1.0 - ADAM_B1) * gv
        v_new = ADAM_B2 * v_ref[...] + (1.0 - ADAM_B2) * (gv * gv)
        m_hat = m_new / (1.0 - ADAM_B1 ** ADAM_STEP)
        v_hat = v_new / (1.0 - ADAM_B2 ** ADAM_STEP)
        d_ref[...] = -ADAM_LR * (m_hat / (jnp.sqrt(v_hat) + ADAM_EPS) + ADAM_WD * w_ref[...])
        nm_ref[...] = m_new
        nv_ref[...] = v_new

    o = jax.ShapeDtypeStruct((R, C), f32)
    return _call(body, name=f"adamw_{R}x{C}", grid=(R // tr,), in_specs=[blk] * 4, out_specs=[blk] * 3,
                 out_shape=[o, o, o], compiler_params=_cp("parallel"))(w, g, m, v)


def _sum_slabs(xs):
    K, R, C = xs.shape
    tr = _tile(R, 256, 8)

    def body(x_ref, o_ref):
        s = x_ref[0]
        for k in range(1, K):
            s = s + x_ref[k]
        o_ref[...] = s

    return _call(body, name=f"sum_slabs_{K}x{R}", grid=(R // tr,),
                 in_specs=[pl.BlockSpec((K, tr, C), lambda i: (0, i, 0))], out_specs=pl.BlockSpec((tr, C), lambda i: (i, 0)),
                 out_shape=jax.ShapeDtypeStruct((R, C), f32), compiler_params=_cp("parallel"))(xs)


def _slab_rows(R, C):
    return _tile(R, max(16, (1 << 19) // C // 16 * 16), 16)


def _add_partial(g, recv, core):
    _, R, C = g.shape
    tr = _slab_rows(R, C)

    def body(c_ref, a_ref, b_ref, o_ref, w_ref):
        s = a_ref[...] + b_ref[...]
        o_ref[...] = s
        w_ref[...] = s.astype(w_ref.dtype)

    blk = pl.BlockSpec((1, tr, C), lambda k, i, c: (k, i, 0))
    return _call(
        body, name=f"add_partial_{R}x{C}",
        grid_spec=pltpu.PrefetchScalarGridSpec(
            num_scalar_prefetch=1, grid=(4, R // tr),
            in_specs=[pl.BlockSpec((1, tr, C), lambda k, i, c: (2 * k + c[0], i, 0)), blk], out_specs=[blk, blk]),
        out_shape=[jax.ShapeDtypeStruct((4, R, C), f32), jax.ShapeDtypeStruct((4, R, C), _WIRE)],
        compiler_params=_cp("parallel", "parallel"))(core, g, recv)


def _sum_final(p, recv, chip):
    _, R, C = p.shape
    tr = _slab_rows(R, C)

    def body(q_ref, p_ref, r_ref, o_ref):
        o_ref[...] = ((p_ref[0] + r_ref[0].astype(f32)) + r_ref[1].astype(f32)) + r_ref[2].astype(f32)

    return _call(
        body, name=f"sum_final_{R}x{C}",
        grid_spec=pltpu.PrefetchScalarGridSpec(
            num_scalar_prefetch=1, grid=(R // tr,),
            in_specs=[pl.BlockSpec((1, tr, C), lambda i, q: (q[0], i, 0)), pl.BlockSpec((3, tr, C), lambda i, q: (0, i, 0))],
            out_specs=pl.BlockSpec((tr, C), lambda i, q: (i, 0))),
        out_shape=jax.ShapeDtypeStruct((R, C), f32), compiler_params=_cp("parallel"))(chip, p, recv)


def _my_pos():
    return lax.axis_index("x"), lax.axis_index("y"), lax.axis_index("c")


def _all_gather(shards):
    n = len(shards)

    def body(*refs):
        start, forward, finish = _all_gather_stages(refs[:n], refs[n:2 * n], *refs[2 * n:])
        start()
        forward()
        finish()

    tag = "_".join(f"{s.shape[0]}x{s.shape[1]}" for s in shards)
    return _call(body, name=f"all_gather_{tag}", in_specs=[ANY] * n, out_specs=[ANY] * n,
                 out_shape=_all_gather_shapes(shards), scratch_shapes=_all_gather_sems(n))(*shards)


def _all_gather_shapes(shards):
    return [jax.ShapeDtypeStruct((NDEV,) + s.shape, s.dtype) for s in shards]


def _all_gather_sems(n):
    return [pltpu.SemaphoreType.DMA((7, n)), pltpu.SemaphoreType.DMA((7, n)), pltpu.SemaphoreType.DMA((n,))]


def _all_gather_stages(x_refs, out_refs, send_sems, recv_sems, local_sems):
    n = len(x_refs)
    x, y, c = _my_pos()
    me, sibling = (x, y, c), (x, y, 1 - c)
    chips = [(1 - x, y), (x, 1 - y), (1 - x, 1 - y)]

    def slab(t, px, py, pc):
        return out_refs[t].at[4 * px + 2 * py + pc]

    def copy(k, t, block, to, from_input=False):
        return pltpu.make_async_remote_copy(
            src_ref=x_refs[t] if from_input else slab(t, *block), dst_ref=slab(t, *block),
            send_sem=send_sems.at[k, t], recv_sem=recv_sems.at[k, t], device_id=to, device_id_type=MESH_ID)

    mine = [pltpu.make_async_copy(x_refs[t], slab(t, *me), local_sems.at[t]) for t in range(n)]
    first = [copy(0, t, me, sibling, True) for t in range(n)]
    first += [copy(1 + j, t, me, (*chip, c), True) for j, chip in enumerate(chips) for t in range(n)]
    passed = [copy(4 + j, t, (*chip, c), sibling) for j, chip in enumerate(chips) for t in range(n)]

    def start():
        for cp in mine + first:
            cp.start()

    def forward():
        for j, chip in enumerate(chips):
            for t in range(n):
                copy(1 + j, t, (*chip, c), me).wait_recv()
                passed[j * n + t].start()

    def finish():
        for t in range(n):
            copy(0, t, sibling, me).wait_recv()
        for j, chip in enumerate(chips):
            for t in range(n):
                copy(4 + j, t, (*chip, 1 - c), me).wait_recv()
        for cp in first + passed:
            cp.wait_send()
        for cp in mine:
            cp.wait()

    return start, forward, finish


def _exchange_cores(gs):
    n = len(gs)

    def body(*refs):
        g_refs, out_refs = refs[:n], refs[n:2 * n]
        send_sems, recv_sems = refs[2 * n:]
        x, y, c = _my_pos()
        cps = [pltpu.make_async_remote_copy(src_ref=g_refs[t].at[2 * q + (1 - c)], dst_ref=out_refs[t].at[q],
                                            send_sem=send_sems.at[q, t], recv_sem=recv_sems.at[q, t],
                                            device_id=(x, y, 1 - c), device_id_type=MESH_ID)
               for t in range(n) for q in range(4)]
        for cp in cps:
            cp.start()
        for cp in cps:
            cp.wait()

    tag = "_".join(f"{g.shape[1]}x{g.shape[2]}" for g in gs)
    return _call(body, name=f"exchange_cores_{tag}", in_specs=[ANY] * n, out_specs=[ANY] * n,
                 out_shape=[jax.ShapeDtypeStruct((4,) + g.shape[1:], g.dtype) for g in gs],
                 scratch_shapes=[pltpu.SemaphoreType.DMA((4, n)), pltpu.SemaphoreType.DMA((4, n))])(*gs)


def _exchange_chips(ps):
    n = len(ps)

    def body(*refs):
        start, finish = _exchange_chips_stages(refs[:n], refs[n:2 * n], *refs[2 * n:])
        start()
        finish()

    tag = "_".join(f"{p.shape[1]}x{p.shape[2]}" for p in ps)
    return _call(body, name=f"exchange_chips_{tag}", in_specs=[ANY] * n, out_specs=[ANY] * n,
                 out_shape=_exchange_chips_shapes(ps), scratch_shapes=_exchange_chips_sems(n))(*ps)


def _exchange_chips_shapes(ps):
    return [jax.ShapeDtypeStruct((3,) + p.shape[1:], p.dtype) for p in ps]


def _exchange_chips_sems(n):
    return [pltpu.SemaphoreType.DMA((3, n)), pltpu.SemaphoreType.DMA((3, n))]


def _exchange_chips_stages(p_refs, out_refs, send_sems, recv_sems):
    n = len(p_refs)
    x, y, c = _my_pos()
    chips = [(x, 1 - y), (1 - x, y), (1 - x, 1 - y)]
    cps = [pltpu.make_async_remote_copy(src_ref=p_refs[t].at[2 * px + py], dst_ref=out_refs[t].at[r],
                                        send_sem=send_sems.at[r, t], recv_sem=recv_sems.at[r, t],
                                        device_id=(px, py, c), device_id_type=MESH_ID)
           for t in range(n) for r, (px, py) in enumerate(chips)]

    def start():
        for cp in cps:
            cp.start()

    def finish():
        for cp in cps:
            cp.wait()

    return start, finish


def _exchange_all_shapes(gs):
    return [jax.ShapeDtypeStruct((NDEV - 1,) + g.shape[1:], g.dtype) for g in gs]


def _exchange_all_sems(n):
    return [pltpu.SemaphoreType.DMA((NDEV - 1, n)), pltpu.SemaphoreType.DMA((NDEV - 1, n))]


def _exchange_all_stages(g_refs, out_refs, send_sems, recv_sems):
    n = len(g_refs)
    x, y, c = _my_pos()
    cps = []
    for t in range(n):
        for r in range(1, NDEV):
            px, py, pc = (1 - x if r & 4 else x), (1 - y if r & 2 else y), (1 - c if r & 1 else c)
            cps.append(pltpu.make_async_remote_copy(
                src_ref=g_refs[t].at[4 * px + 2 * py + pc], dst_ref=out_refs[t].at[r - 1],
                send_sem=send_sems.at[r - 1, t], recv_sem=recv_sems.at[r - 1, t],
                device_id=(px, py, pc), device_id_type=MESH_ID))

    def start():
        for cp in cps:
            cp.start()

    def finish():
        for cp in cps:
            cp.wait()

    return start, finish


def _sum_all(g, recv, dev):
    _, R, C = g.shape
    tr = _slab_rows(R, C)

    def body(d_ref, g_ref, r_ref, o_ref):
        s = g_ref[0].astype(f32)
        for k in range(NDEV - 1):
            s = s + r_ref[k].astype(f32)
        o_ref[...] = s

    return _call(
        body, name=f"sum_all_{R}x{C}",
        grid_spec=pltpu.PrefetchScalarGridSpec(
            num_scalar_prefetch=1, grid=(R // tr,),
            in_specs=[pl.BlockSpec((1, tr, C), lambda i, d: (d[0], i, 0)),
                      pl.BlockSpec((NDEV - 1, tr, C), lambda i, d: (0, i, 0))],
            out_specs=pl.BlockSpec((tr, C), lambda i, d: (i, 0))),
        out_shape=jax.ShapeDtypeStruct((R, C), f32), compiler_params=_cp("parallel"))(dev, g, recv)


def _pack(arrs, dtype, rows_mult):
    flat = jnp.concatenate([a.astype(dtype).reshape(-1) for a in arrs])
    q = rows_mult * _PACK_COLS
    tot = -(-flat.shape[0] // q) * q
    return jnp.pad(flat, (0, tot - flat.shape[0])).reshape(tot // _PACK_COLS, _PACK_COLS)


def _unpack(flat, shapes):
    out, off = [], 0
    for s in shapes:
        n = math.prod(s)
        out.append(flat[..., off:off + n].reshape(flat.shape[:-1] + tuple(s)))
        off += n
    return out


def _width_groups(names, shapes):
    groups = {}
    for n in names:
        groups.setdefault(shapes[n][-1], []).append(n)
    return list(groups.values())


def _to_full(piece, ax):
    t = jnp.moveaxis(piece, 0, ax)
    return t.reshape(t.shape[:ax] + (t.shape[ax] * t.shape[ax + 1],) + t.shape[ax + 2:])


def _to_shards(g, ax):
    t = g.reshape(g.shape[:ax] + (NDEV, g.shape[ax] // NDEV) + g.shape[ax + 1:])
    return jnp.moveaxis(t, ax, 0)


def _split_rows(buf, shapes):
    out, off = [], 0
    for s in shapes:
        r = math.prod(s[:-1])
        out.append(buf[..., off:off + r, :].reshape(buf.shape[:-2] + tuple(s)))
        off += r
    return out


def _item_ax(item):
    return SHARD_AXIS[item[0]] - 1


def _gather_bufs(shards, items):
    shapes = {it: shards[it[0]].shape[1:] for it in items}
    groups = _width_groups(items, shapes)
    bufs = [jnp.concatenate([shards[n][l].astype(_MXU).reshape(-1, shapes[(n, l)][-1]) for n, l in grp], axis=0)
            for grp in groups]
    return groups, bufs


def _gather_unpack(shards, groups, outs):
    full = {}
    for grp, g in zip(groups, outs):
        for it, piece in zip(grp, _split_rows(g, [shards[it[0]].shape[1:] for it in grp])):
            full[it] = _to_full(piece, _item_ax(it))
    return full


def _scatter_bufs(grads, shards, items, dtype=f32):
    shapes = {it: shards[it[0]].shape[1:] for it in items}
    groups = _width_groups(items, shapes)
    bufs = [jnp.concatenate([_to_shards(grads[it], _item_ax(it)).astype(dtype).reshape(NDEV, -1, shapes[it][-1])
                             for it in grp], axis=1) for grp in groups]
    bufs = [jnp.pad(b, ((0, 0), (0, -b.shape[1] % 256), (0, 0))) for b in bufs]
    return groups, bufs


def _mesh_scalars():
    core = jnp.reshape(lax.axis_index("c"), (1,)).astype(jnp.int32)
    chip = jnp.reshape(2 * lax.axis_index("x") + lax.axis_index("y"), (1,)).astype(jnp.int32)
    return core, chip


def _chip_sums(bufs):
    core, _ = _mesh_scalars()
    return [_add_partial(b, r, core) for b, r in zip(bufs, _exchange_cores(bufs))]


def _device_sums(chip_sums, from_chips):
    _, chip = _mesh_scalars()
    return [_sum_final(p, r, chip) for (p, _), r in zip(chip_sums, from_chips)]


def _scatter_unpack(shards, groups, totals):
    out = {}
    for grp, t in zip(groups, totals):
        out.update(zip(grp, _split_rows(t, [shards[it[0]].shape[1:] for it in grp])))
    return out


def _small_scatter_buf(grads, names):
    small = jnp.concatenate([_to_shards(grads[n], SHARD_AXIS[n]).reshape(NDEV, -1) for n in names], axis=1)
    q = 16 * _PACK_COLS
    tot = -(-small.shape[1] // q) * q
    return jnp.pad(small, ((0, 0), (0, tot - small.shape[1]))).reshape(NDEV, tot // _PACK_COLS, _PACK_COLS)


def _all_reduce_small(grads, names):
    shapes = [grads[n].shape for n in names]
    packed = _pack([grads[n] for n in names], f32, 8)
    total = _sum_slabs(_all_gather([packed])[0])
    return dict(zip(names, _unpack(total.reshape(-1), shapes)))


def _as2d(a):
    if a.ndim == 1:
        return a.reshape(1, -1)
    return a.reshape(-1, a.shape[-1])


def kernel(x, mem, mix_norm_g, pool_w, pool_scale, sb_w_qkv, sb_w_o, s5_a_re, s5_a_im, s5_log_dt, s5_b_re, s5_b_im, s5_c_re, s5_c_im, s5_d, s5_w_glu, xa_norm_g, mem_norm_g, xa_wq, xa_wkv, xa_wo, ffn_norm_g, ffn_w_up, ffn_conv_w, ffn_conv_b, ffn_w_down, final_norm_g, loss_target, m_mix_norm_g, m_pool_w, m_pool_scale, m_sb_w_qkv, m_sb_w_o, m_s5_a_re, m_s5_a_im, m_s5_log_dt, m_s5_b_re, m_s5_b_im, m_s5_c_re, m_s5_c_im, m_s5_d, m_s5_w_glu, m_xa_norm_g, m_mem_norm_g, m_xa_wq, m_xa_wkv, m_xa_wo, m_ffn_norm_g, m_ffn_w_up, m_ffn_conv_w, m_ffn_conv_b, m_ffn_w_down, m_final_norm_g, v_mix_norm_g, v_pool_w, v_pool_scale, v_sb_w_qkv, v_sb_w_o, v_s5_a_re, v_s5_a_im, v_s5_log_dt, v_s5_b_re, v_s5_b_im, v_s5_c_re, v_s5_c_im, v_s5_d, v_s5_w_glu, v_xa_norm_g, v_mem_norm_g, v_xa_wq, v_xa_wkv, v_xa_wo, v_ffn_norm_g, v_ffn_w_up, v_ffn_conv_w, v_ffn_conv_b, v_ffn_w_down, v_final_norm_g):
    args = locals()
    w = {n: args[n] for n in WEIGHTS}
    mom = {n: args["m_" + n] for n in WEIGHTS}
    var = {n: args["v_" + n] for n in WEIGHTS}
    h0, memv, target = x[0], mem[0], loss_target[0]
    S, D = h0.shape
    depth = mix_norm_g.shape[0]
    n_mix = 3

    first_layer = {'pool_w': 0, 'sb_w_qkv': 1, 'sb_w_o': 1, 's5_w_glu': 2}

    def layer_of(item):
        return first_layer[item[0]] + n_mix * item[1] if item[0] in first_layer else item[1]

    sb_layer = 1 if depth > 1 else None
    items = [(n, l) for n in MXU_WEIGHTS for l in range(w[n].shape[0])]
    early_w = [it for it in items if sb_layer is None or layer_of(it) < sb_layer or it == ('sb_w_qkv', 0)]
    late_w = [it for it in items if it not in early_w]
    groups_a, bufs_a = _gather_bufs(w, early_w)
    outs_a = _all_gather(bufs_a + [_pack([w[n] for n in VEC_WEIGHTS], f32, 8)])
    wfull = _gather_unpack(w, groups_a, outs_a[:-1])
    full = dict(w)
    for n, piece in zip(VEC_WEIGHTS, _unpack(outs_a[-1].reshape(NDEV, -1), [w[n].shape for n in VEC_WEIGHTS])):
        full[n] = _to_full(piece, SHARD_AXIS[n])
    groups_b, bufs_b = _gather_bufs(w, late_w)

    grads = {}

    def acc(name, j, val):
        grads.setdefault(name, {})[j] = val

    s5 = []
    for j in range(s5_a_re.shape[0]):
        G, P = s5_a_re.shape[1:]
        cg = s5_b_re.shape[3]
        N = G * P
        nb = D // _LANES
        ar, ai = s5_a_re[j].reshape(1, N), s5_a_im[j].reshape(1, N)
        ldt = jnp.repeat(s5_log_dt[j], P).reshape(1, N)
        br = s5_b_re[j].transpose(2, 0, 1).reshape(cg, N)
        bi = s5_b_im[j].transpose(2, 0, 1).reshape(cg, N)
        abr, abi, bbr, bbi = _s5_prep_fwd(ar, ai, ldt, br, bi)
        s5.append(dict(ar=ar, ai=ai, ldt=ldt, br=br, bi=bi, abr=abr, abi=abi, G=G, P=P, cg=cg, N=N, nb=nb,
                       Br=_blockdiag_in(bbr, nb, P), Bi=_blockdiag_in(bbi, nb, P),
                       Cr=_blockdiag_out(s5_c_re[j], nb), Ci=_blockdiag_out(s5_c_im[j], nb)))

    saved = []
    h, delta = h0, None
    for i in range(depth):
        kind, j = i % n_mix, i // n_mix
        sv = dict(kind=kind, j=j)
        mix_dtype = _MXU if kind == 1 else f32
        if delta is None:
            hn = _rms_fwd(h, full['mix_norm_g'][i], out_dtype=mix_dtype)
        else:
            h, hn = _rms_fwd(h, full['mix_norm_g'][i], delta, out_dtype=mix_dtype)
        sv['h'] = h
        if kind == 0:
            p = _pool_fwd(hn)
            t = _pool_mix_fwd(p, wfull['pool_w', j], full['pool_scale'][j])
            sv.update(p=p)
        elif kind == 1:
            qkv = _mm(hn, wfull['sb_w_qkv', j], out_dtype=_MXU)
            if i == sb_layer:
                o, outs_b = _sb_fwd(qkv, bufs_b)
                wfull.update(_gather_unpack(w, groups_b, outs_b))
            else:
                o, _ = _sb_fwd(qkv)
            t = _mm(o, wfull['sb_w_o', j])
            sv.update(hn=hn, qkv=qkv, o=o)
        else:
            pr = s5[j]
            bur, bui = _s5_in_fwd(hn, pr['Br'], pr['Bi'])
            xr, xi = _s5_scan_fwd(bur, bui, pr['abr'], pr['abi'])
            y, z = _s5_out_fwd(xr, xi, pr['Cr'], pr['Ci'], hn, full['s5_d'][j])
            vg = _mm(z, wfull['s5_w_glu', j])
            t = _glu_fwd(vg)
            sv.update(hn=hn, xr=xr, xi=xi, y=y, z=z, vg=vg)
        h1, a = _rms_fwd(h, full['xa_norm_g'][i], t, out_dtype=_MXU)
        memn = _rms_fwd(memv, full['mem_norm_g'][i], out_dtype=_MXU)
        q = _mm(a, wfull['xa_wq', i], out_dtype=_MXU)
        kv = _mm(memn, wfull['xa_wkv', i], out_dtype=_MXU)
        o2 = _xa_fwd(q, kv)
        mo = _mm(o2, wfull['xa_wo', i])
        h2, b = _rms_fwd(h1, full['ffn_norm_g'][i], mo, out_dtype=_MXU)
        u = _mm(b, wfull['ffn_w_up', i])
        act = _convglu_fwd(u, full['ffn_conv_w'][i], full['ffn_conv_b'][i])
        delta = _mm(act, wfull['ffn_w_down', i])
        sv.update(h1=h1, a=a, memn=memn, q=q, kv=kv, o2=o2, h2=h2, b=b, u=u, act=act)
        saved.append(sv)
        h = h2

    loss_local, dh, dg_final = _final_loss(h, delta, full['final_norm_g'], target)
    loss = lax.psum(loss_local, ("x", "y", "c"))
    grads['final_norm_g'] = dg_final

    for i in reversed(range(depth)):
        sv = saved[i]
        kind, j = sv['kind'], sv['j']
        dact = _mm(dh, wfull['ffn_w_down', i], tb=True)
        gdt = _WIRE if sb_layer is not None and i >= sb_layer else f32
        acc('ffn_w_down', i, _mm(sv['act'], dh, ta=True, out_dtype=gdt))
        du, dcw, dcb = _convglu_bwd(sv['u'], full['ffn_conv_w'][i], full['ffn_conv_b'][i], dact)
        acc('ffn_conv_w', i, dcw)
        acc('ffn_conv_b', i, dcb)
        db = _mm(du, wfull['ffn_w_up', i], tb=True)
        acc('ffn_w_up', i, _mm(sv['b'], du, ta=True, out_dtype=gdt))
        dh2, dg = _rms_bwd(sv['h2'], full['ffn_norm_g'][i], db, dh)
        acc('ffn_norm_g', i, dg)
        do2 = _mm(dh2, wfull['xa_wo', i], tb=True, out_dtype=_MXU)
        acc('xa_wo', i, _mm(sv['o2'], dh2, ta=True, out_dtype=gdt))
        dq, dk, dv = _xa_bwd(sv['q'], sv['kv'], do2)
        dkv = jnp.concatenate([dk, dv], axis=1)
        da = _mm(dq, wfull['xa_wq', i], tb=True)
        acc('xa_wq', i, _mm(sv['a'], dq, ta=True, out_dtype=gdt))
        dmemn = _mm(dkv, wfull['xa_wkv', i], tb=True)
        acc('xa_wkv', i, _mm(sv['memn'], dkv, ta=True, out_dtype=gdt))
        _, dg = _rms_bwd(memv, full['mem_norm_g'][i], dmemn)
        acc('mem_norm_g', i, dg)
        dh1, dg = _rms_bwd(sv['h1'], full['xa_norm_g'][i], da, dh2)
        acc('xa_norm_g', i, dg)
        if kind == 0:
            dp, dw, ds = _pool_mix_bwd(sv['p'], wfull['pool_w', j], full['pool_scale'][j], dh1)
            acc('pool_w', j, dw)
            acc('pool_scale', j, ds)
            dhn = _pool_bwd(dp)
        elif kind == 1:
            do = _mm(dh1, wfull['sb_w_o', j], tb=True)
            acc('sb_w_o', j, _mm(sv['o'], dh1, ta=True, out_dtype=gdt))
            if i == sb_layer:
                early_g = [it for it in items if it != ('sb_w_qkv', j) and layer_of(it) >= sb_layer]
                groups_e, wire_e = _scatter_bufs({it: grads[it[0]][it[1]] for it in early_g}, w, early_g, _WIRE)
                dqs, dks, dvs, recv_e = _sb_bwd(sv['qkv'], sv['o'], do, wire_e)
            else:
                dqs, dks, dvs, _ = _sb_bwd(sv['qkv'], sv['o'], do)
            dqkv = jnp.concatenate([dqs, dks, dvs], axis=1)
            dhn = _mm(dqkv, wfull['sb_w_qkv', j], tb=True)
            acc('sb_w_qkv', j, _mm(sv['hn'], dqkv, ta=True))
        else:
            pr = s5[j]
            dval, dgate = _glu_bwd(sv['vg'], dh1)
            dvg = jnp.concatenate([dval, dgate], axis=1)
            dz = _mm(dvg, wfull['s5_w_glu', j], tb=True)
            acc('s5_w_glu', j, _mm(sv['z'], dvg, ta=True, out_dtype=gdt))
            dy, gdr, gdi, du0, dd = _s5_out_bwd(dz, sv['y'], sv['hn'], full['s5_d'][j], pr['Cr'], pr['Ci'])
            acc('s5_d', j, dd)
            gr, gi, dabr, dabi = _s5_scan_bwd(gdr, gdi, sv['xr'], sv['xi'], pr['abr'], pr['abi'])
            dhn = _s5_in_bwd(gr, gi, pr['Br'], pr['Bi'], du0)
            dBr, dBi, dCr, dCi = _s5_wgrad(sv['hn'], gr, gi, sv['xr'], sv['xi'], dy)
            cg, P, G = pr['cg'], pr['P'], pr['G']
            acc('s5_c_re', j, _blockdiag_out_extract(dCr, cg, P))
            acc('s5_c_im', j, _blockdiag_out_extract(dCi, cg, P))
            dar, dai, dldt, dbr, dbi = _s5_prep_bwd(pr['ar'], pr['ai'], pr['ldt'], pr['br'], pr['bi'], dabr, dabi,
                                                    _blockdiag_in_extract(dBr, cg, P), _blockdiag_in_extract(dBi, cg, P))
            acc('s5_a_re', j, dar.reshape(G, P))
            acc('s5_a_im', j, dai.reshape(G, P))
            acc('s5_log_dt', j, dldt.reshape(G, P).sum(axis=1))
            acc('s5_b_re', j, dbr.reshape(cg, G, P).transpose(1, 2, 0))
            acc('s5_b_im', j, dbi.reshape(cg, G, P).transpose(1, 2, 0))
        dh, dg = _rms_bwd(sv['h'], full['mix_norm_g'][i], dhn, dh1)
        acc('mix_norm_g', i, dg)
    grad_x = dh[None]

    gfull = {}
    for n in VEC_WEIGHTS + REPLICATED:
        gfull[n] = grads[n] if n == 'final_norm_g' else jnp.stack([grads[n][k] for k in range(len(grads[n]))])
    if sb_layer is None:
        early_g, local = [], {}
    else:
        dev = jnp.reshape(4 * lax.axis_index("x") + 2 * lax.axis_index("y") + lax.axis_index("c"), (1,)).astype(jnp.int32)
        local = _scatter_unpack(w, groups_e, [_sum_all(g, r, dev) for g, r in zip(wire_e, recv_e)])
    late_g = [it for it in items if it not in early_g]
    groups_l, bufs_l = _scatter_bufs({it: grads[it[0]][it[1]] for it in late_g}, w, late_g)
    sums_l = _chip_sums(bufs_l + [_small_scatter_buf(gfull, VEC_WEIGHTS)])
    totals_l = _device_sums(sums_l, _exchange_chips([wire for _, wire in sums_l]))
    local.update(_scatter_unpack(w, groups_l, totals_l[:-1]))
    gw = {n: jnp.stack([local[(n, l)] for l in range(w[n].shape[0])]) for n in MXU_WEIGHTS}
    gw.update(zip(VEC_WEIGHTS, _unpack(totals_l[-1].reshape(-1), [w[n].shape for n in VEC_WEIGHTS])))
    gw.update(_all_reduce_small(gfull, REPLICATED))

    deltas, new_m, new_v = {}, {}, {}
    for n in WEIGHTS:
        shp = w[n].shape
        d_, m_, v_ = _adamw(_as2d(w[n]), _as2d(gw[n]), _as2d(mom[n]), _as2d(var[n]))
        deltas[n], new_m[n], new_v[n] = d_.reshape(shp), m_.reshape(shp), v_.reshape(shp)

    return (loss, grad_x, *[gw[n] for n in WEIGHTS], *[deltas[n] for n in WEIGHTS],
            *[new_m[n] for n in WEIGHTS], *[new_v[n] for n in WEIGHTS])
```

```python
import functools
import math

import jax
import jax.numpy as jnp
from jax import lax
from jax.experimental import pallas as pl
from jax.experimental.pallas import tpu as pltpu

f32 = jnp.float32
_MXU = jnp.bfloat16
_WIRE = _MXU
_VMEM_LIMIT = 48 * 1024 * 1024
_LANES = 128
_PACK_COLS = 1024

NDEV = 8
EPS = 1e-6
POOL_WINDOWS = (2, 4, 8, 16)
SB_HEAD_DIM = 64
XA_HEADS = 4
S5_GROUP = 16
S5_BLOCK_GROUPS = _LANES // S5_GROUP
ADAM_LR, ADAM_B1, ADAM_B2, ADAM_EPS, ADAM_WD, ADAM_STEP = 0.001, 0.9, 0.999, 1e-08, 0.01, 10

WEIGHTS = ['mix_norm_g', 'pool_w', 'pool_scale', 'sb_w_qkv', 'sb_w_o', 's5_a_re', 's5_a_im', 's5_log_dt',
           's5_b_re', 's5_b_im', 's5_c_re', 's5_c_im', 's5_d', 's5_w_glu', 'xa_norm_g', 'mem_norm_g', 'xa_wq',
           'xa_wkv', 'xa_wo', 'ffn_norm_g', 'ffn_w_up', 'ffn_conv_w', 'ffn_conv_b', 'ffn_w_down', 'final_norm_g']
SHARD_AXIS = {'pool_w': 2, 'pool_scale': 1, 'sb_w_qkv': 2, 'sb_w_o': 1, 's5_d': 1, 's5_w_glu': 2, 'xa_wq': 1,
              'xa_wkv': 2, 'xa_wo': 1, 'ffn_w_up': 2, 'ffn_conv_w': 2, 'ffn_w_down': 1}
MXU_WEIGHTS = ['pool_w', 'sb_w_qkv', 'sb_w_o', 's5_w_glu', 'xa_wq', 'xa_wkv', 'xa_wo', 'ffn_w_up', 'ffn_w_down']
VEC_WEIGHTS = ['pool_scale', 's5_d', 'ffn_conv_w']
REPLICATED = [n for n in WEIGHTS if n not in SHARD_AXIS]

_NN = (((1,), (0,)), ((), ()))
_NT = (((1,), (1,)), ((), ()))
_TN = (((0,), (0,)), ((), ()))
MESH_ID = pl.DeviceIdType.MESH
ANY = pl.BlockSpec(memory_space=pl.ANY)


def _call(body, **kw):
    return pl.pallas_call(body, **kw)


def _cp(*sem):
    return pltpu.CompilerParams(dimension_semantics=sem, vmem_limit_bytes=_VMEM_LIMIT)


def _tile(n, target, mult=_LANES):
    if n <= target:
        return n
    t = (target // mult) * mult
    while t >= mult:
        if n % t == 0:
            return t
        t -= mult
    return n


def _dot(a, b, dims=_NN):
    return lax.dot_general(a, b, dims, preferred_element_type=f32)


def _split(a):
    hi = a.astype(_MXU)
    lo = (a - hi.astype(f32)).astype(_MXU)
    return hi, lo


def _dot_hilo(a, u, dims=_NN):
    hi, lo = _split(a)
    return _dot(hi, u, dims) + _dot(lo, u, dims)


def _dot3(a, b, dims=_NN):
    ah, al = _split(a)
    bh, bl = _split(b)
    return _dot(ah, bh, dims) + _dot(al, bh, dims) + _dot(ah, bl, dims)


def _sigmoid(x):
    return 0.5 * jnp.tanh(0.5 * x) + 0.5


_GELU_C = math.sqrt(2.0 / math.pi)


def _gelu(x):
    return x * (0.5 * (1.0 + jnp.tanh(_GELU_C * (x + 0.044715 * (x * x * x)))))


def _gelu_grad(x):
    t = jnp.tanh(_GELU_C * (x + 0.044715 * (x * x * x)))
    return 0.5 * (1.0 + t) + x * 0.5 * (1.0 - t * t) * _GELU_C * (1.0 + 3.0 * 0.044715 * x * x)


_SUBLANES = 8


def _shift_down(x, k, rows=None):
    r = pltpu.roll(x, k, 0)
    head = jnp.where(lax.broadcasted_iota(jnp.int32, (_SUBLANES, 1), 0) >= k, r[:_SUBLANES], 0.0)
    return jnp.concatenate([head, r[_SUBLANES:]], axis=0)


def _shift_up(x, k, rows, n):
    r = pltpu.roll(x, n - k, 0)
    tail = jnp.where(lax.broadcasted_iota(jnp.int32, (_SUBLANES, 1), 0) < _SUBLANES - k, r[n - _SUBLANES:], 0.0)
    return jnp.concatenate([r[:n - _SUBLANES], tail], axis=0)


def _mm(a, b, *, ta=False, tb=False, out_dtype=f32):
    M, K = (a.shape[1], a.shape[0]) if ta else a.shape
    N = b.shape[0] if tb else b.shape[1]
    tm, tn, tk = _tile(M, 1408), _tile(N, 1536), _tile(K, 1408)
    nk = K // tk
    a_spec = pl.BlockSpec((tk, tm), lambda i, j, k: (k, i)) if ta else pl.BlockSpec((tm, tk), lambda i, j, k: (i, k))
    b_spec = pl.BlockSpec((tn, tk), lambda i, j, k: (j, k)) if tb else pl.BlockSpec((tk, tn), lambda i, j, k: (k, j))
    dims = (((0 if ta else 1,), (1 if tb else 0,)), ((), ()))

    def body(a_ref, b_ref, o_ref, acc_ref):
        k = pl.program_id(2)

        @pl.when(k == 0)
        def _():
            acc_ref[...] = jnp.zeros_like(acc_ref)

        acc_ref[...] += _dot(a_ref[...].astype(_MXU), b_ref[...].astype(_MXU), dims)

        @pl.when(k == nk - 1)
        def _():
            o_ref[...] = acc_ref[...].astype(out_dtype)

    return _call(
        body, name=f"mm_{'t' if ta else 'n'}{'t' if tb else 'n'}_{M}x{K}x{N}",
        grid=(M // tm, N // tn, nk), in_specs=[a_spec, b_spec],
        out_specs=pl.BlockSpec((tm, tn), lambda i, j, k: (i, j)),
        out_shape=jax.ShapeDtypeStruct((M, N), out_dtype),
        scratch_shapes=[pltpu.VMEM((tm, tn), f32)],
        compiler_params=_cp("parallel", "parallel", "arbitrary"))(a, b)


def _mm_add_rms(a, b, res, g, *, out_dtype):
    M, K = a.shape
    N = b.shape[1]
    tm, tk = _tile(M, 512), _tile(K, 1408)
    nk = K // tk
    row = pl.BlockSpec((tm, N), lambda i, k: (i, 0))

    def body(a_ref, b_ref, r_ref, g_ref, s_ref, y_ref, acc_ref):
        k = pl.program_id(1)

        @pl.when(k == 0)
        def _():
            acc_ref[...] = jnp.zeros_like(acc_ref)

        acc_ref[...] += _dot(a_ref[...].astype(_MXU), b_ref[...].astype(_MXU))

        @pl.when(k == nk - 1)
        def _():
            xs = r_ref[...] + acc_ref[...]
            s_ref[...] = xs
            r = lax.rsqrt(jnp.mean(xs * xs, axis=-1, keepdims=True) + EPS)
            y_ref[...] = ((xs * r) * g_ref[...]).astype(out_dtype)

    return _call(
        body, name=f"mm_add_rms_{M}x{K}x{N}", grid=(M // tm, nk),
        in_specs=[pl.BlockSpec((tm, tk), lambda i, k: (i, k)), pl.BlockSpec((tk, N), lambda i, k: (k, 0)), row,
                  pl.BlockSpec((1, N), lambda i, k: (0, 0))],
        out_specs=[row, row], out_shape=[jax.ShapeDtypeStruct((M, N), f32), jax.ShapeDtypeStruct((M, N), out_dtype)],
        scratch_shapes=[pltpu.VMEM((tm, N), f32)],
        compiler_params=_cp("parallel", "arbitrary"))(a, b, res, g.reshape(1, N))


def _rms_fwd(x, g, delta=None, *, out_dtype=f32):
    S, D = x.shape
    ts = _tile(S, 512, 8)
    row = pl.BlockSpec((ts, D), lambda i: (i, 0))
    vec = pl.BlockSpec((1, D), lambda i: (0, 0))
    g2 = g.reshape(1, D)

    def norm(xv, g_ref):
        r = lax.rsqrt(jnp.mean(xv * xv, axis=-1, keepdims=True) + EPS)
        return ((xv * r) * g_ref[...]).astype(out_dtype)

    if delta is None:
        def body(x_ref, g_ref, y_ref):
            y_ref[...] = norm(x_ref[...], g_ref)

        return _call(body, name=f"rms_fwd_{S}", grid=(S // ts,), in_specs=[row, vec], out_specs=row,
                     out_shape=jax.ShapeDtypeStruct((S, D), out_dtype), compiler_params=_cp("parallel"))(x, g2)

    def body(x_ref, d_ref, g_ref, s_ref, y_ref):
        xv = x_ref[...] + d_ref[...]
        s_ref[...] = xv
        y_ref[...] = norm(xv, g_ref)

    return _call(body, name=f"add_rms_fwd_{S}", grid=(S // ts,), in_specs=[row, row, vec], out_specs=[row, row],
                 out_shape=[jax.ShapeDtypeStruct((S, D), f32), jax.ShapeDtypeStruct((S, D), out_dtype)],
                 compiler_params=_cp("parallel"))(x, delta, g2)


def _rms_bwd(x, g, dy, dres=None):
    S, D = x.shape
    ts = _tile(S, 512, 8)
    row = pl.BlockSpec((ts, D), lambda i: (i, 0))
    vec = pl.BlockSpec((1, D), lambda i: (0, 0))
    has_res = dres is not None

    def body(*refs):
        if has_res:
            x_ref, g_ref, dy_ref, dr_ref, dx_ref, dg_ref = refs
        else:
            x_ref, g_ref, dy_ref, dx_ref, dg_ref = refs
        xv = x_ref[...]
        r = lax.rsqrt(jnp.mean(xv * xv, axis=-1, keepdims=True) + EPS)
        xh = xv * r
        dyv = dy_ref[...].astype(f32)

        @pl.when(pl.program_id(0) == 0)
        def _():
            dg_ref[...] = jnp.zeros_like(dg_ref)

        dg_ref[...] += jnp.sum(dyv * xh, axis=0, keepdims=True)
        dxh = dyv * g_ref[...]
        dx = r * (dxh - xh * jnp.mean(dxh * xh, axis=-1, keepdims=True))
        if has_res:
            dx = dx + dr_ref[...]
        dx_ref[...] = dx

    ins = [x, g.reshape(1, D), dy] + ([dres] if has_res else [])
    dx, dg = _call(body, name=f"rms_bwd_{S}_{int(has_res)}", grid=(S // ts,),
                   in_specs=[row, vec, row] + ([row] if has_res else []), out_specs=[row, vec],
                   out_shape=[jax.ShapeDtypeStruct((S, D), f32), jax.ShapeDtypeStruct((1, D), f32)],
                   compiler_params=_cp("arbitrary"))(*ins)
    return dx, dg.reshape(D)


def _pool_windows_sum(x, win, rows):
    s, k = x, 1
    while k < win:
        s = s + _shift_down(s, k, rows)
        k *= 2
    return s


def _pool_windows_sum_up(x, win, rows, n):
    s, k = x, 1
    while k < win:
        s = s + _shift_up(s, k, rows, n)
        k *= 2
    return s


def _pool_fwd(hn):
    S, D = hn.shape
    cg = D // len(POOL_WINDOWS)
    tc = _tile(cg, 128)
    nb = cg // tc
    blk = pl.BlockSpec((S, tc), lambda gi, j: (0, gi * nb + j))

    def body(x_ref, p_ref):
        gi = pl.program_id(0)
        rows = lax.broadcasted_iota(jnp.int32, (S, 1), 0)
        cnt = (rows + 1).astype(f32)
        for k, win in enumerate(POOL_WINDOWS):
            @pl.when(gi == k)
            def _(win=win):
                x = x_ref[...]
                s = _pool_windows_sum(x, win, rows)
                p_ref[...] = (s / jnp.minimum(cnt, float(win)) - x).astype(p_ref.dtype)

    return _call(body, name=f"pool_fwd_{S}", grid=(len(POOL_WINDOWS), nb), in_specs=[blk], out_specs=blk,
                 out_shape=jax.ShapeDtypeStruct((S, D), _MXU), compiler_params=_cp("parallel", "parallel"))(hn)


def _pool_bwd(dp):
    S, D = dp.shape
    cg = D // len(POOL_WINDOWS)
    tc = _tile(cg, 128)
    nb = cg // tc
    blk = pl.BlockSpec((S, tc), lambda gi, j: (0, gi * nb + j))

    def body(dp_ref, dx_ref):
        gi = pl.program_id(0)
        rows = lax.broadcasted_iota(jnp.int32, (S, 1), 0)
        cnt = (rows + 1).astype(f32)
        for k, win in enumerate(POOL_WINDOWS):
            @pl.when(gi == k)
            def _(win=win):
                d = dp_ref[...]
                e = d / jnp.minimum(cnt, float(win))
                dx_ref[...] = _pool_windows_sum_up(e, win, rows, S) - d

    return _call(body, name=f"pool_bwd_{S}", grid=(len(POOL_WINDOWS), nb), in_specs=[blk], out_specs=blk,
                 out_shape=jax.ShapeDtypeStruct((S, D), f32), compiler_params=_cp("parallel", "parallel"))(dp)


def _pool_mix_fwd(p, w, scale):
    S, D = p.shape
    G, cg, _ = w.shape
    ts = _tile(S, 1024, 8)

    def body(p_ref, w_ref, s_ref, y_ref):
        y_ref[...] = _dot(p_ref[...], w_ref[0]) * s_ref[...]

    return _call(body, name=f"pool_mix_fwd_{S}", grid=(S // ts, G),
                 in_specs=[pl.BlockSpec((ts, cg), lambda i, g: (i, g)), pl.BlockSpec((1, cg, cg), lambda i, g: (g, 0, 0)),
                           pl.BlockSpec((1, cg), lambda i, g: (0, g))],
                 out_specs=pl.BlockSpec((ts, cg), lambda i, g: (i, g)),
                 out_shape=jax.ShapeDtypeStruct((S, D), f32), compiler_params=_cp("parallel", "parallel"))(
                     p, w, scale.reshape(1, D))


def _pool_mix_bwd(p, w, scale, dy):
    S, D = p.shape
    G, cg, _ = w.shape
    ts = _tile(S, 1024, 8)

    def body(p_ref, w_ref, s_ref, dy_ref, dp_ref, dw_ref, ds_ref):
        @pl.when(pl.program_id(1) == 0)
        def _():
            dw_ref[...] = jnp.zeros_like(dw_ref)
            ds_ref[...] = jnp.zeros_like(ds_ref)

        pv, wv, dyv = p_ref[...], w_ref[0], dy_ref[...]
        ypre = _dot(pv, wv)
        ds_ref[...] += jnp.sum(dyv * ypre, axis=0, keepdims=True)
        dyp = (dyv * s_ref[...]).astype(_MXU)
        dp_ref[...] = _dot(dyp, wv, _NT)
        dw_ref[0] += _dot(pv, dyp, _TN)

    dp, dw, ds = _call(
        body, name=f"pool_mix_bwd_{S}", grid=(G, S // ts),
        in_specs=[pl.BlockSpec((ts, cg), lambda g, i: (i, g)), pl.BlockSpec((1, cg, cg), lambda g, i: (g, 0, 0)),
                  pl.BlockSpec((1, cg), lambda g, i: (0, g)), pl.BlockSpec((ts, cg), lambda g, i: (i, g))],
        out_specs=[pl.BlockSpec((ts, cg), lambda g, i: (i, g)), pl.BlockSpec((1, cg, cg), lambda g, i: (g, 0, 0)),
                   pl.BlockSpec((1, cg), lambda g, i: (0, g))],
        out_shape=[jax.ShapeDtypeStruct((S, D), f32), jax.ShapeDtypeStruct((G, cg, cg), f32),
                   jax.ShapeDtypeStruct((1, D), f32)],
        compiler_params=_cp("parallel", "arbitrary"))(p, w, scale.reshape(1, D), dy)
    return dp, dw, ds.reshape(D)


def _sb_tile(S):
    return min(256, max(128, S // 4))


_LOG2E = math.log2(math.e)


def _sb_scores(qs, ks, tri, R, U):
    z = _dot(qs, ks, _NT)
    sp = jnp.maximum(z, 0.0) + jnp.log(1.0 + jnp.exp2(jnp.abs(z) * -_LOG2E))
    lb = z - sp
    if tri is not None:
        sp = jnp.where(tri, sp, 0.0)
    c = _dot(sp.astype(_MXU), U)
    a = jnp.exp(lb - c - R)
    if tri is not None:
        a = jnp.where(tri, a, 0.0)
    return sp, lb, a


def _sb_sweep(tile, i, carry):
    def two(p, c):
        kb = i - 1 - 2 * p
        return tile(kb - 1, tile(kb, c, None), None)

    carry = lax.fori_loop(0, i // 2, two, carry)
    return lax.cond(i % 2 == 1, lambda c: tile(i * 0, c, None), lambda c: c, carry)


def _sb_consts(T):
    lane = lax.broadcasted_iota(jnp.int32, (1, _LANES), 1)
    row, col = lax.broadcasted_iota(jnp.int32, (T, T), 0), lax.broadcasted_iota(jnp.int32, (T, T), 1)
    U = jnp.where(row > col, 1.0, 0.0).astype(_MXU)
    heads = [(lane >= SB_HEAD_DIM * h) & (lane < SB_HEAD_DIM * (h + 1)) for h in range(_LANES // SB_HEAD_DIM)]
    return heads, col < row, U


def _sb_fwd(qkv, gather=()):
    S, D3 = qkv.shape
    D = D3 // 3
    HP = D // _LANES
    T = _sb_tile(S)
    nq = S // T
    n = len(gather)
    scale = SB_HEAD_DIM ** -0.5

    def body(*refs):
        q_ref, k_ref, v_ref = refs[:3]
        o_ref = refs[3 + n]
        i = pl.program_id(1)
        if n:
            step = pl.program_id(0) * nq + i
            start, forward, finish = _all_gather_stages(refs[3:3 + n], refs[4 + n:4 + 2 * n], *refs[4 + 2 * n:])
            pl.when(step == 0)(start)
            pl.when(step == (HP // 2) * nq)(forward)
        q = q_ref[...] * scale
        heads, tri, U = _sb_consts(T)
        qms = [jnp.where(hm, q, jnp.zeros_like(q)) for hm in heads]

        def tile(kb, carry, mask):
            off = pl.multiple_of(kb * T, T)
            ks, vs = k_ref[pl.ds(off, T), :], v_ref[pl.ds(off, T), :]
            new = []
            for h, qm in enumerate(qms):
                R, acc = carry[2 * h], carry[2 * h + 1]
                sp, _, a = _sb_scores(qm, ks, mask, R, U)
                new += [R + jnp.sum(sp, axis=1, keepdims=True), acc + _dot(a.astype(_MXU), vs)]
            return tuple(new)

        zero = (jnp.zeros((T, 1), f32), jnp.zeros((T, _LANES), f32)) * len(heads)
        carry = _sb_sweep(tile, i, tile(i, zero, tri))
        out = jnp.zeros((T, _LANES), f32)
        for h, hm in enumerate(heads):
            out = out + jnp.where(hm, carry[2 * h + 1], 0.0)
        o_ref[...] = out
        if n:
            pl.when(step == HP * nq - 1)(finish)

    outs = _call(
        body, name=f"sb_fwd_{S}_{n}", grid=(HP, nq),
        in_specs=[pl.BlockSpec((T, _LANES), lambda hp, i: (i, hp)), pl.BlockSpec((S, _LANES), lambda hp, i: (0, HP + hp)),
                  pl.BlockSpec((S, _LANES), lambda hp, i: (0, 2 * HP + hp))] + [ANY] * n,
        out_specs=[pl.BlockSpec((T, _LANES), lambda hp, i: (i, hp))] + [ANY] * n,
        out_shape=[jax.ShapeDtypeStruct((S, D), f32)] + _all_gather_shapes(gather),
        scratch_shapes=_all_gather_sems(n) if n else [],
        compiler_params=_cp("arbitrary", "arbitrary"))(qkv, qkv, qkv, *gather)
    return outs[0], list(outs[1:])


def _sb_bwd(qkv, o, do, send=()):
    S, D3 = qkv.shape
    D = D3 // 3
    HP = D // _LANES
    T = _sb_tile(S)
    nq = S // T
    n = len(send)
    scale = SB_HEAD_DIM ** -0.5

    def body(*refs):
        q_ref, k_ref, v_ref, o_ref, do_ref = refs[:5]
        dq_ref, dk_ref, dv_ref = refs[5 + n:8 + n]
        dk_acc, dv_acc = refs[8 + 2 * n:10 + 2 * n]
        i = pl.program_id(1)
        if n:
            step = pl.program_id(0) * nq + i
            start, finish = _exchange_all_stages(refs[5:5 + n], refs[8 + n:8 + 2 * n], *refs[10 + 2 * n:])
            pl.when(step == 0)(start)

        @pl.when(i == 0)
        def _():
            dk_acc[...] = jnp.zeros_like(dk_acc)
            dv_acc[...] = jnp.zeros_like(dv_acc)

        q = q_ref[...] * scale
        dob = do_ref[...].astype(_MXU)
        prod = dob.astype(f32) * o_ref[...]
        heads, tri, U = _sb_consts(T)
        qms = [jnp.where(hm, q, jnp.zeros_like(q)) for hm in heads]
        doms = [jnp.where(hm, dob, jnp.zeros_like(dob)) for hm in heads]
        totals = [jnp.sum(jnp.where(hm, prod, 0.0), axis=1, keepdims=True) for hm in heads]

        def tile(kb, carry, mask):
            off = pl.multiple_of(kb * T, T)
            ks, vs = k_ref[pl.ds(off, T), :], v_ref[pl.ds(off, T), :]
            new = []
            dk_t = jnp.zeros((T, _LANES), f32)
            dv_t = jnp.zeros((T, _LANES), f32)
            for h, (qm, dom, total) in enumerate(zip(qms, doms, totals)):
                R, Gs, dq = carry[3 * h:3 * h + 3]
                sp, lb, a = _sb_scores(qm, ks, mask, R, U)
                ab = a.astype(_MXU)
                g = ab.astype(f32) * _dot(dom, vs, _NT)
                before = total - (g + _dot_hilo(g, U) + Gs)
                beta = jnp.exp(lb)
                dz = g - (g + before) * beta
                if mask is not None:
                    dz = jnp.where(mask, dz, 0.0)
                dzb = dz.astype(_MXU)
                dk_t = dk_t + _dot(dzb, qm, _TN)
                dv_t = dv_t + _dot(ab, dom, _TN)
                new += [R + jnp.sum(sp, axis=1, keepdims=True), Gs + jnp.sum(g, axis=1, keepdims=True), dq + _dot(dzb, ks)]
            dk_acc[pl.ds(off, T), :] += dk_t
            dv_acc[pl.ds(off, T), :] += dv_t
            return tuple(new)

        zero1 = jnp.zeros((T, 1), f32)
        carry = _sb_sweep(tile, i, tile(i, (zero1, zero1, jnp.zeros((T, _LANES), f32)) * len(heads), tri))
        dq_out = jnp.zeros((T, _LANES), f32)
        for h, hm in enumerate(heads):
            dq_out = dq_out + jnp.where(hm, carry[3 * h + 2], 0.0)
        dq_ref[...] = (dq_out * scale).astype(dq_ref.dtype)

        @pl.when(i == nq - 1)
        def _():
            dk_ref[...] = dk_acc[...].astype(dk_ref.dtype)
            dv_ref[...] = dv_acc[...].astype(dv_ref.dtype)

        if n:
            pl.when(step == HP * nq - 1)(finish)

    qb = pl.BlockSpec((T, _LANES), lambda hp, i: (i, hp))
    col = pl.BlockSpec((S, _LANES), lambda hp, i: (0, hp))
    out = jax.ShapeDtypeStruct((S, D), _MXU)
    outs = _call(
        body, name=f"sb_bwd_{S}_{n}", grid=(HP, nq),
        in_specs=[qb, pl.BlockSpec((S, _LANES), lambda hp, i: (0, HP + hp)),
                  pl.BlockSpec((S, _LANES), lambda hp, i: (0, 2 * HP + hp)), qb, qb] + [ANY] * n,
        out_specs=[qb, col, col] + [ANY] * n, out_shape=[out, out, out] + _exchange_all_shapes(send),
        scratch_shapes=[pltpu.VMEM((S, _LANES), f32), pltpu.VMEM((S, _LANES), f32)] + (_exchange_all_sems(n) if n else []),
        compiler_params=_cp("arbitrary", "arbitrary"))(qkv, qkv, qkv, o, do, *send)
    return outs[0], outs[1], outs[2], list(outs[3:])


def _cmul(ar, ai, br, bi):
    return ar * br - ai * bi, ar * bi + ai * br


def _s5_coef(ar, ai, ldt):
    dt = jnp.exp(ldt)
    e = jnp.exp(ar * dt)
    abr, abi = e * jnp.cos(ai * dt), e * jnp.sin(ai * dt)
    inv = 1.0 / (ar * ar + ai * ai)
    cr, ci = _cmul(abr - 1.0, abi, ar * inv, -ai * inv)
    return dt, abr, abi, inv, cr, ci


def _s5_prep_fwd(ar, ai, ldt, br, bi):
    N = ar.shape[1]
    cg = br.shape[0]

    def body(ar_ref, ai_ref, ldt_ref, br_ref, bi_ref, abr_ref, abi_ref, bbr_ref, bbi_ref):
        _, abr, abi, _, cr, ci = _s5_coef(ar_ref[...], ai_ref[...], ldt_ref[...])
        abr_ref[...], abi_ref[...] = abr, abi
        bbr_ref[...], bbi_ref[...] = _cmul(cr, ci, br_ref[...], bi_ref[...])

    v, m = jax.ShapeDtypeStruct((1, N), f32), jax.ShapeDtypeStruct((cg, N), f32)
    return _call(body, name="s5_prep_fwd", out_shape=[v, v, m, m])(ar, ai, ldt, br, bi)


def _s5_prep_bwd(ar, ai, ldt, br, bi, dabr, dabi, dbbr, dbbi):
    N = ar.shape[1]
    cg = br.shape[0]

    def body(ar_ref, ai_ref, ldt_ref, br_ref, bi_ref, dabr_ref, dabi_ref, dbbr_ref, dbbi_ref,
             dar_ref, dai_ref, dldt_ref, dbr_ref, dbi_ref):
        a_r, a_i = ar_ref[...], ai_ref[...]
        dt, abr, abi, inv, cr, ci = _s5_coef(a_r, a_i, ldt_ref[...])
        b_r, b_i, gr, gi = br_ref[...], bi_ref[...], dbbr_ref[...], dbbi_ref[...]
        dbr_ref[...], dbi_ref[...] = _cmul(cr, -ci, gr, gi)
        dcr = jnp.sum(gr * b_r + gi * b_i, axis=0, keepdims=True)
        dci = jnp.sum(gi * b_r - gr * b_i, axis=0, keepdims=True)
        ilr, ili = a_r * inv, -a_i * inv
        dwr, dwi = _cmul(ilr, -ili, dcr, dci)
        qr, qi = _cmul(cr, ci, ilr, ili)
        dl1r, dl1i = _cmul(-qr, qi, dcr, dci)
        tr, ti = dabr_ref[...] + dwr, dabi_ref[...] + dwi
        ddlr, ddli = _cmul(abr, -abi, tr, ti)
        dar_ref[...] = dl1r + ddlr * dt
        dai_ref[...] = dl1i + ddli * dt
        dldt_ref[...] = (a_r * ddlr + a_i * ddli) * dt

    v, m = jax.ShapeDtypeStruct((1, N), f32), jax.ShapeDtypeStruct((cg, N), f32)
    return _call(body, name="s5_prep_bwd", out_shape=[v, v, v, m, m])(ar, ai, ldt, br, bi, dabr, dabi, dbbr, dbbi)


def _s5_in_fwd(u, Br, Bi):
    S, D = u.shape
    nb, _, nw = Br.shape
    ts = _tile(S, 512, 8)

    def body(u_ref, br_ref, bi_ref, or_ref, oi_ref):
        uv = u_ref[...]
        or_ref[...] = _dot3(uv, br_ref[0])
        oi_ref[...] = _dot3(uv, bi_ref[0])

    ub = pl.BlockSpec((ts, _LANES), lambda i, k: (i, k))
    wb = pl.BlockSpec((1, _LANES, nw), lambda i, k: (k, 0, 0))
    ob = pl.BlockSpec((ts, nw), lambda i, k: (i, k))
    o = jax.ShapeDtypeStruct((S, nb * nw), f32)
    return _call(body, name=f"s5_in_fwd_{S}", grid=(S // ts, nb), in_specs=[ub, wb, wb], out_specs=[ob, ob],
                 out_shape=[o, o], compiler_params=_cp("parallel", "parallel"))(u, Br, Bi)


def _s5_scan_fwd(bur, bui, abr, abi):
    S, N = bur.shape
    tt, tn = _tile(S, 128, 8), _tile(N, 4096)

    def body(br_ref, bi_ref, ar_ref, ai_ref, xr_ref, xi_ref, sr, si):
        @pl.when(pl.program_id(1) == 0)
        def _():
            sr[...] = jnp.zeros_like(sr)
            si[...] = jnp.zeros_like(si)

        a_r, a_i = ar_ref[...], ai_ref[...]

        def step(t, carry):
            xr, xi = carry
            nr = a_r * xr - a_i * xi + br_ref[pl.ds(t, 1), :]
            ni = a_r * xi + a_i * xr + bi_ref[pl.ds(t, 1), :]
            xr_ref[pl.ds(t, 1), :] = nr
            xi_ref[pl.ds(t, 1), :] = ni
            return nr, ni

        xr, xi = lax.fori_loop(0, tt, step, (sr[...], si[...]))
        sr[...], si[...] = xr, xi

    blk = pl.BlockSpec((tt, tn), lambda n, i: (i, n))
    vec = pl.BlockSpec((1, tn), lambda n, i: (0, n))
    o = jax.ShapeDtypeStruct((S, N), f32)
    return _call(body, name=f"s5_scan_fwd_{S}", grid=(N // tn, S // tt), in_specs=[blk, blk, vec, vec],
                 out_specs=[blk, blk], out_shape=[o, o],
                 scratch_shapes=[pltpu.VMEM((1, tn), f32), pltpu.VMEM((1, tn), f32)],
                 compiler_params=_cp("parallel", "arbitrary"))(bur, bui, abr, abi)


def _s5_scan_bwd(dr, di, xr, xi, abr, abi):
    S, N = dr.shape
    tt, tn = _tile(S, 128, 8), _tile(N, 4096)
    nt = S // tt

    def body(dr_ref, di_ref, xr_ref, xi_ref, ar_ref, ai_ref, gr_ref, gi_ref, dar_ref, dai_ref, sr, si):
        @pl.when(pl.program_id(1) == 0)
        def _():
            sr[...] = jnp.zeros_like(sr)
            si[...] = jnp.zeros_like(si)
            dar_ref[...] = jnp.zeros_like(dar_ref)
            dai_ref[...] = jnp.zeros_like(dai_ref)

        a_r, a_i = ar_ref[...], ai_ref[...]

        def step(j, carry):
            gr, gi, accr, acci = carry
            t = tt - 1 - j
            xr_t, xi_t = xr_ref[pl.ds(t, 1), :], xi_ref[pl.ds(t, 1), :]
            accr = accr + gr * xr_t + gi * xi_t
            acci = acci + gi * xr_t - gr * xi_t
            nr = dr_ref[pl.ds(t, 1), :] + a_r * gr + a_i * gi
            ni = di_ref[pl.ds(t, 1), :] + a_r * gi - a_i * gr
            gr_ref[pl.ds(t, 1), :] = nr
            gi_ref[pl.ds(t, 1), :] = ni
            return nr, ni, accr, acci

        gr, gi, accr, acci = lax.fori_loop(0, tt, step, (sr[...], si[...], dar_ref[...], dai_ref[...]))
        sr[...], si[...] = gr, gi
        dar_ref[...], dai_ref[...] = accr, acci

    blk = pl.BlockSpec((tt, tn), lambda n, i: (nt - 1 - i, n))
    vec = pl.BlockSpec((1, tn), lambda n, i: (0, n))
    o, v = jax.ShapeDtypeStruct((S, N), f32), jax.ShapeDtypeStruct((1, N), f32)
    return _call(body, name=f"s5_scan_bwd_{S}", grid=(N // tn, nt), in_specs=[blk, blk, blk, blk, vec, vec],
                 out_specs=[blk, blk, vec, vec], out_shape=[o, o, v, v],
                 scratch_shapes=[pltpu.VMEM((1, tn), f32), pltpu.VMEM((1, tn), f32)],
                 compiler_params=_cp("parallel", "arbitrary"))(dr, di, xr, xi, abr, abi)


def _s5_out_fwd(xr, xi, Cr, Ci, u, d):
    S, N = xr.shape
    nb, nw, _ = Cr.shape
    D = u.shape[1]
    ts = _tile(S, 512, 8)

    def body(xr_ref, xi_ref, cr_ref, ci_ref, u_ref, d_ref, y_ref, z_ref):
        y = _dot3(xr_ref[...], cr_ref[0]) - _dot3(xi_ref[...], ci_ref[0]) + d_ref[...] * u_ref[...]
        y_ref[...] = y
        z_ref[...] = _gelu(y).astype(z_ref.dtype)

    xb = pl.BlockSpec((ts, nw), lambda i, k: (i, k))
    cb = pl.BlockSpec((1, nw, _LANES), lambda i, k: (k, 0, 0))
    ub = pl.BlockSpec((ts, _LANES), lambda i, k: (i, k))
    db = pl.BlockSpec((1, _LANES), lambda i, k: (0, k))
    return _call(body, name=f"s5_out_fwd_{S}", grid=(S // ts, nb), in_specs=[xb, xb, cb, cb, ub, db],
                 out_specs=[ub, ub], out_shape=[jax.ShapeDtypeStruct((S, D), f32), jax.ShapeDtypeStruct((S, D), _MXU)],
                 compiler_params=_cp("parallel", "parallel"))(xr, xi, Cr, Ci, u, d.reshape(1, D))


def _s5_out_bwd(dz, y, u, d, Cr, Ci):
    S, D = y.shape
    nb, nw, _ = Cr.shape
    ts = _tile(S, 512, 8)

    def body(dz_ref, y_ref, u_ref, d_ref, cr_ref, ci_ref, dy_ref, gr_ref, gi_ref, du_ref, dd_ref):
        @pl.when(pl.program_id(1) == 0)
        def _():
            dd_ref[...] = jnp.zeros_like(dd_ref)

        dy = dz_ref[...] * _gelu_grad(y_ref[...])
        dy_ref[...] = dy
        gr_ref[...] = _dot3(dy, cr_ref[0], _NT)
        gi_ref[...] = -_dot3(dy, ci_ref[0], _NT)
        du_ref[...] = dy * d_ref[...]
        dd_ref[...] += jnp.sum(dy * u_ref[...], axis=0, keepdims=True)

    xb = pl.BlockSpec((ts, nw), lambda k, i: (i, k))
    cb = pl.BlockSpec((1, nw, _LANES), lambda k, i: (k, 0, 0))
    ub = pl.BlockSpec((ts, _LANES), lambda k, i: (i, k))
    db = pl.BlockSpec((1, _LANES), lambda k, i: (0, k))
    a, s = jax.ShapeDtypeStruct((S, D), f32), jax.ShapeDtypeStruct((S, nb * nw), f32)
    dy, gr, gi, du, dd = _call(
        body, name=f"s5_out_bwd_{S}", grid=(nb, S // ts), in_specs=[ub, ub, ub, db, cb, cb],
        out_specs=[ub, xb, xb, ub, db], out_shape=[a, s, s, a, jax.ShapeDtypeStruct((1, D), f32)],
        compiler_params=_cp("parallel", "arbitrary"))(dz, y, u, d.reshape(1, D), Cr, Ci)
    return dy, gr, gi, du, dd.reshape(D)


def _s5_in_bwd(gr, gi, Br, Bi, du0):
    S, N = gr.shape
    nb, _, nw = Br.shape
    ts = _tile(S, 512, 8)

    def body(gr_ref, gi_ref, br_ref, bi_ref, d0_ref, du_ref):
        du_ref[...] = d0_ref[...] + _dot3(gr_ref[...], br_ref[0], _NT) + _dot3(gi_ref[...], bi_ref[0], _NT)

    xb = pl.BlockSpec((ts, nw), lambda i, k: (i, k))
    wb = pl.BlockSpec((1, _LANES, nw), lambda i, k: (k, 0, 0))
    ub = pl.BlockSpec((ts, _LANES), lambda i, k: (i, k))
    return _call(body, name=f"s5_in_bwd_{S}", grid=(S // ts, nb), in_specs=[xb, xb, wb, wb, ub], out_specs=ub,
                 out_shape=jax.ShapeDtypeStruct((S, nb * _LANES), f32), compiler_params=_cp("parallel", "parallel"))(
                     gr, gi, Br, Bi, du0)


def _s5_wgrad(u, gr, gi, xr, xi, dy):
    S, D = u.shape
    nb = D // _LANES
    nw = gr.shape[1] // nb
    ts = _tile(S, 512, 8)

    def body(u_ref, gr_ref, gi_ref, xr_ref, xi_ref, dy_ref, dbr_ref, dbi_ref, dcr_ref, dci_ref):
        @pl.when(pl.program_id(1) == 0)
        def _():
            for r in (dbr_ref, dbi_ref, dcr_ref, dci_ref):
                r[...] = jnp.zeros_like(r)

        uv, dyv = u_ref[...], dy_ref[...]
        dbr_ref[0] += _dot3(uv, gr_ref[...], _TN)
        dbi_ref[0] += _dot3(uv, gi_ref[...], _TN)
        dcr_ref[0] += _dot3(xr_ref[...], dyv, _TN)
        dci_ref[0] -= _dot3(xi_ref[...], dyv, _TN)

    xb = pl.BlockSpec((ts, nw), lambda k, i: (i, k))
    ub = pl.BlockSpec((ts, _LANES), lambda k, i: (i, k))
    wb = pl.BlockSpec((1, _LANES, nw), lambda k, i: (k, 0, 0))
    cb = pl.BlockSpec((1, nw, _LANES), lambda k, i: (k, 0, 0))
    w, c = jax.ShapeDtypeStruct((nb, _LANES, nw), f32), jax.ShapeDtypeStruct((nb, nw, _LANES), f32)
    return _call(body, name=f"s5_wgrad_{S}", grid=(nb, S // ts), in_specs=[ub, xb, xb, xb, xb, ub],
                 out_specs=[wb, wb, cb, cb], out_shape=[w, w, c, c],
                 compiler_params=_cp("parallel", "arbitrary"))(u, gr, gi, xr, xi, dy)


def _glu_fwd(vg):
    S, D2 = vg.shape
    D = D2 // 2
    ts, tc = _tile(S, 1024, 8), _tile(D, 512)
    nc = D // tc

    def body(v_ref, g_ref, o_ref):
        o_ref[...] = v_ref[...] * _sigmoid(g_ref[...])

    return _call(body, name=f"glu_fwd_{S}", grid=(S // ts, nc),
                 in_specs=[pl.BlockSpec((ts, tc), lambda i, j: (i, j)), pl.BlockSpec((ts, tc), lambda i, j: (i, nc + j))],
                 out_specs=pl.BlockSpec((ts, tc), lambda i, j: (i, j)), out_shape=jax.ShapeDtypeStruct((S, D), f32),
                 compiler_params=_cp("parallel", "parallel"))(vg, vg)


def _glu_bwd(vg, dout):
    S, D2 = vg.shape
    D = D2 // 2
    ts, tc = _tile(S, 1024, 8), _tile(D, 512)
    nc = D // tc

    def body(v_ref, g_ref, do_ref, dv_ref, dg_ref):
        sg = _sigmoid(g_ref[...])
        do = do_ref[...]
        dv_ref[...] = (do * sg).astype(dv_ref.dtype)
        dg_ref[...] = (do * v_ref[...] * sg * (1.0 - sg)).astype(dg_ref.dtype)

    blk = pl.BlockSpec((ts, tc), lambda i, j: (i, j))
    o = jax.ShapeDtypeStruct((S, D), _MXU)
    return _call(body, name=f"glu_bwd_{S}", grid=(S // ts, nc),
                 in_specs=[blk, pl.BlockSpec((ts, tc), lambda i, j: (i, nc + j)), blk], out_specs=[blk, blk],
                 out_shape=[o, o], compiler_params=_cp("parallel", "parallel"))(vg, vg, dout)


def _blockdiag_in(b2, nb, P):
    cg = b2.shape[0]
    gb = S5_BLOCK_GROUPS
    t = b2.reshape(cg, nb, gb, P).transpose(1, 0, 2, 3)
    eye = jnp.eye(gb, dtype=b2.dtype)
    return (eye[None, :, None, :, None] * t[:, None]).reshape(nb, gb * cg, gb * P)


def _blockdiag_in_extract(db, cg, P):
    nb = db.shape[0]
    gb = S5_BLOCK_GROUPS
    eye = jnp.eye(gb, dtype=db.dtype)
    t = (db.reshape(nb, gb, cg, gb, P) * eye[None, :, None, :, None]).sum(3)
    return t.transpose(2, 0, 1, 3).reshape(cg, nb * gb * P)


def _blockdiag_out(c, nb):
    G, cg, P = c.shape
    gb = S5_BLOCK_GROUPS
    t = c.reshape(nb, gb, cg, P).transpose(0, 3, 1, 2)
    eye = jnp.eye(gb, dtype=c.dtype)
    return (eye[None, :, None, :, None] * t[:, None]).reshape(nb, gb * P, gb * cg)


def _blockdiag_out_extract(dc, cg, P):
    nb = dc.shape[0]
    gb = S5_BLOCK_GROUPS
    eye = jnp.eye(gb, dtype=dc.dtype)
    t = (dc.reshape(nb, gb, P, gb, cg) * eye[None, :, None, :, None]).sum(1)
    return t.transpose(0, 2, 3, 1).reshape(nb * gb, cg, P)


def _xa_fwd(q, kv):
    S, D = q.shape
    M = kv.shape[0]
    dh = D // XA_HEADS
    ts = _tile(S, 512, 8)
    scale = dh ** -0.5

    def body(q_ref, k_ref, v_ref, o_ref):
        s = _dot(q_ref[...], k_ref[...], _NT) * scale
        e = jnp.exp(s - jnp.max(s, axis=-1, keepdims=True))
        p = e / jnp.sum(e, axis=-1, keepdims=True)
        o_ref[...] = _dot(p.astype(_MXU), v_ref[...]).astype(o_ref.dtype)

    return _call(body, name=f"xa_fwd_{S}", grid=(S // ts, XA_HEADS),
                 in_specs=[pl.BlockSpec((ts, dh), lambda i, h: (i, h)), pl.BlockSpec((M, dh), lambda i, h: (0, h)),
                           pl.BlockSpec((M, dh), lambda i, h: (0, XA_HEADS + h))],
                 out_specs=pl.BlockSpec((ts, dh), lambda i, h: (i, h)), out_shape=jax.ShapeDtypeStruct((S, D), _MXU),
                 compiler_params=_cp("parallel", "parallel"))(q, kv, kv)


def _xa_bwd(q, kv, do):
    S, D = q.shape
    M = kv.shape[0]
    dh = D // XA_HEADS
    ts = _tile(S, 512, 8)
    scale = dh ** -0.5

    def body(q_ref, k_ref, v_ref, do_ref, dq_ref, dk_ref, dv_ref):
        @pl.when(pl.program_id(1) == 0)
        def _():
            dk_ref[...] = jnp.zeros_like(dk_ref)
            dv_ref[...] = jnp.zeros_like(dv_ref)

        qv, kv_, vv, dov = q_ref[...], k_ref[...], v_ref[...], do_ref[...]
        s = _dot(qv, kv_, _NT) * scale
        e = jnp.exp(s - jnp.max(s, axis=-1, keepdims=True))
        p = e / jnp.sum(e, axis=-1, keepdims=True)
        dp = _dot(dov, vv, _NT)
        dv_ref[...] += _dot(p.astype(_MXU), dov, _TN)
        ds = (p * (dp - jnp.sum(dp * p, axis=-1, keepdims=True)) * scale).astype(_MXU)
        dq_ref[...] = _dot(ds, kv_).astype(dq_ref.dtype)
        dk_ref[...] += _dot(ds, qv, _TN)

    qb = pl.BlockSpec((ts, dh), lambda h, i: (i, h))
    mb = pl.BlockSpec((M, dh), lambda h, i: (0, h))
    m = jax.ShapeDtypeStruct((M, D), f32)
    return _call(body, name=f"xa_bwd_{S}", grid=(XA_HEADS, S // ts),
                 in_specs=[qb, mb, pl.BlockSpec((M, dh), lambda h, i: (0, XA_HEADS + h)), qb],
                 out_specs=[qb, mb, mb], out_shape=[jax.ShapeDtypeStruct((S, D), _MXU), m, m],
                 compiler_params=_cp("parallel", "arbitrary"))(q, kv, kv, do)


def _conv(u, cw_ref, cb_ref, rows):
    return _conv_taps(u, cw_ref, cb_ref)[0]


def _conv_taps(u, cw_ref, cb_ref):
    u1, u2 = _shift_down(u, 1), _shift_down(u, 2)
    return cw_ref[2:3, :] * u + cw_ref[1:2, :] * u1 + cw_ref[0:1, :] * u2 + cb_ref[...], u1, u2


def _convglu_fwd(u, cw, cb):
    S, F2 = u.shape
    F = F2 // 2
    tc = _tile(F, 128)
    nc = F // tc

    def body(uv_ref, ug_ref, cwv_ref, cwg_ref, cbv_ref, cbg_ref, o_ref):
        rows = lax.broadcasted_iota(jnp.int32, (S, 1), 0)
        val = _conv(uv_ref[...], cwv_ref, cbv_ref, rows)
        gate = _conv(ug_ref[...], cwg_ref, cbg_ref, rows)
        o_ref[...] = (gate * _sigmoid(gate) * val).astype(o_ref.dtype)

    def col(r, off):
        return pl.BlockSpec((r, tc), lambda j: (0, off + j))

    return _call(body, name=f"convglu_fwd_{S}", grid=(nc,),
                 in_specs=[col(S, 0), col(S, nc), col(3, 0), col(3, nc), col(1, 0), col(1, nc)], out_specs=col(S, 0),
                 out_shape=jax.ShapeDtypeStruct((S, F), _MXU), compiler_params=_cp("parallel"))(
                     u, u, cw, cw, cb.reshape(1, F2), cb.reshape(1, F2))


def _convglu_bwd(u, cw, cb, dact):
    S, F2 = u.shape
    F = F2 // 2
    tc = _tile(F, 128)
    nc = F // tc

    def body(uv_ref, ug_ref, cwv_ref, cwg_ref, cbv_ref, cbg_ref, da_ref, duv_ref, dug_ref, dcwv_ref, dcwg_ref, dcbv_ref, dcbg_ref):
        rows = lax.broadcasted_iota(jnp.int32, (S, 1), 0)
        uv, ug = uv_ref[...], ug_ref[...]
        val, uv1, uv2 = _conv_taps(uv, cwv_ref, cbv_ref)
        gate, ug1, ug2 = _conv_taps(ug, cwg_ref, cbg_ref)
        sg = _sigmoid(gate)
        da = da_ref[...]
        dval = da * (gate * sg)
        dgate = da * val * (sg * (1.0 + gate * (1.0 - sg)))
        for (uu, uu1, uu2), d, cw_ref, du_ref, dcw_ref, dcb_ref in (((uv, uv1, uv2), dval, cwv_ref, duv_ref, dcwv_ref, dcbv_ref),
                                                                    ((ug, ug1, ug2), dgate, cwg_ref, dug_ref, dcwg_ref, dcbg_ref)):
            dcb_ref[...] = jnp.sum(d, axis=0, keepdims=True)
            dcw_ref[2:3, :] = jnp.sum(d * uu, axis=0, keepdims=True)
            dcw_ref[1:2, :] = jnp.sum(d * uu1, axis=0, keepdims=True)
            dcw_ref[0:1, :] = jnp.sum(d * uu2, axis=0, keepdims=True)
            du = cw_ref[2:3, :] * d + cw_ref[1:2, :] * _shift_up(d, 1, rows, S) + cw_ref[0:1, :] * _shift_up(d, 2, rows, S)
            du_ref[...] = du.astype(du_ref.dtype)

    def col(r, off):
        return pl.BlockSpec((r, tc), lambda j: (0, off + j))

    o = jax.ShapeDtypeStruct((S, F), _MXU)
    w, b = jax.ShapeDtypeStruct((3, F), f32), jax.ShapeDtypeStruct((1, F), f32)
    duv, dug, dcwv, dcwg, dcbv, dcbg = _call(
        body, name=f"convglu_bwd_{S}", grid=(nc,),
        in_specs=[col(S, 0), col(S, nc), col(3, 0), col(3, nc), col(1, 0), col(1, nc), col(S, 0)],
        out_specs=[col(S, 0), col(S, 0), col(3, 0), col(3, 0), col(1, 0), col(1, 0)],
        out_shape=[o, o, w, w, b, b], compiler_params=_cp("parallel"))(
            u, u, cw, cw, cb.reshape(1, F2), cb.reshape(1, F2), dact)
    return (jnp.concatenate([duv, dug], axis=1), jnp.concatenate([dcwv, dcwg], axis=1),
            jnp.concatenate([dcbv, dcbg], axis=1).reshape(F2))


def _final_loss(h, delta, g, target):
    S, D = h.shape
    ts = _tile(S, 512, 8)
    row = pl.BlockSpec((ts, D), lambda i: (i, 0))
    vec = pl.BlockSpec((1, D), lambda i: (0, 0))
    one = pl.BlockSpec((1, _LANES), lambda i: (0, 0))

    def body(h_ref, d_ref, g_ref, t_ref, l_ref, dh_ref, dg_ref):
        @pl.when(pl.program_id(0) == 0)
        def _():
            l_ref[...] = jnp.zeros_like(l_ref)
            dg_ref[...] = jnp.zeros_like(dg_ref)

        xv = h_ref[...] + d_ref[...]
        gv = g_ref[...]
        r = lax.rsqrt(jnp.mean(xv * xv, axis=-1, keepdims=True) + EPS)
        xh = xv * r
        err = xh * gv - t_ref[...]
        l_ref[...] += 0.5 * jnp.sum(jnp.mean(err * err, axis=-1, keepdims=True), axis=0, keepdims=True)
        dy = err * (1.0 / D)
        dg_ref[...] += jnp.sum(dy * xh, axis=0, keepdims=True)
        dxh = dy * gv
        dh_ref[...] = r * (dxh - xh * jnp.mean(dxh * xh, axis=-1, keepdims=True))

    loss, dh, dg = _call(body, name=f"final_loss_{S}", grid=(S // ts,), in_specs=[row, row, vec, row],
                         out_specs=[one, row, vec],
                         out_shape=[jax.ShapeDtypeStruct((1, _LANES), f32), jax.ShapeDtypeStruct((S, D), f32),
                                    jax.ShapeDtypeStruct((1, D), f32)],
                         compiler_params=_cp("arbitrary"))(h, delta, g.reshape(1, D), target)
    return loss[0, 0], dh, dg.reshape(D)


def _adamw(w, g, m, v):
    R, C = w.shape
    tr = _tile(R, max(8, (1 << 19) // C // 8 * 8), 8)
    blk = pl.BlockSpec((tr, C), lambda i: (i, 0))

    def body(w_ref, g_ref, m_ref, v_ref, d_ref, nm_ref, nv_ref):
        gv = g_ref[...]
        m_new = ADAM_B1 * m_ref[...] + (1.0 - ADAM_B1) * gv
        v_new = ADAM_B2 * v_ref[...] + (1.0 - ADAM_B2) * (gv * gv)
        m_hat = m_new / (1.0 - ADAM_B1 ** ADAM_STEP)
        v_hat = v_new / (1.0 - ADAM_B2 ** ADAM_STEP)
        d_ref[...] = -ADAM_LR * (m_hat / (jnp.sqrt(v_hat) + ADAM_EPS) + ADAM_WD * w_ref[...])
        nm_ref[...] = m_new
        nv_ref[...] = v_new

    o = jax.ShapeDtypeStruct((R, C), f32)
    return _call(body, name=f"adamw_{R}x{C}", grid=(R // tr,), in_specs=[blk] * 4, out_specs=[blk] * 3,
                 out_shape=[o, o, o], compiler_params=_cp("parallel"))(w, g, m, v)


def _sum_slabs(xs):
    K, R, C = xs.shape
    tr = _tile(R, 256, 8)

    def body(x_ref, o_ref):
        s = x_ref[0]
        for k in range(1, K):
            s = s + x_ref[k]
        o_ref[...] = s

    return _call(body, name=f"sum_slabs_{K}x{R}", grid=(R // tr,),
                 in_specs=[pl.BlockSpec((K, tr, C), lambda i: (0, i, 0))], out_specs=pl.BlockSpec((tr, C), lambda i: (i, 0)),
                 out_shape=jax.ShapeDtypeStruct((R, C), f32), compiler_params=_cp("parallel"))(xs)


def _slab_rows(R, C):
    return _tile(R, max(16, (1 << 19) // C // 16 * 16), 16)


def _add_partial(g, recv, core):
    _, R, C = g.shape
    tr = _slab_rows(R, C)

    def body(c_ref, a_ref, b_ref, o_ref, w_ref):
        s = a_ref[...] + b_ref[...]
        o_ref[...] = s
        w_ref[...] = s.astype(w_ref.dtype)

    blk = pl.BlockSpec((1, tr, C), lambda k, i, c: (k, i, 0))
    return _call(
        body, name=f"add_partial_{R}x{C}",
        grid_spec=pltpu.PrefetchScalarGridSpec(
            num_scalar_prefetch=1, grid=(4, R // tr),
            in_specs=[pl.BlockSpec((1, tr, C), lambda k, i, c: (2 * k + c[0], i, 0)), blk], out_specs=[blk, blk]),
        out_shape=[jax.ShapeDtypeStruct((4, R, C), f32), jax.ShapeDtypeStruct((4, R, C), _WIRE)],
        compiler_params=_cp("parallel", "parallel"))(core, g, recv)


def _sum_final(p, recv, chip):
    _, R, C = p.shape
    tr = _slab_rows(R, C)

    def body(q_ref, p_ref, r_ref, o_ref):
        o_ref[...] = ((p_ref[0] + r_ref[0].astype(f32)) + r_ref[1].astype(f32)) + r_ref[2].astype(f32)

    return _call(
        body, name=f"sum_final_{R}x{C}",
        grid_spec=pltpu.PrefetchScalarGridSpec(
            num_scalar_prefetch=1, grid=(R // tr,),
            in_specs=[pl.BlockSpec((1, tr, C), lambda i, q: (q[0], i, 0)), pl.BlockSpec((3, tr, C), lambda i, q: (0, i, 0))],
            out_specs=pl.BlockSpec((tr, C), lambda i, q: (i, 0))),
        out_shape=jax.ShapeDtypeStruct((R, C), f32), compiler_params=_cp("parallel"))(chip, p, recv)


def _my_pos():
    return lax.axis_index("x"), lax.axis_index("y"), lax.axis_index("c")


def _all_gather(shards):
    n = len(shards)

    def body(*refs):
        start, forward, finish = _all_gather_stages(refs[:n], refs[n:2 * n], *refs[2 * n:])
        start()
        forward()
        finish()

    tag = "_".join(f"{s.shape[0]}x{s.shape[1]}" for s in shards)
    return _call(body, name=f"all_gather_{tag}", in_specs=[ANY] * n, out_specs=[ANY] * n,
                 out_shape=_all_gather_shapes(shards), scratch_shapes=_all_gather_sems(n))(*shards)


def _all_gather_shapes(shards):
    return [jax.ShapeDtypeStruct((NDEV,) + s.shape, s.dtype) for s in shards]


def _all_gather_sems(n):
    return [pltpu.SemaphoreType.DMA((7, n)), pltpu.SemaphoreType.DMA((7, n)), pltpu.SemaphoreType.DMA((n,))]


def _all_gather_stages(x_refs, out_refs, send_sems, recv_sems, local_sems):
    n = len(x_refs)
    x, y, c = _my_pos()
    me, sibling = (x, y, c), (x, y, 1 - c)
    chips = [(1 - x, y), (x, 1 - y), (1 - x, 1 - y)]

    def slab(t, px, py, pc):
        return out_refs[t].at[4 * px + 2 * py + pc]

    def copy(k, t, block, to, from_input=False):
        return pltpu.make_async_remote_copy(
            src_ref=x_refs[t] if from_input else slab(t, *block), dst_ref=slab(t, *block),
            send_sem=send_sems.at[k, t], recv_sem=recv_sems.at[k, t], device_id=to, device_id_type=MESH_ID)

    mine = [pltpu.make_async_copy(x_refs[t], slab(t, *me), local_sems.at[t]) for t in range(n)]
    first = [copy(0, t, me, sibling, True) for t in range(n)]
    first += [copy(1 + j, t, me, (*chip, c), True) for j, chip in enumerate(chips) for t in range(n)]
    passed = [copy(4 + j, t, (*chip, c), sibling) for j, chip in enumerate(chips) for t in range(n)]

    def start():
        for cp in mine + first:
            cp.start()

    def forward():
        for j, chip in enumerate(chips):
            for t in range(n):
                copy(1 + j, t, (*chip, c), me).wait_recv()
                passed[j * n + t].start()

    def finish():
        for t in range(n):
            copy(0, t, sibling, me).wait_recv()
        for j, chip in enumerate(chips):
            for t in range(n):
                copy(4 + j, t, (*chip, 1 - c), me).wait_recv()
        for cp in first + passed:
            cp.wait_send()
        for cp in mine:
            cp.wait()

    return start, forward, finish


def _exchange_cores(gs):
    n = len(gs)

    def body(*refs):
        g_refs, out_refs = refs[:n], refs[n:2 * n]
        send_sems, recv_sems = refs[2 * n:]
        x, y, c = _my_pos()
        cps = [pltpu.make_async_remote_copy(src_ref=g_refs[t].at[2 * q + (1 - c)], dst_ref=out_refs[t].at[q],
                                            send_sem=send_sems.at[q, t], recv_sem=recv_sems.at[q, t],
                                            device_id=(x, y, 1 - c), device_id_type=MESH_ID)
               for t in range(n) for q in range(4)]
        for cp in cps:
            cp.start()
        for cp in cps:
            cp.wait()

    tag = "_".join(f"{g.shape[1]}x{g.shape[2]}" for g in gs)
    return _call(body, name=f"exchange_cores_{tag}", in_specs=[ANY] * n, out_specs=[ANY] * n,
                 out_shape=[jax.ShapeDtypeStruct((4,) + g.shape[1:], g.dtype) for g in gs],
                 scratch_shapes=[pltpu.SemaphoreType.DMA((4, n)), pltpu.SemaphoreType.DMA((4, n))])(*gs)


def _exchange_chips(ps):
    n = len(ps)

    def body(*refs):
        start, finish = _exchange_chips_stages(refs[:n], refs[n:2 * n], *refs[2 * n:])
        start()
        finish()

    tag = "_".join(f"{p.shape[1]}x{p.shape[2]}" for p in ps)
    return _call(body, name=f"exchange_chips_{tag}", in_specs=[ANY] * n, out_specs=[ANY] * n,
                 out_shape=_exchange_chips_shapes(ps), scratch_shapes=_exchange_chips_sems(n))(*ps)


def _exchange_chips_shapes(ps):
    return [jax.ShapeDtypeStruct((3,) + p.shape[1:], p.dtype) for p in ps]


def _exchange_chips_sems(n):
    return [pltpu.SemaphoreType.DMA((3, n)), pltpu.SemaphoreType.DMA((3, n))]


def _exchange_chips_stages(p_refs, out_refs, send_sems, recv_sems):
    n = len(p_refs)
    x, y, c = _my_pos()
    chips = [(x, 1 - y), (1 - x, y), (1 - x, 1 - y)]
    cps = [pltpu.make_async_remote_copy(src_ref=p_refs[t].at[2 * px + py], dst_ref=out_refs[t].at[r],
                                        send_sem=send_sems.at[r, t], recv_sem=recv_sems.at[r, t],
                                        device_id=(px, py, c), device_id_type=MESH_ID)
           for t in range(n) for r, (px, py) in enumerate(chips)]

    def start():
        for cp in cps:
            cp.start()

    def finish():
        for cp in cps:
            cp.wait()

    return start, finish


def _exchange_all_shapes(gs):
    return [jax.ShapeDtypeStruct((NDEV - 1,) + g.shape[1:], g.dtype) for g in gs]


def _exchange_all_sems(n):
    return [pltpu.SemaphoreType.DMA((NDEV - 1, n)), pltpu.SemaphoreType.DMA((NDEV - 1, n))]


def _exchange_all_stages(g_refs, out_refs, send_sems, recv_sems):
    n = len(g_refs)
    x, y, c = _my_pos()
    cps = []
    for t in range(n):
        for r in range(1, NDEV):
            px, py, pc = (1 - x if r & 4 else x), (1 - y if r & 2 else y), (1 - c if r & 1 else c)
            cps.append(pltpu.make_async_remote_copy(
                src_ref=g_refs[t].at[4 * px + 2 * py + pc], dst_ref=out_refs[t].at[r - 1],
                send_sem=send_sems.at[r - 1, t], recv_sem=recv_sems.at[r - 1, t],
                device_id=(px, py, pc), device_id_type=MESH_ID))

    def start():
        for cp in cps:
            cp.start()

    def finish():
        for cp in cps:
            cp.wait()

    return start, finish


def _sum_all(g, recv, dev):
    _, R, C = g.shape
    tr = _slab_rows(R, C)

    def body(d_ref, g_ref, r_ref, o_ref):
        s = g_ref[0].astype(f32)
        for k in range(NDEV - 1):
            s = s + r_ref[k].astype(f32)
        o_ref[...] = s

    return _call(
        body, name=f"sum_all_{R}x{C}",
        grid_spec=pltpu.PrefetchScalarGridSpec(
            num_scalar_prefetch=1, grid=(R // tr,),
            in_specs=[pl.BlockSpec((1, tr, C), lambda i, d: (d[0], i, 0)),
                      pl.BlockSpec((NDEV - 1, tr, C), lambda i, d: (0, i, 0))],
            out_specs=pl.BlockSpec((tr, C), lambda i, d: (i, 0))),
        out_shape=jax.ShapeDtypeStruct((R, C), f32), compiler_params=_cp("parallel"))(dev, g, recv)


def _pack(arrs, dtype, rows_mult):
    flat = jnp.concatenate([a.astype(dtype).reshape(-1) for a in arrs])
    q = rows_mult * _PACK_COLS
    tot = -(-flat.shape[0] // q) * q
    return jnp.pad(flat, (0, tot - flat.shape[0])).reshape(tot // _PACK_COLS, _PACK_COLS)


def _unpack(flat, shapes):
    out, off = [], 0
    for s in shapes:
        n = math.prod(s)
        out.append(flat[..., off:off + n].reshape(flat.shape[:-1] + tuple(s)))
        off += n
    return out


def _width_groups(names, shapes):
    groups = {}
    for n in names:
        groups.setdefault(shapes[n][-1], []).append(n)
    return list(groups.values())


def _to_full(piece, ax):
    t = jnp.moveaxis(piece, 0, ax)
    return t.reshape(t.shape[:ax] + (t.shape[ax] * t.shape[ax + 1],) + t.shape[ax + 2:])


def _to_shards(g, ax):
    t = g.reshape(g.shape[:ax] + (NDEV, g.shape[ax] // NDEV) + g.shape[ax + 1:])
    return jnp.moveaxis(t, ax, 0)


def _split_rows(buf, shapes):
    out, off = [], 0
    for s in shapes:
        r = math.prod(s[:-1])
        out.append(buf[..., off:off + r, :].reshape(buf.shape[:-2] + tuple(s)))
        off += r
    return out


def _item_ax(item):
    return SHARD_AXIS[item[0]] - 1


def _gather_bufs(shards, items):
    shapes = {it: shards[it[0]].shape[1:] for it in items}
    groups = _width_groups(items, shapes)
    bufs = [jnp.concatenate([shards[n][l].astype(_MXU).reshape(-1, shapes[(n, l)][-1]) for n, l in grp], axis=0)
            for grp in groups]
    return groups, bufs


def _gather_unpack(shards, groups, outs):
    full = {}
    for grp, g in zip(groups, outs):
        for it, piece in zip(grp, _split_rows(g, [shards[it[0]].shape[1:] for it in grp])):
            full[it] = _to_full(piece, _item_ax(it))
    return full


def _scatter_bufs(grads, shards, items, dtype=f32):
    shapes = {it: shards[it[0]].shape[1:] for it in items}
    groups = _width_groups(items, shapes)
    bufs = [jnp.concatenate([_to_shards(grads[it], _item_ax(it)).astype(dtype).reshape(NDEV, -1, shapes[it][-1])
                             for it in grp], axis=1) for grp in groups]
    bufs = [jnp.pad(b, ((0, 0), (0, -b.shape[1] % 256), (0, 0))) for b in bufs]
    return groups, bufs


def _mesh_scalars():
    core = jnp.reshape(lax.axis_index("c"), (1,)).astype(jnp.int32)
    chip = jnp.reshape(2 * lax.axis_index("x") + lax.axis_index("y"), (1,)).astype(jnp.int32)
    return core, chip


def _chip_sums(bufs):
    core, _ = _mesh_scalars()
    return [_add_partial(b, r, core) for b, r in zip(bufs, _exchange_cores(bufs))]


def _device_sums(chip_sums, from_chips):
    _, chip = _mesh_scalars()
    return [_sum_final(p, r, chip) for (p, _), r in zip(chip_sums, from_chips)]


def _scatter_unpack(shards, groups, totals):
    out = {}
    for grp, t in zip(groups, totals):
        out.update(zip(grp, _split_rows(t, [shards[it[0]].shape[1:] for it in grp])))
    return out


def _small_scatter_buf(grads, names):
    small = jnp.concatenate([_to_shards(grads[n], SHARD_AXIS[n]).reshape(NDEV, -1) for n in names], axis=1)
    q = 16 * _PACK_COLS
    tot = -(-small.shape[1] // q) * q
    return jnp.pad(small, ((0, 0), (0, tot - small.shape[1]))).reshape(NDEV, tot // _PACK_COLS, _PACK_COLS)


def _all_reduce_small(grads, names):
    shapes = [grads[n].shape for n in names]
    packed = _pack([grads[n] for n in names], f32, 8)
    total = _sum_slabs(_all_gather([packed])[0])
    return dict(zip(names, _unpack(total.reshape(-1), shapes)))


def _as2d(a):
    if a.ndim == 1:
        return a.reshape(1, -1)
    return a.reshape(-1, a.shape[-1])


def kernel(x, mem, mix_norm_g, pool_w, pool_scale, sb_w_qkv, sb_w_o, s5_a_re, s5_a_im, s5_log_dt, s5_b_re, s5_b_im, s5_c_re, s5_c_im, s5_d, s5_w_glu, xa_norm_g, mem_norm_g, xa_wq, xa_wkv, xa_wo, ffn_norm_g, ffn_w_up, ffn_conv_w, ffn_conv_b, ffn_w_down, final_norm_g, loss_target, m_mix_norm_g, m_pool_w, m_pool_scale, m_sb_w_qkv, m_sb_w_o, m_s5_a_re, m_s5_a_im, m_s5_log_dt, m_s5_b_re, m_s5_b_im, m_s5_c_re, m_s5_c_im, m_s5_d, m_s5_w_glu, m_xa_norm_g, m_mem_norm_g, m_xa_wq, m_xa_wkv, m_xa_wo, m_ffn_norm_g, m_ffn_w_up, m_ffn_conv_w, m_ffn_conv_b, m_ffn_w_down, m_final_norm_g, v_mix_norm_g, v_pool_w, v_pool_scale, v_sb_w_qkv, v_sb_w_o, v_s5_a_re, v_s5_a_im, v_s5_log_dt, v_s5_b_re, v_s5_b_im, v_s5_c_re, v_s5_c_im, v_s5_d, v_s5_w_glu, v_xa_norm_g, v_mem_norm_g, v_xa_wq, v_xa_wkv, v_xa_wo, v_ffn_norm_g, v_ffn_w_up, v_ffn_conv_w, v_ffn_conv_b, v_ffn_w_down, v_final_norm_g):
    args = locals()
    w = {n: args[n] for n in WEIGHTS}
    mom = {n: args["m_" + n] for n in WEIGHTS}
    var = {n: args["v_" + n] for n in WEIGHTS}
    h0, memv, target = x[0], mem[0], loss_target[0]
    S, D = h0.shape
    depth = mix_norm_g.shape[0]
    n_mix = 3

    first_layer = {'pool_w': 0, 'sb_w_qkv': 1, 'sb_w_o': 1, 's5_w_glu': 2}

    def layer_of(item):
        return first_layer[item[0]] + n_mix * item[1] if item[0] in first_layer else item[1]

    sb_layer = 1 if depth > 1 else None
    items = [(n, l) for n in MXU_WEIGHTS for l in range(w[n].shape[0])]
    early_w = [it for it in items if sb_layer is None or layer_of(it) < sb_layer or it == ('sb_w_qkv', 0)]
    late_w = [it for it in items if it not in early_w]
    groups_a, bufs_a = _gather_bufs(w, early_w)
    outs_a = _all_gather(bufs_a + [_pack([w[n] for n in VEC_WEIGHTS], f32, 8)])
    wfull = _gather_unpack(w, groups_a, outs_a[:-1])
    full = dict(w)
    for n, piece in zip(VEC_WEIGHTS, _unpack(outs_a[-1].reshape(NDEV, -1), [w[n].shape for n in VEC_WEIGHTS])):
        full[n] = _to_full(piece, SHARD_AXIS[n])
    groups_b, bufs_b = _gather_bufs(w, late_w)

    grads = {}

    def acc(name, j, val):
        grads.setdefault(name, {})[j] = val

    s5 = []
    for j in range(s5_a_re.shape[0]):
        G, P = s5_a_re.shape[1:]
        cg = s5_b_re.shape[3]
        N = G * P
        nb = D // _LANES
        ar, ai = s5_a_re[j].reshape(1, N), s5_a_im[j].reshape(1, N)
        ldt = jnp.repeat(s5_log_dt[j], P).reshape(1, N)
        br = s5_b_re[j].transpose(2, 0, 1).reshape(cg, N)
        bi = s5_b_im[j].transpose(2, 0, 1).reshape(cg, N)
        abr, abi, bbr, bbi = _s5_prep_fwd(ar, ai, ldt, br, bi)
        s5.append(dict(ar=ar, ai=ai, ldt=ldt, br=br, bi=bi, abr=abr, abi=abi, G=G, P=P, cg=cg, N=N, nb=nb,
                       Br=_blockdiag_in(bbr, nb, P), Bi=_blockdiag_in(bbi, nb, P),
                       Cr=_blockdiag_out(s5_c_re[j], nb), Ci=_blockdiag_out(s5_c_im[j], nb)))

    saved = []
    h, delta = h0, None
    for i in range(depth):
        kind, j = i % n_mix, i // n_mix
        sv = dict(kind=kind, j=j)
        if i == 0:
            hn = _rms_fwd(h, full['mix_norm_g'][i], out_dtype=_MXU if kind == 1 else f32)
        sv['h'] = h
        if kind == 0:
            p = _pool_fwd(hn)
            t = _pool_mix_fwd(p, wfull['pool_w', j], full['pool_scale'][j])
            sv.update(p=p)
        elif kind == 1:
            qkv = _mm(hn, wfull['sb_w_qkv', j], out_dtype=_MXU)
            if i == sb_layer:
                o, outs_b = _sb_fwd(qkv, bufs_b)
                wfull.update(_gather_unpack(w, groups_b, outs_b))
            else:
                o, _ = _sb_fwd(qkv)
            h1, a = _mm_add_rms(o, wfull['sb_w_o', j], h, full['xa_norm_g'][i], out_dtype=_MXU)
            sv.update(hn=hn, qkv=qkv, o=o)
        else:
            pr = s5[j]
            bur, bui = _s5_in_fwd(hn, pr['Br'], pr['Bi'])
            xr, xi = _s5_scan_fwd(bur, bui, pr['abr'], pr['abi'])
            y, z = _s5_out_fwd(xr, xi, pr['Cr'], pr['Ci'], hn, full['s5_d'][j])
            vg = _mm(z, wfull['s5_w_glu', j])
            t = _glu_fwd(vg)
            sv.update(hn=hn, xr=xr, xi=xi, y=y, z=z, vg=vg)
        if kind != 1:
            h1, a = _rms_fwd(h, full['xa_norm_g'][i], t, out_dtype=_MXU)
        memn = _rms_fwd(memv, full['mem_norm_g'][i], out_dtype=_MXU)
        q = _mm(a, wfull['xa_wq', i], out_dtype=_MXU)
        kv = _mm(memn, wfull['xa_wkv', i], out_dtype=_MXU)
        o2 = _xa_fwd(q, kv)
        h2, b = _mm_add_rms(o2, wfull['xa_wo', i], h1, full['ffn_norm_g'][i], out_dtype=_MXU)
        u = _mm(b, wfull['ffn_w_up', i])
        act = _convglu_fwd(u, full['ffn_conv_w'][i], full['ffn_conv_b'][i])
        sv.update(h1=h1, a=a, memn=memn, q=q, kv=kv, o2=o2, h2=h2, b=b, u=u, act=act)
        saved.append(sv)
        if i + 1 < depth:
            h, hn = _mm_add_rms(act, wfull['ffn_w_down', i], h2, full['mix_norm_g'][i + 1],
                                out_dtype=_MXU if (i + 1) % n_mix == 1 else f32)
        else:
            h, delta = h2, _mm(act, wfull['ffn_w_down', i])

    loss_local, dh, dg_final = _final_loss(h, delta, full['final_norm_g'], target)
    loss = lax.psum(loss_local, ("x", "y", "c"))
    grads['final_norm_g'] = dg_final

    for i in reversed(range(depth)):
        sv = saved[i]
        kind, j = sv['kind'], sv['j']
        dact = _mm(dh, wfull['ffn_w_down', i], tb=True)
        gdt = _WIRE if sb_layer is not None and i >= sb_layer else f32
        acc('ffn_w_down', i, _mm(sv['act'], dh, ta=True, out_dtype=gdt))
        du, dcw, dcb = _convglu_bwd(sv['u'], full['ffn_conv_w'][i], full['ffn_conv_b'][i], dact)
        acc('ffn_conv_w', i, dcw)
        acc('ffn_conv_b', i, dcb)
        db = _mm(du, wfull['ffn_w_up', i], tb=True)
        acc('ffn_w_up', i, _mm(sv['b'], du, ta=True, out_dtype=gdt))
        dh2, dg = _rms_bwd(sv['h2'], full['ffn_norm_g'][i], db, dh)
        acc('ffn_norm_g', i, dg)
        do2 = _mm(dh2, wfull['xa_wo', i], tb=True, out_dtype=_MXU)
        acc('xa_wo', i, _mm(sv['o2'], dh2, ta=True, out_dtype=gdt))
        dq, dk, dv = _xa_bwd(sv['q'], sv['kv'], do2)
        dkv = jnp.concatenate([dk, dv], axis=1)
        da = _mm(dq, wfull['xa_wq', i], tb=True)
        acc('xa_wq', i, _mm(sv['a'], dq, ta=True, out_dtype=gdt))
        dmemn = _mm(dkv, wfull['xa_wkv', i], tb=True)
        acc('xa_wkv', i, _mm(sv['memn'], dkv, ta=True, out_dtype=gdt))
        _, dg = _rms_bwd(memv, full['mem_norm_g'][i], dmemn)
        acc('mem_norm_g', i, dg)
        dh1, dg = _rms_bwd(sv['h1'], full['xa_norm_g'][i], da, dh2)
        acc('xa_norm_g', i, dg)
        if kind == 0:
            dp, dw, ds = _pool_mix_bwd(sv['p'], wfull['pool_w', j], full['pool_scale'][j], dh1)
            acc('pool_w', j, dw)
            acc('pool_scale', j, ds)
            dhn = _pool_bwd(dp)
        elif kind == 1:
            do = _mm(dh1, wfull['sb_w_o', j], tb=True)
            acc('sb_w_o', j, _mm(sv['o'], dh1, ta=True, out_dtype=gdt))
            if i == sb_layer:
                early_g = [it for it in items if it != ('sb_w_qkv', j) and layer_of(it) >= sb_layer]
                groups_e, wire_e = _scatter_bufs({it: grads[it[0]][it[1]] for it in early_g}, w, early_g, _WIRE)
                dqs, dks, dvs, recv_e = _sb_bwd(sv['qkv'], sv['o'], do, wire_e)
            else:
                dqs, dks, dvs, _ = _sb_bwd(sv['qkv'], sv['o'], do)
            dqkv = jnp.concatenate([dqs, dks, dvs], axis=1)
            dhn = _mm(dqkv, wfull['sb_w_qkv', j], tb=True)
            acc('sb_w_qkv', j, _mm(sv['hn'], dqkv, ta=True))
        else:
            pr = s5[j]
            dval, dgate = _glu_bwd(sv['vg'], dh1)
            dvg = jnp.concatenate([dval, dgate], axis=1)
            dz = _mm(dvg, wfull['s5_w_glu', j], tb=True)
            acc('s5_w_glu', j, _mm(sv['z'], dvg, ta=True, out_dtype=gdt))
            dy, gdr, gdi, du0, dd = _s5_out_bwd(dz, sv['y'], sv['hn'], full['s5_d'][j], pr['Cr'], pr['Ci'])
            acc('s5_d', j, dd)
            gr, gi, dabr, dabi = _s5_scan_bwd(gdr, gdi, sv['xr'], sv['xi'], pr['abr'], pr['abi'])
            dhn = _s5_in_bwd(gr, gi, pr['Br'], pr['Bi'], du0)
            dBr, dBi, dCr, dCi = _s5_wgrad(sv['hn'], gr, gi, sv['xr'], sv['xi'], dy)
            cg, P, G = pr['cg'], pr['P'], pr['G']
            acc('s5_c_re', j, _blockdiag_out_extract(dCr, cg, P))
            acc('s5_c_im', j, _blockdiag_out_extract(dCi, cg, P))
            dar, dai, dldt, dbr, dbi = _s5_prep_bwd(pr['ar'], pr['ai'], pr['ldt'], pr['br'], pr['bi'], dabr, dabi,
                                                    _blockdiag_in_extract(dBr, cg, P), _blockdiag_in_extract(dBi, cg, P))
            acc('s5_a_re', j, dar.reshape(G, P))
            acc('s5_a_im', j, dai.reshape(G, P))
            acc('s5_log_dt', j, dldt.reshape(G, P).sum(axis=1))
            acc('s5_b_re', j, dbr.reshape(cg, G, P).transpose(1, 2, 0))
            acc('s5_b_im', j, dbi.reshape(cg, G, P).transpose(1, 2, 0))
        dh, dg = _rms_bwd(sv['h'], full['mix_norm_g'][i], dhn, dh1)
        acc('mix_norm_g', i, dg)
    grad_x = dh[None]

    gfull = {}
    for n in VEC_WEIGHTS + REPLICATED:
        gfull[n] = grads[n] if n == 'final_norm_g' else jnp.stack([grads[n][k] for k in range(len(grads[n]))])
    if sb_layer is None:
        early_g, local = [], {}
    else:
        dev = jnp.reshape(4 * lax.axis_index("x") + 2 * lax.axis_index("y") + lax.axis_index("c"), (1,)).astype(jnp.int32)
        local = _scatter_unpack(w, groups_e, [_sum_all(g, r, dev) for g, r in zip(wire_e, recv_e)])
    late_g = [it for it in items if it not in early_g]
    groups_l, bufs_l = _scatter_bufs({it: grads[it[0]][it[1]] for it in late_g}, w, late_g)
    sums_l = _chip_sums(bufs_l + [_small_scatter_buf(gfull, VEC_WEIGHTS)])
    totals_l = _device_sums(sums_l, _exchange_chips([wire for _, wire in sums_l]))
    local.update(_scatter_unpack(w, groups_l, totals_l[:-1]))
    gw = {n: jnp.stack([local[(n, l)] for l in range(w[n].shape[0])]) for n in MXU_WEIGHTS}
    gw.update(zip(VEC_WEIGHTS, _unpack(totals_l[-1].reshape(-1), [w[n].shape for n in VEC_WEIGHTS])))
    gw.update(_all_reduce_small(gfull, REPLICATED))

    deltas, new_m, new_v = {}, {}, {}
    for n in WEIGHTS:
        shp = w[n].shape
        d_, m_, v_ = _adamw(_as2d(w[n]), _as2d(gw[n]), _as2d(mom[n]), _as2d(var[n]))
        deltas[n], new_m[n], new_v[n] = d_.reshape(shp), m_.reshape(shp), v_.reshape(shp)

    return (loss, grad_x, *[gw[n] for n in WEIGHTS], *[deltas[n] for n in WEIGHTS],
            *[new_m[n] for n in WEIGHTS], *[new_v[n] for n in WEIGHTS])
```

```python
import functools
import math

import jax
import jax.numpy as jnp
from jax import lax
from jax.experimental import pallas as pl
from jax.experimental.pallas import tpu as pltpu

f32 = jnp.float32
_MXU = jnp.bfloat16
_WIRE = _MXU
_VMEM_LIMIT = 48 * 1024 * 1024
_LANES = 128
_PACK_COLS = 1024

NDEV = 8
EPS = 1e-6
POOL_WINDOWS = (2, 4, 8, 16)
SB_HEAD_DIM = 64
XA_HEADS = 4
S5_GROUP = 16
S5_BLOCK_GROUPS = _LANES // S5_GROUP
ADAM_LR, ADAM_B1, ADAM_B2, ADAM_EPS, ADAM_WD, ADAM_STEP = 0.001, 0.9, 0.999, 1e-08, 0.01, 10

WEIGHTS = ['mix_norm_g', 'pool_w', 'pool_scale', 'sb_w_qkv', 'sb_w_o', 's5_a_re', 's5_a_im', 's5_log_dt',
           's5_b_re', 's5_b_im', 's5_c_re', 's5_c_im', 's5_d', 's5_w_glu', 'xa_norm_g', 'mem_norm_g', 'xa_wq',
           'xa_wkv', 'xa_wo', 'ffn_norm_g', 'ffn_w_up', 'ffn_conv_w', 'ffn_conv_b', 'ffn_w_down', 'final_norm_g']
SHARD_AXIS = {'pool_w': 2, 'pool_scale': 1, 'sb_w_qkv': 2, 'sb_w_o': 1, 's5_d': 1, 's5_w_glu': 2, 'xa_wq': 1,
              'xa_wkv': 2, 'xa_wo': 1, 'ffn_w_up': 2, 'ffn_conv_w': 2, 'ffn_w_down': 1}
MXU_WEIGHTS = ['pool_w', 'sb_w_qkv', 'sb_w_o', 's5_w_glu', 'xa_wq', 'xa_wkv', 'xa_wo', 'ffn_w_up', 'ffn_w_down']
VEC_WEIGHTS = ['pool_scale', 's5_d', 'ffn_conv_w']
REPLICATED = [n for n in WEIGHTS if n not in SHARD_AXIS]

_NN = (((1,), (0,)), ((), ()))
_NT = (((1,), (1,)), ((), ()))
_TN = (((0,), (0,)), ((), ()))
MESH_ID = pl.DeviceIdType.MESH
ANY = pl.BlockSpec(memory_space=pl.ANY)


def _call(body, **kw):
    return pl.pallas_call(body, **kw)


def _cp(*sem):
    return pltpu.CompilerParams(dimension_semantics=sem, vmem_limit_bytes=_VMEM_LIMIT)


def _tile(n, target, mult=_LANES):
    if n <= target:
        return n
    t = (target // mult) * mult
    while t >= mult:
        if n % t == 0:
            return t
        t -= mult
    return n


def _dot(a, b, dims=_NN):
    return lax.dot_general(a, b, dims, preferred_element_type=f32)


def _split(a):
    hi = a.astype(_MXU)
    lo = (a - hi.astype(f32)).astype(_MXU)
    return hi, lo


def _dot_hilo(a, u, dims=_NN):
    hi, lo = _split(a)
    return _dot(hi, u, dims) + _dot(lo, u, dims)


def _dot3(a, b, dims=_NN):
    ah, al = _split(a)
    bh, bl = _split(b)
    return _dot(ah, bh, dims) + _dot(al, bh, dims) + _dot(ah, bl, dims)


def _sigmoid(x):
    return 0.5 * jnp.tanh(0.5 * x) + 0.5


_GELU_C = math.sqrt(2.0 / math.pi)


def _gelu(x):
    return x * (0.5 * (1.0 + jnp.tanh(_GELU_C * (x + 0.044715 * (x * x * x)))))


def _gelu_grad(x):
    t = jnp.tanh(_GELU_C * (x + 0.044715 * (x * x * x)))
    return 0.5 * (1.0 + t) + x * 0.5 * (1.0 - t * t) * _GELU_C * (1.0 + 3.0 * 0.044715 * x * x)


_SUBLANES = 8


def _shift_down(x, k, rows=None):
    r = pltpu.roll(x, k, 0)
    head = jnp.where(lax.broadcasted_iota(jnp.int32, (_SUBLANES, 1), 0) >= k, r[:_SUBLANES], 0.0)
    return jnp.concatenate([head, r[_SUBLANES:]], axis=0)


def _shift_up(x, k, rows, n):
    r = pltpu.roll(x, n - k, 0)
    tail = jnp.where(lax.broadcasted_iota(jnp.int32, (_SUBLANES, 1), 0) < _SUBLANES - k, r[n - _SUBLANES:], 0.0)
    return jnp.concatenate([r[:n - _SUBLANES], tail], axis=0)


def _mm(a, b, *, ta=False, tb=False, out_dtype=f32):
    M, K = (a.shape[1], a.shape[0]) if ta else a.shape
    N = b.shape[0] if tb else b.shape[1]
    tm, tn, tk = _tile(M, 1408), _tile(N, 1536), _tile(K, 1408)
    nk = K // tk
    a_spec = pl.BlockSpec((tk, tm), lambda i, j, k: (k, i)) if ta else pl.BlockSpec((tm, tk), lambda i, j, k: (i, k))
    b_spec = pl.BlockSpec((tn, tk), lambda i, j, k: (j, k)) if tb else pl.BlockSpec((tk, tn), lambda i, j, k: (k, j))
    dims = (((0 if ta else 1,), (1 if tb else 0,)), ((), ()))

    def body(a_ref, b_ref, o_ref, acc_ref):
        k = pl.program_id(2)

        @pl.when(k == 0)
        def _():
            acc_ref[...] = jnp.zeros_like(acc_ref)

        acc_ref[...] += _dot(a_ref[...].astype(_MXU), b_ref[...].astype(_MXU), dims)

        @pl.when(k == nk - 1)
        def _():
            o_ref[...] = acc_ref[...].astype(out_dtype)

    return _call(
        body, name=f"mm_{'t' if ta else 'n'}{'t' if tb else 'n'}_{M}x{K}x{N}",
        grid=(M // tm, N // tn, nk), in_specs=[a_spec, b_spec],
        out_specs=pl.BlockSpec((tm, tn), lambda i, j, k: (i, j)),
        out_shape=jax.ShapeDtypeStruct((M, N), out_dtype),
        scratch_shapes=[pltpu.VMEM((tm, tn), f32)],
        compiler_params=_cp("parallel", "parallel", "arbitrary"))(a, b)


def _mm_add_rms(a, b, res, g, *, out_dtype):
    M, K = a.shape
    N = b.shape[1]
    tm, tk = _tile(M, 512), _tile(K, 1408)
    nk = K // tk
    row = pl.BlockSpec((tm, N), lambda i, k: (i, 0))

    def body(a_ref, b_ref, r_ref, g_ref, s_ref, y_ref, acc_ref):
        k = pl.program_id(1)

        @pl.when(k == 0)
        def _():
            acc_ref[...] = jnp.zeros_like(acc_ref)

        acc_ref[...] += _dot(a_ref[...].astype(_MXU), b_ref[...].astype(_MXU))

        @pl.when(k == nk - 1)
        def _():
            xs = r_ref[...] + acc_ref[...]
            s_ref[...] = xs
            r = lax.rsqrt(jnp.mean(xs * xs, axis=-1, keepdims=True) + EPS)
            y_ref[...] = ((xs * r) * g_ref[...]).astype(out_dtype)

    return _call(
        body, name=f"mm_add_rms_{M}x{K}x{N}", grid=(M // tm, nk),
        in_specs=[pl.BlockSpec((tm, tk), lambda i, k: (i, k)), pl.BlockSpec((tk, N), lambda i, k: (k, 0)), row,
                  pl.BlockSpec((1, N), lambda i, k: (0, 0))],
        out_specs=[row, row], out_shape=[jax.ShapeDtypeStruct((M, N), f32), jax.ShapeDtypeStruct((M, N), out_dtype)],
        scratch_shapes=[pltpu.VMEM((tm, N), f32)],
        compiler_params=_cp("parallel", "arbitrary"))(a, b, res, g.reshape(1, N))


def _mm_rms_bwd(a, b, x, g, dres):
    M, K = a.shape
    N = b.shape[0]
    tm, tk = _tile(M, 512), _tile(K, 1408)
    nk = K // tk
    row = pl.BlockSpec((tm, N), lambda i, k: (i, 0))
    vec = pl.BlockSpec((1, N), lambda i, k: (0, 0))

    def body(a_ref, b_ref, x_ref, g_ref, r_ref, dx_ref, dg_ref, acc_ref):
        i, k = pl.program_id(0), pl.program_id(1)

        @pl.when((i == 0) & (k == 0))
        def _():
            dg_ref[...] = jnp.zeros_like(dg_ref)

        @pl.when(k == 0)
        def _():
            acc_ref[...] = jnp.zeros_like(acc_ref)

        acc_ref[...] += _dot(a_ref[...].astype(_MXU), b_ref[...].astype(_MXU), _NT)

        @pl.when(k == nk - 1)
        def _():
            xv, dyv = x_ref[...], acc_ref[...]
            r = lax.rsqrt(jnp.mean(xv * xv, axis=-1, keepdims=True) + EPS)
            xh = xv * r
            dg_ref[...] += jnp.sum(dyv * xh, axis=0, keepdims=True)
            dxh = dyv * g_ref[...]
            dx_ref[...] = r * (dxh - xh * jnp.mean(dxh * xh, axis=-1, keepdims=True)) + r_ref[...]

    dx, dg = _call(
        body, name=f"mm_rms_bwd_{M}x{K}x{N}", grid=(M // tm, nk),
        in_specs=[pl.BlockSpec((tm, tk), lambda i, k: (i, k)), pl.BlockSpec((N, tk), lambda i, k: (0, k)), row, vec, row],
        out_specs=[row, vec], out_shape=[jax.ShapeDtypeStruct((M, N), f32), jax.ShapeDtypeStruct((1, N), f32)],
        scratch_shapes=[pltpu.VMEM((tm, N), f32)],
        compiler_params=_cp("arbitrary", "arbitrary"))(a, b, x, g.reshape(1, N), dres)
    return dx, dg.reshape(N)


def _rms_fwd(x, g, delta=None, *, out_dtype=f32):
    S, D = x.shape
    ts = _tile(S, 512, 8)
    row = pl.BlockSpec((ts, D), lambda i: (i, 0))
    vec = pl.BlockSpec((1, D), lambda i: (0, 0))
    g2 = g.reshape(1, D)

    def norm(xv, g_ref):
        r = lax.rsqrt(jnp.mean(xv * xv, axis=-1, keepdims=True) + EPS)
        return ((xv * r) * g_ref[...]).astype(out_dtype)

    if delta is None:
        def body(x_ref, g_ref, y_ref):
            y_ref[...] = norm(x_ref[...], g_ref)

        return _call(body, name=f"rms_fwd_{S}", grid=(S // ts,), in_specs=[row, vec], out_specs=row,
                     out_shape=jax.ShapeDtypeStruct((S, D), out_dtype), compiler_params=_cp("parallel"))(x, g2)

    def body(x_ref, d_ref, g_ref, s_ref, y_ref):
        xv = x_ref[...] + d_ref[...]
        s_ref[...] = xv
        y_ref[...] = norm(xv, g_ref)

    return _call(body, name=f"add_rms_fwd_{S}", grid=(S // ts,), in_specs=[row, row, vec], out_specs=[row, row],
                 out_shape=[jax.ShapeDtypeStruct((S, D), f32), jax.ShapeDtypeStruct((S, D), out_dtype)],
                 compiler_params=_cp("parallel"))(x, delta, g2)


def _rms_bwd(x, g, dy, dres=None):
    S, D = x.shape
    ts = _tile(S, 512, 8)
    row = pl.BlockSpec((ts, D), lambda i: (i, 0))
    vec = pl.BlockSpec((1, D), lambda i: (0, 0))
    has_res = dres is not None

    def body(*refs):
        if has_res:
            x_ref, g_ref, dy_ref, dr_ref, dx_ref, dg_ref = refs
        else:
            x_ref, g_ref, dy_ref, dx_ref, dg_ref = refs
        xv = x_ref[...]
        r = lax.rsqrt(jnp.mean(xv * xv, axis=-1, keepdims=True) + EPS)
        xh = xv * r
        dyv = dy_ref[...].astype(f32)

        @pl.when(pl.program_id(0) == 0)
        def _():
            dg_ref[...] = jnp.zeros_like(dg_ref)

        dg_ref[...] += jnp.sum(dyv * xh, axis=0, keepdims=True)
        dxh = dyv * g_ref[...]
        dx = r * (dxh - xh * jnp.mean(dxh * xh, axis=-1, keepdims=True))
        if has_res:
            dx = dx + dr_ref[...]
        dx_ref[...] = dx

    ins = [x, g.reshape(1, D), dy] + ([dres] if has_res else [])
    dx, dg = _call(body, name=f"rms_bwd_{S}_{int(has_res)}", grid=(S // ts,),
                   in_specs=[row, vec, row] + ([row] if has_res else []), out_specs=[row, vec],
                   out_shape=[jax.ShapeDtypeStruct((S, D), f32), jax.ShapeDtypeStruct((1, D), f32)],
                   compiler_params=_cp("arbitrary"))(*ins)
    return dx, dg.reshape(D)


def _pool_windows_sum(x, win, rows):
    s, k = x, 1
    while k < win:
        s = s + _shift_down(s, k, rows)
        k *= 2
    return s


def _pool_windows_sum_up(x, win, rows, n):
    s, k = x, 1
    while k < win:
        s = s + _shift_up(s, k, rows, n)
        k *= 2
    return s


def _pool_fwd(hn):
    S, D = hn.shape
    cg = D // len(POOL_WINDOWS)
    tc = _tile(cg, 128)
    nb = cg // tc
    blk = pl.BlockSpec((S, tc), lambda gi, j: (0, gi * nb + j))

    def body(x_ref, p_ref):
        gi = pl.program_id(0)
        rows = lax.broadcasted_iota(jnp.int32, (S, 1), 0)
        cnt = (rows + 1).astype(f32)
        for k, win in enumerate(POOL_WINDOWS):
            @pl.when(gi == k)
            def _(win=win):
                x = x_ref[...]
                s = _pool_windows_sum(x, win, rows)
                p_ref[...] = (s / jnp.minimum(cnt, float(win)) - x).astype(p_ref.dtype)

    return _call(body, name=f"pool_fwd_{S}", grid=(len(POOL_WINDOWS), nb), in_specs=[blk], out_specs=blk,
                 out_shape=jax.ShapeDtypeStruct((S, D), _MXU), compiler_params=_cp("parallel", "parallel"))(hn)


def _pool_bwd(dp):
    S, D = dp.shape
    cg = D // len(POOL_WINDOWS)
    tc = _tile(cg, 128)
    nb = cg // tc
    blk = pl.BlockSpec((S, tc), lambda gi, j: (0, gi * nb + j))

    def body(dp_ref, dx_ref):
        gi = pl.program_id(0)
        rows = lax.broadcasted_iota(jnp.int32, (S, 1), 0)
        cnt = (rows + 1).astype(f32)
        for k, win in enumerate(POOL_WINDOWS):
            @pl.when(gi == k)
            def _(win=win):
                d = dp_ref[...]
                e = d / jnp.minimum(cnt, float(win))
                dx_ref[...] = _pool_windows_sum_up(e, win, rows, S) - d

    return _call(body, name=f"pool_bwd_{S}", grid=(len(POOL_WINDOWS), nb), in_specs=[blk], out_specs=blk,
                 out_shape=jax.ShapeDtypeStruct((S, D), f32), compiler_params=_cp("parallel", "parallel"))(dp)


def _pool_mix_fwd(p, w, scale):
    S, D = p.shape
    G, cg, _ = w.shape
    ts = _tile(S, 1024, 8)

    def body(p_ref, w_ref, s_ref, y_ref):
        y_ref[...] = _dot(p_ref[...], w_ref[0]) * s_ref[...]

    return _call(body, name=f"pool_mix_fwd_{S}", grid=(S // ts, G),
                 in_specs=[pl.BlockSpec((ts, cg), lambda i, g: (i, g)), pl.BlockSpec((1, cg, cg), lambda i, g: (g, 0, 0)),
                           pl.BlockSpec((1, cg), lambda i, g: (0, g))],
                 out_specs=pl.BlockSpec((ts, cg), lambda i, g: (i, g)),
                 out_shape=jax.ShapeDtypeStruct((S, D), f32), compiler_params=_cp("parallel", "parallel"))(
                     p, w, scale.reshape(1, D))


def _pool_mix_bwd(p, w, scale, dy):
    S, D = p.shape
    G, cg, _ = w.shape
    ts = _tile(S, 1024, 8)

    def body(p_ref, w_ref, s_ref, dy_ref, dp_ref, dw_ref, ds_ref):
        @pl.when(pl.program_id(1) == 0)
        def _():
            dw_ref[...] = jnp.zeros_like(dw_ref)
            ds_ref[...] = jnp.zeros_like(ds_ref)

        pv, wv, dyv = p_ref[...], w_ref[0], dy_ref[...]
        ypre = _dot(pv, wv)
        ds_ref[...] += jnp.sum(dyv * ypre, axis=0, keepdims=True)
        dyp = (dyv * s_ref[...]).astype(_MXU)
        dp_ref[...] = _dot(dyp, wv, _NT)
        dw_ref[0] += _dot(pv, dyp, _TN)

    dp, dw, ds = _call(
        body, name=f"pool_mix_bwd_{S}", grid=(G, S // ts),
        in_specs=[pl.BlockSpec((ts, cg), lambda g, i: (i, g)), pl.BlockSpec((1, cg, cg), lambda g, i: (g, 0, 0)),
                  pl.BlockSpec((1, cg), lambda g, i: (0, g)), pl.BlockSpec((ts, cg), lambda g, i: (i, g))],
        out_specs=[pl.BlockSpec((ts, cg), lambda g, i: (i, g)), pl.BlockSpec((1, cg, cg), lambda g, i: (g, 0, 0)),
                   pl.BlockSpec((1, cg), lambda g, i: (0, g))],
        out_shape=[jax.ShapeDtypeStruct((S, D), f32), jax.ShapeDtypeStruct((G, cg, cg), f32),
                   jax.ShapeDtypeStruct((1, D), f32)],
        compiler_params=_cp("parallel", "arbitrary"))(p, w, scale.reshape(1, D), dy)
    return dp, dw, ds.reshape(D)


def _sb_tile(S):
    return min(256, max(128, S // 4))


_LOG2E = math.log2(math.e)


def _sb_scores(qs, ks, tri, R, U):
    z = _dot(qs, ks, _NT)
    sp = jnp.maximum(z, 0.0) + jnp.log(1.0 + jnp.exp2(jnp.abs(z) * -_LOG2E))
    lb = z - sp
    if tri is not None:
        sp = jnp.where(tri, sp, 0.0)
    c = _dot(sp.astype(_MXU), U)
    a = jnp.exp(lb - c - R)
    if tri is not None:
        a = jnp.where(tri, a, 0.0)
    return sp, lb, a


def _sb_sweep(tile, i, carry):
    def two(p, c):
        kb = i - 1 - 2 * p
        return tile(kb - 1, tile(kb, c, None), None)

    carry = lax.fori_loop(0, i // 2, two, carry)
    return lax.cond(i % 2 == 1, lambda c: tile(i * 0, c, None), lambda c: c, carry)


def _sb_consts(T):
    lane = lax.broadcasted_iota(jnp.int32, (1, _LANES), 1)
    row, col = lax.broadcasted_iota(jnp.int32, (T, T), 0), lax.broadcasted_iota(jnp.int32, (T, T), 1)
    U = jnp.where(row > col, 1.0, 0.0).astype(_MXU)
    heads = [(lane >= SB_HEAD_DIM * h) & (lane < SB_HEAD_DIM * (h + 1)) for h in range(_LANES // SB_HEAD_DIM)]
    return heads, col < row, U


def _sb_fwd(qkv, gather=()):
    S, D3 = qkv.shape
    D = D3 // 3
    HP = D // _LANES
    T = _sb_tile(S)
    nq = S // T
    n = len(gather)
    scale = SB_HEAD_DIM ** -0.5

    def body(*refs):
        q_ref, k_ref, v_ref = refs[:3]
        o_ref = refs[3 + n]
        i = pl.program_id(1)
        if n:
            step = pl.program_id(0) * nq + i
            start, forward, finish = _all_gather_stages(refs[3:3 + n], refs[4 + n:4 + 2 * n], *refs[4 + 2 * n:])
            pl.when(step == 0)(start)
            pl.when(step == (HP // 2) * nq)(forward)
        q = q_ref[...] * scale
        heads, tri, U = _sb_consts(T)
        qms = [jnp.where(hm, q, jnp.zeros_like(q)) for hm in heads]

        def tile(kb, carry, mask):
            off = pl.multiple_of(kb * T, T)
            ks, vs = k_ref[pl.ds(off, T), :], v_ref[pl.ds(off, T), :]
            new = []
            for h, qm in enumerate(qms):
                R, acc = carry[2 * h], carry[2 * h + 1]
                sp, _, a = _sb_scores(qm, ks, mask, R, U)
                new += [R + jnp.sum(sp, axis=1, keepdims=True), acc + _dot(a.astype(_MXU), vs)]
            return tuple(new)

        zero = (jnp.zeros((T, 1), f32), jnp.zeros((T, _LANES), f32)) * len(heads)
        carry = _sb_sweep(tile, i, tile(i, zero, tri))
        out = jnp.zeros((T, _LANES), f32)
        for h, hm in enumerate(heads):
            out = out + jnp.where(hm, carry[2 * h + 1], 0.0)
        o_ref[...] = out
        if n:
            pl.when(step == HP * nq - 1)(finish)

    outs = _call(
        body, name=f"sb_fwd_{S}_{n}", grid=(HP, nq),
        in_specs=[pl.BlockSpec((T, _LANES), lambda hp, i: (i, hp)), pl.BlockSpec((S, _LANES), lambda hp, i: (0, HP + hp)),
                  pl.BlockSpec((S, _LANES), lambda hp, i: (0, 2 * HP + hp))] + [ANY] * n,
        out_specs=[pl.BlockSpec((T, _LANES), lambda hp, i: (i, hp))] + [ANY] * n,
        out_shape=[jax.ShapeDtypeStruct((S, D), f32)] + _all_gather_shapes(gather),
        scratch_shapes=_all_gather_sems(n) if n else [],
        compiler_params=_cp("arbitrary", "arbitrary"))(qkv, qkv, qkv, *gather)
    return outs[0], list(outs[1:])


def _sb_bwd(qkv, o, do, send=()):
    S, D3 = qkv.shape
    D = D3 // 3
    HP = D // _LANES
    T = _sb_tile(S)
    nq = S // T
    n = len(send)
    scale = SB_HEAD_DIM ** -0.5

    def body(*refs):
        q_ref, k_ref, v_ref, o_ref, do_ref = refs[:5]
        dq_ref, dk_ref, dv_ref = refs[5 + n:8 + n]
        dk_acc, dv_acc = refs[8 + 2 * n:10 + 2 * n]
        i = pl.program_id(1)
        if n:
            step = pl.program_id(0) * nq + i
            start, finish = _exchange_all_stages(refs[5:5 + n], refs[8 + n:8 + 2 * n], *refs[10 + 2 * n:])
            pl.when(step == 0)(start)

        @pl.when(i == 0)
        def _():
            dk_acc[...] = jnp.zeros_like(dk_acc)
            dv_acc[...] = jnp.zeros_like(dv_acc)

        q = q_ref[...] * scale
        dob = do_ref[...].astype(_MXU)
        prod = dob.astype(f32) * o_ref[...]
        heads, tri, U = _sb_consts(T)
        qms = [jnp.where(hm, q, jnp.zeros_like(q)) for hm in heads]
        doms = [jnp.where(hm, dob, jnp.zeros_like(dob)) for hm in heads]
        totals = [jnp.sum(jnp.where(hm, prod, 0.0), axis=1, keepdims=True) for hm in heads]

        def tile(kb, carry, mask):
            off = pl.multiple_of(kb * T, T)
            ks, vs = k_ref[pl.ds(off, T), :], v_ref[pl.ds(off, T), :]
            new = []
            dk_t = jnp.zeros((T, _LANES), f32)
            dv_t = jnp.zeros((T, _LANES), f32)
            for h, (qm, dom, total) in enumerate(zip(qms, doms, totals)):
                R, Gs, dq = carry[3 * h:3 * h + 3]
                sp, lb, a = _sb_scores(qm, ks, mask, R, U)
                ab = a.astype(_MXU)
                g = ab.astype(f32) * _dot(dom, vs, _NT)
                before = total - (g + _dot_hilo(g, U) + Gs)
                beta = jnp.exp(lb)
                dz = g - (g + before) * beta
                if mask is not None:
                    dz = jnp.where(mask, dz, 0.0)
                dzb = dz.astype(_MXU)
                dk_t = dk_t + _dot(dzb, qm, _TN)
                dv_t = dv_t + _dot(ab, dom, _TN)
                new += [R + jnp.sum(sp, axis=1, keepdims=True), Gs + jnp.sum(g, axis=1, keepdims=True), dq + _dot(dzb, ks)]
            dk_acc[pl.ds(off, T), :] += dk_t
            dv_acc[pl.ds(off, T), :] += dv_t
            return tuple(new)

        zero1 = jnp.zeros((T, 1), f32)
        carry = _sb_sweep(tile, i, tile(i, (zero1, zero1, jnp.zeros((T, _LANES), f32)) * len(heads), tri))
        dq_out = jnp.zeros((T, _LANES), f32)
        for h, hm in enumerate(heads):
            dq_out = dq_out + jnp.where(hm, carry[3 * h + 2], 0.0)
        dq_ref[...] = (dq_out * scale).astype(dq_ref.dtype)

        @pl.when(i == nq - 1)
        def _():
            dk_ref[...] = dk_acc[...].astype(dk_ref.dtype)
            dv_ref[...] = dv_acc[...].astype(dv_ref.dtype)

        if n:
            pl.when(step == HP * nq - 1)(finish)

    qb = pl.BlockSpec((T, _LANES), lambda hp, i: (i, hp))
    col = pl.BlockSpec((S, _LANES), lambda hp, i: (0, hp))
    out = jax.ShapeDtypeStruct((S, D), _MXU)
    outs = _call(
        body, name=f"sb_bwd_{S}_{n}", grid=(HP, nq),
        in_specs=[qb, pl.BlockSpec((S, _LANES), lambda hp, i: (0, HP + hp)),
                  pl.BlockSpec((S, _LANES), lambda hp, i: (0, 2 * HP + hp)), qb, qb] + [ANY] * n,
        out_specs=[qb, col, col] + [ANY] * n, out_shape=[out, out, out] + _exchange_all_shapes(send),
        scratch_shapes=[pltpu.VMEM((S, _LANES), f32), pltpu.VMEM((S, _LANES), f32)] + (_exchange_all_sems(n) if n else []),
        compiler_params=_cp("arbitrary", "arbitrary"))(qkv, qkv, qkv, o, do, *send)
    return outs[0], outs[1], outs[2], list(outs[3:])


def _cmul(ar, ai, br, bi):
    return ar * br - ai * bi, ar * bi + ai * br


def _s5_coef(ar, ai, ldt):
    dt = jnp.exp(ldt)
    e = jnp.exp(ar * dt)
    abr, abi = e * jnp.cos(ai * dt), e * jnp.sin(ai * dt)
    inv = 1.0 / (ar * ar + ai * ai)
    cr, ci = _cmul(abr - 1.0, abi, ar * inv, -ai * inv)
    return dt, abr, abi, inv, cr, ci


def _s5_prep_fwd(ar, ai, ldt, br, bi):
    N = ar.shape[1]
    cg = br.shape[0]

    def body(ar_ref, ai_ref, ldt_ref, br_ref, bi_ref, abr_ref, abi_ref, bbr_ref, bbi_ref):
        _, abr, abi, _, cr, ci = _s5_coef(ar_ref[...], ai_ref[...], ldt_ref[...])
        abr_ref[...], abi_ref[...] = abr, abi
        bbr_ref[...], bbi_ref[...] = _cmul(cr, ci, br_ref[...], bi_ref[...])

    v, m = jax.ShapeDtypeStruct((1, N), f32), jax.ShapeDtypeStruct((cg, N), f32)
    return _call(body, name="s5_prep_fwd", out_shape=[v, v, m, m])(ar, ai, ldt, br, bi)


def _s5_prep_bwd(ar, ai, ldt, br, bi, dabr, dabi, dbbr, dbbi):
    N = ar.shape[1]
    cg = br.shape[0]

    def body(ar_ref, ai_ref, ldt_ref, br_ref, bi_ref, dabr_ref, dabi_ref, dbbr_ref, dbbi_ref,
             dar_ref, dai_ref, dldt_ref, dbr_ref, dbi_ref):
        a_r, a_i = ar_ref[...], ai_ref[...]
        dt, abr, abi, inv, cr, ci = _s5_coef(a_r, a_i, ldt_ref[...])
        b_r, b_i, gr, gi = br_ref[...], bi_ref[...], dbbr_ref[...], dbbi_ref[...]
        dbr_ref[...], dbi_ref[...] = _cmul(cr, -ci, gr, gi)
        dcr = jnp.sum(gr * b_r + gi * b_i, axis=0, keepdims=True)
        dci = jnp.sum(gi * b_r - gr * b_i, axis=0, keepdims=True)
        ilr, ili = a_r * inv, -a_i * inv
        dwr, dwi = _cmul(ilr, -ili, dcr, dci)
        qr, qi = _cmul(cr, ci, ilr, ili)
        dl1r, dl1i = _cmul(-qr, qi, dcr, dci)
        tr, ti = dabr_ref[...] + dwr, dabi_ref[...] + dwi
        ddlr, ddli = _cmul(abr, -abi, tr, ti)
        dar_ref[...] = dl1r + ddlr * dt
        dai_ref[...] = dl1i + ddli * dt
        dldt_ref[...] = (a_r * ddlr + a_i * ddli) * dt

    v, m = jax.ShapeDtypeStruct((1, N), f32), jax.ShapeDtypeStruct((cg, N), f32)
    return _call(body, name="s5_prep_bwd", out_shape=[v, v, v, m, m])(ar, ai, ldt, br, bi, dabr, dabi, dbbr, dbbi)


def _s5_in_fwd(u, Br, Bi):
    S, D = u.shape
    nb, _, nw = Br.shape
    ts = _tile(S, 512, 8)

    def body(u_ref, br_ref, bi_ref, or_ref, oi_ref):
        uv = u_ref[...]
        or_ref[...] = _dot3(uv, br_ref[0])
        oi_ref[...] = _dot3(uv, bi_ref[0])

    ub = pl.BlockSpec((ts, _LANES), lambda i, k: (i, k))
    wb = pl.BlockSpec((1, _LANES, nw), lambda i, k: (k, 0, 0))
    ob = pl.BlockSpec((ts, nw), lambda i, k: (i, k))
    o = jax.ShapeDtypeStruct((S, nb * nw), f32)
    return _call(body, name=f"s5_in_fwd_{S}", grid=(S // ts, nb), in_specs=[ub, wb, wb], out_specs=[ob, ob],
                 out_shape=[o, o], compiler_params=_cp("parallel", "parallel"))(u, Br, Bi)


def _s5_scan_fwd(bur, bui, abr, abi):
    S, N = bur.shape
    tt, tn = _tile(S, 128, 8), _tile(N, 4096)

    def body(br_ref, bi_ref, ar_ref, ai_ref, xr_ref, xi_ref, sr, si):
        @pl.when(pl.program_id(1) == 0)
        def _():
            sr[...] = jnp.zeros_like(sr)
            si[...] = jnp.zeros_like(si)

        a_r, a_i = ar_ref[...], ai_ref[...]

        def step(t, carry):
            xr, xi = carry
            nr = a_r * xr - a_i * xi + br_ref[pl.ds(t, 1), :]
            ni = a_r * xi + a_i * xr + bi_ref[pl.ds(t, 1), :]
            xr_ref[pl.ds(t, 1), :] = nr
            xi_ref[pl.ds(t, 1), :] = ni
            return nr, ni

        xr, xi = lax.fori_loop(0, tt, step, (sr[...], si[...]))
        sr[...], si[...] = xr, xi

    blk = pl.BlockSpec((tt, tn), lambda n, i: (i, n))
    vec = pl.BlockSpec((1, tn), lambda n, i: (0, n))
    o = jax.ShapeDtypeStruct((S, N), f32)
    return _call(body, name=f"s5_scan_fwd_{S}", grid=(N // tn, S // tt), in_specs=[blk, blk, vec, vec],
                 out_specs=[blk, blk], out_shape=[o, o],
                 scratch_shapes=[pltpu.VMEM((1, tn), f32), pltpu.VMEM((1, tn), f32)],
                 compiler_params=_cp("parallel", "arbitrary"))(bur, bui, abr, abi)


def _s5_scan_bwd(dr, di, xr, xi, abr, abi):
    S, N = dr.shape
    tt, tn = _tile(S, 128, 8), _tile(N, 4096)
    nt = S // tt

    def body(dr_ref, di_ref, xr_ref, xi_ref, ar_ref, ai_ref, gr_ref, gi_ref, dar_ref, dai_ref, sr, si):
        @pl.when(pl.program_id(1) == 0)
        def _():
            sr[...] = jnp.zeros_like(sr)
            si[...] = jnp.zeros_like(si)
            dar_ref[...] = jnp.zeros_like(dar_ref)
            dai_ref[...] = jnp.zeros_like(dai_ref)

        a_r, a_i = ar_ref[...], ai_ref[...]

        def step(j, carry):
            gr, gi, accr, acci = carry
            t = tt - 1 - j
            xr_t, xi_t = xr_ref[pl.ds(t, 1), :], xi_ref[pl.ds(t, 1), :]
            accr = accr + gr * xr_t + gi * xi_t
            acci = acci + gi * xr_t - gr * xi_t
            nr = dr_ref[pl.ds(t, 1), :] + a_r * gr + a_i * gi
            ni = di_ref[pl.ds(t, 1), :] + a_r * gi - a_i * gr
            gr_ref[pl.ds(t, 1), :] = nr
            gi_ref[pl.ds(t, 1), :] = ni
            return nr, ni, accr, acci

        gr, gi, accr, acci = lax.fori_loop(0, tt, step, (sr[...], si[...], dar_ref[...], dai_ref[...]))
        sr[...], si[...] = gr, gi
        dar_ref[...], dai_ref[...] = accr, acci

    blk = pl.BlockSpec((tt, tn), lambda n, i: (nt - 1 - i, n))
    vec = pl.BlockSpec((1, tn), lambda n, i: (0, n))
    o, v = jax.ShapeDtypeStruct((S, N), f32), jax.ShapeDtypeStruct((1, N), f32)
    return _call(body, name=f"s5_scan_bwd_{S}", grid=(N // tn, nt), in_specs=[blk, blk, blk, blk, vec, vec],
                 out_specs=[blk, blk, vec, vec], out_shape=[o, o, v, v],
                 scratch_shapes=[pltpu.VMEM((1, tn), f32), pltpu.VMEM((1, tn), f32)],
                 compiler_params=_cp("parallel", "arbitrary"))(dr, di, xr, xi, abr, abi)


def _s5_out_fwd(xr, xi, Cr, Ci, u, d):
    S, N = xr.shape
    nb, nw, _ = Cr.shape
    D = u.shape[1]
    ts = _tile(S, 512, 8)

    def body(xr_ref, xi_ref, cr_ref, ci_ref, u_ref, d_ref, y_ref, z_ref):
        y = _dot3(xr_ref[...], cr_ref[0]) - _dot3(xi_ref[...], ci_ref[0]) + d_ref[...] * u_ref[...]
        y_ref[...] = y
        z_ref[...] = _gelu(y).astype(z_ref.dtype)

    xb = pl.BlockSpec((ts, nw), lambda i, k: (i, k))
    cb = pl.BlockSpec((1, nw, _LANES), lambda i, k: (k, 0, 0))
    ub = pl.BlockSpec((ts, _LANES), lambda i, k: (i, k))
    db = pl.BlockSpec((1, _LANES), lambda i, k: (0, k))
    return _call(body, name=f"s5_out_fwd_{S}", grid=(S // ts, nb), in_specs=[xb, xb, cb, cb, ub, db],
                 out_specs=[ub, ub], out_shape=[jax.ShapeDtypeStruct((S, D), f32), jax.ShapeDtypeStruct((S, D), _MXU)],
                 compiler_params=_cp("parallel", "parallel"))(xr, xi, Cr, Ci, u, d.reshape(1, D))


def _s5_out_bwd(dz, y, u, d, Cr, Ci):
    S, D = y.shape
    nb, nw, _ = Cr.shape
    ts = _tile(S, 512, 8)

    def body(dz_ref, y_ref, u_ref, d_ref, cr_ref, ci_ref, dy_ref, gr_ref, gi_ref, du_ref, dd_ref):
        @pl.when(pl.program_id(1) == 0)
        def _():
            dd_ref[...] = jnp.zeros_like(dd_ref)

        dy = dz_ref[...] * _gelu_grad(y_ref[...])
        dy_ref[...] = dy
        gr_ref[...] = _dot3(dy, cr_ref[0], _NT)
        gi_ref[...] = -_dot3(dy, ci_ref[0], _NT)
        du_ref[...] = dy * d_ref[...]
        dd_ref[...] += jnp.sum(dy * u_ref[...], axis=0, keepdims=True)

    xb = pl.BlockSpec((ts, nw), lambda k, i: (i, k))
    cb = pl.BlockSpec((1, nw, _LANES), lambda k, i: (k, 0, 0))
    ub = pl.BlockSpec((ts, _LANES), lambda k, i: (i, k))
    db = pl.BlockSpec((1, _LANES), lambda k, i: (0, k))
    a, s = jax.ShapeDtypeStruct((S, D), f32), jax.ShapeDtypeStruct((S, nb * nw), f32)
    dy, gr, gi, du, dd = _call(
        body, name=f"s5_out_bwd_{S}", grid=(nb, S // ts), in_specs=[ub, ub, ub, db, cb, cb],
        out_specs=[ub, xb, xb, ub, db], out_shape=[a, s, s, a, jax.ShapeDtypeStruct((1, D), f32)],
        compiler_params=_cp("parallel", "arbitrary"))(dz, y, u, d.reshape(1, D), Cr, Ci)
    return dy, gr, gi, du, dd.reshape(D)


def _s5_in_bwd(gr, gi, Br, Bi, du0):
    S, N = gr.shape
    nb, _, nw = Br.shape
    ts = _tile(S, 512, 8)

    def body(gr_ref, gi_ref, br_ref, bi_ref, d0_ref, du_ref):
        du_ref[...] = d0_ref[...] + _dot3(gr_ref[...], br_ref[0], _NT) + _dot3(gi_ref[...], bi_ref[0], _NT)

    xb = pl.BlockSpec((ts, nw), lambda i, k: (i, k))
    wb = pl.BlockSpec((1, _LANES, nw), lambda i, k: (k, 0, 0))
    ub = pl.BlockSpec((ts, _LANES), lambda i, k: (i, k))
    return _call(body, name=f"s5_in_bwd_{S}", grid=(S // ts, nb), in_specs=[xb, xb, wb, wb, ub], out_specs=ub,
                 out_shape=jax.ShapeDtypeStruct((S, nb * _LANES), f32), compiler_params=_cp("parallel", "parallel"))(
                     gr, gi, Br, Bi, du0)


def _s5_wgrad(u, gr, gi, xr, xi, dy):
    S, D = u.shape
    nb = D // _LANES
    nw = gr.shape[1] // nb
    ts = _tile(S, 512, 8)

    def body(u_ref, gr_ref, gi_ref, xr_ref, xi_ref, dy_ref, dbr_ref, dbi_ref, dcr_ref, dci_ref):
        @pl.when(pl.program_id(1) == 0)
        def _():
            for r in (dbr_ref, dbi_ref, dcr_ref, dci_ref):
                r[...] = jnp.zeros_like(r)

        uv, dyv = u_ref[...], dy_ref[...]
        dbr_ref[0] += _dot3(uv, gr_ref[...], _TN)
        dbi_ref[0] += _dot3(uv, gi_ref[...], _TN)
        dcr_ref[0] += _dot3(xr_ref[...], dyv, _TN)
        dci_ref[0] -= _dot3(xi_ref[...], dyv, _TN)

    xb = pl.BlockSpec((ts, nw), lambda k, i: (i, k))
    ub = pl.BlockSpec((ts, _LANES), lambda k, i: (i, k))
    wb = pl.BlockSpec((1, _LANES, nw), lambda k, i: (k, 0, 0))
    cb = pl.BlockSpec((1, nw, _LANES), lambda k, i: (k, 0, 0))
    w, c = jax.ShapeDtypeStruct((nb, _LANES, nw), f32), jax.ShapeDtypeStruct((nb, nw, _LANES), f32)
    return _call(body, name=f"s5_wgrad_{S}", grid=(nb, S // ts), in_specs=[ub, xb, xb, xb, xb, ub],
                 out_specs=[wb, wb, cb, cb], out_shape=[w, w, c, c],
                 compiler_params=_cp("parallel", "arbitrary"))(u, gr, gi, xr, xi, dy)


def _glu_fwd(vg):
    S, D2 = vg.shape
    D = D2 // 2
    ts, tc = _tile(S, 1024, 8), _tile(D, 512)
    nc = D // tc

    def body(v_ref, g_ref, o_ref):
        o_ref[...] = v_ref[...] * _sigmoid(g_ref[...])

    return _call(body, name=f"glu_fwd_{S}", grid=(S // ts, nc),
                 in_specs=[pl.BlockSpec((ts, tc), lambda i, j: (i, j)), pl.BlockSpec((ts, tc), lambda i, j: (i, nc + j))],
                 out_specs=pl.BlockSpec((ts, tc), lambda i, j: (i, j)), out_shape=jax.ShapeDtypeStruct((S, D), f32),
                 compiler_params=_cp("parallel", "parallel"))(vg, vg)


def _glu_bwd(vg, dout):
    S, D2 = vg.shape
    D = D2 // 2
    ts, tc = _tile(S, 1024, 8), _tile(D, 512)
    nc = D // tc

    def body(v_ref, g_ref, do_ref, dv_ref, dg_ref):
        sg = _sigmoid(g_ref[...])
        do = do_ref[...]
        dv_ref[...] = (do * sg).astype(dv_ref.dtype)
        dg_ref[...] = (do * v_ref[...] * sg * (1.0 - sg)).astype(dg_ref.dtype)

    blk = pl.BlockSpec((ts, tc), lambda i, j: (i, j))
    o = jax.ShapeDtypeStruct((S, D), _MXU)
    return _call(body, name=f"glu_bwd_{S}", grid=(S // ts, nc),
                 in_specs=[blk, pl.BlockSpec((ts, tc), lambda i, j: (i, nc + j)), blk], out_specs=[blk, blk],
                 out_shape=[o, o], compiler_params=_cp("parallel", "parallel"))(vg, vg, dout)


def _blockdiag_in(b2, nb, P):
    cg = b2.shape[0]
    gb = S5_BLOCK_GROUPS
    t = b2.reshape(cg, nb, gb, P).transpose(1, 0, 2, 3)
    eye = jnp.eye(gb, dtype=b2.dtype)
    return (eye[None, :, None, :, None] * t[:, None]).reshape(nb, gb * cg, gb * P)


def _blockdiag_in_extract(db, cg, P):
    nb = db.shape[0]
    gb = S5_BLOCK_GROUPS
    eye = jnp.eye(gb, dtype=db.dtype)
    t = (db.reshape(nb, gb, cg, gb, P) * eye[None, :, None, :, None]).sum(3)
    return t.transpose(2, 0, 1, 3).reshape(cg, nb * gb * P)


def _blockdiag_out(c, nb):
    G, cg, P = c.shape
    gb = S5_BLOCK_GROUPS
    t = c.reshape(nb, gb, cg, P).transpose(0, 3, 1, 2)
    eye = jnp.eye(gb, dtype=c.dtype)
    return (eye[None, :, None, :, None] * t[:, None]).reshape(nb, gb * P, gb * cg)


def _blockdiag_out_extract(dc, cg, P):
    nb = dc.shape[0]
    gb = S5_BLOCK_GROUPS
    eye = jnp.eye(gb, dtype=dc.dtype)
    t = (dc.reshape(nb, gb, P, gb, cg) * eye[None, :, None, :, None]).sum(1)
    return t.transpose(0, 2, 3, 1).reshape(nb * gb, cg, P)


def _xa_fwd(q, kv):
    S, D = q.shape
    M = kv.shape[0]
    dh = D // XA_HEADS
    ts = _tile(S, 512, 8)
    scale = dh ** -0.5

    def body(q_ref, k_ref, v_ref, o_ref):
        s = _dot(q_ref[...], k_ref[...], _NT) * scale
        e = jnp.exp(s - jnp.max(s, axis=-1, keepdims=True))
        p = e / jnp.sum(e, axis=-1, keepdims=True)
        o_ref[...] = _dot(p.astype(_MXU), v_ref[...]).astype(o_ref.dtype)

    return _call(body, name=f"xa_fwd_{S}", grid=(S // ts, XA_HEADS),
                 in_specs=[pl.BlockSpec((ts, dh), lambda i, h: (i, h)), pl.BlockSpec((M, dh), lambda i, h: (0, h)),
                           pl.BlockSpec((M, dh), lambda i, h: (0, XA_HEADS + h))],
                 out_specs=pl.BlockSpec((ts, dh), lambda i, h: (i, h)), out_shape=jax.ShapeDtypeStruct((S, D), _MXU),
                 compiler_params=_cp("parallel", "parallel"))(q, kv, kv)


def _xa_bwd(q, kv, do):
    S, D = q.shape
    M = kv.shape[0]
    dh = D // XA_HEADS
    ts = _tile(S, 512, 8)
    scale = dh ** -0.5

    def body(q_ref, k_ref, v_ref, do_ref, dq_ref, dk_ref, dv_ref):
        @pl.when(pl.program_id(1) == 0)
        def _():
            dk_ref[...] = jnp.zeros_like(dk_ref)
            dv_ref[...] = jnp.zeros_like(dv_ref)

        qv, kv_, vv, dov = q_ref[...], k_ref[...], v_ref[...], do_ref[...]
        s = _dot(qv, kv_, _NT) * scale
        e = jnp.exp(s - jnp.max(s, axis=-1, keepdims=True))
        p = e / jnp.sum(e, axis=-1, keepdims=True)
        dp = _dot(dov, vv, _NT)
        dv_ref[...] += _dot(p.astype(_MXU), dov, _TN)
        ds = (p * (dp - jnp.sum(dp * p, axis=-1, keepdims=True)) * scale).astype(_MXU)
        dq_ref[...] = _dot(ds, kv_).astype(dq_ref.dtype)
        dk_ref[...] += _dot(ds, qv, _TN)

    qb = pl.BlockSpec((ts, dh), lambda h, i: (i, h))
    mb = pl.BlockSpec((M, dh), lambda h, i: (0, h))
    m = jax.ShapeDtypeStruct((M, D), f32)
    return _call(body, name=f"xa_bwd_{S}", grid=(XA_HEADS, S // ts),
                 in_specs=[qb, mb, pl.BlockSpec((M, dh), lambda h, i: (0, XA_HEADS + h)), qb],
                 out_specs=[qb, mb, mb], out_shape=[jax.ShapeDtypeStruct((S, D), _MXU), m, m],
                 compiler_params=_cp("parallel", "arbitrary"))(q, kv, kv, do)


def _conv(u, cw_ref, cb_ref, rows):
    return _conv_taps(u, cw_ref, cb_ref)[0]


def _conv_taps(u, cw_ref, cb_ref):
    u1, u2 = _shift_down(u, 1), _shift_down(u, 2)
    return cw_ref[2:3, :] * u + cw_ref[1:2, :] * u1 + cw_ref[0:1, :] * u2 + cb_ref[...], u1, u2


def _convglu_fwd(u, cw, cb):
    S, F2 = u.shape
    F = F2 // 2
    tc = _tile(F, 128)
    nc = F // tc

    def body(uv_ref, ug_ref, cwv_ref, cwg_ref, cbv_ref, cbg_ref, o_ref):
        rows = lax.broadcasted_iota(jnp.int32, (S, 1), 0)
        val = _conv(uv_ref[...], cwv_ref, cbv_ref, rows)
        gate = _conv(ug_ref[...], cwg_ref, cbg_ref, rows)
        o_ref[...] = (gate * _sigmoid(gate) * val).astype(o_ref.dtype)

    def col(r, off):
        return pl.BlockSpec((r, tc), lambda j: (0, off + j))

    return _call(body, name=f"convglu_fwd_{S}", grid=(nc,),
                 in_specs=[col(S, 0), col(S, nc), col(3, 0), col(3, nc), col(1, 0), col(1, nc)], out_specs=col(S, 0),
                 out_shape=jax.ShapeDtypeStruct((S, F), _MXU), compiler_params=_cp("parallel"))(
                     u, u, cw, cw, cb.reshape(1, F2), cb.reshape(1, F2))


def _convglu_bwd(u, cw, cb, dact):
    S, F2 = u.shape
    F = F2 // 2
    tc = _tile(F, 128)
    nc = F // tc

    def body(uv_ref, ug_ref, cwv_ref, cwg_ref, cbv_ref, cbg_ref, da_ref, duv_ref, dug_ref, dcwv_ref, dcwg_ref, dcbv_ref, dcbg_ref):
        rows = lax.broadcasted_iota(jnp.int32, (S, 1), 0)
        uv, ug = uv_ref[...], ug_ref[...]
        val, uv1, uv2 = _conv_taps(uv, cwv_ref, cbv_ref)
        gate, ug1, ug2 = _conv_taps(ug, cwg_ref, cbg_ref)
        sg = _sigmoid(gate)
        da = da_ref[...]
        dval = da * (gate * sg)
        dgate = da * val * (sg * (1.0 + gate * (1.0 - sg)))
        for (uu, uu1, uu2), d, cw_ref, du_ref, dcw_ref, dcb_ref in (((uv, uv1, uv2), dval, cwv_ref, duv_ref, dcwv_ref, dcbv_ref),
                                                                    ((ug, ug1, ug2), dgate, cwg_ref, dug_ref, dcwg_ref, dcbg_ref)):
            dcb_ref[...] = jnp.sum(d, axis=0, keepdims=True)
            dcw_ref[2:3, :] = jnp.sum(d * uu, axis=0, keepdims=True)
            dcw_ref[1:2, :] = jnp.sum(d * uu1, axis=0, keepdims=True)
            dcw_ref[0:1, :] = jnp.sum(d * uu2, axis=0, keepdims=True)
            du = cw_ref[2:3, :] * d + cw_ref[1:2, :] * _shift_up(d, 1, rows, S) + cw_ref[0:1, :] * _shift_up(d, 2, rows, S)
            du_ref[...] = du.astype(du_ref.dtype)

    def col(r, off):
        return pl.BlockSpec((r, tc), lambda j: (0, off + j))

    o = jax.ShapeDtypeStruct((S, F), _MXU)
    w, b = jax.ShapeDtypeStruct((3, F), f32), jax.ShapeDtypeStruct((1, F), f32)
    duv, dug, dcwv, dcwg, dcbv, dcbg = _call(
        body, name=f"convglu_bwd_{S}", grid=(nc,),
        in_specs=[col(S, 0), col(S, nc), col(3, 0), col(3, nc), col(1, 0), col(1, nc), col(S, 0)],
        out_specs=[col(S, 0), col(S, 0), col(3, 0), col(3, 0), col(1, 0), col(1, 0)],
        out_shape=[o, o, w, w, b, b], compiler_params=_cp("parallel"))(
            u, u, cw, cw, cb.reshape(1, F2), cb.reshape(1, F2), dact)
    return (jnp.concatenate([duv, dug], axis=1), jnp.concatenate([dcwv, dcwg], axis=1),
            jnp.concatenate([dcbv, dcbg], axis=1).reshape(F2))


def _final_loss(h, delta, g, target):
    S, D = h.shape
    ts = _tile(S, 512, 8)
    row = pl.BlockSpec((ts, D), lambda i: (i, 0))
    vec = pl.BlockSpec((1, D), lambda i: (0, 0))
    one = pl.BlockSpec((1, _LANES), lambda i: (0, 0))

    def body(h_ref, d_ref, g_ref, t_ref, l_ref, dh_ref, dg_ref):
        @pl.when(pl.program_id(0) == 0)
        def _():
            l_ref[...] = jnp.zeros_like(l_ref)
            dg_ref[...] = jnp.zeros_like(dg_ref)

        xv = h_ref[...] + d_ref[...]
        gv = g_ref[...]
        r = lax.rsqrt(jnp.mean(xv * xv, axis=-1, keepdims=True) + EPS)
        xh = xv * r
        err = xh * gv - t_ref[...]
        l_ref[...] += 0.5 * jnp.sum(jnp.mean(err * err, axis=-1, keepdims=True), axis=0, keepdims=True)
        dy = err * (1.0 / D)
        dg_ref[...] += jnp.sum(dy * xh, axis=0, keepdims=True)
        dxh = dy * gv
        dh_ref[...] = r * (dxh - xh * jnp.mean(dxh * xh, axis=-1, keepdims=True))

    loss, dh, dg = _call(body, name=f"final_loss_{S}", grid=(S // ts,), in_specs=[row, row, vec, row],
                         out_specs=[one, row, vec],
                         out_shape=[jax.ShapeDtypeStruct((1, _LANES), f32), jax.ShapeDtypeStruct((S, D), f32),
                                    jax.ShapeDtypeStruct((1, D), f32)],
                         compiler_params=_cp("arbitrary"))(h, delta, g.reshape(1, D), target)
    return loss[0, 0], dh, dg.reshape(D)


def _adamw(w, g, m, v):
    R, C = w.shape
    tr = _tile(R, max(8, (1 << 19) // C // 8 * 8), 8)
    blk = pl.BlockSpec((tr, C), lambda i: (i, 0))

    def body(w_ref, g_ref, m_ref, v_ref, d_ref, nm_ref, nv_ref):
        gv = g_ref[...]
        m_new = ADAM_B1 * m_ref[...] + (1.0 - ADAM_B1) * gv
        v_new = ADAM_B2 * v_ref[...] + (1.0 - ADAM_B2) * (gv * gv)
        m_hat = m_new / (1.0 - ADAM_B1 ** ADAM_STEP)
        v_hat = v_new / (1.0 - ADAM_B2 ** ADAM_STEP)
        d_ref[...] = -ADAM_LR * (m_hat / (jnp.sqrt(v_hat) + ADAM_EPS) + ADAM_WD * w_ref[...])
        nm_ref[...] = m_new
        nv_ref[...] = v_new

    o = jax.ShapeDtypeStruct((R, C), f32)
    return _call(body, name=f"adamw_{R}x{C}", grid=(R // tr,), in_specs=[blk] * 4, out_specs=[blk] * 3,
                 out_shape=[o, o, o], compiler_params=_cp("parallel"))(w, g, m, v)


def _sum_slabs(xs):
    K, R, C = xs.shape
    tr = _tile(R, 256, 8)

    def body(x_ref, o_ref):
        s = x_ref[0]
        for k in range(1, K):
            s = s + x_ref[k]
        o_ref[...] = s

    return _call(body, name=f"sum_slabs_{K}x{R}", grid=(R // tr,),
                 in_specs=[pl.BlockSpec((K, tr, C), lambda i: (0, i, 0))], out_specs=pl.BlockSpec((tr, C), lambda i: (i, 0)),
                 out_shape=jax.ShapeDtypeStruct((R, C), f32), compiler_params=_cp("parallel"))(xs)


def _slab_rows(R, C):
    return _tile(R, max(16, (1 << 19) // C // 16 * 16), 16)


def _add_partial(g, recv, core):
    _, R, C = g.shape
    tr = _slab_rows(R, C)

    def body(c_ref, a_ref, b_ref, o_ref, w_ref):
        s = a_ref[...] + b_ref[...]
        o_ref[...] = s
        w_ref[...] = s.astype(w_ref.dtype)

    blk = pl.BlockSpec((1, tr, C), lambda k, i, c: (k, i, 0))
    return _call(
        body, name=f"add_partial_{R}x{C}",
        grid_spec=pltpu.PrefetchScalarGridSpec(
            num_scalar_prefetch=1, grid=(4, R // tr),
            in_specs=[pl.BlockSpec((1, tr, C), lambda k, i, c: (2 * k + c[0], i, 0)), blk], out_specs=[blk, blk]),
        out_shape=[jax.ShapeDtypeStruct((4, R, C), f32), jax.ShapeDtypeStruct((4, R, C), _WIRE)],
        compiler_params=_cp("parallel", "parallel"))(core, g, recv)


def _sum_final(p, recv, chip):
    _, R, C = p.shape
    tr = _slab_rows(R, C)

    def body(q_ref, p_ref, r_ref, o_ref):
        o_ref[...] = ((p_ref[0] + r_ref[0].astype(f32)) + r_ref[1].astype(f32)) + r_ref[2].astype(f32)

    return _call(
        body, name=f"sum_final_{R}x{C}",
        grid_spec=pltpu.PrefetchScalarGridSpec(
            num_scalar_prefetch=1, grid=(R // tr,),
            in_specs=[pl.BlockSpec((1, tr, C), lambda i, q: (q[0], i, 0)), pl.BlockSpec((3, tr, C), lambda i, q: (0, i, 0))],
            out_specs=pl.BlockSpec((tr, C), lambda i, q: (i, 0))),
        out_shape=jax.ShapeDtypeStruct((R, C), f32), compiler_params=_cp("parallel"))(chip, p, recv)


def _my_pos():
    return lax.axis_index("x"), lax.axis_index("y"), lax.axis_index("c")


def _all_gather(shards):
    n = len(shards)

    def body(*refs):
        start, forward, finish = _all_gather_stages(refs[:n], refs[n:2 * n], *refs[2 * n:])
        start()
        forward()
        finish()

    tag = "_".join(f"{s.shape[0]}x{s.shape[1]}" for s in shards)
    return _call(body, name=f"all_gather_{tag}", in_specs=[ANY] * n, out_specs=[ANY] * n,
                 out_shape=_all_gather_shapes(shards), scratch_shapes=_all_gather_sems(n))(*shards)


def _all_gather_shapes(shards):
    return [jax.ShapeDtypeStruct((NDEV,) + s.shape, s.dtype) for s in shards]


def _all_gather_sems(n):
    return [pltpu.SemaphoreType.DMA((7, n)), pltpu.SemaphoreType.DMA((7, n)), pltpu.SemaphoreType.DMA((n,))]


def _all_gather_stages(x_refs, out_refs, send_sems, recv_sems, local_sems):
    n = len(x_refs)
    x, y, c = _my_pos()
    me, sibling = (x, y, c), (x, y, 1 - c)
    chips = [(1 - x, y), (x, 1 - y), (1 - x, 1 - y)]

    def slab(t, px, py, pc):
        return out_refs[t].at[4 * px + 2 * py + pc]

    def copy(k, t, block, to, from_input=False):
        return pltpu.make_async_remote_copy(
            src_ref=x_refs[t] if from_input else slab(t, *block), dst_ref=slab(t, *block),
            send_sem=send_sems.at[k, t], recv_sem=recv_sems.at[k, t], device_id=to, device_id_type=MESH_ID)

    mine = [pltpu.make_async_copy(x_refs[t], slab(t, *me), local_sems.at[t]) for t in range(n)]
    first = [copy(0, t, me, sibling, True) for t in range(n)]
    first += [copy(1 + j, t, me, (*chip, c), True) for j, chip in enumerate(chips) for t in range(n)]
    passed = [copy(4 + j, t, (*chip, c), sibling) for j, chip in enumerate(chips) for t in range(n)]

    def start():
        for cp in mine + first:
            cp.start()

    def forward():
        for j, chip in enumerate(chips):
            for t in range(n):
                copy(1 + j, t, (*chip, c), me).wait_recv()
                passed[j * n + t].start()

    def finish():
        for t in range(n):
            copy(0, t, sibling, me).wait_recv()
        for j, chip in enumerate(chips):
            for t in range(n):
                copy(4 + j, t, (*chip, 1 - c), me).wait_recv()
        for cp in first + passed:
            cp.wait_send()
        for cp in mine:
            cp.wait()

    return start, forward, finish


def _exchange_cores(gs):
    n = len(gs)

    def body(*refs):
        g_refs, out_refs = refs[:n], refs[n:2 * n]
        send_sems, recv_sems = refs[2 * n:]
        x, y, c = _my_pos()
        cps = [pltpu.make_async_remote_copy(src_ref=g_refs[t].at[2 * q + (1 - c)], dst_ref=out_refs[t].at[q],
                                            send_sem=send_sems.at[q, t], recv_sem=recv_sems.at[q, t],
                                            device_id=(x, y, 1 - c), device_id_type=MESH_ID)
               for t in range(n) for q in range(4)]
        for cp in cps:
            cp.start()
        for cp in cps:
            cp.wait()

    tag = "_".join(f"{g.shape[1]}x{g.shape[2]}" for g in gs)
    return _call(body, name=f"exchange_cores_{tag}", in_specs=[ANY] * n, out_specs=[ANY] * n,
                 out_shape=[jax.ShapeDtypeStruct((4,) + g.shape[1:], g.dtype) for g in gs],
                 scratch_shapes=[pltpu.SemaphoreType.DMA((4, n)), pltpu.SemaphoreType.DMA((4, n))])(*gs)


def _exchange_chips(ps):
    n = len(ps)

    def body(*refs):
        start, finish = _exchange_chips_stages(refs[:n], refs[n:2 * n], *refs[2 * n:])
        start()
        finish()

    tag = "_".join(f"{p.shape[1]}x{p.shape[2]}" for p in ps)
    return _call(body, name=f"exchange_chips_{tag}", in_specs=[ANY] * n, out_specs=[ANY] * n,
                 out_shape=_exchange_chips_shapes(ps), scratch_shapes=_exchange_chips_sems(n))(*ps)


def _exchange_chips_shapes(ps):
    return [jax.ShapeDtypeStruct((3,) + p.shape[1:], p.dtype) for p in ps]


def _exchange_chips_sems(n):
    return [pltpu.SemaphoreType.DMA((3, n)), pltpu.SemaphoreType.DMA((3, n))]


def _exchange_chips_stages(p_refs, out_refs, send_sems, recv_sems):
    n = len(p_refs)
    x, y, c = _my_pos()
    chips = [(x, 1 - y), (1 - x, y), (1 - x, 1 - y)]
    cps = [pltpu.make_async_remote_copy(src_ref=p_refs[t].at[2 * px + py], dst_ref=out_refs[t].at[r],
                                        send_sem=send_sems.at[r, t], recv_sem=recv_sems.at[r, t],
                                        device_id=(px, py, c), device_id_type=MESH_ID)
           for t in range(n) for r, (px, py) in enumerate(chips)]

    def start():
        for cp in cps:
            cp.start()

    def finish():
        for cp in cps:
            cp.wait()

    return start, finish


def _exchange_all_shapes(gs):
    return [jax.ShapeDtypeStruct((NDEV - 1,) + g.shape[1:], g.dtype) for g in gs]


def _exchange_all_sems(n):
    return [pltpu.SemaphoreType.DMA((NDEV - 1, n)), pltpu.SemaphoreType.DMA((NDEV - 1, n))]


def _exchange_all_stages(g_refs, out_refs, send_sems, recv_sems):
    n = len(g_refs)
    x, y, c = _my_pos()
    cps = []
    for t in range(n):
        for r in range(1, NDEV):
            px, py, pc = (1 - x if r & 4 else x), (1 - y if r & 2 else y), (1 - c if r & 1 else c)
            cps.append(pltpu.make_async_remote_copy(
                src_ref=g_refs[t].at[4 * px + 2 * py + pc], dst_ref=out_refs[t].at[r - 1],
                send_sem=send_sems.at[r - 1, t], recv_sem=recv_sems.at[r - 1, t],
                device_id=(px, py, pc), device_id_type=MESH_ID))

    def start():
        for cp in cps:
            cp.start()

    def finish():
        for cp in cps:
            cp.wait()

    return start, finish


def _sum_all(g, recv, dev):
    _, R, C = g.shape
    tr = _slab_rows(R, C)

    def body(d_ref, g_ref, r_ref, o_ref):
        s = g_ref[0].astype(f32)
        for k in range(NDEV - 1):
            s = s + r_ref[k].astype(f32)
        o_ref[...] = s

    return _call(
        body, name=f"sum_all_{R}x{C}",
        grid_spec=pltpu.PrefetchScalarGridSpec(
            num_scalar_prefetch=1, grid=(R // tr,),
            in_specs=[pl.BlockSpec((1, tr, C), lambda i, d: (d[0], i, 0)),
                      pl.BlockSpec((NDEV - 1, tr, C), lambda i, d: (0, i, 0))],
            out_specs=pl.BlockSpec((tr, C), lambda i, d: (i, 0))),
        out_shape=jax.ShapeDtypeStruct((R, C), f32), compiler_params=_cp("parallel"))(dev, g, recv)


def _pack(arrs, dtype, rows_mult):
    flat = jnp.concatenate([a.astype(dtype).reshape(-1) for a in arrs])
    q = rows_mult * _PACK_COLS
    tot = -(-flat.shape[0] // q) * q
    return jnp.pad(flat, (0, tot - flat.shape[0])).reshape(tot // _PACK_COLS, _PACK_COLS)


def _unpack(flat, shapes):
    out, off = [], 0
    for s in shapes:
        n = math.prod(s)
        out.append(flat[..., off:off + n].reshape(flat.shape[:-1] + tuple(s)))
        off += n
    return out


def _width_groups(names, shapes):
    groups = {}
    for n in names:
        groups.setdefault(shapes[n][-1], []).append(n)
    return list(groups.values())


def _to_full(piece, ax):
    t = jnp.moveaxis(piece, 0, ax)
    return t.reshape(t.shape[:ax] + (t.shape[ax] * t.shape[ax + 1],) + t.shape[ax + 2:])


def _to_shards(g, ax):
    t = g.reshape(g.shape[:ax] + (NDEV, g.shape[ax] // NDEV) + g.shape[ax + 1:])
    return jnp.moveaxis(t, ax, 0)


def _split_rows(buf, shapes):
    out, off = [], 0
    for s in shapes:
        r = math.prod(s[:-1])
        out.append(buf[..., off:off + r, :].reshape(buf.shape[:-2] + tuple(s)))
        off += r
    return out


def _item_ax(item):
    return SHARD_AXIS[item[0]] - 1


def _gather_bufs(shards, items):
    shapes = {it: shards[it[0]].shape[1:] for it in items}
    groups = _width_groups(items, shapes)
    bufs = [jnp.concatenate([shards[n][l].astype(_MXU).reshape(-1, shapes[(n, l)][-1]) for n, l in grp], axis=0)
            for grp in groups]
    return groups, bufs


def _gather_unpack(shards, groups, outs):
    full = {}
    for grp, g in zip(groups, outs):
        for it, piece in zip(grp, _split_rows(g, [shards[it[0]].shape[1:] for it in grp])):
            full[it] = _to_full(piece, _item_ax(it))
    return full


def _scatter_bufs(grads, shards, items, dtype=f32):
    shapes = {it: shards[it[0]].shape[1:] for it in items}
    groups = _width_groups(items, shapes)
    bufs = [jnp.concatenate([_to_shards(grads[it], _item_ax(it)).astype(dtype).reshape(NDEV, -1, shapes[it][-1])
                             for it in grp], axis=1) for grp in groups]
    bufs = [jnp.pad(b, ((0, 0), (0, -b.shape[1] % 256), (0, 0))) for b in bufs]
    return groups, bufs


def _mesh_scalars():
    core = jnp.reshape(lax.axis_index("c"), (1,)).astype(jnp.int32)
    chip = jnp.reshape(2 * lax.axis_index("x") + lax.axis_index("y"), (1,)).astype(jnp.int32)
    return core, chip


def _chip_sums(bufs):
    core, _ = _mesh_scalars()
    return [_add_partial(b, r, core) for b, r in zip(bufs, _exchange_cores(bufs))]


def _device_sums(chip_sums, from_chips):
    _, chip = _mesh_scalars()
    return [_sum_final(p, r, chip) for (p, _), r in zip(chip_sums, from_chips)]


def _scatter_unpack(shards, groups, totals):
    out = {}
    for grp, t in zip(groups, totals):
        out.update(zip(grp, _split_rows(t, [shards[it[0]].shape[1:] for it in grp])))
    return out


def _small_scatter_buf(grads, names):
    small = jnp.concatenate([_to_shards(grads[n], SHARD_AXIS[n]).reshape(NDEV, -1) for n in names], axis=1)
    q = 16 * _PACK_COLS
    tot = -(-small.shape[1] // q) * q
    return jnp.pad(small, ((0, 0), (0, tot - small.shape[1]))).reshape(NDEV, tot // _PACK_COLS, _PACK_COLS)


def _all_reduce_small(grads, names):
    shapes = [grads[n].shape for n in names]
    packed = _pack([grads[n] for n in names], f32, 8)
    total = _sum_slabs(_all_gather([packed])[0])
    return dict(zip(names, _unpack(total.reshape(-1), shapes)))


def _as2d(a):
    if a.ndim == 1:
        return a.reshape(1, -1)
    return a.reshape(-1, a.shape[-1])


def kernel(x, mem, mix_norm_g, pool_w, pool_scale, sb_w_qkv, sb_w_o, s5_a_re, s5_a_im, s5_log_dt, s5_b_re, s5_b_im, s5_c_re, s5_c_im, s5_d, s5_w_glu, xa_norm_g, mem_norm_g, xa_wq, xa_wkv, xa_wo, ffn_norm_g, ffn_w_up, ffn_conv_w, ffn_conv_b, ffn_w_down, final_norm_g, loss_target, m_mix_norm_g, m_pool_w, m_pool_scale, m_sb_w_qkv, m_sb_w_o, m_s5_a_re, m_s5_a_im, m_s5_log_dt, m_s5_b_re, m_s5_b_im, m_s5_c_re, m_s5_c_im, m_s5_d, m_s5_w_glu, m_xa_norm_g, m_mem_norm_g, m_xa_wq, m_xa_wkv, m_xa_wo, m_ffn_norm_g, m_ffn_w_up, m_ffn_conv_w, m_ffn_conv_b, m_ffn_w_down, m_final_norm_g, v_mix_norm_g, v_pool_w, v_pool_scale, v_sb_w_qkv, v_sb_w_o, v_s5_a_re, v_s5_a_im, v_s5_log_dt, v_s5_b_re, v_s5_b_im, v_s5_c_re, v_s5_c_im, v_s5_d, v_s5_w_glu, v_xa_norm_g, v_mem_norm_g, v_xa_wq, v_xa_wkv, v_xa_wo, v_ffn_norm_g, v_ffn_w_up, v_ffn_conv_w, v_ffn_conv_b, v_ffn_w_down, v_final_norm_g):
    args = locals()
    w = {n: args[n] for n in WEIGHTS}
    mom = {n: args["m_" + n] for n in WEIGHTS}
    var = {n: args["v_" + n] for n in WEIGHTS}
    h0, memv, target = x[0], mem[0], loss_target[0]
    S, D = h0.shape
    depth = mix_norm_g.shape[0]
    n_mix = 3

    first_layer = {'pool_w': 0, 'sb_w_qkv': 1, 'sb_w_o': 1, 's5_w_glu': 2}

    def layer_of(item):
        return first_layer[item[0]] + n_mix * item[1] if item[0] in first_layer else item[1]

    sb_layer = 1 if depth > 1 else None
    items = [(n, l) for n in MXU_WEIGHTS for l in range(w[n].shape[0])]
    early_w = [it for it in items if sb_layer is None or layer_of(it) < sb_layer or it == ('sb_w_qkv', 0)]
    late_w = [it for it in items if it not in early_w]
    groups_a, bufs_a = _gather_bufs(w, early_w)
    outs_a = _all_gather(bufs_a + [_pack([w[n] for n in VEC_WEIGHTS], f32, 8)])
    wfull = _gather_unpack(w, groups_a, outs_a[:-1])
    full = dict(w)
    for n, piece in zip(VEC_WEIGHTS, _unpack(outs_a[-1].reshape(NDEV, -1), [w[n].shape for n in VEC_WEIGHTS])):
        full[n] = _to_full(piece, SHARD_AXIS[n])
    groups_b, bufs_b = _gather_bufs(w, late_w)

    grads = {}

    def acc(name, j, val):
        grads.setdefault(name, {})[j] = val

    s5 = []
    for j in range(s5_a_re.shape[0]):
        G, P = s5_a_re.shape[1:]
        cg = s5_b_re.shape[3]
        N = G * P
        nb = D // _LANES
        ar, ai = s5_a_re[j].reshape(1, N), s5_a_im[j].reshape(1, N)
        ldt = jnp.repeat(s5_log_dt[j], P).reshape(1, N)
        br = s5_b_re[j].transpose(2, 0, 1).reshape(cg, N)
        bi = s5_b_im[j].transpose(2, 0, 1).reshape(cg, N)
        abr, abi, bbr, bbi = _s5_prep_fwd(ar, ai, ldt, br, bi)
        s5.append(dict(ar=ar, ai=ai, ldt=ldt, br=br, bi=bi, abr=abr, abi=abi, G=G, P=P, cg=cg, N=N, nb=nb,
                       Br=_blockdiag_in(bbr, nb, P), Bi=_blockdiag_in(bbi, nb, P),
                       Cr=_blockdiag_out(s5_c_re[j], nb), Ci=_blockdiag_out(s5_c_im[j], nb)))

    saved = []
    h, delta = h0, None
    for i in range(depth):
        kind, j = i % n_mix, i // n_mix
        sv = dict(kind=kind, j=j)
        if i == 0:
            hn = _rms_fwd(h, full['mix_norm_g'][i], out_dtype=_MXU if kind == 1 else f32)
        sv['h'] = h
        if kind == 0:
            p = _pool_fwd(hn)
            t = _pool_mix_fwd(p, wfull['pool_w', j], full['pool_scale'][j])
            sv.update(p=p)
        elif kind == 1:
            qkv = _mm(hn, wfull['sb_w_qkv', j], out_dtype=_MXU)
            if i == sb_layer:
                o, outs_b = _sb_fwd(qkv, bufs_b)
                wfull.update(_gather_unpack(w, groups_b, outs_b))
            else:
                o, _ = _sb_fwd(qkv)
            h1, a = _mm_add_rms(o, wfull['sb_w_o', j], h, full['xa_norm_g'][i], out_dtype=_MXU)
            sv.update(hn=hn, qkv=qkv, o=o)
        else:
            pr = s5[j]
            bur, bui = _s5_in_fwd(hn, pr['Br'], pr['Bi'])
            xr, xi = _s5_scan_fwd(bur, bui, pr['abr'], pr['abi'])
            y, z = _s5_out_fwd(xr, xi, pr['Cr'], pr['Ci'], hn, full['s5_d'][j])
            vg = _mm(z, wfull['s5_w_glu', j])
            t = _glu_fwd(vg)
            sv.update(hn=hn, xr=xr, xi=xi, y=y, z=z, vg=vg)
        if kind != 1:
            h1, a = _rms_fwd(h, full['xa_norm_g'][i], t, out_dtype=_MXU)
        memn = _rms_fwd(memv, full['mem_norm_g'][i], out_dtype=_MXU)
        q = _mm(a, wfull['xa_wq', i], out_dtype=_MXU)
        kv = _mm(memn, wfull['xa_wkv', i], out_dtype=_MXU)
        o2 = _xa_fwd(q, kv)
        h2, b = _mm_add_rms(o2, wfull['xa_wo', i], h1, full['ffn_norm_g'][i], out_dtype=_MXU)
        u = _mm(b, wfull['ffn_w_up', i])
        act = _convglu_fwd(u, full['ffn_conv_w'][i], full['ffn_conv_b'][i])
        sv.update(h1=h1, a=a, memn=memn, q=q, kv=kv, o2=o2, h2=h2, b=b, u=u, act=act)
        saved.append(sv)
        if i + 1 < depth:
            h, hn = _mm_add_rms(act, wfull['ffn_w_down', i], h2, full['mix_norm_g'][i + 1],
                                out_dtype=_MXU if (i + 1) % n_mix == 1 else f32)
        else:
            h, delta = h2, _mm(act, wfull['ffn_w_down', i])

    loss_local, dh, dg_final = _final_loss(h, delta, full['final_norm_g'], target)
    loss = lax.psum(loss_local, ("x", "y", "c"))
    grads['final_norm_g'] = dg_final

    for i in reversed(range(depth)):
        sv = saved[i]
        kind, j = sv['kind'], sv['j']
        dact = _mm(dh, wfull['ffn_w_down', i], tb=True)
        gdt = _WIRE if sb_layer is not None and i >= sb_layer else f32
        acc('ffn_w_down', i, _mm(sv['act'], dh, ta=True, out_dtype=gdt))
        du, dcw, dcb = _convglu_bwd(sv['u'], full['ffn_conv_w'][i], full['ffn_conv_b'][i], dact)
        acc('ffn_conv_w', i, dcw)
        acc('ffn_conv_b', i, dcb)
        dh2, dg = _mm_rms_bwd(du, wfull['ffn_w_up', i], sv['h2'], full['ffn_norm_g'][i], dh)
        acc('ffn_w_up', i, _mm(sv['b'], du, ta=True, out_dtype=gdt))
        acc('ffn_norm_g', i, dg)
        do2 = _mm(dh2, wfull['xa_wo', i], tb=True, out_dtype=_MXU)
        acc('xa_wo', i, _mm(sv['o2'], dh2, ta=True, out_dtype=gdt))
        dq, dk, dv = _xa_bwd(sv['q'], sv['kv'], do2)
        dkv = jnp.concatenate([dk, dv], axis=1)
        acc('xa_wq', i, _mm(sv['a'], dq, ta=True, out_dtype=gdt))
        dmemn = _mm(dkv, wfull['xa_wkv', i], tb=True)
        acc('xa_wkv', i, _mm(sv['memn'], dkv, ta=True, out_dtype=gdt))
        _, dg = _rms_bwd(memv, full['mem_norm_g'][i], dmemn)
        acc('mem_norm_g', i, dg)
        dh1, dg = _mm_rms_bwd(dq, wfull['xa_wq', i], sv['h1'], full['xa_norm_g'][i], dh2)
        acc('xa_norm_g', i, dg)
        if kind == 0:
            dp, dw, ds = _pool_mix_bwd(sv['p'], wfull['pool_w', j], full['pool_scale'][j], dh1)
            acc('pool_w', j, dw)
            acc('pool_scale', j, ds)
            dhn = _pool_bwd(dp)
        elif kind == 1:
            do = _mm(dh1, wfull['sb_w_o', j], tb=True)
            acc('sb_w_o', j, _mm(sv['o'], dh1, ta=True, out_dtype=gdt))
            if i == sb_layer:
                early_g = [it for it in items if it != ('sb_w_qkv', j) and layer_of(it) >= sb_layer]
                groups_e, wire_e = _scatter_bufs({it: grads[it[0]][it[1]] for it in early_g}, w, early_g, _WIRE)
                dqs, dks, dvs, recv_e = _sb_bwd(sv['qkv'], sv['o'], do, wire_e)
            else:
                dqs, dks, dvs, _ = _sb_bwd(sv['qkv'], sv['o'], do)
            dqkv = jnp.concatenate([dqs, dks, dvs], axis=1)
            dhn = _mm(dqkv, wfull['sb_w_qkv', j], tb=True)
            acc('sb_w_qkv', j, _mm(sv['hn'], dqkv, ta=True))
        else:
            pr = s5[j]
            dval, dgate = _glu_bwd(sv['vg'], dh1)
            dvg = jnp.concatenate([dval, dgate], axis=1)
            dz = _mm(dvg, wfull['s5_w_glu', j], tb=True)
            acc('s5_w_glu', j, _mm(sv['z'], dvg, ta=True, out_dtype=gdt))
            dy, gdr, gdi, du0, dd = _s5_out_bwd(dz, sv['y'], sv['hn'], full['s5_d'][j], pr['Cr'], pr['Ci'])
            acc('s5_d', j, dd)
            gr, gi, dabr, dabi = _s5_scan_bwd(gdr, gdi, sv['xr'], sv['xi'], pr['abr'], pr['abi'])
            dhn = _s5_in_bwd(gr, gi, pr['Br'], pr['Bi'], du0)
            dBr, dBi, dCr, dCi = _s5_wgrad(sv['hn'], gr, gi, sv['xr'], sv['xi'], dy)
            cg, P, G = pr['cg'], pr['P'], pr['G']
            acc('s5_c_re', j, _blockdiag_out_extract(dCr, cg, P))
            acc('s5_c_im', j, _blockdiag_out_extract(dCi, cg, P))
            dar, dai, dldt, dbr, dbi = _s5_prep_bwd(pr['ar'], pr['ai'], pr['ldt'], pr['br'], pr['bi'], dabr, dabi,
                                                    _blockdiag_in_extract(dBr, cg, P), _blockdiag_in_extract(dBi, cg, P))
            acc('s5_a_re', j, dar.reshape(G, P))
            acc('s5_a_im', j, dai.reshape(G, P))
            acc('s5_log_dt', j, dldt.reshape(G, P).sum(axis=1))
            acc('s5_b_re', j, dbr.reshape(cg, G, P).transpose(1, 2, 0))
            acc('s5_b_im', j, dbi.reshape(cg, G, P).transpose(1, 2, 0))
        dh, dg = _rms_bwd(sv['h'], full['mix_norm_g'][i], dhn, dh1)
        acc('mix_norm_g', i, dg)
    grad_x = dh[None]

    gfull = {}
    for n in VEC_WEIGHTS + REPLICATED:
        gfull[n] = grads[n] if n == 'final_norm_g' else jnp.stack([grads[n][k] for k in range(len(grads[n]))])
    if sb_layer is None:
        early_g, local = [], {}
    else:
        dev = jnp.reshape(4 * lax.axis_index("x") + 2 * lax.axis_index("y") + lax.axis_index("c"), (1,)).astype(jnp.int32)
        local = _scatter_unpack(w, groups_e, [_sum_all(g, r, dev) for g, r in zip(wire_e, recv_e)])
    late_g = [it for it in items if it not in early_g]
    groups_l, bufs_l = _scatter_bufs({it: grads[it[0]][it[1]] for it in late_g}, w, late_g)
    sums_l = _chip_sums(bufs_l + [_small_scatter_buf(gfull, VEC_WEIGHTS)])
    totals_l = _device_sums(sums_l, _exchange_chips([wire for _, wire in sums_l]))
    local.update(_scatter_unpack(w, groups_l, totals_l[:-1]))
    gw = {n: jnp.stack([local[(n, l)] for l in range(w[n].shape[0])]) for n in MXU_WEIGHTS}
    gw.update(zip(VEC_WEIGHTS, _unpack(totals_l[-1].reshape(-1), [w[n].shape for n in VEC_WEIGHTS])))
    gw.update(_all_reduce_small(gfull, REPLICATED))

    deltas, new_m, new_v = {}, {}, {}
    for n in WEIGHTS:
        shp = w[n].shape
        d_, m_, v_ = _adamw(_as2d(w[n]), _as2d(gw[n]), _as2d(mom[n]), _as2d(var[n]))
        deltas[n], new_m[n], new_v[n] = d_.reshape(shp), m_.reshape(shp), v_.reshape(shp)

    return (loss, grad_x, *[gw[n] for n in WEIGHTS], *[deltas[n] for n in WEIGHTS],
            *[new_m[n] for n in WEIGHTS], *[new_v[n] for n in WEIGHTS])
```
